```python
import jax, jax.numpy as jnp
from jax import lax
import numpy as np

D_MODEL = 2048
BATCH = 8
SEQ = 2048
DEPTH = 1

GLA_HEADS = 4
GLA_DK = 128
GLA_DV = 256
GLA_GATE_RANK = 16
GLA_GATE_NORM = 16.0
GLA_CHUNK = 64
SWA_Q_HEADS = 16
SWA_KV_HEADS = 4
SWA_GROUP = SWA_Q_HEADS // SWA_KV_HEADS
SWA_HEAD_DIM = 64
SWA_WINDOW = 128
SWA_BLOCK = 128
N_GROUPS = 4
EXPERTS_PER_GROUP = 16
N_EXPERTS = N_GROUPS * EXPERTS_PER_GROUP
EXPERT_TOP_K = 2
D_EXPERT = 256
RMS_EPS = 1e-6

GLA_QK_W = GLA_HEADS * GLA_DK
GLA_V_W = GLA_HEADS * GLA_DV
SWA_Q_W = SWA_Q_HEADS * SWA_HEAD_DIM
SWA_KV_W = SWA_KV_HEADS * SWA_HEAD_DIM
MIX_WIDTH = GLA_V_W + SWA_Q_W
IN_SPLITS = (GLA_QK_W, GLA_QK_W, GLA_V_W, GLA_V_W, GLA_GATE_RANK, SWA_Q_W, SWA_KV_W, SWA_KV_W)
IN_WIDTH = GLA_QK_W * 2 + GLA_V_W * 2 + GLA_GATE_RANK + SWA_Q_W + SWA_KV_W * 2

kernel_name = "hymba_gla_swa_sink_alibi_hmoe"


def rms_norm(x, g):
    xf = x.astype(jnp.float32)
    y = xf * lax.rsqrt(jnp.mean(xf * xf, axis=-1, keepdims=True) + RMS_EPS)
    return (y * g.astype(jnp.float32)).astype(x.dtype)


def split_columns(p):
    outs, start = [], 0
    for w in IN_SPLITS:
        outs.append(p[..., start:start + w])
        start += w
    return outs


def alibi_slopes(n_heads):
    return 2.0 ** (-8.0 * jnp.arange(1, n_heads + 1, dtype=jnp.float32) / n_heads)


def gla_mixer(q, k, v, r, gate_lr, w_gk_up, b_gk, norm_g):
    B, T = q.shape[0], q.shape[1]
    C = GLA_CHUNK
    N = T // C
    f32 = jnp.float32
    qf = q.astype(f32).reshape(B, N, C, GLA_HEADS, GLA_DK) * (GLA_DK ** -0.5)
    kf = k.astype(f32).reshape(B, N, C, GLA_HEADS, GLA_DK)
    vf = v.astype(f32).reshape(B, N, C, GLA_HEADS, GLA_DV)
    log_a = jax.nn.log_sigmoid((gate_lr @ w_gk_up + b_gk).astype(f32)) / GLA_GATE_NORM
    log_a = log_a.reshape(B, N, C, GLA_HEADS, GLA_DK)
    bcum = jnp.cumsum(log_a, axis=2)
    b_mid = bcum[:, :, C // 2 - 1:C // 2]
    q_in = qf * jnp.exp(bcum - b_mid)
    k_in = kf * jnp.exp(b_mid - bcum)
    A = jnp.einsum('bnihd,bnjhd->bnhij', q_in, k_in)
    causal = jnp.tril(jnp.ones((C, C), dtype=bool))
    A = jnp.where(causal, A, 0.0)
    o_intra = jnp.einsum('bnhij,bnjhv->bnihv', A, vf)
    b_last = bcum[:, :, -1]
    k_dec = kf * jnp.exp(b_last[:, :, None] - bcum)
    kv = jnp.einsum('bnjhd,bnjhv->bnhdv', k_dec, vf)
    decay = jnp.exp(b_last)

    def step(S, inp):
        kv_n, dec_n = inp
        return dec_n[..., None] * S + kv_n, S

    S0 = jnp.zeros((B, GLA_HEADS, GLA_DK, GLA_DV), f32)
    _, S_prev = lax.scan(step, S0, (jnp.moveaxis(kv, 1, 0), jnp.moveaxis(decay, 1, 0)))
    o_inter = jnp.einsum('bnihd,nbhdv->bnihv', qf * jnp.exp(bcum), S_prev)
    o = (o_intra + o_inter).reshape(B, T, GLA_HEADS, GLA_DV)
    o = o * lax.rsqrt(jnp.mean(o * o, axis=-1, keepdims=True) + RMS_EPS) * norm_g.astype(f32)
    o = o.reshape(B, T, GLA_V_W) * jax.nn.silu(r.astype(f32))
    return o.astype(q.dtype)


def swa_mixer(q, k, v, sinks):
    B, T = q.shape[0], q.shape[1]
    Q = SWA_BLOCK
    NB = T // Q
    f32 = jnp.float32
    qb = q.astype(f32).reshape(B, NB, Q, SWA_KV_HEADS, SWA_GROUP, SWA_HEAD_DIM) * (SWA_HEAD_DIM ** -0.5)

    def band(t):
        t = t.astype(f32).reshape(B, T, SWA_KV_HEADS, SWA_HEAD_DIM)
        tp = jnp.pad(t, ((0, 0), (Q, 0), (0, 0), (0, 0))).reshape(B, NB + 1, Q, SWA_KV_HEADS, SWA_HEAD_DIM)
        return jnp.concatenate([tp[:, :-1], tp[:, 1:]], axis=2)

    kb, vb = band(k), band(v)
    s = jnp.einsum('bnqhgd,bnkhd->bhgnqk', qb, kb)
    rel = jnp.arange(Q)[:, None] + Q - jnp.arange(2 * Q)[None, :]
    key_pos = jnp.arange(NB)[:, None] * Q - Q + jnp.arange(2 * Q)[None, :]
    valid = ((rel >= 0) & (rel < SWA_WINDOW))[None] & (key_pos >= 0)[:, None, :]
    slopes = alibi_slopes(SWA_Q_HEADS).reshape(SWA_KV_HEADS, SWA_GROUP)
    s = s - slopes[:, :, None, None, None] * rel.astype(f32)
    s = jnp.where(valid, s, -jnp.inf)
    sink = sinks.astype(f32).reshape(SWA_KV_HEADS, SWA_GROUP)[:, :, None, None, None]
    m = jnp.maximum(jnp.max(s, axis=-1, keepdims=True), sink)
    p = jnp.exp(s - m)
    p = p / (jnp.sum(p, axis=-1, keepdims=True) + jnp.exp(sink - m))
    o = jnp.einsum('bhgnqk,bnkhd->bnqhgd', p, vb)
    return o.reshape(B, T, SWA_Q_W).astype(q.dtype)


def hier_moe(x, w_group, b_group, w_router, b_router, w_gate, w_up, w_down):
    B, T, D = x.shape
    f32 = jnp.float32
    xt = x.reshape(B * T, D)
    g_logits = (xt @ w_group + b_group).astype(f32)
    g_prob = jax.nn.softmax(g_logits, axis=-1)
    g_idx = jnp.argmax(g_logits, axis=-1)
    g_p = jnp.take_along_axis(g_prob, g_idx[:, None], axis=-1)[:, 0]
    e_logits = (xt @ w_router + b_router).astype(f32).reshape(-1, N_GROUPS, EXPERTS_PER_GROUP)
    sel = jnp.take_along_axis(e_logits, g_idx[:, None, None], axis=1)[:, 0]
    e_prob = jax.nn.softmax(sel, axis=-1)
    top_w, top_i = lax.top_k(e_prob, EXPERT_TOP_K)
    top_w = top_w / jnp.sum(top_w, axis=-1, keepdims=True)
    within = jnp.sum(jax.nn.one_hot(top_i, EXPERTS_PER_GROUP, dtype=f32) * top_w[..., None], axis=1)
    combine = (jax.nn.one_hot(g_idx, N_GROUPS, dtype=f32)[:, :, None]
               * (g_p[:, None] * within)[:, None, :]).astype(x.dtype)
    wg = w_gate.reshape(N_GROUPS, EXPERTS_PER_GROUP, D, D_EXPERT)
    wu = w_up.reshape(N_GROUPS, EXPERTS_PER_GROUP, D, D_EXPERT)
    wd = w_down.reshape(N_GROUPS, EXPERTS_PER_GROUP, D_EXPERT, D)
    out = jnp.zeros_like(xt)
    for g in range(N_GROUPS):
        h = jax.nn.silu(jnp.einsum('td,edf->tef', xt, wg[g])) * jnp.einsum('td,edf->tef', xt, wu[g])
        out = out + jnp.einsum('tef,efd->td', h * combine[:, g, :, None], wd[g])
    return out.reshape(B, T, D)


def setup_inputs(seed: int = 0) -> dict:
    key = jax.random.key(seed)
    ks = jax.random.split(key, 20)
    L, D = DEPTH, D_MODEL
    nrm = lambda k, shape, scale: jax.random.normal(k, shape, jnp.float32) * scale
    return {
        "x": nrm(ks[0], (BATCH, SEQ, D), 1.0),
        "norm_mix_g": 1.0 + nrm(ks[1], (L, D), 0.02),
        "w_in": nrm(ks[2], (L, D, IN_WIDTH), D ** -0.5),
        "w_gk_up": nrm(ks[3], (L, GLA_GATE_RANK, GLA_QK_W), GLA_GATE_RANK ** -0.5),
        "b_gk": nrm(ks[4], (L, GLA_QK_W), 0.1) + 2.0,
        "gla_norm_g": 1.0 + nrm(ks[5], (L, GLA_HEADS, GLA_DV), 0.02),
        "swa_sinks": nrm(ks[6], (L, SWA_Q_HEADS), 0.5),
        "w_out": nrm(ks[7], (L, MIX_WIDTH, D), MIX_WIDTH ** -0.5),
        "norm_ffn_g": 1.0 + nrm(ks[8], (L, D), 0.02),
        "w_group": nrm(ks[9], (L, D, N_GROUPS), D ** -0.5),
        "b_group": nrm(ks[10], (L, N_GROUPS), 0.01),
        "w_router": nrm(ks[11], (L, D, N_EXPERTS), D ** -0.5),
        "b_router": nrm(ks[12], (L, N_EXPERTS), 0.01),
        "w_gate": nrm(ks[13], (L, N_EXPERTS, D, D_EXPERT), D ** -0.5),
        "w_up": nrm(ks[14], (L, N_EXPERTS, D, D_EXPERT), D ** -0.5),
        "w_down": nrm(ks[15], (L, N_EXPERTS, D_EXPERT, D), D_EXPERT ** -0.5),
        "norm_final_g": 1.0 + nrm(ks[16], (D,), 0.02),
    }


def reference(x, norm_mix_g, w_in, w_gk_up, b_gk, gla_norm_g, swa_sinks, w_out,
              norm_ffn_g, w_group, b_group, w_router, b_router, w_gate, w_up, w_down,
              norm_final_g):
    h = x
    for l in range(DEPTH):
        xn = rms_norm(h, norm_mix_g[l])
        proj = xn @ w_in[l]
        gq, gk, gv, gr, glr, sq, sk, sv = split_columns(proj)
        o_gla = gla_mixer(gq, gk, gv, gr, glr, w_gk_up[l], b_gk[l], gla_norm_g[l])
        o_swa = swa_mixer(sq, sk, sv, swa_sinks[l])
        mixed = jnp.concatenate([o_gla, o_swa], axis=-1)
        h = h + mixed @ w_out[l]
        hn = rms_norm(h, norm_ffn_g[l])
        h = h + hier_moe(hn, w_group[l], b_group[l], w_router[l], b_router[l],
                         w_gate[l], w_up[l], w_down[l])
    return rms_norm(h, norm_final_g)
```

```python
import functools

import jax
import jax.numpy as jnp
from jax import lax
from jax.experimental import pallas as pl
from jax.experimental.pallas import tpu as pltpu

F32 = jnp.float32
BF16 = jnp.bfloat16

D_MODEL = 2048
GLA_HEADS = 4
GLA_DK = 128
GLA_DV = 256
GLA_RANK = 16
GLA_GATE_NORM = 16.0
GLA_CHUNK = 64
SWA_Q_HEADS = 16
SWA_KV_HEADS = 4
SWA_GROUP = SWA_Q_HEADS // SWA_KV_HEADS
SWA_HEAD_DIM = 64
SWA_BLOCK = 128
N_GROUPS = 4
EXPERTS_PER_GROUP = 16
N_EXPERTS = N_GROUPS * EXPERTS_PER_GROUP
D_EXPERT = 256
RMS_EPS = 1e-6

GLA_QK_W = GLA_HEADS * GLA_DK
GLA_V_W = GLA_HEADS * GLA_DV
SWA_Q_W = SWA_Q_HEADS * SWA_HEAD_DIM
SWA_KV_W = SWA_KV_HEADS * SWA_HEAD_DIM
OFF_GQ = 0
OFF_GK = OFF_GQ + GLA_QK_W
OFF_GV = OFF_GK + GLA_QK_W
OFF_GR = OFF_GV + GLA_V_W
OFF_SQ = OFF_GR + GLA_V_W
OFF_SK = OFF_SQ + SWA_Q_W
OFF_SV = OFF_SK + SWA_KV_W
PROJ_W = OFF_SV + SWA_KV_W
LANES = 128

IN_TM = 1024
IN_TN = 768
OUT_TM = 256
MOE_TR = 256
CMB_TM = 256
VMEM_LIMIT = 56 * 1024 * 1024


def _silu(x):
    return x / (1.0 + jnp.exp(-x))


def _inproj_kernel(x_ref, g_ref, w_ref, wlr_ref, proj_ref, glr_ref, xn_ref):
    @pl.when(pl.program_id(1) == 0)
    def _():
        x = x_ref[...]
        ms = jnp.mean(x * x, axis=-1, keepdims=True)
        xn = (x * lax.rsqrt(ms + RMS_EPS) * g_ref[...]).astype(BF16)
        xn_ref[...] = xn
        glr_ref[...] = jnp.dot(xn, wlr_ref[...], preferred_element_type=F32)

    proj_ref[...] = jnp.dot(xn_ref[...], w_ref[...], preferred_element_type=F32).astype(BF16)


def _in_proj(x2, g, w_main, w_lr):
    m = x2.shape[0]
    tm = min(IN_TM, m)
    return pl.pallas_call(
        _inproj_kernel,
        grid=(m // tm, PROJ_W // IN_TN),
        in_specs=[
            pl.BlockSpec((tm, D_MODEL), lambda i, j: (i, 0)),
            pl.BlockSpec((1, D_MODEL), lambda i, j: (0, 0)),
            pl.BlockSpec((D_MODEL, IN_TN), lambda i, j: (0, j)),
            pl.BlockSpec((D_MODEL, LANES), lambda i, j: (0, 0)),
        ],
        out_specs=[
            pl.BlockSpec((tm, IN_TN), lambda i, j: (i, j)),
            pl.BlockSpec((tm, LANES), lambda i, j: (i, 0)),
        ],
        out_shape=[
            jax.ShapeDtypeStruct((m, PROJ_W), BF16),
            jax.ShapeDtypeStruct((m, LANES), F32),
        ],
        scratch_shapes=[pltpu.VMEM((tm, D_MODEL), BF16)],
        compiler_params=pltpu.CompilerParams(
            dimension_semantics=("arbitrary", "arbitrary"), vmem_limit_bytes=VMEM_LIMIT),
        name="in_proj",
    )(x2, g, w_main, w_lr)


def _gla_kernel(q_ref, k_ref, v_ref, r_ref, glr_ref, wup_ref, bgk_ref, ng_ref, o_ref,
                la_ref, st_ref):
    t = q_ref.shape[0]
    c = GLA_CHUNK
    z = jnp.dot(glr_ref[...].astype(BF16), wup_ref[...], preferred_element_type=F32) + bgk_ref[...]
    la_ref[...] = (jnp.minimum(z, 0.0) - jnp.log1p(jnp.exp(-jnp.abs(z)))) * (1.0 / GLA_GATE_NORM)
    st_ref[...] = jnp.zeros_like(st_ref)

    ii = lax.broadcasted_iota(jnp.int32, (c, c), 0)
    jj = lax.broadcasted_iota(jnp.int32, (c, c), 1)
    causal = jj <= ii
    tri = causal.astype(F32)
    nt = (((1,), (1,)), ((), ()))
    tn = (((0,), (0,)), ((), ()))

    def chunk(n, carry):
        rows = pl.ds(pl.multiple_of(n * c, c), c)
        bcum = jnp.dot(tri, la_ref[rows, :], precision=lax.Precision.HIGHEST,
                       preferred_element_type=F32)
        b_mid = bcum[c // 2 - 1:c // 2, :]
        b_last = bcum[c - 1:c, :]
        q = q_ref[rows, :].astype(F32) * (GLA_DK ** -0.5)
        k = k_ref[rows, :].astype(F32)
        v = v_ref[rows, :]
        q_in = (q * jnp.exp(bcum - b_mid)).astype(BF16)
        k_in = (k * jnp.exp(b_mid - bcum)).astype(BF16)
        a = lax.dot_general(q_in, k_in, nt, preferred_element_type=F32)
        a = jnp.where(causal, a, 0.0).astype(BF16)
        o = jnp.dot(a, v, preferred_element_type=F32)
        s_t = st_ref[...]
        q_dec = (q * jnp.exp(bcum)).astype(BF16)
        o = o + lax.dot_general(q_dec, s_t.astype(BF16), nt, preferred_element_type=F32)
        k_dec = (k * jnp.exp(b_last - bcum)).astype(BF16)
        kv_t = lax.dot_general(v, k_dec, tn, preferred_element_type=F32)
        st_ref[...] = s_t * jnp.exp(b_last) + kv_t
        ms = jnp.mean(o * o, axis=-1, keepdims=True)
        o = o * lax.rsqrt(ms + RMS_EPS) * ng_ref[...]
        o = o * _silu(r_ref[rows, :].astype(F32))
        o_ref[rows, :] = o.astype(BF16)
        return carry

    lax.fori_loop(0, t // c, chunk, 0)


def _gla(proj, glr, wup, bgk, ng, batch, seq):
    m = proj.shape[0]
    return pl.pallas_call(
        _gla_kernel,
        grid=(batch, GLA_HEADS),
        in_specs=[
            pl.BlockSpec((seq, GLA_DK), lambda b, h: (b, OFF_GQ // GLA_DK + h)),
            pl.BlockSpec((seq, GLA_DK), lambda b, h: (b, OFF_GK // GLA_DK + h)),
            pl.BlockSpec((seq, GLA_DV), lambda b, h: (b, OFF_GV // GLA_DV + h)),
            pl.BlockSpec((seq, GLA_DV), lambda b, h: (b, OFF_GR // GLA_DV + h)),
            pl.BlockSpec((seq, LANES), lambda b, h: (b, 0)),
            pl.BlockSpec((LANES, GLA_DK), lambda b, h: (0, h)),
            pl.BlockSpec((1, GLA_DK), lambda b, h: (0, h)),
            pl.BlockSpec((1, GLA_DV), lambda b, h: (0, h)),
        ],
        out_specs=pl.BlockSpec((seq, GLA_DV), lambda b, h: (b, h)),
        out_shape=jax.ShapeDtypeStruct((m, GLA_V_W), BF16),
        scratch_shapes=[pltpu.VMEM((seq, GLA_DK), F32), pltpu.VMEM((GLA_DV, GLA_DK), F32)],
        compiler_params=pltpu.CompilerParams(
            dimension_semantics=("arbitrary", "arbitrary"), vmem_limit_bytes=VMEM_LIMIT),
        name="gla",
    )(proj, proj, proj, proj, glr, wup, bgk, ng)


def _swa_kernel(sink_ref, q_ref, kp_ref, kc_ref, vp_ref, vc_ref, o_ref):
    blk = SWA_BLOCK
    half = SWA_HEAD_DIM
    n = pl.program_id(1)
    k_all = jnp.concatenate([kp_ref[...], kc_ref[...]], axis=0)
    v_all = jnp.concatenate([vp_ref[...], vc_ref[...]], axis=0)
    qi = lax.broadcasted_iota(jnp.int32, (blk, 2 * blk), 0)
    kj = lax.broadcasted_iota(jnp.int32, (blk, 2 * blk), 1)
    rel = qi + blk - kj
    valid = (rel >= 0) & (rel < blk) & ((kj >= blk) | (n > 0))
    relf = rel.astype(F32)
    lane_kv = lax.broadcasted_iota(jnp.int32, (2 * blk, LANES), 1)
    lane_q = lax.broadcasted_iota(jnp.int32, (blk, LANES), 1)
    nt = (((1,), (1,)), ((), ()))

    def dup_half(slab, rolled, hh):
        lo = lane_kv < half
        if hh == 0:
            return jnp.where(lo, slab, rolled).astype(BF16)
        return jnp.where(lo, rolled, slab).astype(BF16)

    for p in range(SWA_KV_HEADS // 2):
        k_slab = k_all[:, p * LANES:(p + 1) * LANES].astype(F32)
        v_slab = v_all[:, p * LANES:(p + 1) * LANES].astype(F32)
        k_roll = pltpu.roll(k_slab, half, 1)
        v_roll = pltpu.roll(v_slab, half, 1)
        for hh in range(2):
            h = 2 * p + hh
            kd = dup_half(k_slab, k_roll, hh)
            vd = dup_half(v_slab, v_roll, hh)
            for gp in range(SWA_GROUP // 2):
                col = (h * (SWA_GROUP // 2) + gp) * LANES
                qs = q_ref[:, col:col + LANES]
                outs = []
                for gg in range(2):
                    head = h * SWA_GROUP + 2 * gp + gg
                    keep = (lane_q < half) if gg == 0 else (lane_q >= half)
                    qm = jnp.where(keep, qs, jnp.zeros_like(qs))
                    s = lax.dot_general(qm, kd, nt, preferred_element_type=F32) * (SWA_HEAD_DIM ** -0.5)
                    slope = 2.0 ** (-8.0 * (head + 1) / SWA_Q_HEADS)
                    s = s - slope * relf
                    s = jnp.where(valid, s, -jnp.inf)
                    sink = sink_ref[head]
                    mx = jnp.maximum(jnp.max(s, axis=-1, keepdims=True), sink)
                    pe = jnp.exp(s - mx)
                    den = jnp.sum(pe, axis=-1, keepdims=True) + jnp.exp(sink - mx)
                    o = jnp.dot(pe.astype(BF16), vd, preferred_element_type=F32)
                    outs.append(o / den)
                o_ref[:, col:col + LANES] = jnp.where(lane_q < half, outs[0], outs[1]).astype(BF16)


def _swa(proj, sinks, batch, seq):
    m = proj.shape[0]
    nb = seq // SWA_BLOCK
    qcol = OFF_SQ // SWA_Q_W
    kcol = OFF_SK // SWA_KV_W
    vcol = OFF_SV // SWA_KV_W
    cur = lambda c: (lambda b, n: (b * nb + n, c))
    prev = lambda c: (lambda b, n: (b * nb + jnp.maximum(n - 1, 0), c))
    return pl.pallas_call(
        _swa_kernel,
        grid=(batch, nb),
        in_specs=[
            pl.BlockSpec(memory_space=pltpu.SMEM),
            pl.BlockSpec((SWA_BLOCK, SWA_Q_W), cur(qcol)),
            pl.BlockSpec((SWA_BLOCK, SWA_KV_W), prev(kcol)),
            pl.BlockSpec((SWA_BLOCK, SWA_KV_W), cur(kcol)),
            pl.BlockSpec((SWA_BLOCK, SWA_KV_W), prev(vcol)),
            pl.BlockSpec((SWA_BLOCK, SWA_KV_W), cur(vcol)),
        ],
        out_specs=pl.BlockSpec((SWA_BLOCK, SWA_Q_W), lambda b, n: (b * nb + n, 0)),
        out_shape=jax.ShapeDtypeStruct((m, SWA_Q_W), BF16),
        compiler_params=pltpu.CompilerParams(
            dimension_semantics=("arbitrary", "arbitrary"), vmem_limit_bytes=VMEM_LIMIT),
        name="swa",
    )(sinks, proj, proj, proj, proj, proj)


def _out_route_kernel(og_ref, os_ref, x_ref, wo_ref, g_ref, wr_ref, br_ref,
                      h_ref, hn_ref, route_ref):
    h = x_ref[...]
    h = h + jnp.dot(og_ref[...], wo_ref[0:GLA_V_W, :], preferred_element_type=F32)
    h = h + jnp.dot(os_ref[...], wo_ref[GLA_V_W:, :], preferred_element_type=F32)
    h_ref[...] = h
    ms = jnp.mean(h * h, axis=-1, keepdims=True)
    hn = h * lax.rsqrt(ms + RMS_EPS) * g_ref[...]
    hn_ref[...] = hn
    logits = jnp.dot(hn.astype(BF16), wr_ref[...], preferred_element_type=F32) + br_ref[...]

    lane = lax.broadcasted_iota(jnp.int32, logits.shape, 1)
    lanef = lane.astype(F32)
    big = float(LANES)
    ninf = -jnp.inf
    gl = jnp.where(lane < N_GROUPS, logits, ninf)
    gmax = jnp.max(gl, axis=-1, keepdims=True)
    g_p = 1.0 / jnp.sum(jnp.exp(gl - gmax), axis=-1, keepdims=True)
    g_idx = jnp.min(jnp.where(gl == gmax, lanef, big), axis=-1, keepdims=True)
    lo = N_GROUPS + EXPERTS_PER_GROUP * g_idx
    el = jnp.where((lanef >= lo) & (lanef < lo + EXPERTS_PER_GROUP), logits, ninf)
    m1 = jnp.max(el, axis=-1, keepdims=True)
    i1 = jnp.min(jnp.where(el == m1, lanef, big), axis=-1, keepdims=True)
    el2 = jnp.where(lanef == i1, ninf, el)
    m2 = jnp.max(el2, axis=-1, keepdims=True)
    i2 = jnp.min(jnp.where(el2 == m2, lanef, big), axis=-1, keepdims=True)
    d = jnp.exp(m2 - m1)
    c1 = g_p / (1.0 + d)
    c2 = g_p * d / (1.0 + d)
    route = jnp.where(lane == 0, i1 - N_GROUPS,
                      jnp.where(lane == 1, i2 - N_GROUPS,
                                jnp.where(lane == 2, c1, jnp.where(lane == 3, c2, 0.0))))
    route_ref[...] = route


def _out_route(o_gla, o_swa, x2, w_out, g, w_rt, b_rt):
    m = x2.shape[0]
    tm = min(OUT_TM, m)
    row = lambda i: (i, 0)
    fixed = lambda i: (0, 0)
    return pl.pallas_call(
        _out_route_kernel,
        grid=(m // tm,),
        in_specs=[
            pl.BlockSpec((tm, GLA_V_W), row),
            pl.BlockSpec((tm, SWA_Q_W), row),
            pl.BlockSpec((tm, D_MODEL), row),
            pl.BlockSpec((GLA_V_W + SWA_Q_W, D_MODEL), fixed),
            pl.BlockSpec((1, D_MODEL), fixed),
            pl.BlockSpec((D_MODEL, LANES), fixed),
            pl.BlockSpec((1, LANES), fixed),
        ],
        out_specs=[
            pl.BlockSpec((tm, D_MODEL), row),
            pl.BlockSpec((tm, D_MODEL), row),
            pl.BlockSpec((tm, LANES), row),
        ],
        out_shape=[
            jax.ShapeDtypeStruct((m, D_MODEL), F32),
            jax.ShapeDtypeStruct((m, D_MODEL), F32),
            jax.ShapeDtypeStruct((m, LANES), F32),
        ],
        compiler_params=pltpu.CompilerParams(
            dimension_semantics=("arbitrary",), vmem_limit_bytes=VMEM_LIMIT),
        name="out_route",
    )(o_gla, o_swa, x2, w_out, g, w_rt, b_rt)


def _moe_kernel(te_ref, nt_ref, tok_ref, hn_hbm, wg_ref, wu_ref, wd_ref, y_ref,
                xbuf, sem, wgb, wub, wdb):
    tr = MOE_TR
    t = pl.program_id(0)
    nt = nt_ref[0]
    slot = lax.rem(t, 2)

    def issue(tile, dst_slot):
        base = tile * tr

        def body(r, carry):
            tok = tok_ref[base + r]
            pltpu.make_async_copy(hn_hbm.at[pl.ds(tok, 1)], xbuf.at[dst_slot, pl.ds(r, 1)],
                                  sem.at[dst_slot]).start()
            return carry

        lax.fori_loop(0, tr, body, 0, unroll=8)

    @pl.when(t == 0)
    def _():
        issue(0, 0)

    @pl.when(t + 1 < nt)
    def _():
        issue(t + 1, 1 - slot)

    @pl.when(t < nt)
    def _():
        changed = (t == 0) | (te_ref[t] != te_ref[jnp.maximum(t - 1, 0)])

        @pl.when(changed)
        def _():
            wgb[...] = wg_ref[...].astype(BF16)
            wub[...] = wu_ref[...].astype(BF16)
            wdb[...] = wd_ref[...].astype(BF16)

        pltpu.make_async_copy(hn_hbm.at[pl.ds(0, tr)], xbuf.at[slot], sem.at[slot]).wait()
        x = xbuf[slot].astype(BF16)
        g = jnp.dot(x, wgb[...], preferred_element_type=F32)
        u = jnp.dot(x, wub[...], preferred_element_type=F32)
        hmid = (_silu(g) * u).astype(BF16)
        y_ref[...] = jnp.dot(hmid, wdb[...], preferred_element_type=F32)

    @pl.when(t >= nt)
    def _():
        y_ref[...] = jnp.zeros_like(y_ref)


def _moe(hn, w_gate, w_up, w_down, tile_expert, num_tiles, tok_ids):
    p_rows = tok_ids.shape[0]
    n_tiles = p_rows // MOE_TR
    wsel = lambda t, te, nt, tok: (te[t], 0, 0)
    grid_spec = pltpu.PrefetchScalarGridSpec(
        num_scalar_prefetch=3,
        grid=(n_tiles,),
        in_specs=[
            pl.BlockSpec(memory_space=pl.ANY),
            pl.BlockSpec((None, D_MODEL, D_EXPERT), wsel),
            pl.BlockSpec((None, D_MODEL, D_EXPERT), wsel),
            pl.BlockSpec((None, D_EXPERT, D_MODEL), wsel),
        ],
        out_specs=pl.BlockSpec((MOE_TR, D_MODEL), lambda t, te, nt, tok: (t, 0)),
        scratch_shapes=[
            pltpu.VMEM((2, MOE_TR, D_MODEL), F32),
            pltpu.SemaphoreType.DMA((2,)),
            pltpu.VMEM((D_MODEL, D_EXPERT), BF16),
            pltpu.VMEM((D_MODEL, D_EXPERT), BF16),
            pltpu.VMEM((D_EXPERT, D_MODEL), BF16),
        ],
    )
    return pl.pallas_call(
        _moe_kernel,
        grid_spec=grid_spec,
        out_shape=jax.ShapeDtypeStruct((p_rows, D_MODEL), F32),
        compiler_params=pltpu.CompilerParams(
            dimension_semantics=("arbitrary",), vmem_limit_bytes=VMEM_LIMIT),
        name="moe",
    )(tile_expert, num_tiles, tok_ids, hn, w_gate, w_up, w_down)


def _combine_kernel(p1_ref, p2_ref, y_hbm, h_ref, route_ref, g_ref, o_ref, ybuf, sem):
    tm = h_ref.shape[0]
    t = pl.program_id(0)
    nsteps = pl.num_programs(0)
    slot = lax.rem(t, 2)

    def issue(tile, dst_slot):
        base = tile * tm

        def body(r, carry):
            pltpu.make_async_copy(y_hbm.at[pl.ds(p1_ref[base + r], 1)],
                                  ybuf.at[dst_slot, 0, pl.ds(r, 1)], sem.at[dst_slot, 0]).start()
            pltpu.make_async_copy(y_hbm.at[pl.ds(p2_ref[base + r], 1)],
                                  ybuf.at[dst_slot, 1, pl.ds(r, 1)], sem.at[dst_slot, 1]).start()
            return carry

        lax.fori_loop(0, tm, body, 0, unroll=8)

    @pl.when(t == 0)
    def _():
        issue(0, 0)

    @pl.when(t + 1 < nsteps)
    def _():
        issue(t + 1, 1 - slot)

    for kk in range(2):
        pltpu.make_async_copy(y_hbm.at[pl.ds(0, tm)], ybuf.at[slot, kk], sem.at[slot, kk]).wait()
    route = route_ref[...]
    h = h_ref[...] + route[:, 2:3] * ybuf[slot, 0] + route[:, 3:4] * ybuf[slot, 1]
    ms = jnp.mean(h * h, axis=-1, keepdims=True)
    o_ref[...] = h * lax.rsqrt(ms + RMS_EPS) * g_ref[...]


def _combine(y, h, route, g, pos1, pos2):
    m = h.shape[0]
    tm = min(CMB_TM, m)
    row = lambda i, p1, p2: (i, 0)
    grid_spec = pltpu.PrefetchScalarGridSpec(
        num_scalar_prefetch=2,
        grid=(m // tm,),
        in_specs=[
            pl.BlockSpec(memory_space=pl.ANY),
            pl.BlockSpec((tm, D_MODEL), row),
            pl.BlockSpec((tm, LANES), row),
            pl.BlockSpec((1, D_MODEL), lambda i, p1, p2: (0, 0)),
        ],
        out_specs=pl.BlockSpec((tm, D_MODEL), row),
        scratch_shapes=[
            pltpu.VMEM((2, 2, tm, D_MODEL), F32),
            pltpu.SemaphoreType.DMA((2, 2)),
        ],
    )
    return pl.pallas_call(
        _combine_kernel,
        grid_spec=grid_spec,
        out_shape=jax.ShapeDtypeStruct((m, D_MODEL), F32),
        compiler_params=pltpu.CompilerParams(
            dimension_semantics=("arbitrary",), vmem_limit_bytes=VMEM_LIMIT),
        name="combine",
    )(pos1, pos2, y, h, route, g)


def _dispatch_plan(route, m):
    tr = MOE_TR
    p_rows = 2 * m + N_EXPERTS * tr
    n_tiles = p_rows // tr
    e_flat = route[:, 0:2].astype(jnp.int32).reshape(-1)
    onehot = (e_flat[:, None] == jnp.arange(N_EXPERTS, dtype=jnp.int32)[None, :]).astype(jnp.int32)
    csum = jnp.cumsum(onehot, axis=0)
    rank = jnp.sum((csum - onehot) * onehot, axis=1)
    counts = csum[-1]
    padded = ((counts + tr - 1) // tr) * tr
    ends = jnp.cumsum(padded)
    starts = ends - padded
    pos = starts[e_flat] + rank
    num_tiles = (ends[-1] // tr).astype(jnp.int32)
    tile_start = jnp.minimum(jnp.arange(n_tiles, dtype=jnp.int32), num_tiles - 1) * tr
    tile_expert = jnp.sum((tile_start[:, None] >= ends[None, :]).astype(jnp.int32), axis=1)
    tok_ids = jnp.zeros((p_rows,), jnp.int32).at[pos].set(jnp.arange(2 * m, dtype=jnp.int32) // 2)
    pos2d = pos.reshape(m, 2)
    return tile_expert.astype(jnp.int32), num_tiles.reshape(1), tok_ids, pos2d[:, 0], pos2d[:, 1]


def kernel(x, norm_mix_g, w_in, w_gk_up, b_gk, gla_norm_g, swa_sinks, w_out, norm_ffn_g,
           w_group, b_group, w_router, b_router, w_gate, w_up, w_down, norm_final_g):
    batch, seq, d = x.shape
    m = batch * seq
    assert w_in.shape[0] == 1, "single-layer block"
    x2 = x.reshape(m, d)
    lr0 = OFF_GR + GLA_V_W
    w_main = jnp.concatenate([w_in[0][:, :lr0], w_in[0][:, lr0 + GLA_RANK:]], axis=1).astype(BF16)
    w_lr = jnp.pad(w_in[0][:, lr0:lr0 + GLA_RANK], ((0, 0), (0, LANES - GLA_RANK))).astype(BF16)
    wup = jnp.pad(w_gk_up[0], ((0, LANES - GLA_RANK), (0, 0))).astype(BF16)
    w_rt = jnp.pad(jnp.concatenate([w_group[0], w_router[0]], axis=1),
                   ((0, 0), (0, LANES - N_GROUPS - N_EXPERTS))).astype(BF16)
    b_rt = jnp.pad(jnp.concatenate([b_group[0], b_router[0]]),
                   (0, LANES - N_GROUPS - N_EXPERTS)).reshape(1, LANES)

    proj, glr = _in_proj(x2, norm_mix_g[0].reshape(1, d), w_main, w_lr)
    o_gla = _gla(proj, glr, wup, b_gk[0].reshape(1, GLA_QK_W),
                 gla_norm_g[0].reshape(1, GLA_V_W), batch, seq)
    o_swa = _swa(proj, swa_sinks[0], batch, seq)
    h_mid, hn, route = _out_route(o_gla, o_swa, x2, w_out[0].astype(BF16),
                                  norm_ffn_g[0].reshape(1, d), w_rt, b_rt)
    tile_expert, num_tiles, tok_ids, pos1, pos2 = _dispatch_plan(route, m)
    y = _moe(hn, w_gate[0], w_up[0], w_down[0], tile_expert, num_tiles, tok_ids)
    out = _combine(y, h_mid, route, norm_final_g.reshape(1, d), pos1, pos2)
    return out.reshape(batch, seq, d)
```

```python
import functools

import jax
import jax.numpy as jnp
from jax import lax
from jax.experimental import pallas as pl
from jax.experimental.pallas import tpu as pltpu

F32 = jnp.float32
BF16 = jnp.bfloat16

D_MODEL = 2048
GLA_HEADS = 4
GLA_DK = 128
GLA_DV = 256
GLA_RANK = 16
GLA_GATE_NORM = 16.0
GLA_CHUNK = 64
SWA_Q_HEADS = 16
SWA_KV_HEADS = 4
SWA_GROUP = SWA_Q_HEADS // SWA_KV_HEADS
SWA_HEAD_DIM = 64
SWA_BLOCK = 128
N_GROUPS = 4
EXPERTS_PER_GROUP = 16
N_EXPERTS = N_GROUPS * EXPERTS_PER_GROUP
D_EXPERT = 256
RMS_EPS = 1e-6

GLA_QK_W = GLA_HEADS * GLA_DK
GLA_V_W = GLA_HEADS * GLA_DV
SWA_Q_W = SWA_Q_HEADS * SWA_HEAD_DIM
SWA_KV_W = SWA_KV_HEADS * SWA_HEAD_DIM
OFF_GQ = 0
OFF_GK = OFF_GQ + GLA_QK_W
OFF_GV = OFF_GK + GLA_QK_W
OFF_GR = OFF_GV + GLA_V_W
OFF_SQ = OFF_GR + GLA_V_W
OFF_SK = OFF_SQ + SWA_Q_W
OFF_SV = OFF_SK + SWA_KV_W
PROJ_W = OFF_SV + SWA_KV_W
LANES = 128

IN_TM = 1024
IN_TN = 768
OUT_TM = 256
MOE_TR = 256
CMB_TM = 256
VMEM_LIMIT = 56 * 1024 * 1024


def _silu(x):
    return x / (1.0 + jnp.exp(-x))


def _inproj_kernel(x_ref, g_ref, w_ref, wlr_ref, proj_ref, glr_ref, xn_ref):
    @pl.when(pl.program_id(1) == 0)
    def _():
        x = x_ref[...]
        ms = jnp.mean(x * x, axis=-1, keepdims=True)
        xn = (x * lax.rsqrt(ms + RMS_EPS) * g_ref[...]).astype(BF16)
        xn_ref[...] = xn
        glr_ref[...] = jnp.dot(xn, wlr_ref[...], preferred_element_type=F32)

    proj_ref[...] = jnp.dot(xn_ref[...], w_ref[...], preferred_element_type=F32).astype(BF16)


def _in_proj(x2, g, w_main, w_lr):
    m = x2.shape[0]
    tm = min(IN_TM, m)
    return pl.pallas_call(
        _inproj_kernel,
        grid=(m // tm, PROJ_W // IN_TN),
        in_specs=[
            pl.BlockSpec((tm, D_MODEL), lambda i, j: (i, 0)),
            pl.BlockSpec((1, D_MODEL), lambda i, j: (0, 0)),
            pl.BlockSpec((D_MODEL, IN_TN), lambda i, j: (0, j)),
            pl.BlockSpec((D_MODEL, LANES), lambda i, j: (0, 0)),
        ],
        out_specs=[
            pl.BlockSpec((tm, IN_TN), lambda i, j: (i, j)),
            pl.BlockSpec((tm, LANES), lambda i, j: (i, 0)),
        ],
        out_shape=[
            jax.ShapeDtypeStruct((m, PROJ_W), BF16),
            jax.ShapeDtypeStruct((m, LANES), F32),
        ],
        scratch_shapes=[pltpu.VMEM((tm, D_MODEL), BF16)],
        compiler_params=pltpu.CompilerParams(
            dimension_semantics=("arbitrary", "arbitrary"), vmem_limit_bytes=VMEM_LIMIT),
        name="in_proj",
    )(x2, g, w_main, w_lr)


def _gla_kernel(q_ref, k_ref, v_ref, r_ref, glr_ref, wup_ref, bgk_ref, ng_ref, o_ref,
                la_ref, oi_ref, qi_ref, ki_ref, qd_ref, kd_ref, a_ref, kv_ref, dec_ref, sp_ref):
    t = q_ref.shape[0]
    c = GLA_CHUNK
    nchunk = t // c
    z = jnp.dot(glr_ref[...].astype(BF16), wup_ref[...], preferred_element_type=F32) + bgk_ref[...]
    la_ref[...] = (jnp.minimum(z, 0.0) - jnp.log1p(jnp.exp(-jnp.abs(z)))) * (1.0 / GLA_GATE_NORM)

    ii = lax.broadcasted_iota(jnp.int32, (c, c), 0)
    jj = lax.broadcasted_iota(jnp.int32, (c, c), 1)
    causal = jj <= ii
    tri = causal.astype(BF16)
    nt = (((1,), (1,)), ((), ()))
    tn = (((0,), (0,)), ((), ()))

    def chunk_rows(n):
        return pl.ds(pl.multiple_of(n * c, c), c)

    def decays(n, carry):
        rows = chunk_rows(n)
        la = la_ref[rows, :]
        hi = la.astype(BF16)
        r1 = la - hi.astype(F32)
        mid = r1.astype(BF16)
        lo = (r1 - mid.astype(F32)).astype(BF16)
        parts = jnp.dot(tri, jnp.concatenate([hi, mid, lo], axis=1), preferred_element_type=F32)
        bcum = parts[:, 0:GLA_DK] + parts[:, GLA_DK:2 * GLA_DK] + parts[:, 2 * GLA_DK:]
        b_mid = bcum[c // 2 - 1:c // 2, :]
        b_last = bcum[c - 1:c, :]
        q = q_ref[rows, :].astype(F32) * (GLA_DK ** -0.5)
        k = k_ref[rows, :].astype(F32)
        qi_ref[rows, :] = (q * jnp.exp(bcum - b_mid)).astype(BF16)
        ki_ref[rows, :] = (k * jnp.exp(b_mid - bcum)).astype(BF16)
        qd_ref[rows, :] = (q * jnp.exp(bcum)).astype(BF16)
        kd_ref[rows, :] = (k * jnp.exp(b_last - bcum)).astype(BF16)
        dec_ref[n] = jnp.broadcast_to(jnp.exp(b_last), dec_ref.shape[1:])
        return carry

    lax.fori_loop(0, nchunk, decays, 0, unroll=4)

    def scores(n, carry):
        rows = chunk_rows(n)
        a = lax.dot_general(qi_ref[rows, :], ki_ref[rows, :], nt, preferred_element_type=F32)
        a_ref[rows, :] = jnp.where(causal, a, 0.0).astype(BF16)
        return carry

    lax.fori_loop(0, nchunk, scores, 0, unroll=8)

    def intra(n, carry):
        rows = chunk_rows(n)
        v = v_ref[rows, :]
        oi_ref[rows, :] = jnp.dot(a_ref[rows, :], v, preferred_element_type=F32)
        kv_ref[n] = lax.dot_general(v, kd_ref[rows, :], tn, preferred_element_type=F32)
        return carry

    lax.fori_loop(0, nchunk, intra, 0, unroll=4)

    def scan(n, s_t):
        sp_ref[n] = s_t.astype(BF16)
        return s_t * dec_ref[n][0:1, :] + kv_ref[n]

    lax.fori_loop(0, nchunk, scan, jnp.zeros((GLA_DV, GLA_DK), F32))

    def inter(n, carry):
        rows = chunk_rows(n)
        o = oi_ref[rows, :] + lax.dot_general(qd_ref[rows, :], sp_ref[n], nt,
                                              preferred_element_type=F32)
        ms = jnp.mean(o * o, axis=-1, keepdims=True)
        o = o * lax.rsqrt(ms + RMS_EPS) * ng_ref[...]
        o = o * _silu(r_ref[rows, :].astype(F32))
        o_ref[rows, :] = o.astype(BF16)
        return carry

    lax.fori_loop(0, nchunk, inter, 0, unroll=4)


def _gla(proj, glr, wup, bgk, ng, batch, seq):
    m = proj.shape[0]
    return pl.pallas_call(
        _gla_kernel,
        grid=(batch, GLA_HEADS),
        in_specs=[
            pl.BlockSpec((seq, GLA_DK), lambda b, h: (b, OFF_GQ // GLA_DK + h)),
            pl.BlockSpec((seq, GLA_DK), lambda b, h: (b, OFF_GK // GLA_DK + h)),
            pl.BlockSpec((seq, GLA_DV), lambda b, h: (b, OFF_GV // GLA_DV + h)),
            pl.BlockSpec((seq, GLA_DV), lambda b, h: (b, OFF_GR // GLA_DV + h)),
            pl.BlockSpec((seq, LANES), lambda b, h: (b, 0)),
            pl.BlockSpec((LANES, GLA_DK), lambda b, h: (0, h)),
            pl.BlockSpec((1, GLA_DK), lambda b, h: (0, h)),
            pl.BlockSpec((1, GLA_DV), lambda b, h: (0, h)),
        ],
        out_specs=pl.BlockSpec((seq, GLA_DV), lambda b, h: (b, h)),
        out_shape=jax.ShapeDtypeStruct((m, GLA_V_W), BF16),
        scratch_shapes=[
            pltpu.VMEM((seq, GLA_DK), F32),
            pltpu.VMEM((seq, GLA_DV), F32),
            pltpu.VMEM((seq, GLA_DK), BF16),
            pltpu.VMEM((seq, GLA_DK), BF16),
            pltpu.VMEM((seq, GLA_DK), BF16),
            pltpu.VMEM((seq, GLA_DK), BF16),
            pltpu.VMEM((seq, GLA_CHUNK), BF16),
            pltpu.VMEM((seq // GLA_CHUNK, GLA_DV, GLA_DK), F32),
            pltpu.VMEM((seq // GLA_CHUNK, 8, GLA_DK), F32),
            pltpu.VMEM((seq // GLA_CHUNK, GLA_DV, GLA_DK), BF16),
        ],
        compiler_params=pltpu.CompilerParams(
            dimension_semantics=("arbitrary", "arbitrary"), vmem_limit_bytes=VMEM_LIMIT),
        name="gla",
    )(proj, proj, proj, proj, glr, wup, bgk, ng)


def _swa_kernel(sink_ref, q_ref, kp_ref, kc_ref, vp_ref, vc_ref, o_ref):
    blk = SWA_BLOCK
    half = SWA_HEAD_DIM
    n = pl.program_id(1)
    k_all = jnp.concatenate([kp_ref[...], kc_ref[...]], axis=0)
    v_all = jnp.concatenate([vp_ref[...], vc_ref[...]], axis=0)
    qi = lax.broadcasted_iota(jnp.int32, (blk, 2 * blk), 0)
    kj = lax.broadcasted_iota(jnp.int32, (blk, 2 * blk), 1)
    rel = qi + blk - kj
    valid = (rel >= 0) & (rel < blk) & ((kj >= blk) | (n > 0))
    relf = rel.astype(F32)
    lane_kv = lax.broadcasted_iota(jnp.int32, (2 * blk, LANES), 1)
    lane_q = lax.broadcasted_iota(jnp.int32, (blk, LANES), 1)
    nt = (((1,), (1,)), ((), ()))

    def dup_half(slab, rolled, hh):
        lo = lane_kv < half
        if hh == 0:
            return jnp.where(lo, slab, rolled).astype(BF16)
        return jnp.where(lo, rolled, slab).astype(BF16)

    for p in range(SWA_KV_HEADS // 2):
        k_slab = k_all[:, p * LANES:(p + 1) * LANES].astype(F32)
        v_slab = v_all[:, p * LANES:(p + 1) * LANES].astype(F32)
        k_roll = pltpu.roll(k_slab, half, 1)
        v_roll = pltpu.roll(v_slab, half, 1)
        for hh in range(2):
            h = 2 * p + hh
            kd = dup_half(k_slab, k_roll, hh)
            vd = dup_half(v_slab, v_roll, hh)
            for gp in range(SWA_GROUP // 2):
                col = (h * (SWA_GROUP // 2) + gp) * LANES
                qs = q_ref[:, col:col + LANES]
                outs = []
                for gg in range(2):
                    head = h * SWA_GROUP + 2 * gp + gg
                    keep = (lane_q < half) if gg == 0 else (lane_q >= half)
                    qm = jnp.where(keep, qs, jnp.zeros_like(qs))
                    s = lax.dot_general(qm, kd, nt, preferred_element_type=F32) * (SWA_HEAD_DIM ** -0.5)
                    slope = 2.0 ** (-8.0 * (head + 1) / SWA_Q_HEADS)
                    s = s - slope * relf
                    s = jnp.where(valid, s, -jnp.inf)
                    sink = sink_ref[head]
                    mx = jnp.maximum(jnp.max(s, axis=-1, keepdims=True), sink)
                    pe = jnp.exp(s - mx)
                    den = jnp.sum(pe, axis=-1, keepdims=True) + jnp.exp(sink - mx)
                    o = jnp.dot(pe.astype(BF16), vd, preferred_element_type=F32)
                    outs.append(o / den)
                o_ref[:, col:col + LANES] = jnp.where(lane_q < half, outs[0], outs[1]).astype(BF16)


def _swa(proj, sinks, batch, seq):
    m = proj.shape[0]
    nb = seq // SWA_BLOCK
    qcol = OFF_SQ // SWA_Q_W
    kcol = OFF_SK // SWA_KV_W
    vcol = OFF_SV // SWA_KV_W
    cur = lambda c: (lambda b, n: (b * nb + n, c))
    prev = lambda c: (lambda b, n: (b * nb + jnp.maximum(n - 1, 0), c))
    return pl.pallas_call(
        _swa_kernel,
        grid=(batch, nb),
        in_specs=[
            pl.BlockSpec(memory_space=pltpu.SMEM),
            pl.BlockSpec((SWA_BLOCK, SWA_Q_W), cur(qcol)),
            pl.BlockSpec((SWA_BLOCK, SWA_KV_W), prev(kcol)),
            pl.BlockSpec((SWA_BLOCK, SWA_KV_W), cur(kcol)),
            pl.BlockSpec((SWA_BLOCK, SWA_KV_W), prev(vcol)),
            pl.BlockSpec((SWA_BLOCK, SWA_KV_W), cur(vcol)),
        ],
        out_specs=pl.BlockSpec((SWA_BLOCK, SWA_Q_W), lambda b, n: (b * nb + n, 0)),
        out_shape=jax.ShapeDtypeStruct((m, SWA_Q_W), BF16),
        compiler_params=pltpu.CompilerParams(
            dimension_semantics=("arbitrary", "arbitrary"), vmem_limit_bytes=VMEM_LIMIT),
        name="swa",
    )(sinks, proj, proj, proj, proj, proj)


def _out_route_kernel(og_ref, os_ref, x_ref, wo_ref, g_ref, wr_ref, br_ref,
                      h_ref, hn_ref, route_ref):
    h = x_ref[...]
    h = h + jnp.dot(og_ref[...], wo_ref[0:GLA_V_W, :], preferred_element_type=F32)
    h = h + jnp.dot(os_ref[...], wo_ref[GLA_V_W:, :], preferred_element_type=F32)
    h_ref[...] = h
    ms = jnp.mean(h * h, axis=-1, keepdims=True)
    hn = h * lax.rsqrt(ms + RMS_EPS) * g_ref[...]
    hn_ref[...] = hn
    logits = jnp.dot(hn.astype(BF16), wr_ref[...], preferred_element_type=F32) + br_ref[...]

    lane = lax.broadcasted_iota(jnp.int32, logits.shape, 1)
    lanef = lane.astype(F32)
    big = float(LANES)
    ninf = -jnp.inf
    gl = jnp.where(lane < N_GROUPS, logits, ninf)
    gmax = jnp.max(gl, axis=-1, keepdims=True)
    g_p = 1.0 / jnp.sum(jnp.exp(gl - gmax), axis=-1, keepdims=True)
    g_idx = jnp.min(jnp.where(gl == gmax, lanef, big), axis=-1, keepdims=True)
    lo = N_GROUPS + EXPERTS_PER_GROUP * g_idx
    el = jnp.where((lanef >= lo) & (lanef < lo + EXPERTS_PER_GROUP), logits, ninf)
    m1 = jnp.max(el, axis=-1, keepdims=True)
    i1 = jnp.min(jnp.where(el == m1, lanef, big), axis=-1, keepdims=True)
    el2 = jnp.where(lanef == i1, ninf, el)
    m2 = jnp.max(el2, axis=-1, keepdims=True)
    i2 = jnp.min(jnp.where(el2 == m2, lanef, big), axis=-1, keepdims=True)
    d = jnp.exp(m2 - m1)
    c1 = g_p / (1.0 + d)
    c2 = g_p * d / (1.0 + d)
    route = jnp.where(lane == 0, i1 - N_GROUPS,
                      jnp.where(lane == 1, i2 - N_GROUPS,
                                jnp.where(lane == 2, c1, jnp.where(lane == 3, c2, 0.0))))
    route_ref[...] = route


def _out_route(o_gla, o_swa, x2, w_out, g, w_rt, b_rt):
    m = x2.shape[0]
    tm = min(OUT_TM, m)
    row = lambda i: (i, 0)
    fixed = lambda i: (0, 0)
    return pl.pallas_call(
        _out_route_kernel,
        grid=(m // tm,),
        in_specs=[
            pl.BlockSpec((tm, GLA_V_W), row),
            pl.BlockSpec((tm, SWA_Q_W), row),
            pl.BlockSpec((tm, D_MODEL), row),
            pl.BlockSpec((GLA_V_W + SWA_Q_W, D_MODEL), fixed),
            pl.BlockSpec((1, D_MODEL), fixed),
            pl.BlockSpec((D_MODEL, LANES), fixed),
            pl.BlockSpec((1, LANES), fixed),
        ],
        out_specs=[
            pl.BlockSpec((tm, D_MODEL), row),
            pl.BlockSpec((tm, D_MODEL), row),
            pl.BlockSpec((tm, LANES), row),
        ],
        out_shape=[
            jax.ShapeDtypeStruct((m, D_MODEL), F32),
            jax.ShapeDtypeStruct((m, D_MODEL), F32),
            jax.ShapeDtypeStruct((m, LANES), F32),
        ],
        compiler_params=pltpu.CompilerParams(
            dimension_semantics=("arbitrary",), vmem_limit_bytes=VMEM_LIMIT),
        name="out_route",
    )(o_gla, o_swa, x2, w_out, g, w_rt, b_rt)


def _moe_kernel(te_ref, nt_ref, ids_hbm, hn_hbm, wg_ref, wu_ref, wd_ref, y_ref,
                xbuf, gsem, ids_smem, isem, wgb, wub, wdb):
    tr = MOE_TR
    t = pl.program_id(0)
    nt = nt_ref[0]

    def ids_copy(tile, s):
        return pltpu.make_async_copy(ids_hbm.at[tile], ids_smem.at[s], isem.at[s])

    @pl.when(t == 0)
    def _():
        ids_copy(0, 0).start()

    @pl.when(t < nt)
    def _():
        s = lax.rem(t, 2)
        ids_copy(t, s).wait()

        @pl.when(t + 1 < nt)
        def _():
            ids_copy(t + 1, 1 - s).start()

        for r in range(tr):
            tok = ids_smem[s, r // LANES, r % LANES]
            pltpu.make_async_copy(hn_hbm.at[pl.ds(tok, 1)], xbuf.at[s, pl.ds(r, 1)],
                                  gsem.at[s]).start(priority=r % 2)

    @pl.when((t >= 1) & (t <= nt))
    def _():
        c = t - 1
        s = lax.rem(c, 2)
        changed = (c == 0) | (te_ref[c] != te_ref[jnp.maximum(c - 1, 0)])

        @pl.when(changed)
        def _():
            wgb[...] = wg_ref[...].astype(BF16)
            wub[...] = wu_ref[...].astype(BF16)
            wdb[...] = wd_ref[...].astype(BF16)

        pltpu.make_async_copy(hn_hbm.at[pl.ds(0, tr)], xbuf.at[s], gsem.at[s]).wait()
        x = xbuf[s].astype(BF16)
        g = jnp.dot(x, wgb[...], preferred_element_type=F32)
        u = jnp.dot(x, wub[...], preferred_element_type=F32)
        hmid = (_silu(g) * u).astype(BF16)
        y_ref[...] = jnp.dot(hmid, wdb[...], preferred_element_type=F32)

    @pl.when(t > nt)
    def _():
        y_ref[...] = jnp.zeros_like(y_ref)


def _moe(hn, w_gate, w_up, w_down, tile_expert, num_tiles, tok_ids):
    p_rows = tok_ids.shape[0]
    n_tiles = p_rows // MOE_TR
    ids3 = tok_ids.reshape(n_tiles, MOE_TR // LANES, LANES)
    prev = lambda t: jnp.maximum(t - 1, 0)
    wsel = lambda t, te, nt: (te[prev(t)], 0, 0)
    grid_spec = pltpu.PrefetchScalarGridSpec(
        num_scalar_prefetch=2,
        grid=(n_tiles + 1,),
        in_specs=[
            pl.BlockSpec(memory_space=pl.ANY),
            pl.BlockSpec(memory_space=pl.ANY),
            pl.BlockSpec((None, D_MODEL, D_EXPERT), wsel),
            pl.BlockSpec((None, D_MODEL, D_EXPERT), wsel),
            pl.BlockSpec((None, D_EXPERT, D_MODEL), wsel),
        ],
        out_specs=pl.BlockSpec((MOE_TR, D_MODEL), lambda t, te, nt: (prev(t), 0)),
        scratch_shapes=[
            pltpu.VMEM((2, MOE_TR, D_MODEL), F32),
            pltpu.SemaphoreType.DMA((2,)),
            pltpu.SMEM((2, MOE_TR // LANES, LANES), jnp.int32),
            pltpu.SemaphoreType.DMA((2,)),
            pltpu.VMEM((D_MODEL, D_EXPERT), BF16),
            pltpu.VMEM((D_MODEL, D_EXPERT), BF16),
            pltpu.VMEM((D_EXPERT, D_MODEL), BF16),
        ],
    )
    return pl.pallas_call(
        _moe_kernel,
        grid_spec=grid_spec,
        out_shape=jax.ShapeDtypeStruct((p_rows, D_MODEL), F32),
        compiler_params=pltpu.CompilerParams(
            dimension_semantics=("arbitrary",), vmem_limit_bytes=VMEM_LIMIT),
        name="moe",
    )(tile_expert, num_tiles, ids3, hn, w_gate, w_up, w_down)


def _combine_kernel(pos_hbm, y_hbm, h_ref, route_ref, g_ref, o_ref, ybuf, gsem, ids_smem, isem):
    tm = h_ref.shape[0]
    t = pl.program_id(0)
    ntile = pl.num_programs(0) - 1

    def ids_copy(tile, s):
        return pltpu.make_async_copy(pos_hbm.at[tile], ids_smem.at[s], isem.at[s])

    @pl.when(t == 0)
    def _():
        ids_copy(0, 0).start()

    @pl.when(t < ntile)
    def _():
        s = lax.rem(t, 2)
        ids_copy(t, s).wait()

        @pl.when(t + 1 < ntile)
        def _():
            ids_copy(t + 1, 1 - s).start()

        for kk in range(2):
            for r in range(tm):
                flat = kk * tm + r
                row = ids_smem[s, flat // LANES, flat % LANES]
                pltpu.make_async_copy(y_hbm.at[pl.ds(row, 1)], ybuf.at[s, kk, pl.ds(r, 1)],
                                      gsem.at[s, kk]).start(priority=r % 2)

    @pl.when(t >= 1)
    def _():
        s = lax.rem(t - 1, 2)
        for kk in range(2):
            pltpu.make_async_copy(y_hbm.at[pl.ds(0, tm)], ybuf.at[s, kk], gsem.at[s, kk]).wait()
        route = route_ref[...]
        h = h_ref[...] + route[:, 2:3] * ybuf[s, 0] + route[:, 3:4] * ybuf[s, 1]
        ms = jnp.mean(h * h, axis=-1, keepdims=True)
        o_ref[...] = h * lax.rsqrt(ms + RMS_EPS) * g_ref[...]


def _combine(y, h, route, g, pos2d):
    m = h.shape[0]
    tm = min(CMB_TM, m)
    ntile = m // tm
    pos3 = pos2d.reshape(ntile, tm, 2).transpose(0, 2, 1).reshape(ntile, 2 * tm // LANES, LANES)
    row = lambda i: (jnp.maximum(i - 1, 0), 0)
    return pl.pallas_call(
        _combine_kernel,
        grid=(ntile + 1,),
        in_specs=[
            pl.BlockSpec(memory_space=pl.ANY),
            pl.BlockSpec(memory_space=pl.ANY),
            pl.BlockSpec((tm, D_MODEL), row),
            pl.BlockSpec((tm, LANES), row),
            pl.BlockSpec((1, D_MODEL), lambda i: (0, 0)),
        ],
        out_specs=pl.BlockSpec((tm, D_MODEL), row),
        out_shape=jax.ShapeDtypeStruct((m, D_MODEL), F32),
        scratch_shapes=[
            pltpu.VMEM((2, 2, tm, D_MODEL), F32),
            pltpu.SemaphoreType.DMA((2, 2)),
            pltpu.SMEM((2, 2 * tm // LANES, LANES), jnp.int32),
            pltpu.SemaphoreType.DMA((2,)),
        ],
        compiler_params=pltpu.CompilerParams(
            dimension_semantics=("arbitrary",), vmem_limit_bytes=VMEM_LIMIT),
        name="combine",
    )(pos3, y, h, route, g)


def _dispatch_plan(route, m):
    tr = MOE_TR
    p_rows = 2 * m + N_EXPERTS * tr
    n_tiles = p_rows // tr
    e_flat = route[:, 0:2].astype(jnp.int32).reshape(-1)
    onehot = (e_flat[:, None] == jnp.arange(N_EXPERTS, dtype=jnp.int32)[None, :]).astype(jnp.int32)
    csum = jnp.cumsum(onehot, axis=0)
    rank = jnp.sum((csum - onehot) * onehot, axis=1)
    counts = csum[-1]
    padded = ((counts + tr - 1) // tr) * tr
    ends = jnp.cumsum(padded)
    starts = ends - padded
    pos = starts[e_flat] + rank
    num_tiles = (ends[-1] // tr).astype(jnp.int32)
    tile_start = jnp.minimum(jnp.arange(n_tiles, dtype=jnp.int32), num_tiles - 1) * tr
    tile_expert = jnp.sum((tile_start[:, None] >= ends[None, :]).astype(jnp.int32), axis=1)
    tok_ids = jnp.zeros((p_rows,), jnp.int32).at[pos].set(
        jnp.arange(2 * m, dtype=jnp.int32) // 2, unique_indices=True)
    return tile_expert.astype(jnp.int32), num_tiles.reshape(1), tok_ids, pos.reshape(m, 2)


def kernel(x, norm_mix_g, w_in, w_gk_up, b_gk, gla_norm_g, swa_sinks, w_out, norm_ffn_g,
           w_group, b_group, w_router, b_router, w_gate, w_up, w_down, norm_final_g):
    batch, seq, d = x.shape
    m = batch * seq
    assert w_in.shape[0] == 1, "single-layer block"
    x2 = x.reshape(m, d)
    lr0 = OFF_GR + GLA_V_W
    w_main = jnp.concatenate([w_in[0][:, :lr0], w_in[0][:, lr0 + GLA_RANK:]], axis=1).astype(BF16)
    w_lr = jnp.pad(w_in[0][:, lr0:lr0 + GLA_RANK], ((0, 0), (0, LANES - GLA_RANK))).astype(BF16)
    wup = jnp.pad(w_gk_up[0], ((0, LANES - GLA_RANK), (0, 0))).astype(BF16)
    w_rt = jnp.pad(jnp.concatenate([w_group[0], w_router[0]], axis=1),
                   ((0, 0), (0, LANES - N_GROUPS - N_EXPERTS))).astype(BF16)
    b_rt = jnp.pad(jnp.concatenate([b_group[0], b_router[0]]),
                   (0, LANES - N_GROUPS - N_EXPERTS)).reshape(1, LANES)

    proj, glr = _in_proj(x2, norm_mix_g[0].reshape(1, d), w_main, w_lr)
    o_gla = _gla(proj, glr, wup, b_gk[0].reshape(1, GLA_QK_W),
                 gla_norm_g[0].reshape(1, GLA_V_W), batch, seq)
    o_swa = _swa(proj, swa_sinks[0], batch, seq)
    h_mid, hn, route = _out_route(o_gla, o_swa, x2, w_out[0].astype(BF16),
                                  norm_ffn_g[0].reshape(1, d), w_rt, b_rt)
    tile_expert, num_tiles, tok_ids, pos2d = _dispatch_plan(route, m)
    y = _moe(hn, w_gate[0], w_up[0], w_down[0], tile_expert, num_tiles, tok_ids)
    out = _combine(y, h_mid, route, norm_final_g.reshape(1, d), pos2d)
    return out.reshape(batch, seq, d)
```

```python
import functools

import jax
import jax.numpy as jnp
from jax import lax
from jax.experimental import pallas as pl
from jax.experimental.pallas import tpu as pltpu

F32 = jnp.float32
BF16 = jnp.bfloat16

D_MODEL = 2048
GLA_HEADS = 4
GLA_DK = 128
GLA_DV = 256
GLA_RANK = 16
GLA_GATE_NORM = 16.0
GLA_CHUNK = 64
SWA_Q_HEADS = 16
SWA_KV_HEADS = 4
SWA_GROUP = SWA_Q_HEADS // SWA_KV_HEADS
SWA_HEAD_DIM = 64
SWA_BLOCK = 128
N_GROUPS = 4
EXPERTS_PER_GROUP = 16
N_EXPERTS = N_GROUPS * EXPERTS_PER_GROUP
D_EXPERT = 256
RMS_EPS = 1e-6

GLA_QK_W = GLA_HEADS * GLA_DK
GLA_V_W = GLA_HEADS * GLA_DV
SWA_Q_W = SWA_Q_HEADS * SWA_HEAD_DIM
SWA_KV_W = SWA_KV_HEADS * SWA_HEAD_DIM
OFF_GQ = 0
OFF_GK = OFF_GQ + GLA_QK_W
OFF_GV = OFF_GK + GLA_QK_W
OFF_GR = OFF_GV + GLA_V_W
OFF_SQ = OFF_GR + GLA_V_W
OFF_SK = OFF_SQ + SWA_Q_W
OFF_SV = OFF_SK + SWA_KV_W
PROJ_W = OFF_SV + SWA_KV_W
LANES = 128

IN_TM = 1024
IN_TN = 768
OUT_TM = 256
MOE_TR = 256
CMB_TM = 256
VMEM_LIMIT = 56 * 1024 * 1024


def _silu(x):
    return x / (1.0 + jnp.exp(-x))


def _inproj_kernel(x_ref, g_ref, w_ref, wlr_ref, proj_ref, glr_ref, xn_ref):
    @pl.when(pl.program_id(1) == 0)
    def _():
        x = x_ref[...]
        ms = jnp.mean(x * x, axis=-1, keepdims=True)
        xn = (x * lax.rsqrt(ms + RMS_EPS) * g_ref[...]).astype(BF16)
        xn_ref[...] = xn
        glr_ref[...] = jnp.dot(xn, wlr_ref[...], preferred_element_type=F32)

    proj_ref[...] = jnp.dot(xn_ref[...], w_ref[...], preferred_element_type=F32).astype(BF16)


def _in_proj(x2, g, w_main, w_lr):
    m = x2.shape[0]
    tm = min(IN_TM, m)
    return pl.pallas_call(
        _inproj_kernel,
        grid=(m // tm, PROJ_W // IN_TN),
        in_specs=[
            pl.BlockSpec((tm, D_MODEL), lambda i, j: (i, 0)),
            pl.BlockSpec((1, D_MODEL), lambda i, j: (0, 0)),
            pl.BlockSpec((D_MODEL, IN_TN), lambda i, j: (0, j)),
            pl.BlockSpec((D_MODEL, LANES), lambda i, j: (0, 0)),
        ],
        out_specs=[
            pl.BlockSpec((tm, IN_TN), lambda i, j: (i, j)),
            pl.BlockSpec((tm, LANES), lambda i, j: (i, 0)),
        ],
        out_shape=[
            jax.ShapeDtypeStruct((m, PROJ_W), BF16),
            jax.ShapeDtypeStruct((m, LANES), F32),
        ],
        scratch_shapes=[pltpu.VMEM((tm, D_MODEL), BF16)],
        compiler_params=pltpu.CompilerParams(
            dimension_semantics=("arbitrary", "arbitrary"), vmem_limit_bytes=VMEM_LIMIT),
        name="in_proj",
    )(x2, g, w_main, w_lr)


def _gla_kernel(q_ref, k_ref, v_ref, r_ref, glr_ref, wup_ref, bgk_ref, ng_ref, o_ref,
                la_ref, oi_ref, qi_ref, ki_ref, qd_ref, kd_ref, a_ref, kv_ref, dec_ref, sp_ref):
    t = q_ref.shape[0]
    c = GLA_CHUNK
    nchunk = t // c
    z = jnp.dot(glr_ref[...].astype(BF16), wup_ref[...], preferred_element_type=F32) + bgk_ref[...]
    la_ref[...] = (jnp.minimum(z, 0.0) - jnp.log1p(jnp.exp(-jnp.abs(z)))) * (1.0 / GLA_GATE_NORM)

    ii = lax.broadcasted_iota(jnp.int32, (c, c), 0)
    jj = lax.broadcasted_iota(jnp.int32, (c, c), 1)
    causal = jj <= ii
    tri = causal.astype(BF16)
    nt = (((1,), (1,)), ((), ()))
    tn = (((0,), (0,)), ((), ()))

    def chunk_rows(n):
        return pl.ds(pl.multiple_of(n * c, c), c)

    def decays(n, carry):
        rows = chunk_rows(n)
        la = la_ref[rows, :]
        hi = la.astype(BF16)
        r1 = la - hi.astype(F32)
        mid = r1.astype(BF16)
        lo = (r1 - mid.astype(F32)).astype(BF16)
        parts = jnp.dot(tri, jnp.concatenate([hi, mid, lo], axis=1), preferred_element_type=F32)
        bcum = parts[:, 0:GLA_DK] + parts[:, GLA_DK:2 * GLA_DK] + parts[:, 2 * GLA_DK:]
        b_mid = bcum[c // 2 - 1:c // 2, :]
        b_last = bcum[c - 1:c, :]
        q = q_ref[rows, :].astype(F32) * (GLA_DK ** -0.5)
        k = k_ref[rows, :].astype(F32)
        qi_ref[rows, :] = (q * jnp.exp(bcum - b_mid)).astype(BF16)
        ki_ref[rows, :] = (k * jnp.exp(b_mid - bcum)).astype(BF16)
        qd_ref[rows, :] = (q * jnp.exp(bcum)).astype(BF16)
        kd_ref[rows, :] = (k * jnp.exp(b_last - bcum)).astype(BF16)
        dec_ref[n] = jnp.broadcast_to(jnp.exp(b_last), dec_ref.shape[1:])
        return carry

    lax.fori_loop(0, nchunk, decays, 0, unroll=4)

    def scores(n, carry):
        rows = chunk_rows(n)
        a = lax.dot_general(qi_ref[rows, :], ki_ref[rows, :], nt, preferred_element_type=F32)
        a_ref[rows, :] = jnp.where(causal, a, 0.0).astype(BF16)
        return carry

    lax.fori_loop(0, nchunk, scores, 0, unroll=8)

    def intra(n, carry):
        rows = chunk_rows(n)
        v = v_ref[rows, :]
        oi_ref[rows, :] = jnp.dot(a_ref[rows, :], v, preferred_element_type=F32)
        kv_ref[n] = lax.dot_general(v, kd_ref[rows, :], tn, preferred_element_type=F32)
        return carry

    lax.fori_loop(0, nchunk, intra, 0, unroll=4)

    def scan(n, s_t):
        sp_ref[n] = s_t.astype(BF16)
        return s_t * dec_ref[n][0:1, :] + kv_ref[n]

    lax.fori_loop(0, nchunk, scan, jnp.zeros((GLA_DV, GLA_DK), F32))

    def inter(n, carry):
        rows = chunk_rows(n)
        o = oi_ref[rows, :] + lax.dot_general(qd_ref[rows, :], sp_ref[n], nt,
                                              preferred_element_type=F32)
        ms = jnp.mean(o * o, axis=-1, keepdims=True)
        o = o * lax.rsqrt(ms + RMS_EPS) * ng_ref[...]
        o = o * _silu(r_ref[rows, :].astype(F32))
        o_ref[rows, :] = o.astype(BF16)
        return carry

    lax.fori_loop(0, nchunk, inter, 0, unroll=4)


def _gla(proj, glr, wup, bgk, ng, batch, seq):
    m = proj.shape[0]
    return pl.pallas_call(
        _gla_kernel,
        grid=(batch, GLA_HEADS),
        in_specs=[
            pl.BlockSpec((seq, GLA_DK), lambda b, h: (b, OFF_GQ // GLA_DK + h)),
            pl.BlockSpec((seq, GLA_DK), lambda b, h: (b, OFF_GK // GLA_DK + h)),
            pl.BlockSpec((seq, GLA_DV), lambda b, h: (b, OFF_GV // GLA_DV + h)),
            pl.BlockSpec((seq, GLA_DV), lambda b, h: (b, OFF_GR // GLA_DV + h)),
            pl.BlockSpec((seq, LANES), lambda b, h: (b, 0)),
            pl.BlockSpec((LANES, GLA_DK), lambda b, h: (0, h)),
            pl.BlockSpec((1, GLA_DK), lambda b, h: (0, h)),
            pl.BlockSpec((1, GLA_DV), lambda b, h: (0, h)),
        ],
        out_specs=pl.BlockSpec((seq, GLA_DV), lambda b, h: (b, h)),
        out_shape=jax.ShapeDtypeStruct((m, GLA_V_W), BF16),
        scratch_shapes=[
            pltpu.VMEM((seq, GLA_DK), F32),
            pltpu.VMEM((seq, GLA_DV), F32),
            pltpu.VMEM((seq, GLA_DK), BF16),
            pltpu.VMEM((seq, GLA_DK), BF16),
            pltpu.VMEM((seq, GLA_DK), BF16),
            pltpu.VMEM((seq, GLA_DK), BF16),
            pltpu.VMEM((seq, GLA_CHUNK), BF16),
            pltpu.VMEM((seq // GLA_CHUNK, GLA_DV, GLA_DK), F32),
            pltpu.VMEM((seq // GLA_CHUNK, 8, GLA_DK), F32),
            pltpu.VMEM((seq // GLA_CHUNK, GLA_DV, GLA_DK), BF16),
        ],
        compiler_params=pltpu.CompilerParams(
            dimension_semantics=("arbitrary", "arbitrary"), vmem_limit_bytes=VMEM_LIMIT),
        name="gla",
    )(proj, proj, proj, proj, glr, wup, bgk, ng)


def _swa_kernel(sink_ref, q_ref, kp_ref, kc_ref, vp_ref, vc_ref, o_ref):
    blk = SWA_BLOCK
    half = SWA_HEAD_DIM
    n = pl.program_id(1)
    k_all = jnp.concatenate([kp_ref[...], kc_ref[...]], axis=0)
    v_all = jnp.concatenate([vp_ref[...], vc_ref[...]], axis=0)
    qi = lax.broadcasted_iota(jnp.int32, (blk, 2 * blk), 0)
    kj = lax.broadcasted_iota(jnp.int32, (blk, 2 * blk), 1)
    rel = qi + blk - kj
    valid = (rel >= 0) & (rel < blk) & ((kj >= blk) | (n > 0))
    relf = rel.astype(F32)
    lane_kv = lax.broadcasted_iota(jnp.int32, (2 * blk, LANES), 1)
    lane_q = lax.broadcasted_iota(jnp.int32, (blk, LANES), 1)
    nt = (((1,), (1,)), ((), ()))

    def dup_half(slab, rolled, hh):
        lo = lane_kv < half
        if hh == 0:
            return jnp.where(lo, slab, rolled).astype(BF16)
        return jnp.where(lo, rolled, slab).astype(BF16)

    for p in range(SWA_KV_HEADS // 2):
        k_slab = k_all[:, p * LANES:(p + 1) * LANES].astype(F32)
        v_slab = v_all[:, p * LANES:(p + 1) * LANES].astype(F32)
        k_roll = pltpu.roll(k_slab, half, 1)
        v_roll = pltpu.roll(v_slab, half, 1)
        for hh in range(2):
            h = 2 * p + hh
            kd = dup_half(k_slab, k_roll, hh)
            vd = dup_half(v_slab, v_roll, hh)
            for gp in range(SWA_GROUP // 2):
                col = (h * (SWA_GROUP // 2) + gp) * LANES
                qs = q_ref[:, col:col + LANES]
                outs = []
                for gg in range(2):
                    head = h * SWA_GROUP + 2 * gp + gg
                    keep = (lane_q < half) if gg == 0 else (lane_q >= half)
                    qm = jnp.where(keep, qs, jnp.zeros_like(qs))
                    s = lax.dot_general(qm, kd, nt, preferred_element_type=F32) * (SWA_HEAD_DIM ** -0.5)
                    slope = 2.0 ** (-8.0 * (head + 1) / SWA_Q_HEADS)
                    s = s - slope * relf
                    s = jnp.where(valid, s, -jnp.inf)
                    sink = sink_ref[head]
                    mx = jnp.maximum(jnp.max(s, axis=-1, keepdims=True), sink)
                    pe = jnp.exp(s - mx)
                    den = jnp.sum(pe, axis=-1, keepdims=True) + jnp.exp(sink - mx)
                    o = jnp.dot(pe.astype(BF16), vd, preferred_element_type=F32)
                    outs.append(o / den)
                o_ref[:, col:col + LANES] = jnp.where(lane_q < half, outs[0], outs[1]).astype(BF16)


def _swa(proj, sinks, batch, seq):
    m = proj.shape[0]
    nb = seq // SWA_BLOCK
    qcol = OFF_SQ // SWA_Q_W
    kcol = OFF_SK // SWA_KV_W
    vcol = OFF_SV // SWA_KV_W
    cur = lambda c: (lambda b, n: (b * nb + n, c))
    prev = lambda c: (lambda b, n: (b * nb + jnp.maximum(n - 1, 0), c))
    return pl.pallas_call(
        _swa_kernel,
        grid=(batch, nb),
        in_specs=[
            pl.BlockSpec(memory_space=pltpu.SMEM),
            pl.BlockSpec((SWA_BLOCK, SWA_Q_W), cur(qcol)),
            pl.BlockSpec((SWA_BLOCK, SWA_KV_W), prev(kcol)),
            pl.BlockSpec((SWA_BLOCK, SWA_KV_W), cur(kcol)),
            pl.BlockSpec((SWA_BLOCK, SWA_KV_W), prev(vcol)),
            pl.BlockSpec((SWA_BLOCK, SWA_KV_W), cur(vcol)),
        ],
        out_specs=pl.BlockSpec((SWA_BLOCK, SWA_Q_W), lambda b, n: (b * nb + n, 0)),
        out_shape=jax.ShapeDtypeStruct((m, SWA_Q_W), BF16),
        compiler_params=pltpu.CompilerParams(
            dimension_semantics=("arbitrary", "arbitrary"), vmem_limit_bytes=VMEM_LIMIT),
        name="swa",
    )(sinks, proj, proj, proj, proj, proj)


def _out_route_kernel(og_ref, os_ref, x_ref, wo_ref, g_ref, wr_ref, br_ref,
                      h_ref, hn_ref, route_ref, cnt_ref):
    h = x_ref[...]
    h = h + jnp.dot(og_ref[...], wo_ref[0:GLA_V_W, :], preferred_element_type=F32)
    h = h + jnp.dot(os_ref[...], wo_ref[GLA_V_W:, :], preferred_element_type=F32)
    h_ref[...] = h
    ms = jnp.mean(h * h, axis=-1, keepdims=True)
    hn = h * lax.rsqrt(ms + RMS_EPS) * g_ref[...]
    hn_ref[...] = hn
    logits = jnp.dot(hn.astype(BF16), wr_ref[...], preferred_element_type=F32) + br_ref[...]

    lane = lax.broadcasted_iota(jnp.int32, logits.shape, 1)
    lanef = lane.astype(F32)
    big = float(LANES)
    ninf = -jnp.inf
    gl = jnp.where(lane < N_GROUPS, logits, ninf)
    gmax = jnp.max(gl, axis=-1, keepdims=True)
    g_p = 1.0 / jnp.sum(jnp.exp(gl - gmax), axis=-1, keepdims=True)
    g_idx = jnp.min(jnp.where(gl == gmax, lanef, big), axis=-1, keepdims=True)
    lo = N_GROUPS + EXPERTS_PER_GROUP * g_idx
    el = jnp.where((lanef >= lo) & (lanef < lo + EXPERTS_PER_GROUP), logits, ninf)
    m1 = jnp.max(el, axis=-1, keepdims=True)
    i1 = jnp.min(jnp.where(el == m1, lanef, big), axis=-1, keepdims=True)
    el2 = jnp.where(lanef == i1, ninf, el)
    m2 = jnp.max(el2, axis=-1, keepdims=True)
    i2 = jnp.min(jnp.where(el2 == m2, lanef, big), axis=-1, keepdims=True)
    d = jnp.exp(m2 - m1)
    c1 = g_p / (1.0 + d)
    c2 = g_p * d / (1.0 + d)
    tm = logits.shape[0]
    chosen = ((lanef == i1) | (lanef == i2)).astype(BF16)
    ri = lax.broadcasted_iota(jnp.int32, (tm, tm), 0)
    ci = lax.broadcasted_iota(jnp.int32, (tm, tm), 1)
    earlier = jnp.dot((ci < ri).astype(BF16), chosen, preferred_element_type=F32)
    r1 = jnp.sum(jnp.where(lanef == i1, earlier, 0.0), axis=-1, keepdims=True)
    r2 = jnp.sum(jnp.where(lanef == i2, earlier, 0.0), axis=-1, keepdims=True)
    cnt = jnp.sum(chosen.astype(F32), axis=0, keepdims=True)
    cnt_ref[...] = jnp.broadcast_to(cnt, cnt_ref.shape)

    fields = [i1 - N_GROUPS, i2 - N_GROUPS, c1, c2, r1, r2]
    route = jnp.zeros_like(logits)
    for idx, val in enumerate(fields):
        route = jnp.where(lane == idx, val, route)
    route_ref[...] = route


def _out_route(o_gla, o_swa, x2, w_out, g, w_rt, b_rt):
    m = x2.shape[0]
    tm = min(OUT_TM, m)
    row = lambda i: (i, 0)
    fixed = lambda i: (0, 0)
    return pl.pallas_call(
        _out_route_kernel,
        grid=(m // tm,),
        in_specs=[
            pl.BlockSpec((tm, GLA_V_W), row),
            pl.BlockSpec((tm, SWA_Q_W), row),
            pl.BlockSpec((tm, D_MODEL), row),
            pl.BlockSpec((GLA_V_W + SWA_Q_W, D_MODEL), fixed),
            pl.BlockSpec((1, D_MODEL), fixed),
            pl.BlockSpec((D_MODEL, LANES), fixed),
            pl.BlockSpec((1, LANES), fixed),
        ],
        out_specs=[
            pl.BlockSpec((tm, D_MODEL), row),
            pl.BlockSpec((tm, D_MODEL), row),
            pl.BlockSpec((tm, LANES), row),
            pl.BlockSpec((1, 8, LANES), lambda i: (i, 0, 0)),
        ],
        out_shape=[
            jax.ShapeDtypeStruct((m, D_MODEL), F32),
            jax.ShapeDtypeStruct((m, D_MODEL), F32),
            jax.ShapeDtypeStruct((m, LANES), F32),
            jax.ShapeDtypeStruct((m // tm, 8, LANES), F32),
        ],
        compiler_params=pltpu.CompilerParams(
            dimension_semantics=("arbitrary",), vmem_limit_bytes=VMEM_LIMIT),
        name="out_route",
    )(o_gla, o_swa, x2, w_out, g, w_rt, b_rt)


def _moe_kernel(te_ref, nt_ref, ids_hbm, hn_hbm, wg_ref, wu_ref, wd_ref, y_ref,
                xbuf, gsem, ids_smem, isem, wgb, wub, wdb):
    tr = MOE_TR
    t = pl.program_id(0)
    nt = nt_ref[0]

    def ids_copy(tile, s):
        return pltpu.make_async_copy(ids_hbm.at[tile], ids_smem.at[s], isem.at[s])

    @pl.when(t == 0)
    def _():
        ids_copy(0, 0).start()

    gather_ok = t < nt
    compute_ok = (t >= 1) & (t <= nt)
    gs = lax.rem(t, 2)
    cs = 1 - gs
    n_down = D_MODEL // D_EXPERT
    group_sizes = [32, 56, 56] + [(tr - 144) // n_down] * n_down
    assert sum(group_sizes) == tr
    group_starts = [sum(group_sizes[:i]) for i in range(len(group_sizes))]

    def issue_group(i):
        for r in range(group_starts[i], group_starts[i] + group_sizes[i]):
            tok = ids_smem[gs, r // LANES, r % LANES]
            pltpu.make_async_copy(hn_hbm.at[pl.ds(tok, 1)], xbuf.at[gs, pl.ds(r, 1)],
                                  gsem.at[gs]).start(priority=r % 2)

    def experts(between):
        c = t - 1
        changed = (c == 0) | (te_ref[c] != te_ref[jnp.maximum(c - 1, 0)])

        @pl.when(changed)
        def _():
            wgb[...] = wg_ref[...].astype(BF16)
            wub[...] = wu_ref[...].astype(BF16)
            wdb[...] = wd_ref[...].astype(BF16)

        pltpu.make_async_copy(hn_hbm.at[pl.ds(0, tr)], xbuf.at[cs], gsem.at[cs]).wait()
        x = xbuf[cs].astype(BF16)
        between(0)
        g = jnp.dot(x, wgb[...], preferred_element_type=F32)
        between(1)
        u = jnp.dot(x, wub[...], preferred_element_type=F32)
        between(2)
        hmid = (_silu(g) * u).astype(BF16)
        for nc in range(n_down):
            cols = slice(nc * D_EXPERT, (nc + 1) * D_EXPERT)
            piece = jnp.dot(hmid, wdb[:, cols], preferred_element_type=F32)
            between(3 + nc)
            y_ref[:, cols] = piece

    @pl.when(gather_ok)
    def _():
        ids_copy(t, gs).wait()

        @pl.when(t + 1 < nt)
        def _():
            ids_copy(t + 1, cs).start()

    @pl.when(gather_ok & compute_ok)
    def _():
        experts(issue_group)

    @pl.when(gather_ok & jnp.logical_not(compute_ok))
    def _():
        for i in range(len(group_sizes)):
            issue_group(i)

    @pl.when(compute_ok & jnp.logical_not(gather_ok))
    def _():
        experts(lambda i: None)

    @pl.when(t > nt)
    def _():
        y_ref[...] = jnp.zeros_like(y_ref)


def _moe(hn, w_gate, w_up, w_down, tile_expert, num_tiles, tok_ids):
    p_rows = tok_ids.shape[0]
    n_tiles = p_rows // MOE_TR
    ids3 = tok_ids.reshape(n_tiles, MOE_TR // LANES, LANES)
    prev = lambda t: jnp.maximum(t - 1, 0)
    wsel = lambda t, te, nt: (te[prev(t)], 0, 0)
    grid_spec = pltpu.PrefetchScalarGridSpec(
        num_scalar_prefetch=2,
        grid=(n_tiles + 1,),
        in_specs=[
            pl.BlockSpec(memory_space=pl.ANY),
            pl.BlockSpec(memory_space=pl.ANY),
            pl.BlockSpec((None, D_MODEL, D_EXPERT), wsel),
            pl.BlockSpec((None, D_MODEL, D_EXPERT), wsel),
            pl.BlockSpec((None, D_EXPERT, D_MODEL), wsel),
        ],
        out_specs=pl.BlockSpec((MOE_TR, D_MODEL), lambda t, te, nt: (prev(t), 0)),
        scratch_shapes=[
            pltpu.VMEM((2, MOE_TR, D_MODEL), F32),
            pltpu.SemaphoreType.DMA((2,)),
            pltpu.SMEM((2, MOE_TR // LANES, LANES), jnp.int32),
            pltpu.SemaphoreType.DMA((2,)),
            pltpu.VMEM((D_MODEL, D_EXPERT), BF16),
            pltpu.VMEM((D_MODEL, D_EXPERT), BF16),
            pltpu.VMEM((D_EXPERT, D_MODEL), BF16),
        ],
    )
    return pl.pallas_call(
        _moe_kernel,
        grid_spec=grid_spec,
        out_shape=jax.ShapeDtypeStruct((p_rows, D_MODEL), F32),
        compiler_params=pltpu.CompilerParams(
            dimension_semantics=("arbitrary",), vmem_limit_bytes=VMEM_LIMIT),
        name="moe",
    )(tile_expert, num_tiles, ids3, hn, w_gate, w_up, w_down)


def _combine_kernel(pos_hbm, y_hbm, h_ref, route_ref, g_ref, o_ref, ybuf, gsem, ids_smem, isem):
    tm = h_ref.shape[0]
    t = pl.program_id(0)
    ntile = pl.num_programs(0) - 1

    def ids_copy(tile, s):
        return pltpu.make_async_copy(pos_hbm.at[tile], ids_smem.at[s], isem.at[s])

    @pl.when(t == 0)
    def _():
        ids_copy(0, 0).start()

    @pl.when(t < ntile)
    def _():
        s = lax.rem(t, 2)
        ids_copy(t, s).wait()

        @pl.when(t + 1 < ntile)
        def _():
            ids_copy(t + 1, 1 - s).start()

        for kk in range(2):
            for r in range(tm):
                flat = kk * tm + r
                row = ids_smem[s, flat // LANES, flat % LANES]
                pltpu.make_async_copy(y_hbm.at[pl.ds(row, 1)], ybuf.at[s, kk, pl.ds(r, 1)],
                                      gsem.at[s, kk]).start(priority=r % 2)

    @pl.when(t >= 1)
    def _():
        s = lax.rem(t - 1, 2)
        for kk in range(2):
            pltpu.make_async_copy(y_hbm.at[pl.ds(0, tm)], ybuf.at[s, kk], gsem.at[s, kk]).wait()
        route = route_ref[...]
        h = h_ref[...] + route[:, 2:3] * ybuf[s, 0] + route[:, 3:4] * ybuf[s, 1]
        ms = jnp.mean(h * h, axis=-1, keepdims=True)
        o_ref[...] = h * lax.rsqrt(ms + RMS_EPS) * g_ref[...]


def _combine(y, h, route, g, pos2d):
    m = h.shape[0]
    tm = min(CMB_TM, m)
    ntile = m // tm
    pos3 = pos2d.reshape(ntile, tm, 2).transpose(0, 2, 1).reshape(ntile, 2 * tm // LANES, LANES)
    row = lambda i: (jnp.maximum(i - 1, 0), 0)
    return pl.pallas_call(
        _combine_kernel,
        grid=(ntile + 1,),
        in_specs=[
            pl.BlockSpec(memory_space=pl.ANY),
            pl.BlockSpec(memory_space=pl.ANY),
            pl.BlockSpec((tm, D_MODEL), row),
            pl.BlockSpec((tm, LANES), row),
            pl.BlockSpec((1, D_MODEL), lambda i: (0, 0)),
        ],
        out_specs=pl.BlockSpec((tm, D_MODEL), row),
        out_shape=jax.ShapeDtypeStruct((m, D_MODEL), F32),
        scratch_shapes=[
            pltpu.VMEM((2, 2, tm, D_MODEL), F32),
            pltpu.SemaphoreType.DMA((2, 2)),
            pltpu.SMEM((2, 2 * tm // LANES, LANES), jnp.int32),
            pltpu.SemaphoreType.DMA((2,)),
        ],
        compiler_params=pltpu.CompilerParams(
            dimension_semantics=("arbitrary",), vmem_limit_bytes=VMEM_LIMIT),
        name="combine",
    )(pos3, y, h, route, g)


def _dispatch_plan(route, tile_counts, m):
    tr = MOE_TR
    p_rows = 2 * m + N_EXPERTS * tr
    n_tiles = p_rows // tr
    n_tok_tiles = tile_counts.shape[0]
    cnt = tile_counts[:, 0, N_GROUPS:N_GROUPS + N_EXPERTS].astype(jnp.int32)
    before_tile = jnp.cumsum(cnt, axis=0) - cnt
    counts = jnp.sum(cnt, axis=0)
    padded = ((counts + tr - 1) // tr) * tr
    ends = jnp.cumsum(padded)
    starts = ends - padded
    base = (starts[None, :] + before_tile).astype(jnp.int32)
    expert = route[:, 0:2].astype(jnp.int32).reshape(n_tok_tiles, m // n_tok_tiles, 2)
    rank = route[:, 4:6].astype(jnp.int32).reshape(n_tok_tiles, m // n_tok_tiles, 2)
    hit = expert[..., None] == jnp.arange(N_EXPERTS, dtype=jnp.int32)
    pos = rank + jnp.sum(jnp.where(hit, base[:, None, None, :], 0), axis=-1)
    pos = pos.reshape(m, 2)
    num_tiles = (ends[-1] // tr).astype(jnp.int32)
    tile_start = jnp.minimum(jnp.arange(n_tiles, dtype=jnp.int32), num_tiles - 1) * tr
    tile_expert = jnp.sum((tile_start[:, None] >= ends[None, :]).astype(jnp.int32), axis=1)
    tok_ids = jnp.zeros((p_rows,), jnp.int32).at[pos.reshape(-1)].set(
        jnp.arange(2 * m, dtype=jnp.int32) // 2, unique_indices=True)
    return tile_expert.astype(jnp.int32), num_tiles.reshape(1), tok_ids, pos


def kernel(x, norm_mix_g, w_in, w_gk_up, b_gk, gla_norm_g, swa_sinks, w_out, norm_ffn_g,
           w_group, b_group, w_router, b_router, w_gate, w_up, w_down, norm_final_g):
    batch, seq, d = x.shape
    m = batch * seq
    assert w_in.shape[0] == 1, "single-layer block"
    x2 = x.reshape(m, d)
    lr0 = OFF_GR + GLA_V_W
    w_in_b = w_in[0].astype(BF16)
    w_main = jnp.concatenate([w_in_b[:, :lr0], w_in_b[:, lr0 + GLA_RANK:]], axis=1)
    w_lr = jnp.pad(w_in_b[:, lr0:lr0 + GLA_RANK], ((0, 0), (0, LANES - GLA_RANK)))
    wup = jnp.pad(w_gk_up[0], ((0, LANES - GLA_RANK), (0, 0))).astype(BF16)
    w_rt = jnp.pad(jnp.concatenate([w_group[0], w_router[0]], axis=1),
                   ((0, 0), (0, LANES - N_GROUPS - N_EXPERTS))).astype(BF16)
    b_rt = jnp.pad(jnp.concatenate([b_group[0], b_router[0]]),
                   (0, LANES - N_GROUPS - N_EXPERTS)).reshape(1, LANES)

    proj, glr = _in_proj(x2, norm_mix_g[0].reshape(1, d), w_main, w_lr)
    o_gla = _gla(proj, glr, wup, b_gk[0].reshape(1, GLA_QK_W),
                 gla_norm_g[0].reshape(1, GLA_V_W), batch, seq)
    o_swa = _swa(proj, swa_sinks[0], batch, seq)
    h_mid, hn, route, tile_counts = _out_route(o_gla, o_swa, x2, w_out[0].astype(BF16),
                                               norm_ffn_g[0].reshape(1, d), w_rt, b_rt)
    tile_expert, num_tiles, tok_ids, pos2d = _dispatch_plan(route, tile_counts, m)
    y = _moe(hn, w_gate[0], w_up[0], w_down[0], tile_expert, num_tiles, tok_ids)
    out = _combine(y, h_mid, route, norm_final_g.reshape(1, d), pos2d)
    return out.reshape(batch, seq, d)
```

```python
import functools

import jax
import jax.numpy as jnp
from jax import lax
from jax.experimental import pallas as pl
from jax.experimental.pallas import tpu as pltpu

F32 = jnp.float32
BF16 = jnp.bfloat16

D_MODEL = 2048
GLA_HEADS = 4
GLA_DK = 128
GLA_DV = 256
GLA_RANK = 16
GLA_GATE_NORM = 16.0
GLA_CHUNK = 64
SWA_Q_HEADS = 16
SWA_KV_HEADS = 4
SWA_GROUP = SWA_Q_HEADS // SWA_KV_HEADS
SWA_HEAD_DIM = 64
SWA_BLOCK = 128
N_GROUPS = 4
EXPERTS_PER_GROUP = 16
N_EXPERTS = N_GROUPS * EXPERTS_PER_GROUP
D_EXPERT = 256
RMS_EPS = 1e-6

GLA_QK_W = GLA_HEADS * GLA_DK
GLA_V_W = GLA_HEADS * GLA_DV
SWA_Q_W = SWA_Q_HEADS * SWA_HEAD_DIM
SWA_KV_W = SWA_KV_HEADS * SWA_HEAD_DIM
OFF_GQ = 0
OFF_GK = OFF_GQ + GLA_QK_W
OFF_GV = OFF_GK + GLA_QK_W
OFF_GR = OFF_GV + GLA_V_W
OFF_SQ = OFF_GR + GLA_V_W
OFF_SK = OFF_SQ + SWA_Q_W
OFF_SV = OFF_SK + SWA_KV_W
PROJ_W = OFF_SV + SWA_KV_W
LANES = 128

IN_TM = 1024
IN_TN = 768
OUT_TM = 256
MOE_TR = 256
MOE_ROW_PITCH = 24
CMB_TM = 256
VMEM_LIMIT = 56 * 1024 * 1024


def _silu(x):
    return x / (1.0 + jnp.exp(-x))


def _inproj_kernel(x_ref, g_ref, w_ref, wlr_ref, proj_ref, glr_ref, xn_ref):
    @pl.when(pl.program_id(1) == 0)
    def _():
        x = x_ref[...]
        ms = jnp.mean(x * x, axis=-1, keepdims=True)
        xn = (x * lax.rsqrt(ms + RMS_EPS) * g_ref[...]).astype(BF16)
        xn_ref[...] = xn
        glr_ref[...] = jnp.dot(xn, wlr_ref[...], preferred_element_type=F32)

    proj_ref[...] = jnp.dot(xn_ref[...], w_ref[...], preferred_element_type=F32).astype(BF16)


def _in_proj(x2, g, w_main, w_lr):
    m = x2.shape[0]
    tm = min(IN_TM, m)
    return pl.pallas_call(
        _inproj_kernel,
        grid=(m // tm, PROJ_W // IN_TN),
        in_specs=[
            pl.BlockSpec((tm, D_MODEL), lambda i, j: (i, 0)),
            pl.BlockSpec((1, D_MODEL), lambda i, j: (0, 0)),
            pl.BlockSpec((D_MODEL, IN_TN), lambda i, j: (0, j)),
            pl.BlockSpec((D_MODEL, LANES), lambda i, j: (0, 0)),
        ],
        out_specs=[
            pl.BlockSpec((tm, IN_TN), lambda i, j: (i, j)),
            pl.BlockSpec((tm, LANES), lambda i, j: (i, 0)),
        ],
        out_shape=[
            jax.ShapeDtypeStruct((m, PROJ_W), BF16),
            jax.ShapeDtypeStruct((m, LANES), F32),
        ],
        scratch_shapes=[pltpu.VMEM((tm, D_MODEL), BF16)],
        compiler_params=pltpu.CompilerParams(
            dimension_semantics=("arbitrary", "arbitrary"), vmem_limit_bytes=VMEM_LIMIT),
        name="in_proj",
    )(x2, g, w_main, w_lr)


def _gla_kernel(q_ref, k_ref, v_ref, r_ref, glr_ref, wup_ref, bgk_ref, ng_ref, o_ref,
                la_ref, oi_ref, qi_ref, ki_ref, qd_ref, kd_ref, a_ref, kv_ref, dec_ref, sp_ref):
    t = q_ref.shape[0]
    c = GLA_CHUNK
    nchunk = t // c
    z = jnp.dot(glr_ref[...].astype(BF16), wup_ref[...], preferred_element_type=F32) + bgk_ref[...]
    la_ref[...] = (jnp.minimum(z, 0.0) - jnp.log1p(jnp.exp(-jnp.abs(z)))) * (1.0 / GLA_GATE_NORM)

    ii = lax.broadcasted_iota(jnp.int32, (c, c), 0)
    jj = lax.broadcasted_iota(jnp.int32, (c, c), 1)
    causal = jj <= ii
    tri = causal.astype(BF16)
    nt = (((1,), (1,)), ((), ()))
    tn = (((0,), (0,)), ((), ()))

    def chunk_rows(n):
        return pl.ds(pl.multiple_of(n * c, c), c)

    def decays(n, carry):
        rows = chunk_rows(n)
        la = la_ref[rows, :]
        hi = la.astype(BF16)
        r1 = la - hi.astype(F32)
        mid = r1.astype(BF16)
        lo = (r1 - mid.astype(F32)).astype(BF16)
        parts = jnp.dot(tri, jnp.concatenate([hi, mid, lo], axis=1), preferred_element_type=F32)
        bcum = parts[:, 0:GLA_DK] + parts[:, GLA_DK:2 * GLA_DK] + parts[:, 2 * GLA_DK:]
        b_mid = bcum[c // 2 - 1:c // 2, :]
        b_last = bcum[c - 1:c, :]
        q = q_ref[rows, :].astype(F32) * (GLA_DK ** -0.5)
        k = k_ref[rows, :].astype(F32)
        qi_ref[rows, :] = (q * jnp.exp(bcum - b_mid)).astype(BF16)
        ki_ref[rows, :] = (k * jnp.exp(b_mid - bcum)).astype(BF16)
        qd_ref[rows, :] = (q * jnp.exp(bcum)).astype(BF16)
        kd_ref[rows, :] = (k * jnp.exp(b_last - bcum)).astype(BF16)
        dec_ref[n] = jnp.broadcast_to(jnp.exp(b_last), dec_ref.shape[1:])
        return carry

    lax.fori_loop(0, nchunk, decays, 0, unroll=4)

    def scores(n, carry):
        rows = chunk_rows(n)
        a = lax.dot_general(qi_ref[rows, :], ki_ref[rows, :], nt, preferred_element_type=F32)
        a_ref[rows, :] = jnp.where(causal, a, 0.0).astype(BF16)
        return carry

    lax.fori_loop(0, nchunk, scores, 0, unroll=8)

    def intra(n, carry):
        rows = chunk_rows(n)
        v = v_ref[rows, :]
        oi_ref[rows, :] = jnp.dot(a_ref[rows, :], v, preferred_element_type=F32)
        kv_ref[n] = lax.dot_general(v, kd_ref[rows, :], tn, preferred_element_type=F32)
        return carry

    lax.fori_loop(0, nchunk, intra, 0, unroll=4)

    def scan(n, s_t):
        sp_ref[n] = s_t.astype(BF16)
        return s_t * dec_ref[n][0:1, :] + kv_ref[n]

    lax.fori_loop(0, nchunk, scan, jnp.zeros((GLA_DV, GLA_DK), F32))

    def inter(n, carry):
        rows = chunk_rows(n)
        o = oi_ref[rows, :] + lax.dot_general(qd_ref[rows, :], sp_ref[n], nt,
                                              preferred_element_type=F32)
        ms = jnp.mean(o * o, axis=-1, keepdims=True)
        o = o * lax.rsqrt(ms + RMS_EPS) * ng_ref[...]
        o = o * _silu(r_ref[rows, :].astype(F32))
        o_ref[rows, :] = o.astype(BF16)
        return carry

    lax.fori_loop(0, nchunk, inter, 0, unroll=4)


def _gla(proj, glr, wup, bgk, ng, batch, seq):
    m = proj.shape[0]
    return pl.pallas_call(
        _gla_kernel,
        grid=(batch, GLA_HEADS),
        in_specs=[
            pl.BlockSpec((seq, GLA_DK), lambda b, h: (b, OFF_GQ // GLA_DK + h)),
            pl.BlockSpec((seq, GLA_DK), lambda b, h: (b, OFF_GK // GLA_DK + h)),
            pl.BlockSpec((seq, GLA_DV), lambda b, h: (b, OFF_GV // GLA_DV + h)),
            pl.BlockSpec((seq, GLA_DV), lambda b, h: (b, OFF_GR // GLA_DV + h)),
            pl.BlockSpec((seq, LANES), lambda b, h: (b, 0)),
            pl.BlockSpec((LANES, GLA_DK), lambda b, h: (0, h)),
            pl.BlockSpec((1, GLA_DK), lambda b, h: (0, h)),
            pl.BlockSpec((1, GLA_DV), lambda b, h: (0, h)),
        ],
        out_specs=pl.BlockSpec((seq, GLA_DV), lambda b, h: (b, h)),
        out_shape=jax.ShapeDtypeStruct((m, GLA_V_W), BF16),
        scratch_shapes=[
            pltpu.VMEM((seq, GLA_DK), F32),
            pltpu.VMEM((seq, GLA_DV), F32),
            pltpu.VMEM((seq, GLA_DK), BF16),
            pltpu.VMEM((seq, GLA_DK), BF16),
            pltpu.VMEM((seq, GLA_DK), BF16),
            pltpu.VMEM((seq, GLA_DK), BF16),
            pltpu.VMEM((seq, GLA_CHUNK), BF16),
            pltpu.VMEM((seq // GLA_CHUNK, GLA_DV, GLA_DK), F32),
            pltpu.VMEM((seq // GLA_CHUNK, 8, GLA_DK), F32),
            pltpu.VMEM((seq // GLA_CHUNK, GLA_DV, GLA_DK), BF16),
        ],
        compiler_params=pltpu.CompilerParams(
            dimension_semantics=("arbitrary", "arbitrary"), vmem_limit_bytes=VMEM_LIMIT),
        name="gla",
    )(proj, proj, proj, proj, glr, wup, bgk, ng)


def _swa_kernel(sink_ref, q_ref, kp_ref, kc_ref, vp_ref, vc_ref, o_ref):
    blk = SWA_BLOCK
    half = SWA_HEAD_DIM
    n = pl.program_id(1)
    k_all = jnp.concatenate([kp_ref[...], kc_ref[...]], axis=0)
    v_all = jnp.concatenate([vp_ref[...], vc_ref[...]], axis=0)
    qi = lax.broadcasted_iota(jnp.int32, (blk, 2 * blk), 0)
    kj = lax.broadcasted_iota(jnp.int32, (blk, 2 * blk), 1)
    rel = qi + blk - kj
    valid = (rel >= 0) & (rel < blk) & ((kj >= blk) | (n > 0))
    relf = rel.astype(F32)
    lane_kv = lax.broadcasted_iota(jnp.int32, (2 * blk, LANES), 1)
    lane_q = lax.broadcasted_iota(jnp.int32, (blk, LANES), 1)
    nt = (((1,), (1,)), ((), ()))

    def dup_half(slab, rolled, hh):
        lo = lane_kv < half
        if hh == 0:
            return jnp.where(lo, slab, rolled).astype(BF16)
        return jnp.where(lo, rolled, slab).astype(BF16)

    for p in range(SWA_KV_HEADS // 2):
        k_slab = k_all[:, p * LANES:(p + 1) * LANES].astype(F32)
        v_slab = v_all[:, p * LANES:(p + 1) * LANES].astype(F32)
        k_roll = pltpu.roll(k_slab, half, 1)
        v_roll = pltpu.roll(v_slab, half, 1)
        for hh in range(2):
            h = 2 * p + hh
            kd = dup_half(k_slab, k_roll, hh)
            vd = dup_half(v_slab, v_roll, hh)
            for gp in range(SWA_GROUP // 2):
                col = (h * (SWA_GROUP // 2) + gp) * LANES
                qs = q_ref[:, col:col + LANES]
                outs = []
                for gg in range(2):
                    head = h * SWA_GROUP + 2 * gp + gg
                    keep = (lane_q < half) if gg == 0 else (lane_q >= half)
                    qm = jnp.where(keep, qs, jnp.zeros_like(qs))
                    s = lax.dot_general(qm, kd, nt, preferred_element_type=F32) * (SWA_HEAD_DIM ** -0.5)
                    slope = 2.0 ** (-8.0 * (head + 1) / SWA_Q_HEADS)
                    s = s - slope * relf
                    s = jnp.where(valid, s, -jnp.inf)
                    sink = sink_ref[head]
                    mx = jnp.maximum(jnp.max(s, axis=-1, keepdims=True), sink)
                    pe = jnp.exp(s - mx)
                    den = jnp.sum(pe, axis=-1, keepdims=True) + jnp.exp(sink - mx)
                    o = jnp.dot(pe.astype(BF16), vd, preferred_element_type=F32)
                    outs.append(o / den)
                o_ref[:, col:col + LANES] = jnp.where(lane_q < half, outs[0], outs[1]).astype(BF16)


def _swa(proj, sinks, batch, seq):
    m = proj.shape[0]
    nb = seq // SWA_BLOCK
    qcol = OFF_SQ // SWA_Q_W
    kcol = OFF_SK // SWA_KV_W
    vcol = OFF_SV // SWA_KV_W
    cur = lambda c: (lambda b, n: (b * nb + n, c))
    prev = lambda c: (lambda b, n: (b * nb + jnp.maximum(n - 1, 0), c))
    return pl.pallas_call(
        _swa_kernel,
        grid=(batch, nb),
        in_specs=[
            pl.BlockSpec(memory_space=pltpu.SMEM),
            pl.BlockSpec((SWA_BLOCK, SWA_Q_W), cur(qcol)),
            pl.BlockSpec((SWA_BLOCK, SWA_KV_W), prev(kcol)),
            pl.BlockSpec((SWA_BLOCK, SWA_KV_W), cur(kcol)),
            pl.BlockSpec((SWA_BLOCK, SWA_KV_W), prev(vcol)),
            pl.BlockSpec((SWA_BLOCK, SWA_KV_W), cur(vcol)),
        ],
        out_specs=pl.BlockSpec((SWA_BLOCK, SWA_Q_W), lambda b, n: (b * nb + n, 0)),
        out_shape=jax.ShapeDtypeStruct((m, SWA_Q_W), BF16),
        compiler_params=pltpu.CompilerParams(
            dimension_semantics=("arbitrary", "arbitrary"), vmem_limit_bytes=VMEM_LIMIT),
        name="swa",
    )(sinks, proj, proj, proj, proj, proj)


def _out_route_kernel(og_ref, os_ref, x_ref, wo_ref, g_ref, wr_ref, br_ref,
                      h_ref, hn_ref, route_ref, cnt_ref):
    h = x_ref[...]
    h = h + jnp.dot(og_ref[...], wo_ref[0:GLA_V_W, :], preferred_element_type=F32)
    h = h + jnp.dot(os_ref[...], wo_ref[GLA_V_W:, :], preferred_element_type=F32)
    h_ref[...] = h
    ms = jnp.mean(h * h, axis=-1, keepdims=True)
    hn = h * lax.rsqrt(ms + RMS_EPS) * g_ref[...]
    nseg = D_MODEL // LANES
    for j in range(nseg):
        hn_ref[pl.ds(j, h.shape[0], stride=nseg), :] = hn[:, j * LANES:(j + 1) * LANES]
    logits = jnp.dot(hn.astype(BF16), wr_ref[...], preferred_element_type=F32) + br_ref[...]

    lane = lax.broadcasted_iota(jnp.int32, logits.shape, 1)
    lanef = lane.astype(F32)
    big = float(LANES)
    ninf = -jnp.inf
    gl = jnp.where(lane < N_GROUPS, logits, ninf)
    gmax = jnp.max(gl, axis=-1, keepdims=True)
    g_p = 1.0 / jnp.sum(jnp.exp(gl - gmax), axis=-1, keepdims=True)
    g_idx = jnp.min(jnp.where(gl == gmax, lanef, big), axis=-1, keepdims=True)
    lo = N_GROUPS + EXPERTS_PER_GROUP * g_idx
    el = jnp.where((lanef >= lo) & (lanef < lo + EXPERTS_PER_GROUP), logits, ninf)
    m1 = jnp.max(el, axis=-1, keepdims=True)
    i1 = jnp.min(jnp.where(el == m1, lanef, big), axis=-1, keepdims=True)
    el2 = jnp.where(lanef == i1, ninf, el)
    m2 = jnp.max(el2, axis=-1, keepdims=True)
    i2 = jnp.min(jnp.where(el2 == m2, lanef, big), axis=-1, keepdims=True)
    d = jnp.exp(m2 - m1)
    c1 = g_p / (1.0 + d)
    c2 = g_p * d / (1.0 + d)
    tm = logits.shape[0]
    chosen = ((lanef == i1) | (lanef == i2)).astype(BF16)
    ri = lax.broadcasted_iota(jnp.int32, (tm, tm), 0)
    ci = lax.broadcasted_iota(jnp.int32, (tm, tm), 1)
    earlier = jnp.dot((ci < ri).astype(BF16), chosen, preferred_element_type=F32)
    r1 = jnp.sum(jnp.where(lanef == i1, earlier, 0.0), axis=-1, keepdims=True)
    r2 = jnp.sum(jnp.where(lanef == i2, earlier, 0.0), axis=-1, keepdims=True)
    cnt = jnp.sum(chosen.astype(F32), axis=0, keepdims=True)
    cnt_ref[...] = jnp.broadcast_to(cnt, cnt_ref.shape)

    fields = [i1 - N_GROUPS, i2 - N_GROUPS, c1, c2, r1, r2]
    route = jnp.zeros_like(logits)
    for idx, val in enumerate(fields):
        route = jnp.where(lane == idx, val, route)
    route_ref[...] = route


def _out_route(o_gla, o_swa, x2, w_out, g, w_rt, b_rt):
    m = x2.shape[0]
    tm = min(OUT_TM, m)
    row = lambda i: (i, 0)
    fixed = lambda i: (0, 0)
    return pl.pallas_call(
        _out_route_kernel,
        grid=(m // tm,),
        in_specs=[
            pl.BlockSpec((tm, GLA_V_W), row),
            pl.BlockSpec((tm, SWA_Q_W), row),
            pl.BlockSpec((tm, D_MODEL), row),
            pl.BlockSpec((GLA_V_W + SWA_Q_W, D_MODEL), fixed),
            pl.BlockSpec((1, D_MODEL), fixed),
            pl.BlockSpec((D_MODEL, LANES), fixed),
            pl.BlockSpec((1, LANES), fixed),
        ],
        out_specs=[
            pl.BlockSpec((tm, D_MODEL), row),
            pl.BlockSpec((tm * (D_MODEL // LANES), LANES), row),
            pl.BlockSpec((tm, LANES), row),
            pl.BlockSpec((1, 8, LANES), lambda i: (i, 0, 0)),
        ],
        out_shape=[
            jax.ShapeDtypeStruct((m, D_MODEL), F32),
            jax.ShapeDtypeStruct((m * (D_MODEL // LANES), LANES), F32),
            jax.ShapeDtypeStruct((m, LANES), F32),
            jax.ShapeDtypeStruct((m // tm, 8, LANES), F32),
        ],
        compiler_params=pltpu.CompilerParams(
            dimension_semantics=("arbitrary",), vmem_limit_bytes=VMEM_LIMIT),
        name="out_route",
    )(o_gla, o_swa, x2, w_out, g, w_rt, b_rt)


def _moe_kernel(te_ref, nt_ref, ids_hbm, hn_hbm, wg_ref, wu_ref, wd_ref, y_ref,
                xbuf, gsem, ids_smem, isem, wgb, wub, wdb):
    tr = MOE_TR
    t = pl.program_id(0)
    nt = nt_ref[0]

    def ids_copy(tile, s):
        return pltpu.make_async_copy(ids_hbm.at[tile], ids_smem.at[s], isem.at[s])

    @pl.when(t == 0)
    def _():
        ids_copy(0, 0).start()

    gs = lax.rem(t, 2)
    cs = 1 - gs
    nseg = D_MODEL // LANES
    pitch = MOE_ROW_PITCH

    @pl.when(t < nt)
    def _():
        ids_copy(t, gs).wait()

        @pl.when(t + 1 < nt)
        def _():
            ids_copy(t + 1, cs).start()

        for r in range(tr):
            tok = ids_smem[gs, r // LANES, r % LANES]
            src = hn_hbm.at[pl.ds(pl.multiple_of(tok * nseg, nseg), nseg)]
            pltpu.make_async_copy(src, xbuf.at[gs, pl.ds(r * pitch, nseg)],
                                  gsem.at[gs]).start(priority=r % 2)

    @pl.when((t >= 1) & (t <= nt))
    def _():
        c = t - 1
        changed = (c == 0) | (te_ref[c] != te_ref[jnp.maximum(c - 1, 0)])

        @pl.when(changed)
        def _():
            wgb[...] = wg_ref[...].astype(BF16)
            wub[...] = wu_ref[...].astype(BF16)
            wdb[...] = wd_ref[...].astype(BF16)

        pltpu.make_async_copy(hn_hbm.at[pl.ds(0, tr * nseg)], xbuf.at[cs, pl.ds(0, tr * nseg)],
                              gsem.at[cs]).wait()
        xrows = xbuf.at[cs]
        x = jnp.concatenate([xrows[pl.ds(j, tr, stride=pitch), :].astype(BF16) for j in range(nseg)],
                            axis=1)
        g = jnp.dot(x, wgb[...], preferred_element_type=F32)
        u = jnp.dot(x, wub[...], preferred_element_type=F32)
        hmid = (_silu(g) * u).astype(BF16)
        y_ref[...] = jnp.dot(hmid, wdb[...], preferred_element_type=F32)

    @pl.when(t > nt)
    def _():
        y_ref[...] = jnp.zeros_like(y_ref)


def _moe(hn, w_gate, w_up, w_down, tile_expert, num_tiles, tok_ids):
    p_rows = tok_ids.shape[0]
    n_tiles = p_rows // MOE_TR
    ids3 = tok_ids.reshape(n_tiles, MOE_TR // LANES, LANES)
    prev = lambda t: jnp.maximum(t - 1, 0)
    wsel = lambda t, te, nt: (te[prev(t)], 0, 0)
    grid_spec = pltpu.PrefetchScalarGridSpec(
        num_scalar_prefetch=2,
        grid=(n_tiles + 1,),
        in_specs=[
            pl.BlockSpec(memory_space=pl.ANY),
            pl.BlockSpec(memory_space=pl.ANY),
            pl.BlockSpec((None, D_MODEL, D_EXPERT), wsel),
            pl.BlockSpec((None, D_MODEL, D_EXPERT), wsel),
            pl.BlockSpec((None, D_EXPERT, D_MODEL), wsel),
        ],
        out_specs=pl.BlockSpec((MOE_TR, D_MODEL), lambda t, te, nt: (prev(t), 0)),
        scratch_shapes=[
            pltpu.VMEM((2, MOE_TR * MOE_ROW_PITCH, LANES), F32),
            pltpu.SemaphoreType.DMA((2,)),
            pltpu.SMEM((2, MOE_TR // LANES, LANES), jnp.int32),
            pltpu.SemaphoreType.DMA((2,)),
            pltpu.VMEM((D_MODEL, D_EXPERT), BF16),
            pltpu.VMEM((D_MODEL, D_EXPERT), BF16),
            pltpu.VMEM((D_EXPERT, D_MODEL), BF16),
        ],
    )
    return pl.pallas_call(
        _moe_kernel,
        grid_spec=grid_spec,
        out_shape=jax.ShapeDtypeStruct((p_rows, D_MODEL), F32),
        compiler_params=pltpu.CompilerParams(
            dimension_semantics=("arbitrary",), vmem_limit_bytes=VMEM_LIMIT),
        name="moe",
    )(tile_expert, num_tiles, ids3, hn, w_gate, w_up, w_down)


def _combine_kernel(pos_hbm, y_hbm, h_ref, route_ref, g_ref, o_ref, ybuf, gsem, ids_smem, isem):
    tm = h_ref.shape[0]
    t = pl.program_id(0)
    ntile = pl.num_programs(0) - 1

    def ids_copy(tile, s):
        return pltpu.make_async_copy(pos_hbm.at[tile], ids_smem.at[s], isem.at[s])

    @pl.when(t == 0)
    def _():
        ids_copy(0, 0).start()

    @pl.when(t < ntile)
    def _():
        s = lax.rem(t, 2)
        ids_copy(t, s).wait()

        @pl.when(t + 1 < ntile)
        def _():
            ids_copy(t + 1, 1 - s).start()

        for kk in range(2):
            for r in range(tm):
                flat = kk * tm + r
                row = ids_smem[s, flat // LANES, flat % LANES]
                pltpu.make_async_copy(y_hbm.at[pl.ds(row, 1)], ybuf.at[s, kk, pl.ds(r, 1)],
                                      gsem.at[s, kk]).start(priority=r % 2)

    @pl.when(t >= 1)
    def _():
        s = lax.rem(t - 1, 2)
        for kk in range(2):
            pltpu.make_async_copy(y_hbm.at[pl.ds(0, tm)], ybuf.at[s, kk], gsem.at[s, kk]).wait()
        route = route_ref[...]
        h = h_ref[...] + route[:, 2:3] * ybuf[s, 0] + route[:, 3:4] * ybuf[s, 1]
        ms = jnp.mean(h * h, axis=-1, keepdims=True)
        o_ref[...] = h * lax.rsqrt(ms + RMS_EPS) * g_ref[...]


def _combine(y, h, route, g, pos2d):
    m = h.shape[0]
    tm = min(CMB_TM, m)
    ntile = m // tm
    pos3 = pos2d.reshape(ntile, tm, 2).transpose(0, 2, 1).reshape(ntile, 2 * tm // LANES, LANES)
    row = lambda i: (jnp.maximum(i - 1, 0), 0)
    return pl.pallas_call(
        _combine_kernel,
        grid=(ntile + 1,),
        in_specs=[
            pl.BlockSpec(memory_space=pl.ANY),
            pl.BlockSpec(memory_space=pl.ANY),
            pl.BlockSpec((tm, D_MODEL), row),
            pl.BlockSpec((tm, LANES), row),
            pl.BlockSpec((1, D_MODEL), lambda i: (0, 0)),
        ],
        out_specs=pl.BlockSpec((tm, D_MODEL), row),
        out_shape=jax.ShapeDtypeStruct((m, D_MODEL), F32),
        scratch_shapes=[
            pltpu.VMEM((2, 2, tm, D_MODEL), F32),
            pltpu.SemaphoreType.DMA((2, 2)),
            pltpu.SMEM((2, 2 * tm // LANES, LANES), jnp.int32),
            pltpu.SemaphoreType.DMA((2,)),
        ],
        compiler_params=pltpu.CompilerParams(
            dimension_semantics=("arbitrary",), vmem_limit_bytes=VMEM_LIMIT),
        name="combine",
    )(pos3, y, h, route, g)


def _dispatch_plan(route, tile_counts, m):
    tr = MOE_TR
    p_rows = 2 * m + N_EXPERTS * tr
    n_tiles = p_rows // tr
    n_tok_tiles = tile_counts.shape[0]
    cnt = tile_counts[:, 0, N_GROUPS:N_GROUPS + N_EXPERTS].astype(jnp.int32)
    before_tile = jnp.cumsum(cnt, axis=0) - cnt
    counts = jnp.sum(cnt, axis=0)
    padded = ((counts + tr - 1) // tr) * tr
    ends = jnp.cumsum(padded)
    starts = ends - padded
    base = (starts[None, :] + before_tile).astype(jnp.int32)
    expert = route[:, 0:2].astype(jnp.int32).reshape(n_tok_tiles, m // n_tok_tiles, 2)
    rank = route[:, 4:6].astype(jnp.int32).reshape(n_tok_tiles, m // n_tok_tiles, 2)
    hit = expert[..., None] == jnp.arange(N_EXPERTS, dtype=jnp.int32)
    pos = rank + jnp.sum(jnp.where(hit, base[:, None, None, :], 0), axis=-1)
    pos = pos.reshape(m, 2)
    num_tiles = (ends[-1] // tr).astype(jnp.int32)
    tile_start = jnp.minimum(jnp.arange(n_tiles, dtype=jnp.int32), num_tiles - 1) * tr
    tile_expert = jnp.sum((tile_start[:, None] >= ends[None, :]).astype(jnp.int32), axis=1)
    tok_ids = jnp.zeros((p_rows,), jnp.int32).at[pos.reshape(-1)].set(
        jnp.arange(2 * m, dtype=jnp.int32) // 2, unique_indices=True)
    return tile_expert.astype(jnp.int32), num_tiles.reshape(1), tok_ids, pos


def kernel(x, norm_mix_g, w_in, w_gk_up, b_gk, gla_norm_g, swa_sinks, w_out, norm_ffn_g,
           w_group, b_group, w_router, b_router, w_gate, w_up, w_down, norm_final_g):
    batch, seq, d = x.shape
    m = batch * seq
    assert w_in.shape[0] == 1, "single-layer block"
    x2 = x.reshape(m, d)
    lr0 = OFF_GR + GLA_V_W
    w_in_b = w_in[0].astype(BF16)
    w_main = jnp.concatenate([w_in_b[:, :lr0], w_in_b[:, lr0 + GLA_RANK:]], axis=1)
    w_lr = jnp.pad(w_in_b[:, lr0:lr0 + GLA_RANK], ((0, 0), (0, LANES - GLA_RANK)))
    wup = jnp.pad(w_gk_up[0], ((0, LANES - GLA_RANK), (0, 0))).astype(BF16)
    w_rt = jnp.pad(jnp.concatenate([w_group[0], w_router[0]], axis=1),
                   ((0, 0), (0, LANES - N_GROUPS - N_EXPERTS))).astype(BF16)
    b_rt = jnp.pad(jnp.concatenate([b_group[0], b_router[0]]),
                   (0, LANES - N_GROUPS - N_EXPERTS)).reshape(1, LANES)

    proj, glr = _in_proj(x2, norm_mix_g[0].reshape(1, d), w_main, w_lr)
    o_gla = _gla(proj, glr, wup, b_gk[0].reshape(1, GLA_QK_W),
                 gla_norm_g[0].reshape(1, GLA_V_W), batch, seq)
    o_swa = _swa(proj, swa_sinks[0], batch, seq)
    h_mid, hn, route, tile_counts = _out_route(o_gla, o_swa, x2, w_out[0].astype(BF16),
                                               norm_ffn_g[0].reshape(1, d), w_rt, b_rt)
    tile_expert, num_tiles, tok_ids, pos2d = _dispatch_plan(route, tile_counts, m)
    y = _moe(hn, w_gate[0], w_up[0], w_down[0], tile_expert, num_tiles, tok_ids)
    out = _combine(y, h_mid, route, norm_final_g.reshape(1, d), pos2d)
    return out.reshape(batch, seq, d)
```

```python
import functools

import jax
import jax.numpy as jnp
from jax import lax
from jax.experimental import pallas as pl
from jax.experimental.pallas import tpu as pltpu

F32 = jnp.float32
BF16 = jnp.bfloat16

D_MODEL = 2048
GLA_HEADS = 4
GLA_DK = 128
GLA_DV = 256
GLA_RANK = 16
GLA_GATE_NORM = 16.0
GLA_CHUNK = 64
SWA_Q_HEADS = 16
SWA_KV_HEADS = 4
SWA_GROUP = SWA_Q_HEADS // SWA_KV_HEADS
SWA_HEAD_DIM = 64
SWA_BLOCK = 128
N_GROUPS = 4
EXPERTS_PER_GROUP = 16
N_EXPERTS = N_GROUPS * EXPERTS_PER_GROUP
D_EXPERT = 256
RMS_EPS = 1e-6

GLA_QK_W = GLA_HEADS * GLA_DK
GLA_V_W = GLA_HEADS * GLA_DV
SWA_Q_W = SWA_Q_HEADS * SWA_HEAD_DIM
SWA_KV_W = SWA_KV_HEADS * SWA_HEAD_DIM
OFF_GQ = 0
OFF_GK = OFF_GQ + GLA_QK_W
OFF_GV = OFF_GK + GLA_QK_W
OFF_GR = OFF_GV + GLA_V_W
OFF_SQ = OFF_GR + GLA_V_W
OFF_SK = OFF_SQ + SWA_Q_W
OFF_SV = OFF_SK + SWA_KV_W
PROJ_W = OFF_SV + SWA_KV_W
LANES = 128

IN_TM = 1024
IN_TN = 768
OUT_TM = 256
MOE_TR = 256
CMB_TM = 256
VMEM_LIMIT = 56 * 1024 * 1024


def _silu(x):
    return x / (1.0 + jnp.exp(-x))


def _inproj_kernel(x_ref, g_ref, w_ref, wlr_ref, proj_ref, glr_ref, xn_ref):
    @pl.when(pl.program_id(1) == 0)
    def _():
        x = x_ref[...]
        ms = jnp.mean(x * x, axis=-1, keepdims=True)
        xn = (x * lax.rsqrt(ms + RMS_EPS) * g_ref[...]).astype(BF16)
        xn_ref[...] = xn
        glr_ref[...] = jnp.dot(xn, wlr_ref[...], preferred_element_type=F32)

    proj_ref[...] = jnp.dot(xn_ref[...], w_ref[...], preferred_element_type=F32).astype(BF16)


def _in_proj(x2, g, w_main, w_lr):
    m = x2.shape[0]
    tm = min(IN_TM, m)
    return pl.pallas_call(
        _inproj_kernel,
        grid=(m // tm, PROJ_W // IN_TN),
        in_specs=[
            pl.BlockSpec((tm, D_MODEL), lambda i, j: (i, 0)),
            pl.BlockSpec((1, D_MODEL), lambda i, j: (0, 0)),
            pl.BlockSpec((D_MODEL, IN_TN), lambda i, j: (0, j)),
            pl.BlockSpec((D_MODEL, LANES), lambda i, j: (0, 0)),
        ],
        out_specs=[
            pl.BlockSpec((tm, IN_TN), lambda i, j: (i, j)),
            pl.BlockSpec((tm, LANES), lambda i, j: (i, 0)),
        ],
        out_shape=[
            jax.ShapeDtypeStruct((m, PROJ_W), BF16),
            jax.ShapeDtypeStruct((m, LANES), F32),
        ],
        scratch_shapes=[pltpu.VMEM((tm, D_MODEL), BF16)],
        compiler_params=pltpu.CompilerParams(
            dimension_semantics=("arbitrary", "arbitrary"), vmem_limit_bytes=VMEM_LIMIT),
        name="in_proj",
    )(x2, g, w_main, w_lr)


def _gla_kernel(q_ref, k_ref, v_ref, r_ref, glr_ref, wup_ref, bgk_ref, ng_ref, o_ref,
                la_ref, oi_ref, qi_ref, ki_ref, qd_ref, kd_ref, a_ref, kv_ref, dec_ref, sp_ref):
    t = q_ref.shape[0]
    c = GLA_CHUNK
    nchunk = t // c
    z = jnp.dot(glr_ref[...].astype(BF16), wup_ref[...], preferred_element_type=F32) + bgk_ref[...]
    la_ref[...] = (jnp.minimum(z, 0.0) - jnp.log1p(jnp.exp(-jnp.abs(z)))) * (1.0 / GLA_GATE_NORM)

    ii = lax.broadcasted_iota(jnp.int32, (c, c), 0)
    jj = lax.broadcasted_iota(jnp.int32, (c, c), 1)
    causal = jj <= ii
    tri = causal.astype(BF16)
    nt = (((1,), (1,)), ((), ()))
    tn = (((0,), (0,)), ((), ()))

    def chunk_rows(n):
        return pl.ds(pl.multiple_of(n * c, c), c)

    def decays(n, carry):
        rows = chunk_rows(n)
        la = la_ref[rows, :]
        hi = la.astype(BF16)
        r1 = la - hi.astype(F32)
        mid = r1.astype(BF16)
        lo = (r1 - mid.astype(F32)).astype(BF16)
        parts = jnp.dot(tri, jnp.concatenate([hi, mid, lo], axis=1), preferred_element_type=F32)
        bcum = parts[:, 0:GLA_DK] + parts[:, GLA_DK:2 * GLA_DK] + parts[:, 2 * GLA_DK:]
        b_mid = bcum[c // 2 - 1:c // 2, :]
        b_last = bcum[c - 1:c, :]
        q = q_ref[rows, :].astype(F32) * (GLA_DK ** -0.5)
        k = k_ref[rows, :].astype(F32)
        qi_ref[rows, :] = (q * jnp.exp(bcum - b_mid)).astype(BF16)
        ki_ref[rows, :] = (k * jnp.exp(b_mid - bcum)).astype(BF16)
        qd_ref[rows, :] = (q * jnp.exp(bcum)).astype(BF16)
        kd_ref[rows, :] = (k * jnp.exp(b_last - bcum)).astype(BF16)
        dec_ref[n] = jnp.broadcast_to(jnp.exp(b_last), dec_ref.shape[1:])
        return carry

    lax.fori_loop(0, nchunk, decays, 0, unroll=4)

    def scores(n, carry):
        rows = chunk_rows(n)
        a = lax.dot_general(qi_ref[rows, :], ki_ref[rows, :], nt, preferred_element_type=F32)
        a_ref[rows, :] = jnp.where(causal, a, 0.0).astype(BF16)
        return carry

    lax.fori_loop(0, nchunk, scores, 0, unroll=8)

    def intra(n, carry):
        rows = chunk_rows(n)
        v = v_ref[rows, :]
        oi_ref[rows, :] = jnp.dot(a_ref[rows, :], v, preferred_element_type=F32)
        kv_ref[n] = lax.dot_general(v, kd_ref[rows, :], tn, preferred_element_type=F32)
        return carry

    lax.fori_loop(0, nchunk, intra, 0, unroll=4)

    def scan(n, s_t):
        sp_ref[n] = s_t.astype(BF16)
        return s_t * dec_ref[n][0:1, :] + kv_ref[n]

    lax.fori_loop(0, nchunk, scan, jnp.zeros((GLA_DV, GLA_DK), F32))

    def inter(n, carry):
        rows = chunk_rows(n)
        o = oi_ref[rows, :] + lax.dot_general(qd_ref[rows, :], sp_ref[n], nt,
                                              preferred_element_type=F32)
        ms = jnp.mean(o * o, axis=-1, keepdims=True)
        o = o * lax.rsqrt(ms + RMS_EPS) * ng_ref[...]
        o = o * _silu(r_ref[rows, :].astype(F32))
        o_ref[rows, :] = o.astype(BF16)
        return carry

    lax.fori_loop(0, nchunk, inter, 0, unroll=4)


def _gla(proj, glr, wup, bgk, ng, batch, seq):
    m = proj.shape[0]
    return pl.pallas_call(
        _gla_kernel,
        grid=(batch, GLA_HEADS),
        in_specs=[
            pl.BlockSpec((seq, GLA_DK), lambda b, h: (b, OFF_GQ // GLA_DK + h)),
            pl.BlockSpec((seq, GLA_DK), lambda b, h: (b, OFF_GK // GLA_DK + h)),
            pl.BlockSpec((seq, GLA_DV), lambda b, h: (b, OFF_GV // GLA_DV + h)),
            pl.BlockSpec((seq, GLA_DV), lambda b, h: (b, OFF_GR // GLA_DV + h)),
            pl.BlockSpec((seq, LANES), lambda b, h: (b, 0)),
            pl.BlockSpec((LANES, GLA_DK), lambda b, h: (0, h)),
            pl.BlockSpec((1, GLA_DK), lambda b, h: (0, h)),
            pl.BlockSpec((1, GLA_DV), lambda b, h: (0, h)),
        ],
        out_specs=pl.BlockSpec((seq, GLA_DV), lambda b, h: (b, h)),
        out_shape=jax.ShapeDtypeStruct((m, GLA_V_W), BF16),
        scratch_shapes=[
            pltpu.VMEM((seq, GLA_DK), F32),
            pltpu.VMEM((seq, GLA_DV), F32),
            pltpu.VMEM((seq, GLA_DK), BF16),
            pltpu.VMEM((seq, GLA_DK), BF16),
            pltpu.VMEM((seq, GLA_DK), BF16),
            pltpu.VMEM((seq, GLA_DK), BF16),
            pltpu.VMEM((seq, GLA_CHUNK), BF16),
            pltpu.VMEM((seq // GLA_CHUNK, GLA_DV, GLA_DK), F32),
            pltpu.VMEM((seq // GLA_CHUNK, 8, GLA_DK), F32),
            pltpu.VMEM((seq // GLA_CHUNK, GLA_DV, GLA_DK), BF16),
        ],
        compiler_params=pltpu.CompilerParams(
            dimension_semantics=("arbitrary", "arbitrary"), vmem_limit_bytes=VMEM_LIMIT),
        name="gla",
    )(proj, proj, proj, proj, glr, wup, bgk, ng)


def _swa_kernel(sink_ref, q_ref, kp_ref, kc_ref, vp_ref, vc_ref, o_ref):
    blk = SWA_BLOCK
    half = SWA_HEAD_DIM
    n = pl.program_id(1)
    k_all = jnp.concatenate([kp_ref[...], kc_ref[...]], axis=0)
    v_all = jnp.concatenate([vp_ref[...], vc_ref[...]], axis=0)
    qi = lax.broadcasted_iota(jnp.int32, (blk, 2 * blk), 0)
    kj = lax.broadcasted_iota(jnp.int32, (blk, 2 * blk), 1)
    rel = qi + blk - kj
    valid = (rel >= 0) & (rel < blk) & ((kj >= blk) | (n > 0))
    relf = rel.astype(F32)
    lane_kv = lax.broadcasted_iota(jnp.int32, (2 * blk, LANES), 1)
    lane_q = lax.broadcasted_iota(jnp.int32, (blk, LANES), 1)
    nt = (((1,), (1,)), ((), ()))

    def dup_half(slab, rolled, hh):
        lo = lane_kv < half
        if hh == 0:
            return jnp.where(lo, slab, rolled).astype(BF16)
        return jnp.where(lo, rolled, slab).astype(BF16)

    for p in range(SWA_KV_HEADS // 2):
        k_slab = k_all[:, p * LANES:(p + 1) * LANES].astype(F32)
        v_slab = v_all[:, p * LANES:(p + 1) * LANES].astype(F32)
        k_roll = pltpu.roll(k_slab, half, 1)
        v_roll = pltpu.roll(v_slab, half, 1)
        for hh in range(2):
            h = 2 * p + hh
            kd = dup_half(k_slab, k_roll, hh)
            vd = dup_half(v_slab, v_roll, hh)
            for gp in range(SWA_GROUP // 2):
                col = (h * (SWA_GROUP // 2) + gp) * LANES
                qs = q_ref[:, col:col + LANES]
                outs = []
                for gg in range(2):
                    head = h * SWA_GROUP + 2 * gp + gg
                    keep = (lane_q < half) if gg == 0 else (lane_q >= half)
                    qm = jnp.where(keep, qs, jnp.zeros_like(qs))
                    s = lax.dot_general(qm, kd, nt, preferred_element_type=F32) * (SWA_HEAD_DIM ** -0.5)
                    slope = 2.0 ** (-8.0 * (head + 1) / SWA_Q_HEADS)
                    s = s - slope * relf
                    s = jnp.where(valid, s, -jnp.inf)
                    sink = sink_ref[head]
                    mx = jnp.maximum(jnp.max(s, axis=-1, keepdims=True), sink)
                    pe = jnp.exp(s - mx)
                    den = jnp.sum(pe, axis=-1, keepdims=True) + jnp.exp(sink - mx)
                    o = jnp.dot(pe.astype(BF16), vd, preferred_element_type=F32)
                    outs.append(o / den)
                o_ref[:, col:col + LANES] = jnp.where(lane_q < half, outs[0], outs[1]).astype(BF16)


def _swa(proj, sinks, batch, seq):
    m = proj.shape[0]
    nb = seq // SWA_BLOCK
    qcol = OFF_SQ // SWA_Q_W
    kcol = OFF_SK // SWA_KV_W
    vcol = OFF_SV // SWA_KV_W
    cur = lambda c: (lambda b, n: (b * nb + n, c))
    prev = lambda c: (lambda b, n: (b * nb + jnp.maximum(n - 1, 0), c))
    return pl.pallas_call(
        _swa_kernel,
        grid=(batch, nb),
        in_specs=[
            pl.BlockSpec(memory_space=pltpu.SMEM),
            pl.BlockSpec((SWA_BLOCK, SWA_Q_W), cur(qcol)),
            pl.BlockSpec((SWA_BLOCK, SWA_KV_W), prev(kcol)),
            pl.BlockSpec((SWA_BLOCK, SWA_KV_W), cur(kcol)),
            pl.BlockSpec((SWA_BLOCK, SWA_KV_W), prev(vcol)),
            pl.BlockSpec((SWA_BLOCK, SWA_KV_W), cur(vcol)),
        ],
        out_specs=pl.BlockSpec((SWA_BLOCK, SWA_Q_W), lambda b, n: (b * nb + n, 0)),
        out_shape=jax.ShapeDtypeStruct((m, SWA_Q_W), BF16),
        compiler_params=pltpu.CompilerParams(
            dimension_semantics=("arbitrary", "arbitrary"), vmem_limit_bytes=VMEM_LIMIT),
        name="swa",
    )(sinks, proj, proj, proj, proj, proj)


def _out_route_kernel(og_ref, os_ref, x_ref, wo_ref, g_ref, wr_ref, br_ref,
                      h_ref, hn_ref, route_ref, cnt_ref):
    h = x_ref[...]
    h = h + jnp.dot(og_ref[...], wo_ref[0:GLA_V_W, :], preferred_element_type=F32)
    h = h + jnp.dot(os_ref[...], wo_ref[GLA_V_W:, :], preferred_element_type=F32)
    h_ref[...] = h
    ms = jnp.mean(h * h, axis=-1, keepdims=True)
    hn = h * lax.rsqrt(ms + RMS_EPS) * g_ref[...]
    hn_ref[...] = hn
    logits = jnp.dot(hn.astype(BF16), wr_ref[...], preferred_element_type=F32) + br_ref[...]

    lane = lax.broadcasted_iota(jnp.int32, logits.shape, 1)
    lanef = lane.astype(F32)
    big = float(LANES)
    ninf = -jnp.inf
    gl = jnp.where(lane < N_GROUPS, logits, ninf)
    gmax = jnp.max(gl, axis=-1, keepdims=True)
    g_p = 1.0 / jnp.sum(jnp.exp(gl - gmax), axis=-1, keepdims=True)
    g_idx = jnp.min(jnp.where(gl == gmax, lanef, big), axis=-1, keepdims=True)
    lo = N_GROUPS + EXPERTS_PER_GROUP * g_idx
    el = jnp.where((lanef >= lo) & (lanef < lo + EXPERTS_PER_GROUP), logits, ninf)
    m1 = jnp.max(el, axis=-1, keepdims=True)
    i1 = jnp.min(jnp.where(el == m1, lanef, big), axis=-1, keepdims=True)
    el2 = jnp.where(lanef == i1, ninf, el)
    m2 = jnp.max(el2, axis=-1, keepdims=True)
    i2 = jnp.min(jnp.where(el2 == m2, lanef, big), axis=-1, keepdims=True)
    d = jnp.exp(m2 - m1)
    c1 = g_p / (1.0 + d)
    c2 = g_p * d / (1.0 + d)
    tm = logits.shape[0]
    chosen = ((lanef == i1) | (lanef == i2)).astype(BF16)
    ri = lax.broadcasted_iota(jnp.int32, (tm, tm), 0)
    ci = lax.broadcasted_iota(jnp.int32, (tm, tm), 1)
    earlier = jnp.dot((ci < ri).astype(BF16), chosen, preferred_element_type=F32)
    r1 = jnp.sum(jnp.where(lanef == i1, earlier, 0.0), axis=-1, keepdims=True)
    r2 = jnp.sum(jnp.where(lanef == i2, earlier, 0.0), axis=-1, keepdims=True)
    cnt = jnp.sum(chosen.astype(F32), axis=0, keepdims=True)
    cnt_ref[...] = jnp.broadcast_to(cnt, cnt_ref.shape)

    fields = [i1 - N_GROUPS, i2 - N_GROUPS, c1, c2, r1, r2]
    route = jnp.zeros_like(logits)
    for idx, val in enumerate(fields):
        route = jnp.where(lane == idx, val, route)
    route_ref[...] = route


def _out_route(o_gla, o_swa, x2, w_out, g, w_rt, b_rt):
    m = x2.shape[0]
    tm = min(OUT_TM, m)
    row = lambda i: (i, 0)
    fixed = lambda i: (0, 0)
    return pl.pallas_call(
        _out_route_kernel,
        grid=(m // tm,),
        in_specs=[
            pl.BlockSpec((tm, GLA_V_W), row),
            pl.BlockSpec((tm, SWA_Q_W), row),
            pl.BlockSpec((tm, D_MODEL), row),
            pl.BlockSpec((GLA_V_W + SWA_Q_W, D_MODEL), fixed),
            pl.BlockSpec((1, D_MODEL), fixed),
            pl.BlockSpec((D_MODEL, LANES), fixed),
            pl.BlockSpec((1, LANES), fixed),
        ],
        out_specs=[
            pl.BlockSpec((tm, D_MODEL), row),
            pl.BlockSpec((tm, D_MODEL), row),
            pl.BlockSpec((tm, LANES), row),
            pl.BlockSpec((1, 8, LANES), lambda i: (i, 0, 0)),
        ],
        out_shape=[
            jax.ShapeDtypeStruct((m, D_MODEL), F32),
            jax.ShapeDtypeStruct((m, D_MODEL), F32),
            jax.ShapeDtypeStruct((m, LANES), F32),
            jax.ShapeDtypeStruct((m // tm, 8, LANES), F32),
        ],
        compiler_params=pltpu.CompilerParams(
            dimension_semantics=("arbitrary",), vmem_limit_bytes=VMEM_LIMIT),
        name="out_route",
    )(o_gla, o_swa, x2, w_out, g, w_rt, b_rt)


LANES_LOG2 = 7
MOE_ROW_GROUP_LOG2 = 3
MOE_ROW_GROUP = 1 << MOE_ROW_GROUP_LOG2
IDS_WINDOW_ROWS = 3


def _moe_kernel(te_ref, cb_ref, nv_ref, ne_ref, nt_ref, tok_hbm, hn_hbm, wg_hbm, wu_hbm, wd_hbm, y_ref,
                xbuf, gsem, ids_smem, isem, wgs, wus, wds, wsem, wgb, wub, wdb):
    tr = MOE_TR
    grp = MOE_ROW_GROUP
    t = pl.program_id(0)
    nt = nt_ref[0]
    gs = lax.rem(t, 2)
    cs = 1 - gs

    def ids_copy(tile, s):
        row0 = lax.shift_right_logical(cb_ref[tile], LANES_LOG2)
        return pltpu.make_async_copy(tok_hbm.at[pl.ds(row0, IDS_WINDOW_ROWS)], ids_smem.at[s], isem.at[s])

    def weight_copies(e):
        return (pltpu.make_async_copy(wg_hbm.at[e], wgs, wsem.at[0]),
                pltpu.make_async_copy(wu_hbm.at[e], wus, wsem.at[1]),
                pltpu.make_async_copy(wd_hbm.at[e], wds, wsem.at[2]))

    def rows_used(tile):
        groups = lax.shift_right_logical(nv_ref[tile] + (grp - 1), MOE_ROW_GROUP_LOG2)
        return lax.shift_left(groups, MOE_ROW_GROUP_LOG2)

    @pl.when(t == 0)
    def _():
        xbuf[...] = jnp.zeros_like(xbuf)
        ids_copy(0, 0).start()
        for cp in weight_copies(te_ref[0]):
            cp.start()

    @pl.when(t < nt)
    def _():
        ids_copy(t, gs).wait()

        @pl.when(t + 1 < nt)
        def _():
            ids_copy(t + 1, cs).start()

        off = cb_ref[t] & (LANES - 1)
        n_real = nv_ref[t]
        for g0 in range(0, tr, grp):
            @pl.when(g0 < n_real)
            def _():
                for r in range(g0, g0 + grp):
                    q = off + r
                    tok = ids_smem[gs, lax.shift_right_logical(q, LANES_LOG2), q & (LANES - 1)]
                    pltpu.make_async_copy(hn_hbm.at[pl.ds(tok, 1)], xbuf.at[gs, pl.ds(r, 1)],
                                          gsem.at[gs]).start(priority=r % 2)

    @pl.when((t >= 1) & (t <= nt))
    def _():
        c = t - 1
        changed = (c == 0) | (te_ref[c] != te_ref[jnp.maximum(c - 1, 0)])

        @pl.when(changed)
        def _():
            for cp in weight_copies(te_ref[c]):
                cp.wait()
            wgb[...] = wgs[...].astype(BF16)
            wub[...] = wus[...].astype(BF16)
            wdb[...] = wds[...].astype(BF16)
            nxt = ne_ref[c]

            @pl.when(nxt >= 0)
            def _():
                for cp in weight_copies(nxt):
                    cp.start()

        n_rows = pl.multiple_of(rows_used(c), grp)
        pltpu.make_async_copy(hn_hbm.at[pl.ds(0, n_rows)], xbuf.at[cs, pl.ds(0, n_rows)],
                              gsem.at[cs]).wait()
        x = xbuf[cs].astype(BF16)
        g = jnp.dot(x, wgb[...], preferred_element_type=F32)
        u = jnp.dot(x, wub[...], preferred_element_type=F32)
        hmid = (_silu(g) * u).astype(BF16)
        y_ref[...] = jnp.dot(hmid, wdb[...], preferred_element_type=F32)

    @pl.when(t > nt)
    def _():
        y_ref[...] = jnp.zeros_like(y_ref)


def _moe(hn, w_gate, w_up, w_down, plan):
    n_tiles = plan["tile_expert"].shape[0]
    p_rows = n_tiles * MOE_TR
    prev = lambda t: jnp.maximum(t - 1, 0)
    any_spec = pl.BlockSpec(memory_space=pl.ANY)
    grid_spec = pltpu.PrefetchScalarGridSpec(
        num_scalar_prefetch=5,
        grid=(n_tiles + 1,),
        in_specs=[any_spec] * 5,
        out_specs=pl.BlockSpec((MOE_TR, D_MODEL), lambda t, *_: (prev(t), 0)),
        scratch_shapes=[
            pltpu.VMEM((2, MOE_TR, D_MODEL), F32),
            pltpu.SemaphoreType.DMA((2,)),
            pltpu.SMEM((2, IDS_WINDOW_ROWS, LANES), jnp.int32),
            pltpu.SemaphoreType.DMA((2,)),
            pltpu.VMEM((D_MODEL, D_EXPERT), F32),
            pltpu.VMEM((D_MODEL, D_EXPERT), F32),
            pltpu.VMEM((D_EXPERT, D_MODEL), F32),
            pltpu.SemaphoreType.DMA((3,)),
            pltpu.VMEM((D_MODEL, D_EXPERT), BF16),
            pltpu.VMEM((D_MODEL, D_EXPERT), BF16),
            pltpu.VMEM((D_EXPERT, D_MODEL), BF16),
        ],
    )
    return pl.pallas_call(
        _moe_kernel,
        grid_spec=grid_spec,
        out_shape=jax.ShapeDtypeStruct((p_rows, D_MODEL), F32),
        compiler_params=pltpu.CompilerParams(
            dimension_semantics=("arbitrary",), vmem_limit_bytes=VMEM_LIMIT),
        name="moe",
    )(plan["tile_expert"], plan["tile_base"], plan["tile_rows"], plan["next_expert"], plan["num_tiles"],
      plan["sorted_tokens"], hn, w_gate, w_up, w_down)


def _combine_kernel(pos_hbm, y_hbm, h_ref, route_ref, g_ref, o_ref, ybuf, gsem, ids_smem, isem):
    tm = h_ref.shape[0]
    t = pl.program_id(0)
    ntile = pl.num_programs(0) - 1

    def ids_copy(tile, s):
        return pltpu.make_async_copy(pos_hbm.at[tile], ids_smem.at[s], isem.at[s])

    @pl.when(t == 0)
    def _():
        ids_copy(0, 0).start()

    @pl.when(t < ntile)
    def _():
        s = lax.rem(t, 2)
        ids_copy(t, s).wait()

        @pl.when(t + 1 < ntile)
        def _():
            ids_copy(t + 1, 1 - s).start()

        for kk in range(2):
            for r in range(tm):
                flat = kk * tm + r
                row = ids_smem[s, flat // LANES, flat % LANES]
                pltpu.make_async_copy(y_hbm.at[pl.ds(row, 1)], ybuf.at[s, kk, pl.ds(r, 1)],
                                      gsem.at[s, kk]).start(priority=r % 2)

    @pl.when(t >= 1)
    def _():
        s = lax.rem(t - 1, 2)
        for kk in range(2):
            pltpu.make_async_copy(y_hbm.at[pl.ds(0, tm)], ybuf.at[s, kk], gsem.at[s, kk]).wait()
        route = route_ref[...]
        h = h_ref[...] + route[:, 2:3] * ybuf[s, 0] + route[:, 3:4] * ybuf[s, 1]
        ms = jnp.mean(h * h, axis=-1, keepdims=True)
        o_ref[...] = h * lax.rsqrt(ms + RMS_EPS) * g_ref[...]


def _combine(y, h, route, g, pos2d):
    m = h.shape[0]
    tm = min(CMB_TM, m)
    ntile = m // tm
    pos3 = pos2d.reshape(ntile, tm, 2).transpose(0, 2, 1).reshape(ntile, 2 * tm // LANES, LANES)
    row = lambda i: (jnp.maximum(i - 1, 0), 0)
    return pl.pallas_call(
        _combine_kernel,
        grid=(ntile + 1,),
        in_specs=[
            pl.BlockSpec(memory_space=pl.ANY),
            pl.BlockSpec(memory_space=pl.ANY),
            pl.BlockSpec((tm, D_MODEL), row),
            pl.BlockSpec((tm, LANES), row),
            pl.BlockSpec((1, D_MODEL), lambda i: (0, 0)),
        ],
        out_specs=pl.BlockSpec((tm, D_MODEL), row),
        out_shape=jax.ShapeDtypeStruct((m, D_MODEL), F32),
        scratch_shapes=[
            pltpu.VMEM((2, 2, tm, D_MODEL), F32),
            pltpu.SemaphoreType.DMA((2, 2)),
            pltpu.SMEM((2, 2 * tm // LANES, LANES), jnp.int32),
            pltpu.SemaphoreType.DMA((2,)),
        ],
        compiler_params=pltpu.CompilerParams(
            dimension_semantics=("arbitrary",), vmem_limit_bytes=VMEM_LIMIT),
        name="combine",
    )(pos3, y, h, route, g)


def _dispatch_plan(route, tile_counts, m):
    tr = MOE_TR
    p_rows = 2 * m + N_EXPERTS * tr
    n_tiles = p_rows // tr
    n_tok_tiles = tile_counts.shape[0]
    cnt = tile_counts[:, 0, N_GROUPS:N_GROUPS + N_EXPERTS].astype(jnp.int32)
    before_tile = jnp.cumsum(cnt, axis=0) - cnt
    counts = jnp.sum(cnt, axis=0)
    padded = ((counts + tr - 1) // tr) * tr
    ends = jnp.cumsum(padded)
    starts = ends - padded
    base = (starts[None, :] + before_tile).astype(jnp.int32)
    expert = route[:, 0:2].astype(jnp.int32).reshape(n_tok_tiles, m // n_tok_tiles, 2)
    rank = route[:, 4:6].astype(jnp.int32).reshape(n_tok_tiles, m // n_tok_tiles, 2)
    hit = expert[..., None] == jnp.arange(N_EXPERTS, dtype=jnp.int32)
    pos = rank + jnp.sum(jnp.where(hit, base[:, None, None, :], 0), axis=-1)
    pos = pos.reshape(m, 2)
    num_tiles = (ends[-1] // tr).astype(jnp.int32)
    tile_idx = jnp.minimum(jnp.arange(n_tiles, dtype=jnp.int32), num_tiles - 1)
    tile_expert = jnp.sum((tile_idx[:, None] * tr >= ends[None, :]).astype(jnp.int32), axis=1)
    n_slots = 2 * m
    slot = jnp.arange(n_slots, dtype=jnp.int32)
    sorted_key = jnp.sort(route[:, 0:2].astype(jnp.int32).reshape(-1) * n_slots + slot)
    sorted_tokens = (sorted_key & (n_slots - 1)) >> 1
    sorted_tokens = jnp.pad(sorted_tokens, (0, IDS_WINDOW_ROWS * LANES)).reshape(-1, LANES)
    dense_starts = jnp.cumsum(counts) - counts
    in_expert = tile_idx - (starts // tr)[tile_expert]
    tile_base = dense_starts[tile_expert] + in_expert * tr
    tile_rows = jnp.clip(counts[tile_expert] - in_expert * tr, 0, tr)
    tile_rows = jnp.where(jnp.arange(n_tiles) < num_tiles, tile_rows, 0)
    ids = jnp.arange(N_EXPERTS, dtype=jnp.int32)
    present = jnp.where(counts > 0, ids, N_EXPERTS)
    next_ge = lax.cummin(present, axis=0, reverse=True)
    next_gt = jnp.concatenate([next_ge[1:], jnp.full((1,), N_EXPERTS, jnp.int32)])
    next_gt = jnp.where(next_gt >= N_EXPERTS, -1, next_gt)
    plan = dict(tile_expert=tile_expert.astype(jnp.int32), tile_base=tile_base.astype(jnp.int32),
                tile_rows=tile_rows.astype(jnp.int32), next_expert=next_gt[tile_expert].astype(jnp.int32),
                num_tiles=num_tiles.reshape(1), sorted_tokens=sorted_tokens)
    return plan, pos


def kernel(x, norm_mix_g, w_in, w_gk_up, b_gk, gla_norm_g, swa_sinks, w_out, norm_ffn_g,
           w_group, b_group, w_router, b_router, w_gate, w_up, w_down, norm_final_g):
    batch, seq, d = x.shape
    m = batch * seq
    assert w_in.shape[0] == 1, "single-layer block"
    x2 = x.reshape(m, d)
    lr0 = OFF_GR + GLA_V_W
    w_in_b = w_in[0].astype(BF16)
    w_main = jnp.concatenate([w_in_b[:, :lr0], w_in_b[:, lr0 + GLA_RANK:]], axis=1)
    w_lr = jnp.pad(w_in_b[:, lr0:lr0 + GLA_RANK], ((0, 0), (0, LANES - GLA_RANK)))
    wup = jnp.pad(w_gk_up[0], ((0, LANES - GLA_RANK), (0, 0))).astype(BF16)
    w_rt = jnp.pad(jnp.concatenate([w_group[0], w_router[0]], axis=1),
                   ((0, 0), (0, LANES - N_GROUPS - N_EXPERTS))).astype(BF16)
    b_rt = jnp.pad(jnp.concatenate([b_group[0], b_router[0]]),
                   (0, LANES - N_GROUPS - N_EXPERTS)).reshape(1, LANES)

    proj, glr = _in_proj(x2, norm_mix_g[0].reshape(1, d), w_main, w_lr)
    o_gla = _gla(proj, glr, wup, b_gk[0].reshape(1, GLA_QK_W),
                 gla_norm_g[0].reshape(1, GLA_V_W), batch, seq)
    o_swa = _swa(proj, swa_sinks[0], batch, seq)
    h_mid, hn, route, tile_counts = _out_route(o_gla, o_swa, x2, w_out[0].astype(BF16),
                                               norm_ffn_g[0].reshape(1, d), w_rt, b_rt)
    plan, pos2d = _dispatch_plan(route, tile_counts, m)
    y = _moe(hn, w_gate[0], w_up[0], w_down[0], plan)
    out = _combine(y, h_mid, route, norm_final_g.reshape(1, d), pos2d)
    return out.reshape(batch, seq, d)
```

```python
import functools

import jax
import jax.numpy as jnp
import numpy as np
from jax import lax
from jax.experimental import pallas as pl
from jax.experimental.pallas import tpu as pltpu

F32 = jnp.float32
BF16 = jnp.bfloat16

D_MODEL = 2048
GLA_HEADS = 4
GLA_DK = 128
GLA_DV = 256
GLA_RANK = 16
GLA_GATE_NORM = 16.0
GLA_CHUNK = 64
SWA_Q_HEADS = 16
SWA_KV_HEADS = 4
SWA_GROUP = SWA_Q_HEADS // SWA_KV_HEADS
SWA_HEAD_DIM = 64
SWA_BLOCK = 128
N_GROUPS = 4
EXPERTS_PER_GROUP = 16
N_EXPERTS = N_GROUPS * EXPERTS_PER_GROUP
D_EXPERT = 256
RMS_EPS = 1e-6

GLA_QK_W = GLA_HEADS * GLA_DK
GLA_V_W = GLA_HEADS * GLA_DV
SWA_Q_W = SWA_Q_HEADS * SWA_HEAD_DIM
SWA_KV_W = SWA_KV_HEADS * SWA_HEAD_DIM
OFF_GQ = 0
OFF_GK = OFF_GQ + GLA_QK_W
OFF_GV = OFF_GK + GLA_QK_W
OFF_GR = OFF_GV + GLA_V_W
OFF_SQ = OFF_GR + GLA_V_W
OFF_SK = OFF_SQ + SWA_Q_W
OFF_SV = OFF_SK + SWA_KV_W
PROJ_W = OFF_SV + SWA_KV_W
LANES = 128

IN_TM = 1024
IN_TN = 768
OUT_TM = 256
MOE_TR = 256
CMB_TM = 256
VMEM_LIMIT = 56 * 1024 * 1024


def _silu(x):
    return x / (1.0 + jnp.exp(-x))


ALIBI_PARTS = 3


def _bf16_parts(value, n):
    parts, rest = [], np.float32(value)
    for _ in range(n):
        piece = np.float32(np.asarray(rest).astype(jnp.bfloat16))
        parts.append(float(piece))
        rest = np.float32(rest - piece)
    return parts


def _inproj_kernel(x_ref, g_ref, w_ref, wlr_ref, proj_ref, glr_ref, xn_ref):
    @pl.when(pl.program_id(1) == 0)
    def _():
        x = x_ref[...]
        ms = jnp.mean(x * x, axis=-1, keepdims=True)
        xn = (x * lax.rsqrt(ms + RMS_EPS) * g_ref[...]).astype(BF16)
        xn_ref[...] = xn
        glr_ref[...] = jnp.dot(xn, wlr_ref[...], preferred_element_type=F32)

    proj_ref[...] = jnp.dot(xn_ref[...], w_ref[...], preferred_element_type=F32).astype(BF16)


def _in_proj(x2, g, w_main, w_lr):
    m = x2.shape[0]
    tm = min(IN_TM, m)
    return pl.pallas_call(
        _inproj_kernel,
        grid=(m // tm, PROJ_W // IN_TN),
        in_specs=[
            pl.BlockSpec((tm, D_MODEL), lambda i, j: (i, 0)),
            pl.BlockSpec((1, D_MODEL), lambda i, j: (0, 0)),
            pl.BlockSpec((D_MODEL, IN_TN), lambda i, j: (0, j)),
            pl.BlockSpec((D_MODEL, LANES), lambda i, j: (0, 0)),
        ],
        out_specs=[
            pl.BlockSpec((tm, IN_TN), lambda i, j: (i, j)),
            pl.BlockSpec((tm, LANES), lambda i, j: (i, 0)),
        ],
        out_shape=[
            jax.ShapeDtypeStruct((m, PROJ_W), BF16),
            jax.ShapeDtypeStruct((m, LANES), F32),
        ],
        scratch_shapes=[pltpu.VMEM((tm, D_MODEL), BF16)],
        compiler_params=pltpu.CompilerParams(
            dimension_semantics=("arbitrary", "arbitrary"), vmem_limit_bytes=VMEM_LIMIT),
        name="in_proj",
    )(x2, g, w_main, w_lr)


def _gla_kernel(q_ref, k_ref, v_ref, r_ref, glr_ref, wup_ref, bgk_ref, ng_ref, o_ref,
                la_ref, oi_ref, qi_ref, ki_ref, qd_ref, kd_ref, a_ref, kv_ref, dec_ref, sp_ref):
    t = q_ref.shape[0]
    c = GLA_CHUNK
    nchunk = t // c
    z = jnp.dot(glr_ref[...].astype(BF16), wup_ref[...], preferred_element_type=F32) + bgk_ref[...]
    la_ref[...] = (jnp.minimum(z, 0.0) - jnp.log1p(jnp.exp(-jnp.abs(z)))) * (1.0 / GLA_GATE_NORM)

    ii = lax.broadcasted_iota(jnp.int32, (c, c), 0)
    jj = lax.broadcasted_iota(jnp.int32, (c, c), 1)
    causal = jj <= ii
    tri = causal.astype(BF16)
    nt = (((1,), (1,)), ((), ()))
    tn = (((0,), (0,)), ((), ()))

    def chunk_rows(n):
        return pl.ds(pl.multiple_of(n * c, c), c)

    def decays(n, carry):
        rows = chunk_rows(n)
        la = la_ref[rows, :]
        hi = la.astype(BF16)
        r1 = la - hi.astype(F32)
        mid = r1.astype(BF16)
        lo = (r1 - mid.astype(F32)).astype(BF16)
        parts = jnp.dot(tri, jnp.concatenate([hi, mid, lo], axis=1), preferred_element_type=F32)
        bcum = parts[:, 0:GLA_DK] + parts[:, GLA_DK:2 * GLA_DK] + parts[:, 2 * GLA_DK:]
        b_mid = bcum[c // 2 - 1:c // 2, :]
        b_last = bcum[c - 1:c, :]
        q = q_ref[rows, :].astype(F32) * (GLA_DK ** -0.5)
        k = k_ref[rows, :].astype(F32)
        qi_ref[rows, :] = (q * jnp.exp(bcum - b_mid)).astype(BF16)
        ki_ref[rows, :] = (k * jnp.exp(b_mid - bcum)).astype(BF16)
        qd_ref[rows, :] = (q * jnp.exp(bcum)).astype(BF16)
        kd_ref[rows, :] = (k * jnp.exp(b_last - bcum)).astype(BF16)
        dec_ref[n] = jnp.broadcast_to(jnp.exp(b_last), dec_ref.shape[1:])
        return carry

    lax.fori_loop(0, nchunk, decays, 0, unroll=4)

    def scores(n, carry):
        rows = chunk_rows(n)
        a = lax.dot_general(qi_ref[rows, :], ki_ref[rows, :], nt, preferred_element_type=F32)
        a_ref[rows, :] = jnp.where(causal, a, 0.0).astype(BF16)
        return carry

    lax.fori_loop(0, nchunk, scores, 0, unroll=8)

    def intra(n, carry):
        rows = chunk_rows(n)
        v = v_ref[rows, :]
        oi_ref[rows, :] = jnp.dot(a_ref[rows, :], v, preferred_element_type=F32)
        kv_ref[n] = lax.dot_general(v, kd_ref[rows, :], tn, preferred_element_type=F32)
        return carry

    lax.fori_loop(0, nchunk, intra, 0, unroll=8)

    def scan(n, s_t):
        sp_ref[n] = s_t.astype(BF16)
        return s_t * dec_ref[n][0:1, :] + kv_ref[n]

    lax.fori_loop(0, nchunk, scan, jnp.zeros((GLA_DV, GLA_DK), F32))

    def inter(n, carry):
        rows = chunk_rows(n)
        o = oi_ref[rows, :] + lax.dot_general(qd_ref[rows, :], sp_ref[n], nt,
                                              preferred_element_type=F32)
        ms = jnp.mean(o * o, axis=-1, keepdims=True)
        o = o * lax.rsqrt(ms + RMS_EPS) * ng_ref[...]
        o = o * _silu(r_ref[rows, :].astype(F32))
        o_ref[rows, :] = o.astype(BF16)
        return carry

    lax.fori_loop(0, nchunk, inter, 0, unroll=8)


def _gla(proj, glr, wup, bgk, ng, batch, seq):
    m = proj.shape[0]
    return pl.pallas_call(
        _gla_kernel,
        grid=(batch, GLA_HEADS),
        in_specs=[
            pl.BlockSpec((seq, GLA_DK), lambda b, h: (b, OFF_GQ // GLA_DK + h)),
            pl.BlockSpec((seq, GLA_DK), lambda b, h: (b, OFF_GK // GLA_DK + h)),
            pl.BlockSpec((seq, GLA_DV), lambda b, h: (b, OFF_GV // GLA_DV + h)),
            pl.BlockSpec((seq, GLA_DV), lambda b, h: (b, OFF_GR // GLA_DV + h)),
            pl.BlockSpec((seq, LANES), lambda b, h: (b, 0)),
            pl.BlockSpec((LANES, GLA_DK), lambda b, h: (0, h)),
            pl.BlockSpec((1, GLA_DK), lambda b, h: (0, h)),
            pl.BlockSpec((1, GLA_DV), lambda b, h: (0, h)),
        ],
        out_specs=pl.BlockSpec((seq, GLA_DV), lambda b, h: (b, h)),
        out_shape=jax.ShapeDtypeStruct((m, GLA_V_W), BF16),
        scratch_shapes=[
            pltpu.VMEM((seq, GLA_DK), F32),
            pltpu.VMEM((seq, GLA_DV), F32),
            pltpu.VMEM((seq, GLA_DK), BF16),
            pltpu.VMEM((seq, GLA_DK), BF16),
            pltpu.VMEM((seq, GLA_DK), BF16),
            pltpu.VMEM((seq, GLA_DK), BF16),
            pltpu.VMEM((seq, GLA_CHUNK), BF16),
            pltpu.VMEM((seq // GLA_CHUNK, GLA_DV, GLA_DK), F32),
            pltpu.VMEM((seq // GLA_CHUNK, 8, GLA_DK), F32),
            pltpu.VMEM((seq // GLA_CHUNK, GLA_DV, GLA_DK), BF16),
        ],
        compiler_params=pltpu.CompilerParams(
            dimension_semantics=("arbitrary", "arbitrary"), vmem_limit_bytes=VMEM_LIMIT),
        name="gla",
    )(proj, proj, proj, proj, glr, wup, bgk, ng)


def _swa_kernel(sink_ref, q_ref, kp_ref, kc_ref, vp_ref, vc_ref, o_ref):
    blk = SWA_BLOCK
    half = SWA_HEAD_DIM
    n = pl.program_id(1)
    k_all = jnp.concatenate([kp_ref[...], kc_ref[...]], axis=0)
    v_all = jnp.concatenate([vp_ref[...], vc_ref[...]], axis=0)
    qi = lax.broadcasted_iota(jnp.int32, (blk, 2 * blk), 0)
    kj = lax.broadcasted_iota(jnp.int32, (blk, 2 * blk), 1)
    rel = qi + blk - kj
    valid = (rel >= 0) & (rel < blk) & ((kj >= blk) | (n > 0))
    sink_col = kj == 0
    lane_kv = lax.broadcasted_iota(jnp.int32, (2 * blk, LANES), 1)
    lo_kv = lane_kv < half
    lane_q = lax.broadcasted_iota(jnp.int32, (blk, LANES), 1)
    lo_q = lane_q < half
    nt = (((1,), (1,)), ((), ()))
    key_idx = lax.broadcasted_iota(jnp.int32, (2 * blk, LANES), 0).astype(F32)
    key_cols = jnp.where((lane_kv & (half - 1)) < ALIBI_PARTS, key_idx, 0.0)
    q_pos = lax.broadcasted_iota(jnp.int32, (blk, 1), 0).astype(F32) + float(blk)
    q_scale = jnp.asarray(SWA_HEAD_DIM ** -0.5, BF16)

    for p in range(SWA_KV_HEADS // 2):
        k_slab = k_all[:, p * LANES:(p + 1) * LANES].astype(F32)
        v_slab = v_all[:, p * LANES:(p + 1) * LANES].astype(F32)
        k_roll = pltpu.roll(k_slab, half, 1)
        v_roll = pltpu.roll(v_slab, half, 1)
        for hh in range(2):
            h = 2 * p + hh
            k_lo, k_hi = (k_slab, k_roll) if hh == 0 else (k_roll, k_slab)
            kd = (jnp.where(lo_kv, k_lo, key_cols).astype(BF16),
                  jnp.where(lo_kv, key_cols, k_hi).astype(BF16))
            v_lo, v_hi = (v_slab, v_roll) if hh == 0 else (v_roll, v_slab)
            vd = jnp.where(lo_kv, v_lo, v_hi)
            vd = jnp.where(key_idx == 0.0, 0.0, vd).astype(BF16)
            for gp in range(SWA_GROUP // 2):
                col = (h * (SWA_GROUP // 2) + gp) * LANES
                qs = q_ref[:, col:col + LANES] * q_scale
                outs = []
                for gg in range(2):
                    head = h * SWA_GROUP + 2 * gp + gg
                    parts = _bf16_parts(2.0 ** (-8.0 * (head + 1) / SWA_Q_HEADS), ALIBI_PARTS)
                    slope = sum(parts)
                    aug = jnp.zeros((blk, LANES), F32)
                    for idx, part in enumerate(parts):
                        aug = jnp.where(lane_q == (half if gg == 0 else 0) + idx, part, aug)
                    keep = lo_q if gg == 0 else jnp.logical_not(lo_q)
                    qm = jnp.where(keep, qs, aug.astype(BF16))
                    s = lax.dot_general(qm, kd[gg], nt, preferred_element_type=F32)
                    sink = sink_ref[head] + slope * q_pos
                    s = jnp.where(sink_col, sink, jnp.where(valid, s, -jnp.inf))
                    mx = jnp.max(s, axis=-1, keepdims=True)
                    pe = jnp.exp(s - mx)
                    den = jnp.sum(pe, axis=-1, keepdims=True)
                    o = jnp.dot(pe.astype(BF16), vd, preferred_element_type=F32)
                    outs.append(o / den)
                o_ref[:, col:col + LANES] = jnp.where(lo_q, outs[0], outs[1]).astype(BF16)


def _swa(proj, sinks, batch, seq):
    m = proj.shape[0]
    nb = seq // SWA_BLOCK
    qcol = OFF_SQ // SWA_Q_W
    kcol = OFF_SK // SWA_KV_W
    vcol = OFF_SV // SWA_KV_W
    cur = lambda c: (lambda b, n: (b * nb + n, c))
    prev = lambda c: (lambda b, n: (b * nb + jnp.maximum(n - 1, 0), c))
    return pl.pallas_call(
        _swa_kernel,
        grid=(batch, nb),
        in_specs=[
            pl.BlockSpec(memory_space=pltpu.SMEM),
            pl.BlockSpec((SWA_BLOCK, SWA_Q_W), cur(qcol)),
            pl.BlockSpec((SWA_BLOCK, SWA_KV_W), prev(kcol)),
            pl.BlockSpec((SWA_BLOCK, SWA_KV_W), cur(kcol)),
            pl.BlockSpec((SWA_BLOCK, SWA_KV_W), prev(vcol)),
            pl.BlockSpec((SWA_BLOCK, SWA_KV_W), cur(vcol)),
        ],
        out_specs=pl.BlockSpec((SWA_BLOCK, SWA_Q_W), lambda b, n: (b * nb + n, 0)),
        out_shape=jax.ShapeDtypeStruct((m, SWA_Q_W), BF16),
        compiler_params=pltpu.CompilerParams(
            dimension_semantics=("arbitrary", "arbitrary"), vmem_limit_bytes=VMEM_LIMIT),
        name="swa",
    )(sinks, proj, proj, proj, proj, proj)


def _out_route_kernel(og_ref, os_ref, x_ref, wo_ref, g_ref, wr_ref, br_ref,
                      h_ref, hn_ref, route_ref, cnt_ref):
    h = x_ref[...]
    h = h + jnp.dot(og_ref[...], wo_ref[0:GLA_V_W, :], preferred_element_type=F32)
    h = h + jnp.dot(os_ref[...], wo_ref[GLA_V_W:, :], preferred_element_type=F32)
    h_ref[...] = h
    ms = jnp.mean(h * h, axis=-1, keepdims=True)
    hn = h * lax.rsqrt(ms + RMS_EPS) * g_ref[...]
    hn_ref[...] = hn
    logits = jnp.dot(hn.astype(BF16), wr_ref[...], preferred_element_type=F32) + br_ref[...]

    lane = lax.broadcasted_iota(jnp.int32, logits.shape, 1)
    lanef = lane.astype(F32)
    big = float(LANES)
    ninf = -jnp.inf
    gl = jnp.where(lane < N_GROUPS, logits, ninf)
    gmax = jnp.max(gl, axis=-1, keepdims=True)
    g_p = 1.0 / jnp.sum(jnp.exp(gl - gmax), axis=-1, keepdims=True)
    g_idx = jnp.min(jnp.where(gl == gmax, lanef, big), axis=-1, keepdims=True)
    lo = N_GROUPS + EXPERTS_PER_GROUP * g_idx
    el = jnp.where((lanef >= lo) & (lanef < lo + EXPERTS_PER_GROUP), logits, ninf)
    m1 = jnp.max(el, axis=-1, keepdims=True)
    i1 = jnp.min(jnp.where(el == m1, lanef, big), axis=-1, keepdims=True)
    el2 = jnp.where(lanef == i1, ninf, el)
    m2 = jnp.max(el2, axis=-1, keepdims=True)
    i2 = jnp.min(jnp.where(el2 == m2, lanef, big), axis=-1, keepdims=True)
    d = jnp.exp(m2 - m1)
    c1 = g_p / (1.0 + d)
    c2 = g_p * d / (1.0 + d)
    tm = logits.shape[0]
    chosen = ((lanef == i1) | (lanef == i2)).astype(BF16)
    ri = lax.broadcasted_iota(jnp.int32, (tm, tm), 0)
    ci = lax.broadcasted_iota(jnp.int32, (tm, tm), 1)
    earlier = jnp.dot((ci < ri).astype(BF16), chosen, preferred_element_type=F32)
    r1 = jnp.sum(jnp.where(lanef == i1, earlier, 0.0), axis=-1, keepdims=True)
    r2 = jnp.sum(jnp.where(lanef == i2, earlier, 0.0), axis=-1, keepdims=True)
    cnt = jnp.sum(chosen.astype(F32), axis=0, keepdims=True)
    cnt_ref[...] = jnp.broadcast_to(cnt, cnt_ref.shape)

    fields = [i1 - N_GROUPS, i2 - N_GROUPS, c1, c2, r1, r2]
    route = jnp.zeros_like(logits)
    for idx, val in enumerate(fields):
        route = jnp.where(lane == idx, val, route)
    route_ref[...] = route


def _out_route(o_gla, o_swa, x2, w_out, g, w_rt, b_rt):
    m = x2.shape[0]
    tm = min(OUT_TM, m)
    row = lambda i: (i, 0)
    fixed = lambda i: (0, 0)
    return pl.pallas_call(
        _out_route_kernel,
        grid=(m // tm,),
        in_specs=[
            pl.BlockSpec((tm, GLA_V_W), row),
            pl.BlockSpec((tm, SWA_Q_W), row),
            pl.BlockSpec((tm, D_MODEL), row),
            pl.BlockSpec((GLA_V_W + SWA_Q_W, D_MODEL), fixed),
            pl.BlockSpec((1, D_MODEL), fixed),
            pl.BlockSpec((D_MODEL, LANES), fixed),
            pl.BlockSpec((1, LANES), fixed),
        ],
        out_specs=[
            pl.BlockSpec((tm, D_MODEL), row),
            pl.BlockSpec((tm, D_MODEL), row),
            pl.BlockSpec((tm, LANES), row),
            pl.BlockSpec((1, 8, LANES), lambda i: (i, 0, 0)),
        ],
        out_shape=[
            jax.ShapeDtypeStruct((m, D_MODEL), F32),
            jax.ShapeDtypeStruct((m, D_MODEL), F32),
            jax.ShapeDtypeStruct((m, LANES), F32),
            jax.ShapeDtypeStruct((m // tm, 8, LANES), F32),
        ],
        compiler_params=pltpu.CompilerParams(
            dimension_semantics=("arbitrary",), vmem_limit_bytes=VMEM_LIMIT),
        name="out_route",
    )(o_gla, o_swa, x2, w_out, g, w_rt, b_rt)


LANES_LOG2 = 7
MOE_ROW_GROUP_LOG2 = 3
MOE_ROW_GROUP = 1 << MOE_ROW_GROUP_LOG2
IDS_WINDOW_ROWS = 3


def _moe_kernel(te_ref, cb_ref, nv_ref, ne_ref, nt_ref, tok_hbm, hn_hbm, wg_hbm, wu_hbm, wd_hbm, y_ref,
                xbuf, gsem, ids_smem, isem, wgs, wus, wds, wsem, wgb, wub, wdb):
    tr = MOE_TR
    grp = MOE_ROW_GROUP
    t = pl.program_id(0)
    nt = nt_ref[0]
    gs = lax.rem(t, 2)
    cs = 1 - gs

    def ids_copy(tile, s):
        row0 = lax.shift_right_logical(cb_ref[tile], LANES_LOG2)
        return pltpu.make_async_copy(tok_hbm.at[pl.ds(row0, IDS_WINDOW_ROWS)], ids_smem.at[s], isem.at[s])

    def weight_copies(e):
        return (pltpu.make_async_copy(wg_hbm.at[e], wgs, wsem.at[0]),
                pltpu.make_async_copy(wu_hbm.at[e], wus, wsem.at[1]),
                pltpu.make_async_copy(wd_hbm.at[e], wds, wsem.at[2]))

    def rows_used(tile):
        groups = lax.shift_right_logical(nv_ref[tile] + (grp - 1), MOE_ROW_GROUP_LOG2)
        return lax.shift_left(groups, MOE_ROW_GROUP_LOG2)

    @pl.when(t == 0)
    def _():
        xbuf[...] = jnp.zeros_like(xbuf)
        ids_copy(0, 0).start()
        for cp in weight_copies(te_ref[0]):
            cp.start()

    @pl.when(t < nt)
    def _():
        ids_copy(t, gs).wait()

        @pl.when(t + 1 < nt)
        def _():
            ids_copy(t + 1, cs).start()

        off = cb_ref[t] & (LANES - 1)
        n_real = nv_ref[t]
        for g0 in range(0, tr, grp):
            @pl.when(g0 < n_real)
            def _():
                for r in range(g0, g0 + grp):
                    q = off + r
                    tok = ids_smem[gs, lax.shift_right_logical(q, LANES_LOG2), q & (LANES - 1)]
                    pltpu.make_async_copy(hn_hbm.at[pl.ds(tok, 1)], xbuf.at[gs, pl.ds(r, 1)],
                                          gsem.at[gs]).start(priority=r % 2)

    @pl.when((t >= 1) & (t <= nt))
    def _():
        c = t - 1
        changed = (c == 0) | (te_ref[c] != te_ref[jnp.maximum(c - 1, 0)])

        @pl.when(changed)
        def _():
            for cp in weight_copies(te_ref[c]):
                cp.wait()
            wgb[...] = wgs[...].astype(BF16)
            wub[...] = wus[...].astype(BF16)
            wdb[...] = wds[...].astype(BF16)
            nxt = ne_ref[c]

            @pl.when(nxt >= 0)
            def _():
                for cp in weight_copies(nxt):
                    cp.start()

        n_rows = pl.multiple_of(rows_used(c), grp)
        pltpu.make_async_copy(hn_hbm.at[pl.ds(0, n_rows)], xbuf.at[cs, pl.ds(0, n_rows)],
                              gsem.at[cs]).wait()
        x = xbuf[cs].astype(BF16)
        g = jnp.dot(x, wgb[...], preferred_element_type=F32)
        u = jnp.dot(x, wub[...], preferred_element_type=F32)
        hmid = (_silu(g) * u).astype(BF16)
        y_ref[...] = jnp.dot(hmid, wdb[...], preferred_element_type=F32)

    @pl.when(t > nt)
    def _():
        y_ref[...] = jnp.zeros_like(y_ref)


def _moe(hn, w_gate, w_up, w_down, plan):
    n_tiles = plan["tile_expert"].shape[0]
    p_rows = n_tiles * MOE_TR
    prev = lambda t: jnp.maximum(t - 1, 0)
    any_spec = pl.BlockSpec(memory_space=pl.ANY)
    grid_spec = pltpu.PrefetchScalarGridSpec(
        num_scalar_prefetch=5,
        grid=(n_tiles + 1,),
        in_specs=[any_spec] * 5,
        out_specs=pl.BlockSpec((MOE_TR, D_MODEL), lambda t, *_: (prev(t), 0)),
        scratch_shapes=[
            pltpu.VMEM((2, MOE_TR, D_MODEL), F32),
            pltpu.SemaphoreType.DMA((2,)),
            pltpu.SMEM((2, IDS_WINDOW_ROWS, LANES), jnp.int32),
            pltpu.SemaphoreType.DMA((2,)),
            pltpu.VMEM((D_MODEL, D_EXPERT), F32),
            pltpu.VMEM((D_MODEL, D_EXPERT), F32),
            pltpu.VMEM((D_EXPERT, D_MODEL), F32),
            pltpu.SemaphoreType.DMA((3,)),
            pltpu.VMEM((D_MODEL, D_EXPERT), BF16),
            pltpu.VMEM((D_MODEL, D_EXPERT), BF16),
            pltpu.VMEM((D_EXPERT, D_MODEL), BF16),
        ],
    )
    return pl.pallas_call(
        _moe_kernel,
        grid_spec=grid_spec,
        out_shape=jax.ShapeDtypeStruct((p_rows, D_MODEL), F32),
        compiler_params=pltpu.CompilerParams(
            dimension_semantics=("arbitrary",), vmem_limit_bytes=VMEM_LIMIT),
        name="moe",
    )(plan["tile_expert"], plan["tile_base"], plan["tile_rows"], plan["next_expert"], plan["num_tiles"],
      plan["sorted_tokens"], hn, w_gate, w_up, w_down)


def _combine_kernel(pos_hbm, y_hbm, h_ref, route_ref, g_ref, o_ref, ybuf, gsem, ids_smem, isem):
    tm = h_ref.shape[0]
    t = pl.program_id(0)
    ntile = pl.num_programs(0) - 1

    def ids_copy(tile, s):
        return pltpu.make_async_copy(pos_hbm.at[tile], ids_smem.at[s], isem.at[s])

    @pl.when(t == 0)
    def _():
        ids_copy(0, 0).start()

    @pl.when(t < ntile)
    def _():
        s = lax.rem(t, 2)
        ids_copy(t, s).wait()

        @pl.when(t + 1 < ntile)
        def _():
            ids_copy(t + 1, 1 - s).start()

        for kk in range(2):
            for r in range(tm):
                flat = kk * tm + r
                row = ids_smem[s, flat // LANES, flat % LANES]
                pltpu.make_async_copy(y_hbm.at[pl.ds(row, 1)], ybuf.at[s, kk, pl.ds(r, 1)],
                                      gsem.at[s, kk]).start(priority=r % 2)

    @pl.when(t >= 1)
    def _():
        s = lax.rem(t - 1, 2)
        for kk in range(2):
            pltpu.make_async_copy(y_hbm.at[pl.ds(0, tm)], ybuf.at[s, kk], gsem.at[s, kk]).wait()
        route = route_ref[...]
        h = h_ref[...] + route[:, 2:3] * ybuf[s, 0] + route[:, 3:4] * ybuf[s, 1]
        ms = jnp.mean(h * h, axis=-1, keepdims=True)
        o_ref[...] = h * lax.rsqrt(ms + RMS_EPS) * g_ref[...]


def _combine(y, h, route, g, pos2d):
    m = h.shape[0]
    tm = min(CMB_TM, m)
    ntile = m // tm
    pos3 = pos2d.reshape(ntile, tm, 2).transpose(0, 2, 1).reshape(ntile, 2 * tm // LANES, LANES)
    row = lambda i: (jnp.maximum(i - 1, 0), 0)
    return pl.pallas_call(
        _combine_kernel,
        grid=(ntile + 1,),
        in_specs=[
            pl.BlockSpec(memory_space=pl.ANY),
            pl.BlockSpec(memory_space=pl.ANY),
            pl.BlockSpec((tm, D_MODEL), row),
            pl.BlockSpec((tm, LANES), row),
            pl.BlockSpec((1, D_MODEL), lambda i: (0, 0)),
        ],
        out_specs=pl.BlockSpec((tm, D_MODEL), row),
        out_shape=jax.ShapeDtypeStruct((m, D_MODEL), F32),
        scratch_shapes=[
            pltpu.VMEM((2, 2, tm, D_MODEL), F32),
            pltpu.SemaphoreType.DMA((2, 2)),
            pltpu.SMEM((2, 2 * tm // LANES, LANES), jnp.int32),
            pltpu.SemaphoreType.DMA((2,)),
        ],
        compiler_params=pltpu.CompilerParams(
            dimension_semantics=("arbitrary",), vmem_limit_bytes=VMEM_LIMIT),
        name="combine",
    )(pos3, y, h, route, g)


def _dispatch_plan(route, tile_counts, m):
    tr = MOE_TR
    p_rows = 2 * m + N_EXPERTS * tr
    n_tiles = p_rows // tr
    n_tok_tiles = tile_counts.shape[0]
    cnt = tile_counts[:, 0, N_GROUPS:N_GROUPS + N_EXPERTS].astype(jnp.int32)
    before_tile = jnp.cumsum(cnt, axis=0) - cnt
    counts = jnp.sum(cnt, axis=0)
    padded = ((counts + tr - 1) // tr) * tr
    ends = jnp.cumsum(padded)
    starts = ends - padded
    base = (starts[None, :] + before_tile).astype(jnp.int32)
    expert = route[:, 0:2].astype(jnp.int32).reshape(n_tok_tiles, m // n_tok_tiles, 2)
    rank = route[:, 4:6].astype(jnp.int32).reshape(n_tok_tiles, m // n_tok_tiles, 2)
    hit = expert[..., None] == jnp.arange(N_EXPERTS, dtype=jnp.int32)
    pos = rank + jnp.sum(jnp.where(hit, base[:, None, None, :], 0), axis=-1)
    pos = pos.reshape(m, 2)
    num_tiles = (ends[-1] // tr).astype(jnp.int32)
    tile_idx = jnp.minimum(jnp.arange(n_tiles, dtype=jnp.int32), num_tiles - 1)
    tile_expert = jnp.sum((tile_idx[:, None] * tr >= ends[None, :]).astype(jnp.int32), axis=1)
    n_slots = 2 * m
    slot = jnp.arange(n_slots, dtype=jnp.int32)
    sorted_key = jnp.sort(route[:, 0:2].astype(jnp.int32).reshape(-1) * n_slots + slot)
    sorted_tokens = (sorted_key & (n_slots - 1)) >> 1
    sorted_tokens = jnp.pad(sorted_tokens, (0, IDS_WINDOW_ROWS * LANES)).reshape(-1, LANES)
    dense_starts = jnp.cumsum(counts) - counts
    in_expert = tile_idx - (starts // tr)[tile_expert]
    tile_base = dense_starts[tile_expert] + in_expert * tr
    tile_rows = jnp.clip(counts[tile_expert] - in_expert * tr, 0, tr)
    tile_rows = jnp.where(jnp.arange(n_tiles) < num_tiles, tile_rows, 0)
    ids = jnp.arange(N_EXPERTS, dtype=jnp.int32)
    present = jnp.where(counts > 0, ids, N_EXPERTS)
    next_ge = lax.cummin(present, axis=0, reverse=True)
    next_gt = jnp.concatenate([next_ge[1:], jnp.full((1,), N_EXPERTS, jnp.int32)])
    next_gt = jnp.where(next_gt >= N_EXPERTS, -1, next_gt)
    plan = dict(tile_expert=tile_expert.astype(jnp.int32), tile_base=tile_base.astype(jnp.int32),
                tile_rows=tile_rows.astype(jnp.int32), next_expert=next_gt[tile_expert].astype(jnp.int32),
                num_tiles=num_tiles.reshape(1), sorted_tokens=sorted_tokens)
    return plan, pos


def kernel(x, norm_mix_g, w_in, w_gk_up, b_gk, gla_norm_g, swa_sinks, w_out, norm_ffn_g,
           w_group, b_group, w_router, b_router, w_gate, w_up, w_down, norm_final_g):
    batch, seq, d = x.shape
    m = batch * seq
    assert w_in.shape[0] == 1, "single-layer block"
    x2 = x.reshape(m, d)
    lr0 = OFF_GR + GLA_V_W
    w_in_b = w_in[0].astype(BF16)
    w_main = jnp.concatenate([w_in_b[:, :lr0], w_in_b[:, lr0 + GLA_RANK:]], axis=1)
    w_lr = jnp.pad(w_in_b[:, lr0:lr0 + GLA_RANK], ((0, 0), (0, LANES - GLA_RANK)))
    wup = jnp.pad(w_gk_up[0], ((0, LANES - GLA_RANK), (0, 0))).astype(BF16)
    w_rt = jnp.pad(jnp.concatenate([w_group[0], w_router[0]], axis=1),
                   ((0, 0), (0, LANES - N_GROUPS - N_EXPERTS))).astype(BF16)
    b_rt = jnp.pad(jnp.concatenate([b_group[0], b_router[0]]),
                   (0, LANES - N_GROUPS - N_EXPERTS)).reshape(1, LANES)

    proj, glr = _in_proj(x2, norm_mix_g[0].reshape(1, d), w_main, w_lr)
    o_gla = _gla(proj, glr, wup, b_gk[0].reshape(1, GLA_QK_W),
                 gla_norm_g[0].reshape(1, GLA_V_W), batch, seq)
    o_swa = _swa(proj, swa_sinks[0], batch, seq)
    h_mid, hn, route, tile_counts = _out_route(o_gla, o_swa, x2, w_out[0].astype(BF16),
                                               norm_ffn_g[0].reshape(1, d), w_rt, b_rt)
    plan, pos2d = _dispatch_plan(route, tile_counts, m)
    y = _moe(hn, w_gate[0], w_up[0], w_down[0], plan)
    out = _combine(y, h_mid, route, norm_final_g.reshape(1, d), pos2d)
    return out.reshape(batch, seq, d)
```

```python
import functools

import jax
import jax.numpy as jnp
import numpy as np
from jax import lax
from jax.experimental import pallas as pl
from jax.experimental.pallas import tpu as pltpu

F32 = jnp.float32
BF16 = jnp.bfloat16

D_MODEL = 2048
GLA_HEADS = 4
GLA_DK = 128
GLA_DV = 256
GLA_RANK = 16
GLA_GATE_NORM = 16.0
GLA_CHUNK = 64
SWA_Q_HEADS = 16
SWA_KV_HEADS = 4
SWA_GROUP = SWA_Q_HEADS // SWA_KV_HEADS
SWA_HEAD_DIM = 64
SWA_BLOCK = 128
N_GROUPS = 4
EXPERTS_PER_GROUP = 16
N_EXPERTS = N_GROUPS * EXPERTS_PER_GROUP
D_EXPERT = 256
RMS_EPS = 1e-6

GLA_QK_W = GLA_HEADS * GLA_DK
GLA_V_W = GLA_HEADS * GLA_DV
SWA_Q_W = SWA_Q_HEADS * SWA_HEAD_DIM
SWA_KV_W = SWA_KV_HEADS * SWA_HEAD_DIM
OFF_GQ = 0
OFF_GK = OFF_GQ + GLA_QK_W
OFF_GV = OFF_GK + GLA_QK_W
OFF_GR = OFF_GV + GLA_V_W
OFF_SQ = OFF_GR + GLA_V_W
OFF_SK = OFF_SQ + SWA_Q_W
OFF_SV = OFF_SK + SWA_KV_W
PROJ_W = OFF_SV + SWA_KV_W
LANES = 128

IN_TM = 1024
IN_TN = 768
OUT_TM = 256
MOE_TR = 256
CMB_TM = 256
VMEM_LIMIT = 56 * 1024 * 1024


def _silu(x):
    return x / (1.0 + jnp.exp(-x))


ALIBI_PARTS = 3


def _bf16_parts(value, n):
    parts, rest = [], np.float32(value)
    for _ in range(n):
        piece = np.float32(np.asarray(rest).astype(jnp.bfloat16))
        parts.append(float(piece))
        rest = np.float32(rest - piece)
    return parts


def _inproj_kernel(x_ref, g_ref, w_ref, wlr_ref, proj_ref, glr_ref, xn_ref):
    @pl.when(pl.program_id(1) == 0)
    def _():
        x = x_ref[...]
        ms = jnp.mean(x * x, axis=-1, keepdims=True)
        xn = (x * lax.rsqrt(ms + RMS_EPS) * g_ref[...]).astype(BF16)
        xn_ref[...] = xn
        glr_ref[...] = jnp.dot(xn, wlr_ref[...], preferred_element_type=F32)

    proj_ref[...] = jnp.dot(xn_ref[...], w_ref[...], preferred_element_type=F32).astype(BF16)


def _in_proj(x2, g, w_main, w_lr):
    m = x2.shape[0]
    tm = min(IN_TM, m)
    return pl.pallas_call(
        _inproj_kernel,
        grid=(m // tm, PROJ_W // IN_TN),
        in_specs=[
            pl.BlockSpec((tm, D_MODEL), lambda i, j: (i, 0)),
            pl.BlockSpec((1, D_MODEL), lambda i, j: (0, 0)),
            pl.BlockSpec((D_MODEL, IN_TN), lambda i, j: (0, j)),
            pl.BlockSpec((D_MODEL, LANES), lambda i, j: (0, 0)),
        ],
        out_specs=[
            pl.BlockSpec((tm, IN_TN), lambda i, j: (i, j)),
            pl.BlockSpec((tm, LANES), lambda i, j: (i, 0)),
        ],
        out_shape=[
            jax.ShapeDtypeStruct((m, PROJ_W), BF16),
            jax.ShapeDtypeStruct((m, LANES), F32),
        ],
        scratch_shapes=[pltpu.VMEM((tm, D_MODEL), BF16)],
        compiler_params=pltpu.CompilerParams(
            dimension_semantics=("arbitrary", "arbitrary"), vmem_limit_bytes=VMEM_LIMIT),
        name="in_proj",
    )(x2, g, w_main, w_lr)


def _gla_kernel(q_ref, k_ref, v_ref, r_ref, glr_ref, wup_ref, bgk_ref, ng_ref, o_ref,
                la_ref, oi_ref, qi_ref, ki_ref, qd_ref, kd_ref, a_ref, kv_ref, dec_ref, sp_ref):
    t = q_ref.shape[0]
    c = GLA_CHUNK
    nchunk = t // c
    z = jnp.dot(glr_ref[...].astype(BF16), wup_ref[...], preferred_element_type=F32) + bgk_ref[...]
    la_ref[...] = (jnp.minimum(z, 0.0) - jnp.log1p(jnp.exp(-jnp.abs(z)))) * (1.0 / GLA_GATE_NORM)

    ii = lax.broadcasted_iota(jnp.int32, (c, c), 0)
    jj = lax.broadcasted_iota(jnp.int32, (c, c), 1)
    causal = jj <= ii
    tri = causal.astype(BF16)
    nt = (((1,), (1,)), ((), ()))
    tn = (((0,), (0,)), ((), ()))

    def chunk_rows(n):
        return pl.ds(pl.multiple_of(n * c, c), c)

    def decays(n, carry):
        rows = chunk_rows(n)
        la = la_ref[rows, :]
        hi = la.astype(BF16)
        r1 = la - hi.astype(F32)
        mid = r1.astype(BF16)
        lo = (r1 - mid.astype(F32)).astype(BF16)
        parts = jnp.dot(tri, jnp.concatenate([hi, mid, lo], axis=1), preferred_element_type=F32)
        bcum = parts[:, 0:GLA_DK] + parts[:, GLA_DK:2 * GLA_DK] + parts[:, 2 * GLA_DK:]
        b_mid = bcum[c // 2 - 1:c // 2, :]
        b_last = bcum[c - 1:c, :]
        q = q_ref[rows, :].astype(F32) * (GLA_DK ** -0.5)
        k = k_ref[rows, :].astype(F32)
        qi_ref[rows, :] = (q * jnp.exp(bcum - b_mid)).astype(BF16)
        ki_ref[rows, :] = (k * jnp.exp(b_mid - bcum)).astype(BF16)
        qd_ref[rows, :] = (q * jnp.exp(bcum)).astype(BF16)
        kd_ref[rows, :] = (k * jnp.exp(b_last - bcum)).astype(BF16)
        dec_ref[n] = jnp.broadcast_to(jnp.exp(b_last), dec_ref.shape[1:])
        return carry

    lax.fori_loop(0, nchunk, decays, 0, unroll=4)

    def scores(n, carry):
        rows = chunk_rows(n)
        a = lax.dot_general(qi_ref[rows, :], ki_ref[rows, :], nt, preferred_element_type=F32)
        a_ref[rows, :] = jnp.where(causal, a, 0.0).astype(BF16)
        return carry

    lax.fori_loop(0, nchunk, scores, 0, unroll=8)

    def intra(n, carry):
        rows = chunk_rows(n)
        v = v_ref[rows, :]
        oi_ref[rows, :] = jnp.dot(a_ref[rows, :], v, preferred_element_type=F32)
        kv_ref[n] = lax.dot_general(v, kd_ref[rows, :], tn, preferred_element_type=F32)
        return carry

    lax.fori_loop(0, nchunk, intra, 0, unroll=8)

    def scan(n, s_t):
        sp_ref[n] = s_t.astype(BF16)
        return s_t * dec_ref[n][0:1, :] + kv_ref[n]

    lax.fori_loop(0, nchunk, scan, jnp.zeros((GLA_DV, GLA_DK), F32))

    def inter(n, carry):
        rows = chunk_rows(n)
        o = oi_ref[rows, :] + lax.dot_general(qd_ref[rows, :], sp_ref[n], nt,
                                              preferred_element_type=F32)
        ms = jnp.mean(o * o, axis=-1, keepdims=True)
        o = o * lax.rsqrt(ms + RMS_EPS) * ng_ref[...]
        o = o * _silu(r_ref[rows, :].astype(F32))
        o_ref[rows, :] = o.astype(BF16)
        return carry

    lax.fori_loop(0, nchunk, inter, 0, unroll=8)


def _gla(proj, glr, wup, bgk, ng, batch, seq):
    m = proj.shape[0]
    return pl.pallas_call(
        _gla_kernel,
        grid=(batch, GLA_HEADS),
        in_specs=[
            pl.BlockSpec((seq, GLA_DK), lambda b, h: (b, OFF_GQ // GLA_DK + h)),
            pl.BlockSpec((seq, GLA_DK), lambda b, h: (b, OFF_GK // GLA_DK + h)),
            pl.BlockSpec((seq, GLA_DV), lambda b, h: (b, OFF_GV // GLA_DV + h)),
            pl.BlockSpec((seq, GLA_DV), lambda b, h: (b, OFF_GR // GLA_DV + h)),
            pl.BlockSpec((seq, LANES), lambda b, h: (b, 0)),
            pl.BlockSpec((LANES, GLA_DK), lambda b, h: (0, h)),
            pl.BlockSpec((1, GLA_DK), lambda b, h: (0, h)),
            pl.BlockSpec((1, GLA_DV), lambda b, h: (0, h)),
        ],
        out_specs=pl.BlockSpec((seq, GLA_DV), lambda b, h: (b, h)),
        out_shape=jax.ShapeDtypeStruct((m, GLA_V_W), BF16),
        scratch_shapes=[
            pltpu.VMEM((seq, GLA_DK), F32),
            pltpu.VMEM((seq, GLA_DV), F32),
            pltpu.VMEM((seq, GLA_DK), BF16),
            pltpu.VMEM((seq, GLA_DK), BF16),
            pltpu.VMEM((seq, GLA_DK), BF16),
            pltpu.VMEM((seq, GLA_DK), BF16),
            pltpu.VMEM((seq, GLA_CHUNK), BF16),
            pltpu.VMEM((seq // GLA_CHUNK, GLA_DV, GLA_DK), F32),
            pltpu.VMEM((seq // GLA_CHUNK, 8, GLA_DK), F32),
            pltpu.VMEM((seq // GLA_CHUNK, GLA_DV, GLA_DK), BF16),
        ],
        compiler_params=pltpu.CompilerParams(
            dimension_semantics=("arbitrary", "arbitrary"), vmem_limit_bytes=VMEM_LIMIT),
        name="gla",
    )(proj, proj, proj, proj, glr, wup, bgk, ng)


def _swa_kernel(sink_ref, q_ref, kp_ref, kc_ref, vp_ref, vc_ref, o_ref):
    blk = SWA_BLOCK
    half = SWA_HEAD_DIM
    n = pl.program_id(1)
    k_all = jnp.concatenate([kp_ref[...], kc_ref[...]], axis=0)
    v_all = jnp.concatenate([vp_ref[...], vc_ref[...]], axis=0)
    qi = lax.broadcasted_iota(jnp.int32, (blk, 2 * blk), 0)
    kj = lax.broadcasted_iota(jnp.int32, (blk, 2 * blk), 1)
    rel = qi + blk - kj
    valid = (rel >= 0) & (rel < blk) & ((kj >= blk) | (n > 0))
    sink_col = kj == 0
    lane_kv = lax.broadcasted_iota(jnp.int32, (2 * blk, LANES), 1)
    lo_kv = lane_kv < half
    lane_q = lax.broadcasted_iota(jnp.int32, (blk, LANES), 1)
    lo_q = lane_q < half
    nt = (((1,), (1,)), ((), ()))
    key_idx = lax.broadcasted_iota(jnp.int32, (2 * blk, LANES), 0).astype(F32)
    key_cols = jnp.where((lane_kv & (half - 1)) < ALIBI_PARTS, key_idx, 0.0)
    q_pos = lax.broadcasted_iota(jnp.int32, (blk, 1), 0).astype(F32) + float(blk)
    q_scale = jnp.asarray(SWA_HEAD_DIM ** -0.5, BF16)

    for p in range(SWA_KV_HEADS // 2):
        k_slab = k_all[:, p * LANES:(p + 1) * LANES].astype(F32)
        v_slab = v_all[:, p * LANES:(p + 1) * LANES].astype(F32)
        k_roll = pltpu.roll(k_slab, half, 1)
        v_roll = pltpu.roll(v_slab, half, 1)
        for hh in range(2):
            h = 2 * p + hh
            k_lo, k_hi = (k_slab, k_roll) if hh == 0 else (k_roll, k_slab)
            kd = (jnp.where(lo_kv, k_lo, key_cols).astype(BF16),
                  jnp.where(lo_kv, key_cols, k_hi).astype(BF16))
            v_lo, v_hi = (v_slab, v_roll) if hh == 0 else (v_roll, v_slab)
            vd = jnp.where(lo_kv, v_lo, v_hi)
            vd = jnp.where(key_idx == 0.0, 0.0, vd).astype(BF16)
            for gp in range(SWA_GROUP // 2):
                col = (h * (SWA_GROUP // 2) + gp) * LANES
                qs = q_ref[:, col:col + LANES] * q_scale
                outs = []
                for gg in range(2):
                    head = h * SWA_GROUP + 2 * gp + gg
                    parts = _bf16_parts(2.0 ** (-8.0 * (head + 1) / SWA_Q_HEADS), ALIBI_PARTS)
                    slope = sum(parts)
                    aug = jnp.zeros((blk, LANES), F32)
                    for idx, part in enumerate(parts):
                        aug = jnp.where(lane_q == (half if gg == 0 else 0) + idx, part, aug)
                    keep = lo_q if gg == 0 else jnp.logical_not(lo_q)
                    qm = jnp.where(keep, qs, aug.astype(BF16))
                    s = lax.dot_general(qm, kd[gg], nt, preferred_element_type=F32)
                    sink = sink_ref[head] + slope * q_pos
                    s = jnp.where(sink_col, sink, jnp.where(valid, s, -jnp.inf))
                    mx = jnp.max(s, axis=-1, keepdims=True)
                    pe = jnp.exp(s - mx)
                    den = jnp.sum(pe, axis=-1, keepdims=True)
                    o = jnp.dot(pe.astype(BF16), vd, preferred_element_type=F32)
                    outs.append(o / den)
                o_ref[:, col:col + LANES] = jnp.where(lo_q, outs[0], outs[1]).astype(BF16)


def _swa(proj, sinks, batch, seq):
    m = proj.shape[0]
    nb = seq // SWA_BLOCK
    qcol = OFF_SQ // SWA_Q_W
    kcol = OFF_SK // SWA_KV_W
    vcol = OFF_SV // SWA_KV_W
    cur = lambda c: (lambda b, n: (b * nb + n, c))
    prev = lambda c: (lambda b, n: (b * nb + jnp.maximum(n - 1, 0), c))
    return pl.pallas_call(
        _swa_kernel,
        grid=(batch, nb),
        in_specs=[
            pl.BlockSpec(memory_space=pltpu.SMEM),
            pl.BlockSpec((SWA_BLOCK, SWA_Q_W), cur(qcol)),
            pl.BlockSpec((SWA_BLOCK, SWA_KV_W), prev(kcol)),
            pl.BlockSpec((SWA_BLOCK, SWA_KV_W), cur(kcol)),
            pl.BlockSpec((SWA_BLOCK, SWA_KV_W), prev(vcol)),
            pl.BlockSpec((SWA_BLOCK, SWA_KV_W), cur(vcol)),
        ],
        out_specs=pl.BlockSpec((SWA_BLOCK, SWA_Q_W), lambda b, n: (b * nb + n, 0)),
        out_shape=jax.ShapeDtypeStruct((m, SWA_Q_W), BF16),
        compiler_params=pltpu.CompilerParams(
            dimension_semantics=("arbitrary", "arbitrary"), vmem_limit_bytes=VMEM_LIMIT),
        name="swa",
    )(sinks, proj, proj, proj, proj, proj)


def _out_route_kernel(og_ref, os_ref, x_ref, wo_ref, g_ref, wr_ref, br_ref,
                      h_ref, hn_ref, route_ref, cnt_ref, hnb_ref):
    t = pl.program_id(0)
    slot = lax.rem(t, 2)

    @pl.when(t == 0)
    def _():
        hnb_ref[...] = jnp.zeros_like(hnb_ref)

    logits = jnp.dot(hnb_ref[1 - slot], wr_ref[...], preferred_element_type=F32) + br_ref[...]

    h = x_ref[...]
    h = h + jnp.dot(og_ref[...], wo_ref[0:GLA_V_W, :], preferred_element_type=F32)
    h = h + jnp.dot(os_ref[...], wo_ref[GLA_V_W:, :], preferred_element_type=F32)
    h_ref[...] = h
    ms = jnp.mean(h * h, axis=-1, keepdims=True)
    hn = h * lax.rsqrt(ms + RMS_EPS) * g_ref[...]
    hn_ref[...] = hn
    hnb_ref[slot] = hn.astype(BF16)

    lane = lax.broadcasted_iota(jnp.int32, logits.shape, 1)
    lanef = lane.astype(F32)
    big = float(LANES)
    ninf = -jnp.inf
    gl = jnp.where(lane < N_GROUPS, logits, ninf)
    gmax = jnp.max(gl, axis=-1, keepdims=True)
    g_p = 1.0 / jnp.sum(jnp.exp(gl - gmax), axis=-1, keepdims=True)
    g_idx = jnp.min(jnp.where(gl == gmax, lanef, big), axis=-1, keepdims=True)
    lo = N_GROUPS + EXPERTS_PER_GROUP * g_idx
    el = jnp.where((lanef >= lo) & (lanef < lo + EXPERTS_PER_GROUP), logits, ninf)
    m1 = jnp.max(el, axis=-1, keepdims=True)
    i1 = jnp.min(jnp.where(el == m1, lanef, big), axis=-1, keepdims=True)
    el2 = jnp.where(lanef == i1, ninf, el)
    m2 = jnp.max(el2, axis=-1, keepdims=True)
    i2 = jnp.min(jnp.where(el2 == m2, lanef, big), axis=-1, keepdims=True)
    d = jnp.exp(m2 - m1)
    c1 = g_p / (1.0 + d)
    c2 = g_p * d / (1.0 + d)
    tm = logits.shape[0]
    chosen = ((lanef == i1) | (lanef == i2)).astype(BF16)
    ri = lax.broadcasted_iota(jnp.int32, (tm, tm), 0)
    ci = lax.broadcasted_iota(jnp.int32, (tm, tm), 1)
    earlier = jnp.dot((ci < ri).astype(BF16), chosen, preferred_element_type=F32)
    r1 = jnp.sum(jnp.where(lanef == i1, earlier, 0.0), axis=-1, keepdims=True)
    r2 = jnp.sum(jnp.where(lanef == i2, earlier, 0.0), axis=-1, keepdims=True)
    cnt = jnp.sum(chosen.astype(F32), axis=0, keepdims=True)
    cnt_ref[...] = jnp.broadcast_to(cnt, cnt_ref.shape)

    fields = [i1 - N_GROUPS, i2 - N_GROUPS, c1, c2, r1, r2]
    route = jnp.zeros_like(logits)
    for idx, val in enumerate(fields):
        route = jnp.where(lane == idx, val, route)
    route_ref[...] = route


def _out_route(o_gla, o_swa, x2, w_out, g, w_rt, b_rt):
    m = x2.shape[0]
    tm = min(OUT_TM, m)
    ntile = m // tm
    row = lambda i: (jnp.minimum(i, ntile - 1), 0)
    routed = lambda i: (jnp.maximum(i - 1, 0), 0)
    fixed = lambda i: (0, 0)
    return pl.pallas_call(
        _out_route_kernel,
        grid=(ntile + 1,),
        in_specs=[
            pl.BlockSpec((tm, GLA_V_W), row),
            pl.BlockSpec((tm, SWA_Q_W), row),
            pl.BlockSpec((tm, D_MODEL), row),
            pl.BlockSpec((GLA_V_W + SWA_Q_W, D_MODEL), fixed),
            pl.BlockSpec((1, D_MODEL), fixed),
            pl.BlockSpec((D_MODEL, LANES), fixed),
            pl.BlockSpec((1, LANES), fixed),
        ],
        out_specs=[
            pl.BlockSpec((tm, D_MODEL), row),
            pl.BlockSpec((tm, D_MODEL), row),
            pl.BlockSpec((tm, LANES), routed),
            pl.BlockSpec((1, 8, LANES), lambda i: (jnp.maximum(i - 1, 0), 0, 0)),
        ],
        out_shape=[
            jax.ShapeDtypeStruct((m, D_MODEL), F32),
            jax.ShapeDtypeStruct((m, D_MODEL), F32),
            jax.ShapeDtypeStruct((m, LANES), F32),
            jax.ShapeDtypeStruct((m // tm, 8, LANES), F32),
        ],
        scratch_shapes=[pltpu.VMEM((2, tm, D_MODEL), BF16)],
        compiler_params=pltpu.CompilerParams(
            dimension_semantics=("arbitrary",), vmem_limit_bytes=VMEM_LIMIT),
        name="out_route",
    )(o_gla, o_swa, x2, w_out, g, w_rt, b_rt)


LANES_LOG2 = 7
MOE_ROW_GROUP_LOG2 = 3
MOE_ROW_GROUP = 1 << MOE_ROW_GROUP_LOG2
IDS_WINDOW_ROWS = 3


def _moe_kernel(te_ref, cb_ref, nv_ref, ne_ref, nt_ref, tok_hbm, hn_hbm, wg_hbm, wu_hbm, wd_hbm, y_ref,
                xbuf, gsem, ids_smem, isem, wgs, wus, wds, wsem, wgb, wub, wdb):
    tr = MOE_TR
    grp = MOE_ROW_GROUP
    t = pl.program_id(0)
    nt = nt_ref[0]
    gs = lax.rem(t, 2)
    cs = 1 - gs

    def ids_copy(tile, s):
        row0 = lax.shift_right_logical(cb_ref[tile], LANES_LOG2)
        return pltpu.make_async_copy(tok_hbm.at[pl.ds(row0, IDS_WINDOW_ROWS)], ids_smem.at[s], isem.at[s])

    def weight_copies(e):
        return (pltpu.make_async_copy(wg_hbm.at[e], wgs, wsem.at[0]),
                pltpu.make_async_copy(wu_hbm.at[e], wus, wsem.at[1]),
                pltpu.make_async_copy(wd_hbm.at[e], wds, wsem.at[2]))

    def rows_used(tile):
        groups = lax.shift_right_logical(nv_ref[tile] + (grp - 1), MOE_ROW_GROUP_LOG2)
        return lax.shift_left(groups, MOE_ROW_GROUP_LOG2)

    @pl.when(t == 0)
    def _():
        xbuf[...] = jnp.zeros_like(xbuf)
        ids_copy(0, 0).start()
        for cp in weight_copies(te_ref[0]):
            cp.start()

    @pl.when(t < nt)
    def _():
        ids_copy(t, gs).wait()

        @pl.when(t + 1 < nt)
        def _():
            ids_copy(t + 1, cs).start()

        off = cb_ref[t] & (LANES - 1)
        n_real = nv_ref[t]
        for g0 in range(0, tr, grp):
            @pl.when(g0 < n_real)
            def _():
                for r in range(g0, g0 + grp):
                    q = off + r
                    tok = ids_smem[gs, lax.shift_right_logical(q, LANES_LOG2), q & (LANES - 1)]
                    pltpu.make_async_copy(hn_hbm.at[pl.ds(tok, 1)], xbuf.at[gs, pl.ds(r, 1)],
                                          gsem.at[gs]).start(priority=r % 2)

    @pl.when((t >= 1) & (t <= nt))
    def _():
        c = t - 1
        changed = (c == 0) | (te_ref[c] != te_ref[jnp.maximum(c - 1, 0)])

        @pl.when(changed)
        def _():
            for cp in weight_copies(te_ref[c]):
                cp.wait()
            wgb[...] = wgs[...].astype(BF16)
            wub[...] = wus[...].astype(BF16)
            wdb[...] = wds[...].astype(BF16)
            nxt = ne_ref[c]

            @pl.when(nxt >= 0)
            def _():
                for cp in weight_copies(nxt):
                    cp.start()

        n_rows = pl.multiple_of(rows_used(c), grp)
        pltpu.make_async_copy(hn_hbm.at[pl.ds(0, n_rows)], xbuf.at[cs, pl.ds(0, n_rows)],
                              gsem.at[cs]).wait()
        x = xbuf[cs].astype(BF16)
        g = jnp.dot(x, wgb[...], preferred_element_type=F32)
        u = jnp.dot(x, wub[...], preferred_element_type=F32)
        hmid = (_silu(g) * u).astype(BF16)
        y_ref[...] = jnp.dot(hmid, wdb[...], preferred_element_type=F32)

    @pl.when(t > nt)
    def _():
        y_ref[...] = jnp.zeros_like(y_ref)


def _moe(hn, w_gate, w_up, w_down, plan):
    n_tiles = plan["tile_expert"].shape[0]
    p_rows = n_tiles * MOE_TR
    prev = lambda t: jnp.maximum(t - 1, 0)
    any_spec = pl.BlockSpec(memory_space=pl.ANY)
    grid_spec = pltpu.PrefetchScalarGridSpec(
        num_scalar_prefetch=5,
        grid=(n_tiles + 1,),
        in_specs=[any_spec] * 5,
        out_specs=pl.BlockSpec((MOE_TR, D_MODEL), lambda t, *_: (prev(t), 0)),
        scratch_shapes=[
            pltpu.VMEM((2, MOE_TR, D_MODEL), F32),
            pltpu.SemaphoreType.DMA((2,)),
            pltpu.SMEM((2, IDS_WINDOW_ROWS, LANES), jnp.int32),
            pltpu.SemaphoreType.DMA((2,)),
            pltpu.VMEM((D_MODEL, D_EXPERT), F32),
            pltpu.VMEM((D_MODEL, D_EXPERT), F32),
            pltpu.VMEM((D_EXPERT, D_MODEL), F32),
            pltpu.SemaphoreType.DMA((3,)),
            pltpu.VMEM((D_MODEL, D_EXPERT), BF16),
            pltpu.VMEM((D_MODEL, D_EXPERT), BF16),
            pltpu.VMEM((D_EXPERT, D_MODEL), BF16),
        ],
    )
    return pl.pallas_call(
        _moe_kernel,
        grid_spec=grid_spec,
        out_shape=jax.ShapeDtypeStruct((p_rows, D_MODEL), F32),
        compiler_params=pltpu.CompilerParams(
            dimension_semantics=("arbitrary",), vmem_limit_bytes=VMEM_LIMIT),
        name="moe",
    )(plan["tile_expert"], plan["tile_base"], plan["tile_rows"], plan["next_expert"], plan["num_tiles"],
      plan["sorted_tokens"], hn, w_gate, w_up, w_down)


def _combine_kernel(pos_hbm, y_hbm, h_ref, route_ref, g_ref, o_ref, ybuf, gsem, ids_smem, isem):
    tm = h_ref.shape[0]
    t = pl.program_id(0)
    ntile = pl.num_programs(0) - 1

    def ids_copy(tile, s):
        return pltpu.make_async_copy(pos_hbm.at[tile], ids_smem.at[s], isem.at[s])

    @pl.when(t == 0)
    def _():
        ids_copy(0, 0).start()

    @pl.when(t < ntile)
    def _():
        s = lax.rem(t, 2)
        ids_copy(t, s).wait()

        @pl.when(t + 1 < ntile)
        def _():
            ids_copy(t + 1, 1 - s).start()

        for kk in range(2):
            for r in range(tm):
                flat = kk * tm + r
                row = ids_smem[s, flat // LANES, flat % LANES]
                pltpu.make_async_copy(y_hbm.at[pl.ds(row, 1)], ybuf.at[s, kk, pl.ds(r, 1)],
                                      gsem.at[s, kk]).start(priority=r % 2)

    @pl.when(t >= 1)
    def _():
        s = lax.rem(t - 1, 2)
        for kk in range(2):
            pltpu.make_async_copy(y_hbm.at[pl.ds(0, tm)], ybuf.at[s, kk], gsem.at[s, kk]).wait()
        route = route_ref[...]
        h = h_ref[...] + route[:, 2:3] * ybuf[s, 0] + route[:, 3:4] * ybuf[s, 1]
        ms = jnp.mean(h * h, axis=-1, keepdims=True)
        o_ref[...] = h * lax.rsqrt(ms + RMS_EPS) * g_ref[...]


def _combine(y, h, route, g, pos2d):
    m = h.shape[0]
    tm = min(CMB_TM, m)
    ntile = m // tm
    pos3 = pos2d.reshape(ntile, tm, 2).transpose(0, 2, 1).reshape(ntile, 2 * tm // LANES, LANES)
    row = lambda i: (jnp.maximum(i - 1, 0), 0)
    return pl.pallas_call(
        _combine_kernel,
        grid=(ntile + 1,),
        in_specs=[
            pl.BlockSpec(memory_space=pl.ANY),
            pl.BlockSpec(memory_space=pl.ANY),
            pl.BlockSpec((tm, D_MODEL), row),
            pl.BlockSpec((tm, LANES), row),
            pl.BlockSpec((1, D_MODEL), lambda i: (0, 0)),
        ],
        out_specs=pl.BlockSpec((tm, D_MODEL), row),
        out_shape=jax.ShapeDtypeStruct((m, D_MODEL), F32),
        scratch_shapes=[
            pltpu.VMEM((2, 2, tm, D_MODEL), F32),
            pltpu.SemaphoreType.DMA((2, 2)),
            pltpu.SMEM((2, 2 * tm // LANES, LANES), jnp.int32),
            pltpu.SemaphoreType.DMA((2,)),
        ],
        compiler_params=pltpu.CompilerParams(
            dimension_semantics=("arbitrary",), vmem_limit_bytes=VMEM_LIMIT),
        name="combine",
    )(pos3, y, h, route, g)


def _dispatch_plan(route, tile_counts, m):
    tr = MOE_TR
    p_rows = 2 * m + N_EXPERTS * tr
    n_tiles = p_rows // tr
    n_tok_tiles = tile_counts.shape[0]
    cnt = tile_counts[:, 0, N_GROUPS:N_GROUPS + N_EXPERTS].astype(jnp.int32)
    before_tile = jnp.cumsum(cnt, axis=0) - cnt
    counts = jnp.sum(cnt, axis=0)
    padded = ((counts + tr - 1) // tr) * tr
    ends = jnp.cumsum(padded)
    starts = ends - padded
    base = (starts[None, :] + before_tile).astype(jnp.int32)
    expert = route[:, 0:2].astype(jnp.int32).reshape(n_tok_tiles, m // n_tok_tiles, 2)
    rank = route[:, 4:6].astype(jnp.int32).reshape(n_tok_tiles, m // n_tok_tiles, 2)
    hit = expert[..., None] == jnp.arange(N_EXPERTS, dtype=jnp.int32)
    pos = rank + jnp.sum(jnp.where(hit, base[:, None, None, :], 0), axis=-1)
    pos = pos.reshape(m, 2)
    num_tiles = (ends[-1] // tr).astype(jnp.int32)
    tile_idx = jnp.minimum(jnp.arange(n_tiles, dtype=jnp.int32), num_tiles - 1)
    tile_expert = jnp.sum((tile_idx[:, None] * tr >= ends[None, :]).astype(jnp.int32), axis=1)
    n_slots = 2 * m
    slot = jnp.arange(n_slots, dtype=jnp.int32)
    sorted_key = jnp.sort(route[:, 0:2].astype(jnp.int32).reshape(-1) * n_slots + slot)
    sorted_tokens = (sorted_key & (n_slots - 1)) >> 1
    sorted_tokens = jnp.pad(sorted_tokens, (0, IDS_WINDOW_ROWS * LANES)).reshape(-1, LANES)
    dense_starts = jnp.cumsum(counts) - counts
    in_expert = tile_idx - (starts // tr)[tile_expert]
    tile_base = dense_starts[tile_expert] + in_expert * tr
    tile_rows = jnp.clip(counts[tile_expert] - in_expert * tr, 0, tr)
    tile_rows = jnp.where(jnp.arange(n_tiles) < num_tiles, tile_rows, 0)
    ids = jnp.arange(N_EXPERTS, dtype=jnp.int32)
    present = jnp.where(counts > 0, ids, N_EXPERTS)
    next_ge = lax.cummin(present, axis=0, reverse=True)
    next_gt = jnp.concatenate([next_ge[1:], jnp.full((1,), N_EXPERTS, jnp.int32)])
    next_gt = jnp.where(next_gt >= N_EXPERTS, -1, next_gt)
    plan = dict(tile_expert=tile_expert.astype(jnp.int32), tile_base=tile_base.astype(jnp.int32),
                tile_rows=tile_rows.astype(jnp.int32), next_expert=next_gt[tile_expert].astype(jnp.int32),
                num_tiles=num_tiles.reshape(1), sorted_tokens=sorted_tokens)
    return plan, pos


def kernel(x, norm_mix_g, w_in, w_gk_up, b_gk, gla_norm_g, swa_sinks, w_out, norm_ffn_g,
           w_group, b_group, w_router, b_router, w_gate, w_up, w_down, norm_final_g):
    batch, seq, d = x.shape
    m = batch * seq
    assert w_in.shape[0] == 1, "single-layer block"
    x2 = x.reshape(m, d)
    lr0 = OFF_GR + GLA_V_W
    w_in_b = w_in[0].astype(BF16)
    w_main = jnp.concatenate([w_in_b[:, :lr0], w_in_b[:, lr0 + GLA_RANK:]], axis=1)
    w_lr = jnp.pad(w_in_b[:, lr0:lr0 + GLA_RANK], ((0, 0), (0, LANES - GLA_RANK)))
    wup = jnp.pad(w_gk_up[0], ((0, LANES - GLA_RANK), (0, 0))).astype(BF16)
    w_rt = jnp.pad(jnp.concatenate([w_group[0], w_router[0]], axis=1),
                   ((0, 0), (0, LANES - N_GROUPS - N_EXPERTS))).astype(BF16)
    b_rt = jnp.pad(jnp.concatenate([b_group[0], b_router[0]]),
                   (0, LANES - N_GROUPS - N_EXPERTS)).reshape(1, LANES)

    proj, glr = _in_proj(x2, norm_mix_g[0].reshape(1, d), w_main, w_lr)
    o_gla = _gla(proj, glr, wup, b_gk[0].reshape(1, GLA_QK_W),
                 gla_norm_g[0].reshape(1, GLA_V_W), batch, seq)
    o_swa = _swa(proj, swa_sinks[0], batch, seq)
    h_mid, hn, route, tile_counts = _out_route(o_gla, o_swa, x2, w_out[0].astype(BF16),
                                               norm_ffn_g[0].reshape(1, d), w_rt, b_rt)
    plan, pos2d = _dispatch_plan(route, tile_counts, m)
    y = _moe(hn, w_gate[0], w_up[0], w_down[0], plan)
    out = _combine(y, h_mid, route, norm_final_g.reshape(1, d), pos2d)
    return out.reshape(batch, seq, d)
```

```python
import functools

import jax
import jax.numpy as jnp
import numpy as np
from jax import lax
from jax.experimental import pallas as pl
from jax.experimental.pallas import tpu as pltpu

F32 = jnp.float32
BF16 = jnp.bfloat16

D_MODEL = 2048
GLA_HEADS = 4
GLA_DK = 128
GLA_DV = 256
GLA_RANK = 16
GLA_GATE_NORM = 16.0
GLA_CHUNK = 64
SWA_Q_HEADS = 16
SWA_KV_HEADS = 4
SWA_GROUP = SWA_Q_HEADS // SWA_KV_HEADS
SWA_HEAD_DIM = 64
SWA_BLOCK = 128
N_GROUPS = 4
EXPERTS_PER_GROUP = 16
N_EXPERTS = N_GROUPS * EXPERTS_PER_GROUP
D_EXPERT = 256
RMS_EPS = 1e-6

GLA_QK_W = GLA_HEADS * GLA_DK
GLA_V_W = GLA_HEADS * GLA_DV
SWA_Q_W = SWA_Q_HEADS * SWA_HEAD_DIM
SWA_KV_W = SWA_KV_HEADS * SWA_HEAD_DIM
OFF_GQ = 0
OFF_GK = OFF_GQ + GLA_QK_W
OFF_GV = OFF_GK + GLA_QK_W
OFF_GR = OFF_GV + GLA_V_W
OFF_SQ = OFF_GR + GLA_V_W
OFF_SK = OFF_SQ + SWA_Q_W
OFF_SV = OFF_SK + SWA_KV_W
PROJ_W = OFF_SV + SWA_KV_W
LANES = 128

IN_TM = 1024
IN_TN = 2304
OUT_TM = 256
MOE_TR = 256
CMB_TM = 256
VMEM_LIMIT = 56 * 1024 * 1024


def _silu(x):
    return x / (1.0 + jnp.exp(-x))


ALIBI_PARTS = 3


def _bf16_parts(value, n):
    parts, rest = [], np.float32(value)
    for _ in range(n):
        piece = np.float32(np.asarray(rest).astype(jnp.bfloat16))
        parts.append(float(piece))
        rest = np.float32(rest - piece)
    return parts


def _inproj_kernel(x_ref, g_ref, w_ref, wlr_ref, proj_ref, glr_ref, xn_ref):
    @pl.when(pl.program_id(1) == 0)
    def _():
        x = x_ref[...]
        ms = jnp.mean(x * x, axis=-1, keepdims=True)
        xn = (x * lax.rsqrt(ms + RMS_EPS) * g_ref[...]).astype(BF16)
        xn_ref[...] = xn
        glr_ref[...] = jnp.dot(xn, wlr_ref[...], preferred_element_type=F32)

    proj_ref[...] = jnp.dot(xn_ref[...], w_ref[...], preferred_element_type=F32).astype(BF16)


def _in_proj(x2, g, w_main, w_lr):
    m = x2.shape[0]
    tm = min(IN_TM, m)
    return pl.pallas_call(
        _inproj_kernel,
        grid=(m // tm, PROJ_W // IN_TN),
        in_specs=[
            pl.BlockSpec((tm, D_MODEL), lambda i, j: (i, 0)),
            pl.BlockSpec((1, D_MODEL), lambda i, j: (0, 0)),
            pl.BlockSpec((D_MODEL, IN_TN), lambda i, j: (0, j)),
            pl.BlockSpec((D_MODEL, LANES), lambda i, j: (0, 0)),
        ],
        out_specs=[
            pl.BlockSpec((tm, IN_TN), lambda i, j: (i, j)),
            pl.BlockSpec((tm, LANES), lambda i, j: (i, 0)),
        ],
        out_shape=[
            jax.ShapeDtypeStruct((m, PROJ_W), BF16),
            jax.ShapeDtypeStruct((m, LANES), F32),
        ],
        scratch_shapes=[pltpu.VMEM((tm, D_MODEL), BF16)],
        compiler_params=pltpu.CompilerParams(
            dimension_semantics=("arbitrary", "arbitrary"), vmem_limit_bytes=VMEM_LIMIT),
        name="in_proj",
    )(x2, g, w_main, w_lr)


def _gla_kernel(q_ref, k_ref, v_ref, r_ref, glr_ref, wup_ref, bgk_ref, ng_ref, o_ref,
                la_ref, oi_ref, qi_ref, ki_ref, qd_ref, kd_ref, a_ref, kv_ref, dec_ref, sp_ref):
    t = q_ref.shape[0]
    c = GLA_CHUNK
    nchunk = t // c
    z = jnp.dot(glr_ref[...].astype(BF16), wup_ref[...], preferred_element_type=F32) + bgk_ref[...]
    la_ref[...] = (jnp.minimum(z, 0.0) - jnp.log1p(jnp.exp(-jnp.abs(z)))) * (1.0 / GLA_GATE_NORM)

    ii = lax.broadcasted_iota(jnp.int32, (c, c), 0)
    jj = lax.broadcasted_iota(jnp.int32, (c, c), 1)
    causal = jj <= ii
    tri = causal.astype(BF16)
    nt = (((1,), (1,)), ((), ()))
    tn = (((0,), (0,)), ((), ()))

    def chunk_rows(n):
        return pl.ds(pl.multiple_of(n * c, c), c)

    def decays(n, carry):
        rows = chunk_rows(n)
        la = la_ref[rows, :]
        hi = la.astype(BF16)
        r1 = la - hi.astype(F32)
        mid = r1.astype(BF16)
        lo = (r1 - mid.astype(F32)).astype(BF16)
        parts = jnp.dot(tri, jnp.concatenate([hi, mid, lo], axis=1), preferred_element_type=F32)
        bcum = parts[:, 0:GLA_DK] + parts[:, GLA_DK:2 * GLA_DK] + parts[:, 2 * GLA_DK:]
        b_mid = bcum[c // 2 - 1:c // 2, :]
        b_last = bcum[c - 1:c, :]
        q = q_ref[rows, :].astype(F32) * (GLA_DK ** -0.5)
        k = k_ref[rows, :].astype(F32)
        qi_ref[rows, :] = (q * jnp.exp(bcum - b_mid)).astype(BF16)
        ki_ref[rows, :] = (k * jnp.exp(b_mid - bcum)).astype(BF16)
        qd_ref[rows, :] = (q * jnp.exp(bcum)).astype(BF16)
        kd_ref[rows, :] = (k * jnp.exp(b_last - bcum)).astype(BF16)
        dec_ref[n] = jnp.broadcast_to(jnp.exp(b_last), dec_ref.shape[1:])
        return carry

    lax.fori_loop(0, nchunk, decays, 0, unroll=4)

    def scores(n, carry):
        rows = chunk_rows(n)
        a = lax.dot_general(qi_ref[rows, :], ki_ref[rows, :], nt, preferred_element_type=F32)
        a_ref[rows, :] = jnp.where(causal, a, 0.0).astype(BF16)
        return carry

    lax.fori_loop(0, nchunk, scores, 0, unroll=8)

    def intra(n, carry):
        rows = chunk_rows(n)
        v = v_ref[rows, :]
        oi_ref[rows, :] = jnp.dot(a_ref[rows, :], v, preferred_element_type=F32)
        kv_ref[n] = lax.dot_general(v, kd_ref[rows, :], tn, preferred_element_type=F32)
        return carry

    lax.fori_loop(0, nchunk, intra, 0, unroll=8)

    def scan(n, s_t):
        sp_ref[n] = s_t.astype(BF16)
        return s_t * dec_ref[n][0:1, :] + kv_ref[n]

    lax.fori_loop(0, nchunk, scan, jnp.zeros((GLA_DV, GLA_DK), F32))

    def inter(n, carry):
        rows = chunk_rows(n)
        o = oi_ref[rows, :] + lax.dot_general(qd_ref[rows, :], sp_ref[n], nt,
                                              preferred_element_type=F32)
        ms = jnp.mean(o * o, axis=-1, keepdims=True)
        o = o * lax.rsqrt(ms + RMS_EPS) * ng_ref[...]
        o = o * _silu(r_ref[rows, :].astype(F32))
        o_ref[rows, :] = o.astype(BF16)
        return carry

    lax.fori_loop(0, nchunk, inter, 0, unroll=8)


def _gla(proj, glr, wup, bgk, ng, batch, seq):
    m = proj.shape[0]
    return pl.pallas_call(
        _gla_kernel,
        grid=(batch, GLA_HEADS),
        in_specs=[
            pl.BlockSpec((seq, GLA_DK), lambda b, h: (b, OFF_GQ // GLA_DK + h)),
            pl.BlockSpec((seq, GLA_DK), lambda b, h: (b, OFF_GK // GLA_DK + h)),
            pl.BlockSpec((seq, GLA_DV), lambda b, h: (b, OFF_GV // GLA_DV + h)),
            pl.BlockSpec((seq, GLA_DV), lambda b, h: (b, OFF_GR // GLA_DV + h)),
            pl.BlockSpec((seq, LANES), lambda b, h: (b, 0)),
            pl.BlockSpec((LANES, GLA_DK), lambda b, h: (0, h)),
            pl.BlockSpec((1, GLA_DK), lambda b, h: (0, h)),
            pl.BlockSpec((1, GLA_DV), lambda b, h: (0, h)),
        ],
        out_specs=pl.BlockSpec((seq, GLA_DV), lambda b, h: (b, h)),
        out_shape=jax.ShapeDtypeStruct((m, GLA_V_W), BF16),
        scratch_shapes=[
            pltpu.VMEM((seq, GLA_DK), F32),
            pltpu.VMEM((seq, GLA_DV), F32),
            pltpu.VMEM((seq, GLA_DK), BF16),
            pltpu.VMEM((seq, GLA_DK), BF16),
            pltpu.VMEM((seq, GLA_DK), BF16),
            pltpu.VMEM((seq, GLA_DK), BF16),
            pltpu.VMEM((seq, GLA_CHUNK), BF16),
            pltpu.VMEM((seq // GLA_CHUNK, GLA_DV, GLA_DK), F32),
            pltpu.VMEM((seq // GLA_CHUNK, 8, GLA_DK), F32),
            pltpu.VMEM((seq // GLA_CHUNK, GLA_DV, GLA_DK), BF16),
        ],
        compiler_params=pltpu.CompilerParams(
            dimension_semantics=("arbitrary", "arbitrary"), vmem_limit_bytes=VMEM_LIMIT),
        name="gla",
    )(proj, proj, proj, proj, glr, wup, bgk, ng)


def _swa_kernel(sink_ref, q_ref, kp_ref, kc_ref, vp_ref, vc_ref, o_ref):
    blk = SWA_BLOCK
    half = SWA_HEAD_DIM
    n = pl.program_id(1)
    k_all = jnp.concatenate([kp_ref[...], kc_ref[...]], axis=0)
    v_all = jnp.concatenate([vp_ref[...], vc_ref[...]], axis=0)
    qi = lax.broadcasted_iota(jnp.int32, (blk, 2 * blk), 0)
    kj = lax.broadcasted_iota(jnp.int32, (blk, 2 * blk), 1)
    rel = qi + blk - kj
    valid = (rel >= 0) & (rel < blk) & ((kj >= blk) | (n > 0))
    sink_col = kj == 0
    lane_kv = lax.broadcasted_iota(jnp.int32, (2 * blk, LANES), 1)
    lo_kv = lane_kv < half
    lane_q = lax.broadcasted_iota(jnp.int32, (blk, LANES), 1)
    lo_q = lane_q < half
    nt = (((1,), (1,)), ((), ()))
    key_idx = lax.broadcasted_iota(jnp.int32, (2 * blk, LANES), 0).astype(F32)
    key_cols = jnp.where((lane_kv & (half - 1)) < ALIBI_PARTS, key_idx, 0.0)
    q_pos = lax.broadcasted_iota(jnp.int32, (blk, 1), 0).astype(F32) + float(blk)
    q_scale = jnp.asarray(SWA_HEAD_DIM ** -0.5, BF16)

    for p in range(SWA_KV_HEADS // 2):
        k_slab = k_all[:, p * LANES:(p + 1) * LANES].astype(F32)
        v_slab = v_all[:, p * LANES:(p + 1) * LANES].astype(F32)
        k_roll = pltpu.roll(k_slab, half, 1)
        v_roll = pltpu.roll(v_slab, half, 1)
        for hh in range(2):
            h = 2 * p + hh
            k_lo, k_hi = (k_slab, k_roll) if hh == 0 else (k_roll, k_slab)
            kd = (jnp.where(lo_kv, k_lo, key_cols).astype(BF16),
                  jnp.where(lo_kv, key_cols, k_hi).astype(BF16))
            v_lo, v_hi = (v_slab, v_roll) if hh == 0 else (v_roll, v_slab)
            vd = jnp.where(lo_kv, v_lo, v_hi)
            vd = jnp.where(key_idx == 0.0, 0.0, vd).astype(BF16)
            for gp in range(SWA_GROUP // 2):
                col = (h * (SWA_GROUP // 2) + gp) * LANES
                qs = q_ref[:, col:col + LANES] * q_scale
                outs = []
                for gg in range(2):
                    head = h * SWA_GROUP + 2 * gp + gg
                    parts = _bf16_parts(2.0 ** (-8.0 * (head + 1) / SWA_Q_HEADS), ALIBI_PARTS)
                    slope = sum(parts)
                    aug = jnp.zeros((blk, LANES), F32)
                    for idx, part in enumerate(parts):
                        aug = jnp.where(lane_q == (half if gg == 0 else 0) + idx, part, aug)
                    keep = lo_q if gg == 0 else jnp.logical_not(lo_q)
                    qm = jnp.where(keep, qs, aug.astype(BF16))
                    s = lax.dot_general(qm, kd[gg], nt, preferred_element_type=F32)
                    sink = sink_ref[head] + slope * q_pos
                    s = jnp.where(sink_col, sink, jnp.where(valid, s, -jnp.inf))
                    mx = jnp.max(s, axis=-1, keepdims=True)
                    pe = jnp.exp(s - mx)
                    den = jnp.sum(pe, axis=-1, keepdims=True)
                    o = jnp.dot(pe.astype(BF16), vd, preferred_element_type=F32)
                    outs.append(o / den)
                o_ref[:, col:col + LANES] = jnp.where(lo_q, outs[0], outs[1]).astype(BF16)


def _swa(proj, sinks, batch, seq):
    m = proj.shape[0]
    nb = seq // SWA_BLOCK
    qcol = OFF_SQ // SWA_Q_W
    kcol = OFF_SK // SWA_KV_W
    vcol = OFF_SV // SWA_KV_W
    cur = lambda c: (lambda b, n: (b * nb + n, c))
    prev = lambda c: (lambda b, n: (b * nb + jnp.maximum(n - 1, 0), c))
    return pl.pallas_call(
        _swa_kernel,
        grid=(batch, nb),
        in_specs=[
            pl.BlockSpec(memory_space=pltpu.SMEM),
            pl.BlockSpec((SWA_BLOCK, SWA_Q_W), cur(qcol)),
            pl.BlockSpec((SWA_BLOCK, SWA_KV_W), prev(kcol)),
            pl.BlockSpec((SWA_BLOCK, SWA_KV_W), cur(kcol)),
            pl.BlockSpec((SWA_BLOCK, SWA_KV_W), prev(vcol)),
            pl.BlockSpec((SWA_BLOCK, SWA_KV_W), cur(vcol)),
        ],
        out_specs=pl.BlockSpec((SWA_BLOCK, SWA_Q_W), lambda b, n: (b * nb + n, 0)),
        out_shape=jax.ShapeDtypeStruct((m, SWA_Q_W), BF16),
        compiler_params=pltpu.CompilerParams(
            dimension_semantics=("arbitrary", "arbitrary"), vmem_limit_bytes=VMEM_LIMIT),
        name="swa",
    )(sinks, proj, proj, proj, proj, proj)


def _out_route_kernel(og_ref, os_ref, x_ref, wo_ref, g_ref, wr_ref, br_ref,
                      h_ref, route_ref, cnt_ref, hnb_ref):
    t = pl.program_id(0)
    slot = lax.rem(t, 2)

    @pl.when(t == 0)
    def _():
        hnb_ref[...] = jnp.zeros_like(hnb_ref)

    logits = jnp.dot(hnb_ref[1 - slot], wr_ref[...], preferred_element_type=F32) + br_ref[...]

    h = x_ref[...]
    h = h + jnp.dot(og_ref[...], wo_ref[0:GLA_V_W, :], preferred_element_type=F32)
    h = h + jnp.dot(os_ref[...], wo_ref[GLA_V_W:, :], preferred_element_type=F32)
    h_ref[...] = h
    ms = jnp.mean(h * h, axis=-1, keepdims=True)
    hn = h * lax.rsqrt(ms + RMS_EPS) * g_ref[...]
    hnb_ref[slot] = hn.astype(BF16)

    lane = lax.broadcasted_iota(jnp.int32, logits.shape, 1)
    lanef = lane.astype(F32)
    big = float(LANES)
    ninf = -jnp.inf
    gl = jnp.where(lane < N_GROUPS, logits, ninf)
    gmax = jnp.max(gl, axis=-1, keepdims=True)
    g_p = 1.0 / jnp.sum(jnp.exp(gl - gmax), axis=-1, keepdims=True)
    g_idx = jnp.min(jnp.where(gl == gmax, lanef, big), axis=-1, keepdims=True)
    lo = N_GROUPS + EXPERTS_PER_GROUP * g_idx
    el = jnp.where((lanef >= lo) & (lanef < lo + EXPERTS_PER_GROUP), logits, ninf)
    m1 = jnp.max(el, axis=-1, keepdims=True)
    i1 = jnp.min(jnp.where(el == m1, lanef, big), axis=-1, keepdims=True)
    el2 = jnp.where(lanef == i1, ninf, el)
    m2 = jnp.max(el2, axis=-1, keepdims=True)
    i2 = jnp.min(jnp.where(el2 == m2, lanef, big), axis=-1, keepdims=True)
    d = jnp.exp(m2 - m1)
    c1 = g_p / (1.0 + d)
    c2 = g_p * d / (1.0 + d)
    tm = logits.shape[0]
    chosen = ((lanef == i1) | (lanef == i2)).astype(BF16)
    ri = lax.broadcasted_iota(jnp.int32, (tm, tm), 0)
    ci = lax.broadcasted_iota(jnp.int32, (tm, tm), 1)
    earlier = jnp.dot((ci < ri).astype(BF16), chosen, preferred_element_type=F32)
    r1 = jnp.sum(jnp.where(lanef == i1, earlier, 0.0), axis=-1, keepdims=True)
    r2 = jnp.sum(jnp.where(lanef == i2, earlier, 0.0), axis=-1, keepdims=True)
    cnt = jnp.sum(chosen.astype(F32), axis=0, keepdims=True)
    cnt_ref[...] = jnp.broadcast_to(cnt, cnt_ref.shape)

    fields = [i1 - N_GROUPS, i2 - N_GROUPS, c1, c2, r1, r2]
    route = jnp.zeros_like(logits)
    for idx, val in enumerate(fields):
        route = jnp.where(lane == idx, val, route)
    route_ref[...] = route


def _out_route(o_gla, o_swa, x2, w_out, g, w_rt, b_rt):
    m = x2.shape[0]
    tm = min(OUT_TM, m)
    ntile = m // tm
    row = lambda i: (jnp.minimum(i, ntile - 1), 0)
    routed = lambda i: (jnp.maximum(i - 1, 0), 0)
    fixed = lambda i: (0, 0)
    return pl.pallas_call(
        _out_route_kernel,
        grid=(ntile + 1,),
        in_specs=[
            pl.BlockSpec((tm, GLA_V_W), row),
            pl.BlockSpec((tm, SWA_Q_W), row),
            pl.BlockSpec((tm, D_MODEL), row),
            pl.BlockSpec((GLA_V_W + SWA_Q_W, D_MODEL), fixed),
            pl.BlockSpec((1, D_MODEL), fixed),
            pl.BlockSpec((D_MODEL, LANES), fixed),
            pl.BlockSpec((1, LANES), fixed),
        ],
        out_specs=[
            pl.BlockSpec((tm, D_MODEL), row),
            pl.BlockSpec((tm, LANES), routed),
            pl.BlockSpec((1, 8, LANES), lambda i: (jnp.maximum(i - 1, 0), 0, 0)),
        ],
        out_shape=[
            jax.ShapeDtypeStruct((m, D_MODEL), F32),
            jax.ShapeDtypeStruct((m, LANES), F32),
            jax.ShapeDtypeStruct((m // tm, 8, LANES), F32),
        ],
        scratch_shapes=[pltpu.VMEM((2, tm, D_MODEL), BF16)],
        compiler_params=pltpu.CompilerParams(
            dimension_semantics=("arbitrary",), vmem_limit_bytes=VMEM_LIMIT),
        name="out_route",
    )(o_gla, o_swa, x2, w_out, g, w_rt, b_rt)


LANES_LOG2 = 7
MOE_ROW_GROUP_LOG2 = 3
MOE_ROW_GROUP = 1 << MOE_ROW_GROUP_LOG2
IDS_WINDOW_ROWS = 3


def _moe_kernel(te_ref, cb_ref, nv_ref, ne_ref, nt_ref, tok_hbm, h_hbm, wg_hbm, wu_hbm, wd_hbm, g_ref, y_ref,
                xbuf, gsem, ids_smem, isem, wgs, wus, wds, wsem, wgb, wub, wdb):
    tr = MOE_TR
    grp = MOE_ROW_GROUP
    t = pl.program_id(0)
    nt = nt_ref[0]
    gs = lax.rem(t, 2)
    cs = 1 - gs

    def ids_copy(tile, s):
        row0 = lax.shift_right_logical(cb_ref[tile], LANES_LOG2)
        return pltpu.make_async_copy(tok_hbm.at[pl.ds(row0, IDS_WINDOW_ROWS)], ids_smem.at[s], isem.at[s])

    def weight_copies(e):
        return (pltpu.make_async_copy(wg_hbm.at[e], wgs, wsem.at[0]),
                pltpu.make_async_copy(wu_hbm.at[e], wus, wsem.at[1]),
                pltpu.make_async_copy(wd_hbm.at[e], wds, wsem.at[2]))

    def rows_used(tile):
        groups = lax.shift_right_logical(nv_ref[tile] + (grp - 1), MOE_ROW_GROUP_LOG2)
        return lax.shift_left(groups, MOE_ROW_GROUP_LOG2)

    @pl.when(t == 0)
    def _():
        xbuf[...] = jnp.zeros_like(xbuf)
        ids_copy(0, 0).start()
        for cp in weight_copies(te_ref[0]):
            cp.start()

    @pl.when(t < nt)
    def _():
        ids_copy(t, gs).wait()

        @pl.when(t + 1 < nt)
        def _():
            ids_copy(t + 1, cs).start()

        off = cb_ref[t] & (LANES - 1)
        n_real = nv_ref[t]
        for g0 in range(0, tr, grp):
            @pl.when(g0 < n_real)
            def _():
                for r in range(g0, g0 + grp):
                    q = off + r
                    tok = ids_smem[gs, lax.shift_right_logical(q, LANES_LOG2), q & (LANES - 1)]
                    pltpu.make_async_copy(h_hbm.at[pl.ds(tok, 1)], xbuf.at[gs, pl.ds(r, 1)],
                                          gsem.at[gs]).start(priority=r % 2)

    @pl.when((t >= 1) & (t <= nt))
    def _():
        c = t - 1
        changed = (c == 0) | (te_ref[c] != te_ref[jnp.maximum(c - 1, 0)])

        @pl.when(changed)
        def _():
            for cp in weight_copies(te_ref[c]):
                cp.wait()
            wgb[...] = wgs[...].astype(BF16)
            wub[...] = wus[...].astype(BF16)
            wdb[...] = wds[...].astype(BF16)
            nxt = ne_ref[c]

            @pl.when(nxt >= 0)
            def _():
                for cp in weight_copies(nxt):
                    cp.start()

        n_rows = pl.multiple_of(rows_used(c), grp)
        pltpu.make_async_copy(h_hbm.at[pl.ds(0, n_rows)], xbuf.at[cs, pl.ds(0, n_rows)],
                              gsem.at[cs]).wait()
        hrow = xbuf[cs]
        ms = jnp.mean(hrow * hrow, axis=-1, keepdims=True)
        x = (hrow * lax.rsqrt(ms + RMS_EPS) * g_ref[...]).astype(BF16)
        g = jnp.dot(x, wgb[...], preferred_element_type=F32)
        u = jnp.dot(x, wub[...], preferred_element_type=F32)
        hmid = (_silu(g) * u).astype(BF16)
        y_ref[...] = jnp.dot(hmid, wdb[...], preferred_element_type=F32)

    @pl.when(t > nt)
    def _():
        y_ref[...] = jnp.zeros_like(y_ref)


def _moe(h, norm_g, w_gate, w_up, w_down, plan):
    n_tiles = plan["tile_expert"].shape[0]
    p_rows = n_tiles * MOE_TR
    prev = lambda t: jnp.maximum(t - 1, 0)
    any_spec = pl.BlockSpec(memory_space=pl.ANY)
    grid_spec = pltpu.PrefetchScalarGridSpec(
        num_scalar_prefetch=5,
        grid=(n_tiles + 1,),
        in_specs=[any_spec] * 5 + [pl.BlockSpec((1, D_MODEL), lambda t, *_: (0, 0))],
        out_specs=pl.BlockSpec((MOE_TR, D_MODEL), lambda t, *_: (prev(t), 0)),
        scratch_shapes=[
            pltpu.VMEM((2, MOE_TR, D_MODEL), F32),
            pltpu.SemaphoreType.DMA((2,)),
            pltpu.SMEM((2, IDS_WINDOW_ROWS, LANES), jnp.int32),
            pltpu.SemaphoreType.DMA((2,)),
            pltpu.VMEM((D_MODEL, D_EXPERT), F32),
            pltpu.VMEM((D_MODEL, D_EXPERT), F32),
            pltpu.VMEM((D_EXPERT, D_MODEL), F32),
            pltpu.SemaphoreType.DMA((3,)),
            pltpu.VMEM((D_MODEL, D_EXPERT), BF16),
            pltpu.VMEM((D_MODEL, D_EXPERT), BF16),
            pltpu.VMEM((D_EXPERT, D_MODEL), BF16),
        ],
    )
    return pl.pallas_call(
        _moe_kernel,
        grid_spec=grid_spec,
        out_shape=jax.ShapeDtypeStruct((p_rows, D_MODEL), F32),
        compiler_params=pltpu.CompilerParams(
            dimension_semantics=("arbitrary",), vmem_limit_bytes=VMEM_LIMIT),
        name="moe",
    )(plan["tile_expert"], plan["tile_base"], plan["tile_rows"], plan["next_expert"], plan["num_tiles"],
      plan["sorted_tokens"], h, w_gate, w_up, w_down, norm_g)


def _combine_kernel(pos_hbm, y_hbm, h_ref, route_ref, g_ref, o_ref, ybuf, gsem, ids_smem, isem):
    tm = h_ref.shape[0]
    t = pl.program_id(0)
    ntile = pl.num_programs(0) - 1

    def ids_copy(tile, s):
        return pltpu.make_async_copy(pos_hbm.at[tile], ids_smem.at[s], isem.at[s])

    @pl.when(t == 0)
    def _():
        ids_copy(0, 0).start()

    @pl.when(t < ntile)
    def _():
        s = lax.rem(t, 2)
        ids_copy(t, s).wait()

        @pl.when(t + 1 < ntile)
        def _():
            ids_copy(t + 1, 1 - s).start()

        for kk in range(2):
            for r in range(tm):
                flat = kk * tm + r
                row = ids_smem[s, flat // LANES, flat % LANES]
                pltpu.make_async_copy(y_hbm.at[pl.ds(row, 1)], ybuf.at[s, kk, pl.ds(r, 1)],
                                      gsem.at[s, kk]).start(priority=r % 2)

    @pl.when(t >= 1)
    def _():
        s = lax.rem(t - 1, 2)
        for kk in range(2):
            pltpu.make_async_copy(y_hbm.at[pl.ds(0, tm)], ybuf.at[s, kk], gsem.at[s, kk]).wait()
        route = route_ref[...]
        h = h_ref[...] + route[:, 2:3] * ybuf[s, 0] + route[:, 3:4] * ybuf[s, 1]
        ms = jnp.mean(h * h, axis=-1, keepdims=True)
        o_ref[...] = h * lax.rsqrt(ms + RMS_EPS) * g_ref[...]


def _combine(y, h, route, g, pos2d):
    m = h.shape[0]
    tm = min(CMB_TM, m)
    ntile = m // tm
    pos3 = pos2d.reshape(ntile, tm, 2).transpose(0, 2, 1).reshape(ntile, 2 * tm // LANES, LANES)
    row = lambda i: (jnp.maximum(i - 1, 0), 0)
    return pl.pallas_call(
        _combine_kernel,
        grid=(ntile + 1,),
        in_specs=[
            pl.BlockSpec(memory_space=pl.ANY),
            pl.BlockSpec(memory_space=pl.ANY),
            pl.BlockSpec((tm, D_MODEL), row),
            pl.BlockSpec((tm, LANES), row),
            pl.BlockSpec((1, D_MODEL), lambda i: (0, 0)),
        ],
        out_specs=pl.BlockSpec((tm, D_MODEL), row),
        out_shape=jax.ShapeDtypeStruct((m, D_MODEL), F32),
        scratch_shapes=[
            pltpu.VMEM((2, 2, tm, D_MODEL), F32),
            pltpu.SemaphoreType.DMA((2, 2)),
            pltpu.SMEM((2, 2 * tm // LANES, LANES), jnp.int32),
            pltpu.SemaphoreType.DMA((2,)),
        ],
        compiler_params=pltpu.CompilerParams(
            dimension_semantics=("arbitrary",), vmem_limit_bytes=VMEM_LIMIT),
        name="combine",
    )(pos3, y, h, route, g)


def _dispatch_plan(route, tile_counts, m):
    tr = MOE_TR
    p_rows = 2 * m + N_EXPERTS * tr
    n_tiles = p_rows // tr
    n_tok_tiles = tile_counts.shape[0]
    cnt = tile_counts[:, 0, N_GROUPS:N_GROUPS + N_EXPERTS].astype(jnp.int32)
    before_tile = jnp.cumsum(cnt, axis=0) - cnt
    counts = jnp.sum(cnt, axis=0)
    padded = ((counts + tr - 1) // tr) * tr
    ends = jnp.cumsum(padded)
    starts = ends - padded
    base = (starts[None, :] + before_tile).astype(jnp.int32)
    expert = route[:, 0:2].astype(jnp.int32).reshape(n_tok_tiles, m // n_tok_tiles, 2)
    rank = route[:, 4:6].astype(jnp.int32).reshape(n_tok_tiles, m // n_tok_tiles, 2)
    hit = expert[..., None] == jnp.arange(N_EXPERTS, dtype=jnp.int32)
    pos = rank + jnp.sum(jnp.where(hit, base[:, None, None, :], 0), axis=-1)
    pos = pos.reshape(m, 2)
    num_tiles = (ends[-1] // tr).astype(jnp.int32)
    tile_idx = jnp.minimum(jnp.arange(n_tiles, dtype=jnp.int32), num_tiles - 1)
    tile_expert = jnp.sum((tile_idx[:, None] * tr >= ends[None, :]).astype(jnp.int32), axis=1)
    n_slots = 2 * m
    slot = jnp.arange(n_slots, dtype=jnp.int32)
    sorted_key = jnp.sort(route[:, 0:2].astype(jnp.int32).reshape(-1) * n_slots + slot)
    sorted_tokens = (sorted_key & (n_slots - 1)) >> 1
    sorted_tokens = jnp.pad(sorted_tokens, (0, IDS_WINDOW_ROWS * LANES)).reshape(-1, LANES)
    dense_starts = jnp.cumsum(counts) - counts
    in_expert = tile_idx - (starts // tr)[tile_expert]
    tile_base = dense_starts[tile_expert] + in_expert * tr
    tile_rows = jnp.clip(counts[tile_expert] - in_expert * tr, 0, tr)
    tile_rows = jnp.where(jnp.arange(n_tiles) < num_tiles, tile_rows, 0)
    ids = jnp.arange(N_EXPERTS, dtype=jnp.int32)
    present = jnp.where(counts > 0, ids, N_EXPERTS)
    next_ge = lax.cummin(present, axis=0, reverse=True)
    next_gt = jnp.concatenate([next_ge[1:], jnp.full((1,), N_EXPERTS, jnp.int32)])
    next_gt = jnp.where(next_gt >= N_EXPERTS, -1, next_gt)
    plan = dict(tile_expert=tile_expert.astype(jnp.int32), tile_base=tile_base.astype(jnp.int32),
                tile_rows=tile_rows.astype(jnp.int32), next_expert=next_gt[tile_expert].astype(jnp.int32),
                num_tiles=num_tiles.reshape(1), sorted_tokens=sorted_tokens)
    return plan, pos


def kernel(x, norm_mix_g, w_in, w_gk_up, b_gk, gla_norm_g, swa_sinks, w_out, norm_ffn_g,
           w_group, b_group, w_router, b_router, w_gate, w_up, w_down, norm_final_g):
    batch, seq, d = x.shape
    m = batch * seq
    assert w_in.shape[0] == 1, "single-layer block"
    x2 = x.reshape(m, d)
    lr0 = OFF_GR + GLA_V_W
    w_in_b = w_in[0].astype(BF16)
    w_main = jnp.concatenate([w_in_b[:, :lr0], w_in_b[:, lr0 + GLA_RANK:]], axis=1)
    w_lr = jnp.pad(w_in_b[:, lr0:lr0 + GLA_RANK], ((0, 0), (0, LANES - GLA_RANK)))
    wup = jnp.pad(w_gk_up[0], ((0, LANES - GLA_RANK), (0, 0))).astype(BF16)
    w_rt = jnp.pad(jnp.concatenate([w_group[0], w_router[0]], axis=1),
                   ((0, 0), (0, LANES - N_GROUPS - N_EXPERTS))).astype(BF16)
    b_rt = jnp.pad(jnp.concatenate([b_group[0], b_router[0]]),
                   (0, LANES - N_GROUPS - N_EXPERTS)).reshape(1, LANES)

    proj, glr = _in_proj(x2, norm_mix_g[0].reshape(1, d), w_main, w_lr)
    o_gla = _gla(proj, glr, wup, b_gk[0].reshape(1, GLA_QK_W),
                 gla_norm_g[0].reshape(1, GLA_V_W), batch, seq)
    o_swa = _swa(proj, swa_sinks[0], batch, seq)
    ffn_g = norm_ffn_g[0].reshape(1, d)
    h_mid, route, tile_counts = _out_route(o_gla, o_swa, x2, w_out[0].astype(BF16), ffn_g, w_rt, b_rt)
    plan, pos2d = _dispatch_plan(route, tile_counts, m)
    y = _moe(h_mid, ffn_g, w_gate[0], w_up[0], w_down[0], plan)
    out = _combine(y, h_mid, route, norm_final_g.reshape(1, d), pos2d)
    return out.reshape(batch, seq, d)
```

```python
import functools

import jax
import jax.numpy as jnp
import numpy as np
from jax import lax
from jax.experimental import pallas as pl
from jax.experimental.pallas import tpu as pltpu

F32 = jnp.float32
BF16 = jnp.bfloat16

D_MODEL = 2048
GLA_HEADS = 4
GLA_DK = 128
GLA_DV = 256
GLA_RANK = 16
GLA_GATE_NORM = 16.0
GLA_CHUNK = 64
SWA_Q_HEADS = 16
SWA_KV_HEADS = 4
SWA_GROUP = SWA_Q_HEADS // SWA_KV_HEADS
SWA_HEAD_DIM = 64
SWA_BLOCK = 128
N_GROUPS = 4
EXPERTS_PER_GROUP = 16
N_EXPERTS = N_GROUPS * EXPERTS_PER_GROUP
D_EXPERT = 256
RMS_EPS = 1e-6

GLA_QK_W = GLA_HEADS * GLA_DK
GLA_V_W = GLA_HEADS * GLA_DV
SWA_Q_W = SWA_Q_HEADS * SWA_HEAD_DIM
SWA_KV_W = SWA_KV_HEADS * SWA_HEAD_DIM
OFF_GQ = 0
OFF_GK = OFF_GQ + GLA_QK_W
OFF_GV = OFF_GK + GLA_QK_W
OFF_GR = OFF_GV + GLA_V_W
OFF_SQ = OFF_GR + GLA_V_W
OFF_SK = OFF_SQ + SWA_Q_W
OFF_SV = OFF_SK + SWA_KV_W
PROJ_W = OFF_SV + SWA_KV_W
LANES = 128

IN_TM = 1024
IN_TN = 2304
OUT_TM = 256
MOE_TR = 256
CMB_TM = 256
VMEM_LIMIT = 56 * 1024 * 1024


def _silu(x):
    return x / (1.0 + jnp.exp(-x))


ALIBI_PARTS = 3


def _bf16_parts(value, n):
    parts, rest = [], np.float32(value)
    for _ in range(n):
        piece = np.float32(np.asarray(rest).astype(jnp.bfloat16))
        parts.append(float(piece))
        rest = np.float32(rest - piece)
    return parts


W_PREP_ROWS = 256
OFF_LR = OFF_GR + GLA_V_W


def _wprep_kernel(w_ref, wm_ref, wl_ref):
    w = w_ref[...]
    wm_ref[:, 0:OFF_LR] = w[:, 0:OFF_LR].astype(BF16)
    wm_ref[:, OFF_LR:] = w[:, OFF_LR + GLA_RANK:].astype(BF16)
    pad = jnp.zeros((w.shape[0], LANES - GLA_RANK), F32)
    wl_ref[...] = jnp.concatenate([w[:, OFF_LR:OFF_LR + GLA_RANK], pad], axis=1).astype(BF16)


def _w_prep(w_in):
    k, n = w_in.shape
    rows = min(W_PREP_ROWS, k)
    return pl.pallas_call(
        _wprep_kernel,
        grid=(k // rows,),
        in_specs=[pl.BlockSpec((rows, n), lambda i: (i, 0))],
        out_specs=[pl.BlockSpec((rows, PROJ_W), lambda i: (i, 0)),
                   pl.BlockSpec((rows, LANES), lambda i: (i, 0))],
        out_shape=[jax.ShapeDtypeStruct((k, PROJ_W), BF16), jax.ShapeDtypeStruct((k, LANES), BF16)],
        compiler_params=pltpu.CompilerParams(
            dimension_semantics=("arbitrary",), vmem_limit_bytes=VMEM_LIMIT),
        name="w_prep",
    )(w_in)


def _inproj_kernel(x_ref, g_ref, w_ref, wlr_ref, proj_ref, glr_ref, xn_ref):
    @pl.when(pl.program_id(1) == 0)
    def _():
        x = x_ref[...]
        ms = jnp.mean(x * x, axis=-1, keepdims=True)
        xn = (x * lax.rsqrt(ms + RMS_EPS) * g_ref[...]).astype(BF16)
        xn_ref[...] = xn
        glr_ref[...] = jnp.dot(xn, wlr_ref[...], preferred_element_type=F32)

    proj_ref[...] = jnp.dot(xn_ref[...], w_ref[...], preferred_element_type=F32).astype(BF16)


def _in_proj(x2, g, w_main, w_lr):
    m = x2.shape[0]
    tm = min(IN_TM, m)
    return pl.pallas_call(
        _inproj_kernel,
        grid=(m // tm, PROJ_W // IN_TN),
        in_specs=[
            pl.BlockSpec((tm, D_MODEL), lambda i, j: (i, 0)),
            pl.BlockSpec((1, D_MODEL), lambda i, j: (0, 0)),
            pl.BlockSpec((D_MODEL, IN_TN), lambda i, j: (0, j)),
            pl.BlockSpec((D_MODEL, LANES), lambda i, j: (0, 0)),
        ],
        out_specs=[
            pl.BlockSpec((tm, IN_TN), lambda i, j: (i, j)),
            pl.BlockSpec((tm, LANES), lambda i, j: (i, 0)),
        ],
        out_shape=[
            jax.ShapeDtypeStruct((m, PROJ_W), BF16),
            jax.ShapeDtypeStruct((m, LANES), F32),
        ],
        scratch_shapes=[pltpu.VMEM((tm, D_MODEL), BF16)],
        compiler_params=pltpu.CompilerParams(
            dimension_semantics=("arbitrary", "arbitrary"), vmem_limit_bytes=VMEM_LIMIT),
        name="in_proj",
    )(x2, g, w_main, w_lr)


def _gla_kernel(q_ref, k_ref, v_ref, r_ref, glr_ref, wup_ref, bgk_ref, ng_ref, o_ref,
                la_ref, oi_ref, qi_ref, ki_ref, qd_ref, kd_ref, a_ref, kv_ref, dec_ref, sp_ref):
    t = q_ref.shape[0]
    c = GLA_CHUNK
    nchunk = t // c
    z = jnp.dot(glr_ref[...].astype(BF16), wup_ref[...], preferred_element_type=F32) + bgk_ref[...]
    la_ref[...] = (jnp.minimum(z, 0.0) - jnp.log1p(jnp.exp(-jnp.abs(z)))) * (1.0 / GLA_GATE_NORM)

    ii = lax.broadcasted_iota(jnp.int32, (c, c), 0)
    jj = lax.broadcasted_iota(jnp.int32, (c, c), 1)
    causal = jj <= ii
    tri = causal.astype(BF16)
    nt = (((1,), (1,)), ((), ()))
    tn = (((0,), (0,)), ((), ()))

    def chunk_rows(n):
        return pl.ds(pl.multiple_of(n * c, c), c)

    def decays(n, carry):
        rows = chunk_rows(n)
        la = la_ref[rows, :]
        hi = la.astype(BF16)
        r1 = la - hi.astype(F32)
        mid = r1.astype(BF16)
        lo = (r1 - mid.astype(F32)).astype(BF16)
        parts = jnp.dot(tri, jnp.concatenate([hi, mid, lo], axis=1), preferred_element_type=F32)
        bcum = parts[:, 0:GLA_DK] + parts[:, GLA_DK:2 * GLA_DK] + parts[:, 2 * GLA_DK:]
        b_mid = bcum[c // 2 - 1:c // 2, :]
        b_last = bcum[c - 1:c, :]
        q = q_ref[rows, :].astype(F32) * (GLA_DK ** -0.5)
        k = k_ref[rows, :].astype(F32)
        qi_ref[rows, :] = (q * jnp.exp(bcum - b_mid)).astype(BF16)
        ki_ref[rows, :] = (k * jnp.exp(b_mid - bcum)).astype(BF16)
        qd_ref[rows, :] = (q * jnp.exp(bcum)).astype(BF16)
        kd_ref[rows, :] = (k * jnp.exp(b_last - bcum)).astype(BF16)
        dec_ref[n] = jnp.broadcast_to(jnp.exp(b_last), dec_ref.shape[1:])
        return carry

    lax.fori_loop(0, nchunk, decays, 0, unroll=4)

    def scores(n, carry):
        rows = chunk_rows(n)
        a = lax.dot_general(qi_ref[rows, :], ki_ref[rows, :], nt, preferred_element_type=F32)
        a_ref[rows, :] = jnp.where(causal, a, 0.0).astype(BF16)
        return carry

    lax.fori_loop(0, nchunk, scores, 0, unroll=8)

    def intra(n, carry):
        rows = chunk_rows(n)
        v = v_ref[rows, :]
        oi_ref[rows, :] = jnp.dot(a_ref[rows, :], v, preferred_element_type=F32)
        kv_ref[n] = lax.dot_general(v, kd_ref[rows, :], tn, preferred_element_type=F32)
        return carry

    lax.fori_loop(0, nchunk, intra, 0, unroll=8)

    def scan(n, s_t):
        sp_ref[n] = s_t.astype(BF16)
        return s_t * dec_ref[n][0:1, :] + kv_ref[n]

    lax.fori_loop(0, nchunk, scan, jnp.zeros((GLA_DV, GLA_DK), F32))

    def inter(n, carry):
        rows = chunk_rows(n)
        o = oi_ref[rows, :] + lax.dot_general(qd_ref[rows, :], sp_ref[n], nt,
                                              preferred_element_type=F32)
        ms = jnp.mean(o * o, axis=-1, keepdims=True)
        o = o * lax.rsqrt(ms + RMS_EPS) * ng_ref[...]
        o = o * _silu(r_ref[rows, :].astype(F32))
        o_ref[rows, :] = o.astype(BF16)
        return carry

    lax.fori_loop(0, nchunk, inter, 0, unroll=8)


def _gla(proj, glr, wup, bgk, ng, batch, seq):
    m = proj.shape[0]
    return pl.pallas_call(
        _gla_kernel,
        grid=(batch, GLA_HEADS),
        in_specs=[
            pl.BlockSpec((seq, GLA_DK), lambda b, h: (b, OFF_GQ // GLA_DK + h)),
            pl.BlockSpec((seq, GLA_DK), lambda b, h: (b, OFF_GK // GLA_DK + h)),
            pl.BlockSpec((seq, GLA_DV), lambda b, h: (b, OFF_GV // GLA_DV + h)),
            pl.BlockSpec((seq, GLA_DV), lambda b, h: (b, OFF_GR // GLA_DV + h)),
            pl.BlockSpec((seq, LANES), lambda b, h: (b, 0)),
            pl.BlockSpec((LANES, GLA_DK), lambda b, h: (0, h)),
            pl.BlockSpec((1, GLA_DK), lambda b, h: (0, h)),
            pl.BlockSpec((1, GLA_DV), lambda b, h: (0, h)),
        ],
        out_specs=pl.BlockSpec((seq, GLA_DV), lambda b, h: (b, h)),
        out_shape=jax.ShapeDtypeStruct((m, GLA_V_W), BF16),
        scratch_shapes=[
            pltpu.VMEM((seq, GLA_DK), F32),
            pltpu.VMEM((seq, GLA_DV), F32),
            pltpu.VMEM((seq, GLA_DK), BF16),
            pltpu.VMEM((seq, GLA_DK), BF16),
            pltpu.VMEM((seq, GLA_DK), BF16),
            pltpu.VMEM((seq, GLA_DK), BF16),
            pltpu.VMEM((seq, GLA_CHUNK), BF16),
            pltpu.VMEM((seq // GLA_CHUNK, GLA_DV, GLA_DK), F32),
            pltpu.VMEM((seq // GLA_CHUNK, 8, GLA_DK), F32),
            pltpu.VMEM((seq // GLA_CHUNK, GLA_DV, GLA_DK), BF16),
        ],
        compiler_params=pltpu.CompilerParams(
            dimension_semantics=("arbitrary", "arbitrary"), vmem_limit_bytes=VMEM_LIMIT),
        name="gla",
    )(proj, proj, proj, proj, glr, wup, bgk, ng)


def _swa_kernel(sink_ref, q_ref, kp_ref, kc_ref, vp_ref, vc_ref, o_ref):
    blk = SWA_BLOCK
    half = SWA_HEAD_DIM
    n = pl.program_id(1)
    k_all = jnp.concatenate([kp_ref[...], kc_ref[...]], axis=0)
    v_all = jnp.concatenate([vp_ref[...], vc_ref[...]], axis=0)
    qi = lax.broadcasted_iota(jnp.int32, (blk, 2 * blk), 0)
    kj = lax.broadcasted_iota(jnp.int32, (blk, 2 * blk), 1)
    rel = qi + blk - kj
    valid = (rel >= 0) & (rel < blk) & ((kj >= blk) | (n > 0))
    sink_col = kj == 0
    lane_kv = lax.broadcasted_iota(jnp.int32, (2 * blk, LANES), 1)
    lo_kv = lane_kv < half
    lane_q = lax.broadcasted_iota(jnp.int32, (blk, LANES), 1)
    lo_q = lane_q < half
    nt = (((1,), (1,)), ((), ()))
    key_idx = lax.broadcasted_iota(jnp.int32, (2 * blk, LANES), 0).astype(F32)
    key_cols = jnp.where((lane_kv & (half - 1)) < ALIBI_PARTS, key_idx, 0.0)
    q_pos = lax.broadcasted_iota(jnp.int32, (blk, 1), 0).astype(F32) + float(blk)
    q_scale = jnp.asarray(SWA_HEAD_DIM ** -0.5, BF16)

    for p in range(SWA_KV_HEADS // 2):
        k_slab = k_all[:, p * LANES:(p + 1) * LANES].astype(F32)
        v_slab = v_all[:, p * LANES:(p + 1) * LANES].astype(F32)
        k_roll = pltpu.roll(k_slab, half, 1)
        v_roll = pltpu.roll(v_slab, half, 1)
        for hh in range(2):
            h = 2 * p + hh
            k_lo, k_hi = (k_slab, k_roll) if hh == 0 else (k_roll, k_slab)
            kd = (jnp.where(lo_kv, k_lo, key_cols).astype(BF16),
                  jnp.where(lo_kv, key_cols, k_hi).astype(BF16))
            v_lo, v_hi = (v_slab, v_roll) if hh == 0 else (v_roll, v_slab)
            vd = jnp.where(lo_kv, v_lo, v_hi)
            vd = jnp.where(key_idx == 0.0, 0.0, vd).astype(BF16)
            for gp in range(SWA_GROUP // 2):
                col = (h * (SWA_GROUP // 2) + gp) * LANES
                qs = q_ref[:, col:col + LANES] * q_scale
                outs = []
                for gg in range(2):
                    head = h * SWA_GROUP + 2 * gp + gg
                    parts = _bf16_parts(2.0 ** (-8.0 * (head + 1) / SWA_Q_HEADS), ALIBI_PARTS)
                    slope = sum(parts)
                    aug = jnp.zeros((blk, LANES), F32)
                    for idx, part in enumerate(parts):
                        aug = jnp.where(lane_q == (half if gg == 0 else 0) + idx, part, aug)
                    keep = lo_q if gg == 0 else jnp.logical_not(lo_q)
                    qm = jnp.where(keep, qs, aug.astype(BF16))
                    s = lax.dot_general(qm, kd[gg], nt, preferred_element_type=F32)
                    sink = sink_ref[head] + slope * q_pos
                    s = jnp.where(sink_col, sink, jnp.where(valid, s, -jnp.inf))
                    mx = jnp.max(s, axis=-1, keepdims=True)
                    pe = jnp.exp(s - mx)
                    den = jnp.sum(pe, axis=-1, keepdims=True)
                    o = jnp.dot(pe.astype(BF16), vd, preferred_element_type=F32)
                    outs.append(o / den)
                o_ref[:, col:col + LANES] = jnp.where(lo_q, outs[0], outs[1]).astype(BF16)


def _swa(proj, sinks, batch, seq):
    m = proj.shape[0]
    nb = seq // SWA_BLOCK
    qcol = OFF_SQ // SWA_Q_W
    kcol = OFF_SK // SWA_KV_W
    vcol = OFF_SV // SWA_KV_W
    cur = lambda c: (lambda b, n: (b * nb + n, c))
    prev = lambda c: (lambda b, n: (b * nb + jnp.maximum(n - 1, 0), c))
    return pl.pallas_call(
        _swa_kernel,
        grid=(batch, nb),
        in_specs=[
            pl.BlockSpec(memory_space=pltpu.SMEM),
            pl.BlockSpec((SWA_BLOCK, SWA_Q_W), cur(qcol)),
            pl.BlockSpec((SWA_BLOCK, SWA_KV_W), prev(kcol)),
            pl.BlockSpec((SWA_BLOCK, SWA_KV_W), cur(kcol)),
            pl.BlockSpec((SWA_BLOCK, SWA_KV_W), prev(vcol)),
            pl.BlockSpec((SWA_BLOCK, SWA_KV_W), cur(vcol)),
        ],
        out_specs=pl.BlockSpec((SWA_BLOCK, SWA_Q_W), lambda b, n: (b * nb + n, 0)),
        out_shape=jax.ShapeDtypeStruct((m, SWA_Q_W), BF16),
        compiler_params=pltpu.CompilerParams(
            dimension_semantics=("arbitrary", "arbitrary"), vmem_limit_bytes=VMEM_LIMIT),
        name="swa",
    )(sinks, proj, proj, proj, proj, proj)


def _out_route_kernel(og_ref, os_ref, x_ref, wo_ref, g_ref, wr_ref, br_ref,
                      h_ref, hn_ref, route_ref, cnt_ref, hnb_ref):
    t = pl.program_id(0)
    slot = lax.rem(t, 2)

    @pl.when(t == 0)
    def _():
        hnb_ref[...] = jnp.zeros_like(hnb_ref)

    logits = jnp.dot(hnb_ref[1 - slot], wr_ref[...], preferred_element_type=F32) + br_ref[...]

    h = x_ref[...]
    h = h + jnp.dot(og_ref[...], wo_ref[0:GLA_V_W, :], preferred_element_type=F32)
    h = h + jnp.dot(os_ref[...], wo_ref[GLA_V_W:, :], preferred_element_type=F32)
    h_ref[...] = h
    ms = jnp.mean(h * h, axis=-1, keepdims=True)
    hn = h * lax.rsqrt(ms + RMS_EPS) * g_ref[...]
    hn_ref[...] = hn
    hnb_ref[slot] = hn.astype(BF16)

    lane = lax.broadcasted_iota(jnp.int32, logits.shape, 1)
    lanef = lane.astype(F32)
    big = float(LANES)
    ninf = -jnp.inf
    gl = jnp.where(lane < N_GROUPS, logits, ninf)
    gmax = jnp.max(gl, axis=-1, keepdims=True)
    g_p = 1.0 / jnp.sum(jnp.exp(gl - gmax), axis=-1, keepdims=True)
    g_idx = jnp.min(jnp.where(gl == gmax, lanef, big), axis=-1, keepdims=True)
    lo = N_GROUPS + EXPERTS_PER_GROUP * g_idx
    el = jnp.where((lanef >= lo) & (lanef < lo + EXPERTS_PER_GROUP), logits, ninf)
    m1 = jnp.max(el, axis=-1, keepdims=True)
    i1 = jnp.min(jnp.where(el == m1, lanef, big), axis=-1, keepdims=True)
    el2 = jnp.where(lanef == i1, ninf, el)
    m2 = jnp.max(el2, axis=-1, keepdims=True)
    i2 = jnp.min(jnp.where(el2 == m2, lanef, big), axis=-1, keepdims=True)
    d = jnp.exp(m2 - m1)
    c1 = g_p / (1.0 + d)
    c2 = g_p * d / (1.0 + d)
    tm = logits.shape[0]
    chosen = ((lanef == i1) | (lanef == i2)).astype(BF16)
    ri = lax.broadcasted_iota(jnp.int32, (tm, tm), 0)
    ci = lax.broadcasted_iota(jnp.int32, (tm, tm), 1)
    earlier = jnp.dot((ci < ri).astype(BF16), chosen, preferred_element_type=F32)
    r1 = jnp.sum(jnp.where(lanef == i1, earlier, 0.0), axis=-1, keepdims=True)
    r2 = jnp.sum(jnp.where(lanef == i2, earlier, 0.0), axis=-1, keepdims=True)
    cnt = jnp.sum(chosen.astype(F32), axis=0, keepdims=True)
    cnt_ref[...] = jnp.broadcast_to(cnt, cnt_ref.shape)

    fields = [i1 - N_GROUPS, i2 - N_GROUPS, c1, c2, r1, r2]
    route = jnp.zeros_like(logits)
    for idx, val in enumerate(fields):
        route = jnp.where(lane == idx, val, route)
    route_ref[...] = route


def _out_route(o_gla, o_swa, x2, w_out, g, w_rt, b_rt):
    m = x2.shape[0]
    tm = min(OUT_TM, m)
    ntile = m // tm
    row = lambda i: (jnp.minimum(i, ntile - 1), 0)
    routed = lambda i: (jnp.maximum(i - 1, 0), 0)
    fixed = lambda i: (0, 0)
    return pl.pallas_call(
        _out_route_kernel,
        grid=(ntile + 1,),
        in_specs=[
            pl.BlockSpec((tm, GLA_V_W), row),
            pl.BlockSpec((tm, SWA_Q_W), row),
            pl.BlockSpec((tm, D_MODEL), row),
            pl.BlockSpec((GLA_V_W + SWA_Q_W, D_MODEL), fixed),
            pl.BlockSpec((1, D_MODEL), fixed),
            pl.BlockSpec((D_MODEL, LANES), fixed),
            pl.BlockSpec((1, LANES), fixed),
        ],
        out_specs=[
            pl.BlockSpec((tm, D_MODEL), row),
            pl.BlockSpec((tm, D_MODEL), row),
            pl.BlockSpec((tm, LANES), routed),
            pl.BlockSpec((1, 8, LANES), lambda i: (jnp.maximum(i - 1, 0), 0, 0)),
        ],
        out_shape=[
            jax.ShapeDtypeStruct((m, D_MODEL), F32),
            jax.ShapeDtypeStruct((m, D_MODEL), F32),
            jax.ShapeDtypeStruct((m, LANES), F32),
            jax.ShapeDtypeStruct((m // tm, 8, LANES), F32),
        ],
        scratch_shapes=[pltpu.VMEM((2, tm, D_MODEL), BF16)],
        compiler_params=pltpu.CompilerParams(
            dimension_semantics=("arbitrary",), vmem_limit_bytes=VMEM_LIMIT),
        name="out_route",
    )(o_gla, o_swa, x2, w_out, g, w_rt, b_rt)


LANES_LOG2 = 7
MOE_ROW_GROUP_LOG2 = 3
MOE_ROW_GROUP = 1 << MOE_ROW_GROUP_LOG2
IDS_WINDOW_ROWS = 3


def _moe_kernel(te_ref, cb_ref, nv_ref, ne_ref, nt_ref, tok_hbm, hn_hbm, wg_hbm, wu_hbm, wd_hbm, y_ref,
                xbuf, gsem, ids_smem, isem, wgs, wus, wds, wsem, wgb, wub, wdb):
    tr = MOE_TR
    grp = MOE_ROW_GROUP
    t = pl.program_id(0)
    nt = nt_ref[0]
    gs = lax.rem(t, 2)
    cs = 1 - gs

    def ids_copy(tile, s):
        row0 = lax.shift_right_logical(cb_ref[tile], LANES_LOG2)
        return pltpu.make_async_copy(tok_hbm.at[pl.ds(row0, IDS_WINDOW_ROWS)], ids_smem.at[s], isem.at[s])

    def weight_copies(e):
        return (pltpu.make_async_copy(wg_hbm.at[e], wgs, wsem.at[0]),
                pltpu.make_async_copy(wu_hbm.at[e], wus, wsem.at[1]),
                pltpu.make_async_copy(wd_hbm.at[e], wds, wsem.at[2]))

    def rows_used(tile):
        groups = lax.shift_right_logical(nv_ref[tile] + (grp - 1), MOE_ROW_GROUP_LOG2)
        return lax.shift_left(groups, MOE_ROW_GROUP_LOG2)

    @pl.when(t == 0)
    def _():
        xbuf[...] = jnp.zeros_like(xbuf)
        ids_copy(0, 0).start()
        for cp in weight_copies(te_ref[0]):
            cp.start()

    @pl.when(t < nt)
    def _():
        ids_copy(t, gs).wait()

        @pl.when(t + 1 < nt)
        def _():
            ids_copy(t + 1, cs).start()

        off = cb_ref[t] & (LANES - 1)
        n_real = nv_ref[t]
        for g0 in range(0, tr, grp):
            @pl.when(g0 < n_real)
            def _():
                for r in range(g0, g0 + grp):
                    q = off + r
                    tok = ids_smem[gs, lax.shift_right_logical(q, LANES_LOG2), q & (LANES - 1)]
                    pltpu.make_async_copy(hn_hbm.at[pl.ds(tok, 1)], xbuf.at[gs, pl.ds(r, 1)],
                                          gsem.at[gs]).start(priority=r % 2)

    @pl.when((t >= 1) & (t <= nt))
    def _():
        c = t - 1
        changed = (c == 0) | (te_ref[c] != te_ref[jnp.maximum(c - 1, 0)])

        @pl.when(changed)
        def _():
            for cp in weight_copies(te_ref[c]):
                cp.wait()
            wgb[...] = wgs[...].astype(BF16)
            wub[...] = wus[...].astype(BF16)
            wdb[...] = wds[...].astype(BF16)
            nxt = ne_ref[c]

            @pl.when(nxt >= 0)
            def _():
                for cp in weight_copies(nxt):
                    cp.start()

        n_rows = pl.multiple_of(rows_used(c), grp)
        pltpu.make_async_copy(hn_hbm.at[pl.ds(0, n_rows)], xbuf.at[cs, pl.ds(0, n_rows)],
                              gsem.at[cs]).wait()
        x = xbuf[cs].astype(BF16)
        g = jnp.dot(x, wgb[...], preferred_element_type=F32)
        u = jnp.dot(x, wub[...], preferred_element_type=F32)
        hmid = (_silu(g) * u).astype(BF16)
        y_ref[...] = jnp.dot(hmid, wdb[...], preferred_element_type=F32)

    @pl.when(t > nt)
    def _():
        y_ref[...] = jnp.zeros_like(y_ref)


def _moe(hn, w_gate, w_up, w_down, plan):
    n_tiles = plan["tile_expert"].shape[0]
    p_rows = n_tiles * MOE_TR
    prev = lambda t: jnp.maximum(t - 1, 0)
    any_spec = pl.BlockSpec(memory_space=pl.ANY)
    grid_spec = pltpu.PrefetchScalarGridSpec(
        num_scalar_prefetch=5,
        grid=(n_tiles + 1,),
        in_specs=[any_spec] * 5,
        out_specs=pl.BlockSpec((MOE_TR, D_MODEL), lambda t, *_: (prev(t), 0)),
        scratch_shapes=[
            pltpu.VMEM((2, MOE_TR, D_MODEL), F32),
            pltpu.SemaphoreType.DMA((2,)),
            pltpu.SMEM((2, IDS_WINDOW_ROWS, LANES), jnp.int32),
            pltpu.SemaphoreType.DMA((2,)),
            pltpu.VMEM((D_MODEL, D_EXPERT), F32),
            pltpu.VMEM((D_MODEL, D_EXPERT), F32),
            pltpu.VMEM((D_EXPERT, D_MODEL), F32),
            pltpu.SemaphoreType.DMA((3,)),
            pltpu.VMEM((D_MODEL, D_EXPERT), BF16),
            pltpu.VMEM((D_MODEL, D_EXPERT), BF16),
            pltpu.VMEM((D_EXPERT, D_MODEL), BF16),
        ],
    )
    return pl.pallas_call(
        _moe_kernel,
        grid_spec=grid_spec,
        out_shape=jax.ShapeDtypeStruct((p_rows, D_MODEL), F32),
        compiler_params=pltpu.CompilerParams(
            dimension_semantics=("arbitrary",), vmem_limit_bytes=VMEM_LIMIT),
        name="moe",
    )(plan["tile_expert"], plan["tile_base"], plan["tile_rows"], plan["next_expert"], plan["num_tiles"],
      plan["sorted_tokens"], hn, w_gate, w_up, w_down)


def _combine_kernel(pos_hbm, y_hbm, h_ref, route_ref, g_ref, o_ref, ybuf, gsem, ids_smem, isem):
    tm = h_ref.shape[0]
    t = pl.program_id(0)
    ntile = pl.num_programs(0) - 1

    def ids_copy(tile, s):
        return pltpu.make_async_copy(pos_hbm.at[tile], ids_smem.at[s], isem.at[s])

    @pl.when(t == 0)
    def _():
        ids_copy(0, 0).start()

    @pl.when(t < ntile)
    def _():
        s = lax.rem(t, 2)
        ids_copy(t, s).wait()

        @pl.when(t + 1 < ntile)
        def _():
            ids_copy(t + 1, 1 - s).start()

        for kk in range(2):
            for r in range(tm):
                flat = kk * tm + r
                row = ids_smem[s, flat // LANES, flat % LANES]
                pltpu.make_async_copy(y_hbm.at[pl.ds(row, 1)], ybuf.at[s, kk, pl.ds(r, 1)],
                                      gsem.at[s, kk]).start(priority=r % 2)

    @pl.when(t >= 1)
    def _():
        s = lax.rem(t - 1, 2)
        for kk in range(2):
            pltpu.make_async_copy(y_hbm.at[pl.ds(0, tm)], ybuf.at[s, kk], gsem.at[s, kk]).wait()
        route = route_ref[...]
        h = h_ref[...] + route[:, 2:3] * ybuf[s, 0] + route[:, 3:4] * ybuf[s, 1]
        ms = jnp.mean(h * h, axis=-1, keepdims=True)
        o_ref[...] = h * lax.rsqrt(ms + RMS_EPS) * g_ref[...]


def _combine(y, h, route, g, pos2d):
    m = h.shape[0]
    tm = min(CMB_TM, m)
    ntile = m // tm
    pos3 = pos2d.reshape(ntile, tm, 2).transpose(0, 2, 1).reshape(ntile, 2 * tm // LANES, LANES)
    row = lambda i: (jnp.maximum(i - 1, 0), 0)
    return pl.pallas_call(
        _combine_kernel,
        grid=(ntile + 1,),
        in_specs=[
            pl.BlockSpec(memory_space=pl.ANY),
            pl.BlockSpec(memory_space=pl.ANY),
            pl.BlockSpec((tm, D_MODEL), row),
            pl.BlockSpec((tm, LANES), row),
            pl.BlockSpec((1, D_MODEL), lambda i: (0, 0)),
        ],
        out_specs=pl.BlockSpec((tm, D_MODEL), row),
        out_shape=jax.ShapeDtypeStruct((m, D_MODEL), F32),
        scratch_shapes=[
            pltpu.VMEM((2, 2, tm, D_MODEL), F32),
            pltpu.SemaphoreType.DMA((2, 2)),
            pltpu.SMEM((2, 2 * tm // LANES, LANES), jnp.int32),
            pltpu.SemaphoreType.DMA((2,)),
        ],
        compiler_params=pltpu.CompilerParams(
            dimension_semantics=("arbitrary",), vmem_limit_bytes=VMEM_LIMIT),
        name="combine",
    )(pos3, y, h, route, g)


def _dispatch_plan(route, tile_counts, m):
    tr = MOE_TR
    p_rows = 2 * m + N_EXPERTS * tr
    n_tiles = p_rows // tr
    n_tok_tiles = tile_counts.shape[0]
    cnt = tile_counts[:, 0, N_GROUPS:N_GROUPS + N_EXPERTS].astype(jnp.int32)
    before_tile = jnp.cumsum(cnt, axis=0) - cnt
    counts = jnp.sum(cnt, axis=0)
    padded = ((counts + tr - 1) // tr) * tr
    ends = jnp.cumsum(padded)
    starts = ends - padded
    base = (starts[None, :] + before_tile).astype(jnp.int32)
    expert = route[:, 0:2].astype(jnp.int32).reshape(n_tok_tiles, m // n_tok_tiles, 2)
    rank = route[:, 4:6].astype(jnp.int32).reshape(n_tok_tiles, m // n_tok_tiles, 2)
    hit = expert[..., None] == jnp.arange(N_EXPERTS, dtype=jnp.int32)
    pos = rank + jnp.sum(jnp.where(hit, base[:, None, None, :], 0), axis=-1)
    pos = pos.reshape(m, 2)
    num_tiles = (ends[-1] // tr).astype(jnp.int32)
    tile_idx = jnp.minimum(jnp.arange(n_tiles, dtype=jnp.int32), num_tiles - 1)
    tile_expert = jnp.sum((tile_idx[:, None] * tr >= ends[None, :]).astype(jnp.int32), axis=1)
    n_slots = 2 * m
    slot = jnp.arange(n_slots, dtype=jnp.int32)
    sorted_key = jnp.sort(route[:, 0:2].astype(jnp.int32).reshape(-1) * n_slots + slot)
    sorted_tokens = (sorted_key & (n_slots - 1)) >> 1
    sorted_tokens = jnp.pad(sorted_tokens, (0, IDS_WINDOW_ROWS * LANES)).reshape(-1, LANES)
    dense_starts = jnp.cumsum(counts) - counts
    in_expert = tile_idx - (starts // tr)[tile_expert]
    tile_base = dense_starts[tile_expert] + in_expert * tr
    tile_rows = jnp.clip(counts[tile_expert] - in_expert * tr, 0, tr)
    tile_rows = jnp.where(jnp.arange(n_tiles) < num_tiles, tile_rows, 0)
    ids = jnp.arange(N_EXPERTS, dtype=jnp.int32)
    present = jnp.where(counts > 0, ids, N_EXPERTS)
    next_ge = lax.cummin(present, axis=0, reverse=True)
    next_gt = jnp.concatenate([next_ge[1:], jnp.full((1,), N_EXPERTS, jnp.int32)])
    next_gt = jnp.where(next_gt >= N_EXPERTS, -1, next_gt)
    plan = dict(tile_expert=tile_expert.astype(jnp.int32), tile_base=tile_base.astype(jnp.int32),
                tile_rows=tile_rows.astype(jnp.int32), next_expert=next_gt[tile_expert].astype(jnp.int32),
                num_tiles=num_tiles.reshape(1), sorted_tokens=sorted_tokens)
    return plan, pos


def kernel(x, norm_mix_g, w_in, w_gk_up, b_gk, gla_norm_g, swa_sinks, w_out, norm_ffn_g,
           w_group, b_group, w_router, b_router, w_gate, w_up, w_down, norm_final_g):
    batch, seq, d = x.shape
    m = batch * seq
    assert w_in.shape[0] == 1, "single-layer block"
    x2 = x.reshape(m, d)
    w_main, w_lr = _w_prep(w_in[0])
    wup = jnp.pad(w_gk_up[0], ((0, LANES - GLA_RANK), (0, 0))).astype(BF16)
    w_rt = jnp.pad(jnp.concatenate([w_group[0], w_router[0]], axis=1),
                   ((0, 0), (0, LANES - N_GROUPS - N_EXPERTS))).astype(BF16)
    b_rt = jnp.pad(jnp.concatenate([b_group[0], b_router[0]]),
                   (0, LANES - N_GROUPS - N_EXPERTS)).reshape(1, LANES)

    proj, glr = _in_proj(x2, norm_mix_g[0].reshape(1, d), w_main, w_lr)
    o_gla = _gla(proj, glr, wup, b_gk[0].reshape(1, GLA_QK_W),
                 gla_norm_g[0].reshape(1, GLA_V_W), batch, seq)
    o_swa = _swa(proj, swa_sinks[0], batch, seq)
    h_mid, hn, route, tile_counts = _out_route(o_gla, o_swa, x2, w_out[0].astype(BF16),
                                               norm_ffn_g[0].reshape(1, d), w_rt, b_rt)
    plan, pos2d = _dispatch_plan(route, tile_counts, m)
    y = _moe(hn, w_gate[0], w_up[0], w_down[0], plan)
    out = _combine(y, h_mid, route, norm_final_g.reshape(1, d), pos2d)
    return out.reshape(batch, seq, d)
```

```python
import functools

import jax
import jax.numpy as jnp
import numpy as np
from jax import lax
from jax.experimental import pallas as pl
from jax.experimental.pallas import tpu as pltpu

F32 = jnp.float32
BF16 = jnp.bfloat16

D_MODEL = 2048
GLA_HEADS = 4
GLA_DK = 128
GLA_DV = 256
GLA_RANK = 16
GLA_GATE_NORM = 16.0
GLA_CHUNK = 64
SWA_Q_HEADS = 16
SWA_KV_HEADS = 4
SWA_GROUP = SWA_Q_HEADS // SWA_KV_HEADS
SWA_HEAD_DIM = 64
SWA_BLOCK = 128
N_GROUPS = 4
EXPERTS_PER_GROUP = 16
N_EXPERTS = N_GROUPS * EXPERTS_PER_GROUP
D_EXPERT = 256
RMS_EPS = 1e-6

GLA_QK_W = GLA_HEADS * GLA_DK
GLA_V_W = GLA_HEADS * GLA_DV
SWA_Q_W = SWA_Q_HEADS * SWA_HEAD_DIM
SWA_KV_W = SWA_KV_HEADS * SWA_HEAD_DIM
OFF_GQ = 0
OFF_GK = OFF_GQ + GLA_QK_W
OFF_GV = OFF_GK + GLA_QK_W
OFF_GR = OFF_GV + GLA_V_W
OFF_SQ = OFF_GR + GLA_V_W
OFF_SK = OFF_SQ + SWA_Q_W
OFF_SV = OFF_SK + SWA_KV_W
PROJ_W = OFF_SV + SWA_KV_W
LANES = 128

IN_TM = 1024
IN_TN = 2304
OUT_TM = 256
ROUTE_W = 8
MOE_TR = 256
CMB_TM = 256
VMEM_LIMIT = 56 * 1024 * 1024


def _silu(x):
    return x / (1.0 + jnp.exp(-x))


ALIBI_PARTS = 3


def _bf16_parts(value, n):
    parts, rest = [], np.float32(value)
    for _ in range(n):
        piece = np.float32(np.asarray(rest).astype(jnp.bfloat16))
        parts.append(float(piece))
        rest = np.float32(rest - piece)
    return parts


W_PREP_ROWS = 256
OFF_LR = OFF_GR + GLA_V_W


def _wprep_kernel(w_ref, wm_ref, wl_ref):
    w = w_ref[...]
    wm_ref[:, 0:OFF_LR] = w[:, 0:OFF_LR].astype(BF16)
    wm_ref[:, OFF_LR:] = w[:, OFF_LR + GLA_RANK:].astype(BF16)
    pad = jnp.zeros((w.shape[0], LANES - GLA_RANK), F32)
    wl_ref[...] = jnp.concatenate([w[:, OFF_LR:OFF_LR + GLA_RANK], pad], axis=1).astype(BF16)


def _w_prep(w_in):
    _, k, n = w_in.shape
    rows = min(W_PREP_ROWS, k)
    return pl.pallas_call(
        _wprep_kernel,
        grid=(k // rows,),
        in_specs=[pl.BlockSpec((None, rows, n), lambda i: (0, i, 0))],
        out_specs=[pl.BlockSpec((rows, PROJ_W), lambda i: (i, 0)),
                   pl.BlockSpec((rows, LANES), lambda i: (i, 0))],
        out_shape=[jax.ShapeDtypeStruct((k, PROJ_W), BF16), jax.ShapeDtypeStruct((k, LANES), BF16)],
        compiler_params=pltpu.CompilerParams(
            dimension_semantics=("arbitrary",), vmem_limit_bytes=VMEM_LIMIT),
        name="w_prep",
    )(w_in)


def _inproj_kernel(x_ref, g_ref, w_ref, wlr_ref, proj_ref, glr_ref, xn_ref):
    @pl.when(pl.program_id(1) == 0)
    def _():
        x = x_ref[...]
        ms = jnp.mean(x * x, axis=-1, keepdims=True)
        xn = (x * lax.rsqrt(ms + RMS_EPS) * g_ref[...]).astype(BF16)
        xn_ref[...] = xn
        glr_ref[...] = jnp.dot(xn, wlr_ref[...], preferred_element_type=F32)

    proj_ref[...] = jnp.dot(xn_ref[...], w_ref[...], preferred_element_type=F32).astype(BF16)


def _in_proj(x2, g, w_main, w_lr):
    m = x2.shape[0]
    tm = min(IN_TM, m)
    return pl.pallas_call(
        _inproj_kernel,
        grid=(m // tm, PROJ_W // IN_TN),
        in_specs=[
            pl.BlockSpec((tm, D_MODEL), lambda i, j: (i, 0)),
            pl.BlockSpec((1, D_MODEL), lambda i, j: (0, 0)),
            pl.BlockSpec((D_MODEL, IN_TN), lambda i, j: (0, j)),
            pl.BlockSpec((D_MODEL, LANES), lambda i, j: (0, 0)),
        ],
        out_specs=[
            pl.BlockSpec((tm, IN_TN), lambda i, j: (i, j)),
            pl.BlockSpec((tm, LANES), lambda i, j: (i, 0)),
        ],
        out_shape=[
            jax.ShapeDtypeStruct((m, PROJ_W), BF16),
            jax.ShapeDtypeStruct((m, LANES), F32),
        ],
        scratch_shapes=[pltpu.VMEM((tm, D_MODEL), BF16)],
        compiler_params=pltpu.CompilerParams(
            dimension_semantics=("arbitrary", "arbitrary"), vmem_limit_bytes=VMEM_LIMIT),
        name="in_proj",
    )(x2, g, w_main, w_lr)


def _gla_kernel(q_ref, k_ref, v_ref, r_ref, glr_ref, wup_ref, bgk_ref, ng_ref, o_ref,
                la_ref, oi_ref, qi_ref, ki_ref, qd_ref, kd_ref, a_ref, kv_ref, dec_ref, sp_ref):
    t = q_ref.shape[0]
    c = GLA_CHUNK
    nchunk = t // c
    z = jnp.dot(glr_ref[...].astype(BF16), wup_ref[...], preferred_element_type=F32) + bgk_ref[...]
    la_ref[...] = (jnp.minimum(z, 0.0) - jnp.log(1.0 + jnp.exp(-jnp.abs(z)))) * (1.0 / GLA_GATE_NORM)

    ii = lax.broadcasted_iota(jnp.int32, (c, c), 0)
    jj = lax.broadcasted_iota(jnp.int32, (c, c), 1)
    causal = jj <= ii
    tri = causal.astype(BF16)
    nt = (((1,), (1,)), ((), ()))
    tn = (((0,), (0,)), ((), ()))

    def chunk_rows(n):
        return pl.ds(pl.multiple_of(n * c, c), c)

    def decays(n, carry):
        rows = chunk_rows(n)
        la = la_ref[rows, :]
        hi = la.astype(BF16)
        r1 = la - hi.astype(F32)
        mid = r1.astype(BF16)
        lo = (r1 - mid.astype(F32)).astype(BF16)
        parts = jnp.dot(tri, jnp.concatenate([hi, mid, lo], axis=1), preferred_element_type=F32)
        bcum = parts[:, 0:GLA_DK] + parts[:, GLA_DK:2 * GLA_DK] + parts[:, 2 * GLA_DK:]
        b_mid = bcum[c // 2 - 1:c // 2, :]
        b_last = bcum[c - 1:c, :]
        q = q_ref[rows, :].astype(F32) * (GLA_DK ** -0.5)
        k = k_ref[rows, :].astype(F32)
        qi_ref[rows, :] = (q * jnp.exp(bcum - b_mid)).astype(BF16)
        ki_ref[rows, :] = (k * jnp.exp(b_mid - bcum)).astype(BF16)
        qd_ref[rows, :] = (q * jnp.exp(bcum)).astype(BF16)
        kd_ref[rows, :] = (k * jnp.exp(b_last - bcum)).astype(BF16)
        dec_ref[n] = jnp.broadcast_to(jnp.exp(b_last), dec_ref.shape[1:])
        return carry

    lax.fori_loop(0, nchunk, decays, 0, unroll=4)

    def scores(n, carry):
        rows = chunk_rows(n)
        a = lax.dot_general(qi_ref[rows, :], ki_ref[rows, :], nt, preferred_element_type=F32)
        a_ref[rows, :] = jnp.where(causal, a, 0.0).astype(BF16)
        return carry

    lax.fori_loop(0, nchunk, scores, 0, unroll=8)

    def intra(n, carry):
        rows = chunk_rows(n)
        v = v_ref[rows, :]
        oi_ref[rows, :] = jnp.dot(a_ref[rows, :], v, preferred_element_type=F32)
        kv_ref[n] = lax.dot_general(v, kd_ref[rows, :], tn, preferred_element_type=F32)
        return carry

    lax.fori_loop(0, nchunk, intra, 0, unroll=8)

    def scan(n, s_t):
        sp_ref[n] = s_t.astype(BF16)
        return s_t * dec_ref[n][0:1, :] + kv_ref[n]

    lax.fori_loop(0, nchunk, scan, jnp.zeros((GLA_DV, GLA_DK), F32))

    def inter(n, carry):
        rows = chunk_rows(n)
        o = oi_ref[rows, :] + lax.dot_general(qd_ref[rows, :], sp_ref[n], nt,
                                              preferred_element_type=F32)
        ms = jnp.mean(o * o, axis=-1, keepdims=True)
        o = o * lax.rsqrt(ms + RMS_EPS) * ng_ref[...]
        o = o * _silu(r_ref[rows, :].astype(F32))
        o_ref[rows, :] = o.astype(BF16)
        return carry

    lax.fori_loop(0, nchunk, inter, 0, unroll=8)


def _gla(proj, glr, wup, bgk, ng, batch, seq):
    m = proj.shape[0]
    return pl.pallas_call(
        _gla_kernel,
        grid=(batch, GLA_HEADS),
        in_specs=[
            pl.BlockSpec((seq, GLA_DK), lambda b, h: (b, OFF_GQ // GLA_DK + h)),
            pl.BlockSpec((seq, GLA_DK), lambda b, h: (b, OFF_GK // GLA_DK + h)),
            pl.BlockSpec((seq, GLA_DV), lambda b, h: (b, OFF_GV // GLA_DV + h)),
            pl.BlockSpec((seq, GLA_DV), lambda b, h: (b, OFF_GR // GLA_DV + h)),
            pl.BlockSpec((seq, LANES), lambda b, h: (b, 0)),
            pl.BlockSpec((LANES, GLA_DK), lambda b, h: (0, h)),
            pl.BlockSpec((1, GLA_DK), lambda b, h: (0, h)),
            pl.BlockSpec((1, GLA_DV), lambda b, h: (0, h)),
        ],
        out_specs=pl.BlockSpec((seq, GLA_DV), lambda b, h: (b, h)),
        out_shape=jax.ShapeDtypeStruct((m, GLA_V_W), BF16),
        scratch_shapes=[
            pltpu.VMEM((seq, GLA_DK), F32),
            pltpu.VMEM((seq, GLA_DV), F32),
            pltpu.VMEM((seq, GLA_DK), BF16),
            pltpu.VMEM((seq, GLA_DK), BF16),
            pltpu.VMEM((seq, GLA_DK), BF16),
            pltpu.VMEM((seq, GLA_DK), BF16),
            pltpu.VMEM((seq, GLA_CHUNK), BF16),
            pltpu.VMEM((seq // GLA_CHUNK, GLA_DV, GLA_DK), F32),
            pltpu.VMEM((seq // GLA_CHUNK, 8, GLA_DK), F32),
            pltpu.VMEM((seq // GLA_CHUNK, GLA_DV, GLA_DK), BF16),
        ],
        compiler_params=pltpu.CompilerParams(
            dimension_semantics=("arbitrary", "arbitrary"), vmem_limit_bytes=VMEM_LIMIT),
        name="gla",
    )(proj, proj, proj, proj, glr, wup, bgk, ng)


def _swa_kernel(sink_ref, q_ref, kp_ref, kc_ref, vp_ref, vc_ref, o_ref):
    blk = SWA_BLOCK
    half = SWA_HEAD_DIM
    n = pl.program_id(1)
    k_all = jnp.concatenate([kp_ref[...], kc_ref[...]], axis=0)
    v_all = jnp.concatenate([vp_ref[...], vc_ref[...]], axis=0)
    qi = lax.broadcasted_iota(jnp.int32, (blk, 2 * blk), 0)
    kj = lax.broadcasted_iota(jnp.int32, (blk, 2 * blk), 1)
    rel = qi + blk - kj
    valid = (rel >= 0) & (rel < blk) & ((kj >= blk) | (n > 0))
    sink_col = kj == 0
    lane_kv = lax.broadcasted_iota(jnp.int32, (2 * blk, LANES), 1)
    lo_kv = lane_kv < half
    lane_q = lax.broadcasted_iota(jnp.int32, (blk, LANES), 1)
    lo_q = lane_q < half
    nt = (((1,), (1,)), ((), ()))
    key_idx = lax.broadcasted_iota(jnp.int32, (2 * blk, LANES), 0).astype(F32)
    key_cols = jnp.where((lane_kv & (half - 1)) < ALIBI_PARTS, key_idx, 0.0)
    q_pos = lax.broadcasted_iota(jnp.int32, (blk, 1), 0).astype(F32) + float(blk)
    q_scale = jnp.asarray(SWA_HEAD_DIM ** -0.5, BF16)

    for p in range(SWA_KV_HEADS // 2):
        k_slab = k_all[:, p * LANES:(p + 1) * LANES].astype(F32)
        v_slab = v_all[:, p * LANES:(p + 1) * LANES].astype(F32)
        k_roll = pltpu.roll(k_slab, half, 1)
        v_roll = pltpu.roll(v_slab, half, 1)
        for hh in range(2):
            h = 2 * p + hh
            k_lo, k_hi = (k_slab, k_roll) if hh == 0 else (k_roll, k_slab)
            kd = (jnp.where(lo_kv, k_lo, key_cols).astype(BF16),
                  jnp.where(lo_kv, key_cols, k_hi).astype(BF16))
            v_lo, v_hi = (v_slab, v_roll) if hh == 0 else (v_roll, v_slab)
            vd = jnp.where(lo_kv, v_lo, v_hi)
            vd = jnp.where(key_idx == 0.0, 0.0, vd).astype(BF16)
            for gp in range(SWA_GROUP // 2):
                col = (h * (SWA_GROUP // 2) + gp) * LANES
                qs = q_ref[:, col:col + LANES] * q_scale
                outs = []
                for gg in range(2):
                    head = h * SWA_GROUP + 2 * gp + gg
                    parts = _bf16_parts(2.0 ** (-8.0 * (head + 1) / SWA_Q_HEADS), ALIBI_PARTS)
                    slope = sum(parts)
                    aug = jnp.zeros((blk, LANES), F32)
                    for idx, part in enumerate(parts):
                        aug = jnp.where(lane_q == (half if gg == 0 else 0) + idx, part, aug)
                    keep = lo_q if gg == 0 else jnp.logical_not(lo_q)
                    qm = jnp.where(keep, qs, aug.astype(BF16))
                    s = lax.dot_general(qm, kd[gg], nt, preferred_element_type=F32)
                    sink = sink_ref[head] + slope * q_pos
                    s = jnp.where(sink_col, sink, jnp.where(valid, s, -jnp.inf))
                    mx = jnp.max(s, axis=-1, keepdims=True)
                    pe = jnp.exp(s - mx)
                    den = jnp.sum(pe, axis=-1, keepdims=True)
                    o = jnp.dot(pe.astype(BF16), vd, preferred_element_type=F32)
                    outs.append(o / den)
                o_ref[:, col:col + LANES] = jnp.where(lo_q, outs[0], outs[1]).astype(BF16)


def _swa(proj, sinks, batch, seq):
    m = proj.shape[0]
    nb = seq // SWA_BLOCK
    qcol = OFF_SQ // SWA_Q_W
    kcol = OFF_SK // SWA_KV_W
    vcol = OFF_SV // SWA_KV_W
    cur = lambda c: (lambda b, n: (b * nb + n, c))
    prev = lambda c: (lambda b, n: (b * nb + jnp.maximum(n - 1, 0), c))
    return pl.pallas_call(
        _swa_kernel,
        grid=(batch, nb),
        in_specs=[
            pl.BlockSpec(memory_space=pltpu.SMEM),
            pl.BlockSpec((SWA_BLOCK, SWA_Q_W), cur(qcol)),
            pl.BlockSpec((SWA_BLOCK, SWA_KV_W), prev(kcol)),
            pl.BlockSpec((SWA_BLOCK, SWA_KV_W), cur(kcol)),
            pl.BlockSpec((SWA_BLOCK, SWA_KV_W), prev(vcol)),
            pl.BlockSpec((SWA_BLOCK, SWA_KV_W), cur(vcol)),
        ],
        out_specs=pl.BlockSpec((SWA_BLOCK, SWA_Q_W), lambda b, n: (b * nb + n, 0)),
        out_shape=jax.ShapeDtypeStruct((m, SWA_Q_W), BF16),
        compiler_params=pltpu.CompilerParams(
            dimension_semantics=("arbitrary", "arbitrary"), vmem_limit_bytes=VMEM_LIMIT),
        name="swa",
    )(sinks, proj, proj, proj, proj, proj)


def _out_route_kernel(og_ref, os_ref, x_ref, wo_ref, g_ref, wr_ref, br_ref,
                      h_ref, hn_ref, route_ref, cnt_ref, hnb_ref):
    t = pl.program_id(0)
    slot = lax.rem(t, 2)

    @pl.when(t == 0)
    def _():
        hnb_ref[...] = jnp.zeros_like(hnb_ref)

    logits = jnp.dot(hnb_ref[1 - slot], wr_ref[...], preferred_element_type=F32) + br_ref[...]

    h = x_ref[...]
    h = h + jnp.dot(og_ref[...], wo_ref[0:GLA_V_W, :], preferred_element_type=F32)
    h = h + jnp.dot(os_ref[...], wo_ref[GLA_V_W:, :], preferred_element_type=F32)
    h_ref[...] = h
    ms = jnp.mean(h * h, axis=-1, keepdims=True)
    hn = h * lax.rsqrt(ms + RMS_EPS) * g_ref[...]
    hn_ref[...] = hn
    hnb_ref[slot] = hn.astype(BF16)

    lane = lax.broadcasted_iota(jnp.int32, logits.shape, 1)
    lanef = lane.astype(F32)
    big = float(LANES)
    ninf = -jnp.inf
    gl = jnp.where(lane < N_GROUPS, logits, ninf)
    gmax = jnp.max(gl, axis=-1, keepdims=True)
    g_p = 1.0 / jnp.sum(jnp.exp(gl - gmax), axis=-1, keepdims=True)
    g_idx = jnp.min(jnp.where(gl == gmax, lanef, big), axis=-1, keepdims=True)
    lo = N_GROUPS + EXPERTS_PER_GROUP * g_idx
    el = jnp.where((lanef >= lo) & (lanef < lo + EXPERTS_PER_GROUP), logits, ninf)
    m1 = jnp.max(el, axis=-1, keepdims=True)
    i1 = jnp.min(jnp.where(el == m1, lanef, big), axis=-1, keepdims=True)
    el2 = jnp.where(lanef == i1, ninf, el)
    m2 = jnp.max(el2, axis=-1, keepdims=True)
    i2 = jnp.min(jnp.where(el2 == m2, lanef, big), axis=-1, keepdims=True)
    d = jnp.exp(m2 - m1)
    c1 = g_p / (1.0 + d)
    c2 = g_p * d / (1.0 + d)
    tm = logits.shape[0]
    chosen = ((lanef == i1) | (lanef == i2)).astype(BF16)
    ri = lax.broadcasted_iota(jnp.int32, (tm, tm), 0)
    ci = lax.broadcasted_iota(jnp.int32, (tm, tm), 1)
    earlier = jnp.dot((ci < ri).astype(BF16), chosen, preferred_element_type=F32)
    r1 = jnp.sum(jnp.where(lanef == i1, earlier, 0.0), axis=-1, keepdims=True)
    r2 = jnp.sum(jnp.where(lanef == i2, earlier, 0.0), axis=-1, keepdims=True)
    cnt = jnp.sum(chosen.astype(F32), axis=0, keepdims=True)
    cnt_ref[...] = jnp.broadcast_to(cnt, cnt_ref.shape)

    fields = [i1 - N_GROUPS, i2 - N_GROUPS, c1, c2, r1, r2]
    route = jnp.zeros_like(logits)
    for idx, val in enumerate(fields):
        route = jnp.where(lane == idx, val, route)
    route_ref[...] = route[:, 0:ROUTE_W]


def _out_route(o_gla, o_swa, x2, w_out, g, w_rt, b_rt):
    m = x2.shape[0]
    tm = min(OUT_TM, m)
    ntile = m // tm
    row = lambda i: (jnp.minimum(i, ntile - 1), 0)
    routed = lambda i: (jnp.maximum(i - 1, 0), 0)
    fixed = lambda i: (0, 0)
    return pl.pallas_call(
        _out_route_kernel,
        grid=(ntile + 1,),
        in_specs=[
            pl.BlockSpec((tm, GLA_V_W), row),
            pl.BlockSpec((tm, SWA_Q_W), row),
            pl.BlockSpec((tm, D_MODEL), row),
            pl.BlockSpec((GLA_V_W + SWA_Q_W, D_MODEL), fixed),
            pl.BlockSpec((1, D_MODEL), fixed),
            pl.BlockSpec((D_MODEL, LANES), fixed),
            pl.BlockSpec((1, LANES), fixed),
        ],
        out_specs=[
            pl.BlockSpec((tm, D_MODEL), row),
            pl.BlockSpec((tm, D_MODEL), row),
            pl.BlockSpec((tm, ROUTE_W), routed),
            pl.BlockSpec((1, 8, LANES), lambda i: (jnp.maximum(i - 1, 0), 0, 0)),
        ],
        out_shape=[
            jax.ShapeDtypeStruct((m, D_MODEL), F32),
            jax.ShapeDtypeStruct((m, D_MODEL), F32),
            jax.ShapeDtypeStruct((m, ROUTE_W), F32),
            jax.ShapeDtypeStruct((m // tm, 8, LANES), F32),
        ],
        scratch_shapes=[pltpu.VMEM((2, tm, D_MODEL), BF16)],
        compiler_params=pltpu.CompilerParams(
            dimension_semantics=("arbitrary",), vmem_limit_bytes=VMEM_LIMIT),
        name="out_route",
    )(o_gla, o_swa, x2, w_out, g, w_rt, b_rt)


LANES_LOG2 = 7
MOE_ROW_GROUP_LOG2 = 3
MOE_ROW_GROUP = 1 << MOE_ROW_GROUP_LOG2
IDS_WINDOW_ROWS = 3


def _moe_kernel(te_ref, cb_ref, nv_ref, ne_ref, nt_ref, tok_hbm, hn_hbm, wg_hbm, wu_hbm, wd_hbm, y_ref,
                xbuf, gsem, ids_smem, isem, wgs, wus, wds, wsem, wgb, wub, wdb):
    tr = MOE_TR
    grp = MOE_ROW_GROUP
    t = pl.program_id(0)
    nt = nt_ref[0]
    gs = lax.rem(t, 2)
    cs = 1 - gs

    def ids_copy(tile, s):
        row0 = lax.shift_right_logical(cb_ref[tile], LANES_LOG2)
        return pltpu.make_async_copy(tok_hbm.at[pl.ds(row0, IDS_WINDOW_ROWS)], ids_smem.at[s], isem.at[s])

    def weight_copies(e):
        return (pltpu.make_async_copy(wg_hbm.at[e], wgs, wsem.at[0]),
                pltpu.make_async_copy(wu_hbm.at[e], wus, wsem.at[1]),
                pltpu.make_async_copy(wd_hbm.at[e], wds, wsem.at[2]))

    def rows_used(tile):
        groups = lax.shift_right_logical(nv_ref[tile] + (grp - 1), MOE_ROW_GROUP_LOG2)
        return lax.shift_left(groups, MOE_ROW_GROUP_LOG2)

    @pl.when(t == 0)
    def _():
        xbuf[...] = jnp.zeros_like(xbuf)
        ids_copy(0, 0).start()
        for cp in weight_copies(te_ref[0]):
            cp.start()

    @pl.when(t < nt)
    def _():
        ids_copy(t, gs).wait()

        @pl.when(t + 1 < nt)
        def _():
            ids_copy(t + 1, cs).start()

        off = cb_ref[t] & (LANES - 1)
        n_real = nv_ref[t]
        for g0 in range(0, tr, grp):
            @pl.when(g0 < n_real)
            def _():
                for r in range(g0, g0 + grp):
                    q = off + r
                    tok = ids_smem[gs, lax.shift_right_logical(q, LANES_LOG2), q & (LANES - 1)]
                    pltpu.make_async_copy(hn_hbm.at[pl.ds(tok, 1)], xbuf.at[gs, pl.ds(r, 1)],
                                          gsem.at[gs]).start(priority=r % 2)

    @pl.when((t >= 1) & (t <= nt))
    def _():
        c = t - 1
        changed = (c == 0) | (te_ref[c] != te_ref[jnp.maximum(c - 1, 0)])

        @pl.when(changed)
        def _():
            for cp in weight_copies(te_ref[c]):
                cp.wait()
            wgb[...] = wgs[...].astype(BF16)
            wub[...] = wus[...].astype(BF16)
            wdb[...] = wds[...].astype(BF16)
            nxt = ne_ref[c]

            @pl.when(nxt >= 0)
            def _():
                for cp in weight_copies(nxt):
                    cp.start()

        n_rows = pl.multiple_of(rows_used(c), grp)
        pltpu.make_async_copy(hn_hbm.at[pl.ds(0, n_rows)], xbuf.at[cs, pl.ds(0, n_rows)],
                              gsem.at[cs]).wait()
        x = xbuf[cs].astype(BF16)
        g = jnp.dot(x, wgb[...], preferred_element_type=F32)
        u = jnp.dot(x, wub[...], preferred_element_type=F32)
        hmid = (_silu(g) * u).astype(BF16)
        y_ref[...] = jnp.dot(hmid, wdb[...], preferred_element_type=F32)

    @pl.when(t > nt)
    def _():
        y_ref[...] = jnp.zeros_like(y_ref)


def _moe(hn, w_gate, w_up, w_down, plan):
    n_tiles = plan["tile_expert"].shape[0]
    p_rows = n_tiles * MOE_TR
    prev = lambda t: jnp.maximum(t - 1, 0)
    any_spec = pl.BlockSpec(memory_space=pl.ANY)
    grid_spec = pltpu.PrefetchScalarGridSpec(
        num_scalar_prefetch=5,
        grid=(n_tiles + 1,),
        in_specs=[any_spec] * 5,
        out_specs=pl.BlockSpec((MOE_TR, D_MODEL), lambda t, *_: (prev(t), 0)),
        scratch_shapes=[
            pltpu.VMEM((2, MOE_TR, D_MODEL), F32),
            pltpu.SemaphoreType.DMA((2,)),
            pltpu.SMEM((2, IDS_WINDOW_ROWS, LANES), jnp.int32),
            pltpu.SemaphoreType.DMA((2,)),
            pltpu.VMEM((D_MODEL, D_EXPERT), F32),
            pltpu.VMEM((D_MODEL, D_EXPERT), F32),
            pltpu.VMEM((D_EXPERT, D_MODEL), F32),
            pltpu.SemaphoreType.DMA((3,)),
            pltpu.VMEM((D_MODEL, D_EXPERT), BF16),
            pltpu.VMEM((D_MODEL, D_EXPERT), BF16),
            pltpu.VMEM((D_EXPERT, D_MODEL), BF16),
        ],
    )
    return pl.pallas_call(
        _moe_kernel,
        grid_spec=grid_spec,
        out_shape=jax.ShapeDtypeStruct((p_rows, D_MODEL), F32),
        compiler_params=pltpu.CompilerParams(
            dimension_semantics=("arbitrary",), vmem_limit_bytes=VMEM_LIMIT),
        name="moe",
    )(plan["tile_expert"], plan["tile_base"], plan["tile_rows"], plan["next_expert"], plan["num_tiles"],
      plan["sorted_tokens"], hn, w_gate, w_up, w_down)


def _combine_kernel(pos_hbm, y_hbm, h_ref, route_ref, g_ref, o_ref, ybuf, gsem, ids_smem, isem):
    tm = h_ref.shape[0]
    t = pl.program_id(0)
    ntile = pl.num_programs(0) - 1

    def ids_copy(tile, s):
        return pltpu.make_async_copy(pos_hbm.at[tile], ids_smem.at[s], isem.at[s])

    @pl.when(t == 0)
    def _():
        ids_copy(0, 0).start()

    @pl.when(t < ntile)
    def _():
        s = lax.rem(t, 2)
        ids_copy(t, s).wait()

        @pl.when(t + 1 < ntile)
        def _():
            ids_copy(t + 1, 1 - s).start()

        for kk in range(2):
            for r in range(tm):
                flat = kk * tm + r
                row = ids_smem[s, flat // LANES, flat % LANES]
                pltpu.make_async_copy(y_hbm.at[pl.ds(row, 1)], ybuf.at[s, kk, pl.ds(r, 1)],
                                      gsem.at[s, kk]).start(priority=r % 2)

    @pl.when(t >= 1)
    def _():
        s = lax.rem(t - 1, 2)
        for kk in range(2):
            pltpu.make_async_copy(y_hbm.at[pl.ds(0, tm)], ybuf.at[s, kk], gsem.at[s, kk]).wait()
        route = route_ref[...]
        h = h_ref[...] + route[:, 2:3] * ybuf[s, 0] + route[:, 3:4] * ybuf[s, 1]
        ms = jnp.mean(h * h, axis=-1, keepdims=True)
        o_ref[...] = h * lax.rsqrt(ms + RMS_EPS) * g_ref[...]


def _combine(y, h, route, g, pos2d):
    m = h.shape[0]
    tm = min(CMB_TM, m)
    ntile = m // tm
    pos3 = pos2d.reshape(ntile, tm, 2).transpose(0, 2, 1).reshape(ntile, 2 * tm // LANES, LANES)
    row = lambda i: (jnp.maximum(i - 1, 0), 0)
    return pl.pallas_call(
        _combine_kernel,
        grid=(ntile + 1,),
        in_specs=[
            pl.BlockSpec(memory_space=pl.ANY),
            pl.BlockSpec(memory_space=pl.ANY),
            pl.BlockSpec((tm, D_MODEL), row),
            pl.BlockSpec((tm, ROUTE_W), row),
            pl.BlockSpec((1, D_MODEL), lambda i: (0, 0)),
        ],
        out_specs=pl.BlockSpec((tm, D_MODEL), row),
        out_shape=jax.ShapeDtypeStruct((m, D_MODEL), F32),
        scratch_shapes=[
            pltpu.VMEM((2, 2, tm, D_MODEL), F32),
            pltpu.SemaphoreType.DMA((2, 2)),
            pltpu.SMEM((2, 2 * tm // LANES, LANES), jnp.int32),
            pltpu.SemaphoreType.DMA((2,)),
        ],
        compiler_params=pltpu.CompilerParams(
            dimension_semantics=("arbitrary",), vmem_limit_bytes=VMEM_LIMIT),
        name="combine",
    )(pos3, y, h, route, g)


def _dispatch_plan(route, tile_counts, m):
    tr = MOE_TR
    p_rows = 2 * m + N_EXPERTS * tr
    n_tiles = p_rows // tr
    n_tok_tiles = tile_counts.shape[0]
    cnt = tile_counts[:, 0, N_GROUPS:N_GROUPS + N_EXPERTS].astype(jnp.int32)
    before_tile = jnp.cumsum(cnt, axis=0) - cnt
    counts = jnp.sum(cnt, axis=0)
    padded = ((counts + tr - 1) // tr) * tr
    ends = jnp.cumsum(padded)
    starts = ends - padded
    base = (starts[None, :] + before_tile).astype(jnp.int32)
    expert = route[:, 0:2].astype(jnp.int32).reshape(n_tok_tiles, m // n_tok_tiles, 2)
    rank = route[:, 4:6].astype(jnp.int32).reshape(n_tok_tiles, m // n_tok_tiles, 2)
    hit = expert[..., None] == jnp.arange(N_EXPERTS, dtype=jnp.int32)
    pos = rank + jnp.sum(jnp.where(hit, base[:, None, None, :], 0), axis=-1)
    pos = pos.reshape(m, 2)
    num_tiles = (ends[-1] // tr).astype(jnp.int32)
    tile_idx = jnp.minimum(jnp.arange(n_tiles, dtype=jnp.int32), num_tiles - 1)
    tile_expert = jnp.sum((tile_idx[:, None] * tr >= ends[None, :]).astype(jnp.int32), axis=1)
    n_slots = 2 * m
    slot = jnp.arange(n_slots, dtype=jnp.int32)
    sorted_key = jnp.sort(route[:, 0:2].astype(jnp.int32).reshape(-1) * n_slots + slot)
    sorted_tokens = (sorted_key & (n_slots - 1)) >> 1
    sorted_tokens = jnp.pad(sorted_tokens, (0, IDS_WINDOW_ROWS * LANES)).reshape(-1, LANES)
    dense_starts = jnp.cumsum(counts) - counts
    in_expert = tile_idx - (starts // tr)[tile_expert]
    tile_base = dense_starts[tile_expert] + in_expert * tr
    tile_rows = jnp.clip(counts[tile_expert] - in_expert * tr, 0, tr)
    tile_rows = jnp.where(jnp.arange(n_tiles) < num_tiles, tile_rows, 0)
    ids = jnp.arange(N_EXPERTS, dtype=jnp.int32)
    present = jnp.where(counts > 0, ids, N_EXPERTS)
    next_ge = lax.cummin(present, axis=0, reverse=True)
    next_gt = jnp.concatenate([next_ge[1:], jnp.full((1,), N_EXPERTS, jnp.int32)])
    next_gt = jnp.where(next_gt >= N_EXPERTS, -1, next_gt)
    plan = dict(tile_expert=tile_expert.astype(jnp.int32), tile_base=tile_base.astype(jnp.int32),
                tile_rows=tile_rows.astype(jnp.int32), next_expert=next_gt[tile_expert].astype(jnp.int32),
                num_tiles=num_tiles.reshape(1), sorted_tokens=sorted_tokens)
    return plan, pos


def kernel(x, norm_mix_g, w_in, w_gk_up, b_gk, gla_norm_g, swa_sinks, w_out, norm_ffn_g,
           w_group, b_group, w_router, b_router, w_gate, w_up, w_down, norm_final_g):
    batch, seq, d = x.shape
    m = batch * seq
    assert w_in.shape[0] == 1, "single-layer block"
    x2 = x.reshape(m, d)
    w_main, w_lr = _w_prep(w_in)
    wup = jnp.pad(w_gk_up[0], ((0, LANES - GLA_RANK), (0, 0))).astype(BF16)
    w_rt = jnp.pad(jnp.concatenate([w_group[0], w_router[0]], axis=1),
                   ((0, 0), (0, LANES - N_GROUPS - N_EXPERTS))).astype(BF16)
    b_rt = jnp.pad(jnp.concatenate([b_group[0], b_router[0]]),
                   (0, LANES - N_GROUPS - N_EXPERTS)).reshape(1, LANES)

    proj, glr = _in_proj(x2, norm_mix_g[0].reshape(1, d), w_main, w_lr)
    o_gla = _gla(proj, glr, wup, b_gk[0].reshape(1, GLA_QK_W),
                 gla_norm_g[0].reshape(1, GLA_V_W), batch, seq)
    o_swa = _swa(proj, swa_sinks[0], batch, seq)
    h_mid, hn, route, tile_counts = _out_route(o_gla, o_swa, x2, w_out[0].astype(BF16),
                                               norm_ffn_g[0].reshape(1, d), w_rt, b_rt)
    plan, pos2d = _dispatch_plan(route, tile_counts, m)
    y = _moe(hn, w_gate[0], w_up[0], w_down[0], plan)
    out = _combine(y, h_mid, route, norm_final_g.reshape(1, d), pos2d)
    return out.reshape(batch, seq, d)
```

```python
import functools

import jax
import jax.numpy as jnp
import numpy as np
from jax import lax
from jax.experimental import pallas as pl
from jax.experimental.pallas import tpu as pltpu

F32 = jnp.float32
BF16 = jnp.bfloat16

D_MODEL = 2048
GLA_HEADS = 4
GLA_DK = 128
GLA_DV = 256
GLA_RANK = 16
GLA_GATE_NORM = 16.0
GLA_CHUNK = 64
SWA_Q_HEADS = 16
SWA_KV_HEADS = 4
SWA_GROUP = SWA_Q_HEADS // SWA_KV_HEADS
SWA_HEAD_DIM = 64
SWA_BLOCK = 128
N_GROUPS = 4
EXPERTS_PER_GROUP = 16
N_EXPERTS = N_GROUPS * EXPERTS_PER_GROUP
D_EXPERT = 256
RMS_EPS = 1e-6

GLA_QK_W = GLA_HEADS * GLA_DK
GLA_V_W = GLA_HEADS * GLA_DV
SWA_Q_W = SWA_Q_HEADS * SWA_HEAD_DIM
SWA_KV_W = SWA_KV_HEADS * SWA_HEAD_DIM
OFF_GQ = 0
OFF_GK = OFF_GQ + GLA_QK_W
OFF_GV = OFF_GK + GLA_QK_W
OFF_GR = OFF_GV + GLA_V_W
OFF_SQ = OFF_GR + GLA_V_W
OFF_SK = OFF_SQ + SWA_Q_W
OFF_SV = OFF_SK + SWA_KV_W
PROJ_W = OFF_SV + SWA_KV_W
LANES = 128

IN_TM = 1024
IN_TN = 2304
OUT_TM = 256
ROUTE_W = 8
MOE_TR = 256
CMB_TM = 256
VMEM_LIMIT = 56 * 1024 * 1024


def _silu(x):
    return x / (1.0 + jnp.exp(-x))


ALIBI_PARTS = 3


def _bf16_parts(value, n):
    parts, rest = [], np.float32(value)
    for _ in range(n):
        piece = np.float32(np.asarray(rest).astype(jnp.bfloat16))
        parts.append(float(piece))
        rest = np.float32(rest - piece)
    return parts


W_PREP_ROWS = 384
OFF_LR = OFF_GR + GLA_V_W


def _wprep_kernel(wt_hbm, wm_ref, wl_ref, buf, lrbuf, sem, lrsem):
    rows = W_PREP_ROWS
    i = pl.program_id(0)
    n = pl.num_programs(0)
    n_lo = OFF_LR // rows
    slot = lax.rem(i, 2)

    def block_copy(step, s):
        start = jnp.where(step < n_lo, step * rows, OFF_LR + GLA_RANK + (step - n_lo) * rows)
        return pltpu.make_async_copy(wt_hbm.at[pl.ds(pl.multiple_of(start, 8), rows)], buf.at[s], sem.at[s])

    @pl.when(i == 0)
    def _():
        block_copy(0, 0).start()
        lrbuf[...] = jnp.zeros_like(lrbuf)
        lr_copy = pltpu.make_async_copy(wt_hbm.at[pl.ds(OFF_LR, GLA_RANK)], lrbuf.at[pl.ds(0, GLA_RANK)],
                                        lrsem.at[0])
        lr_copy.start()
        lr_copy.wait()
        wl_ref[...] = lrbuf[...].T.astype(BF16)

    @pl.when(i + 1 < n)
    def _():
        block_copy(i + 1, 1 - slot).start()

    block_copy(i, slot).wait()
    wm_ref[...] = buf[slot].T.astype(BF16)


def _w_prep(w_t):
    n, k = w_t.shape
    assert OFF_LR % W_PREP_ROWS == 0 and (PROJ_W - OFF_LR) % W_PREP_ROWS == 0 and n == PROJ_W + GLA_RANK
    return pl.pallas_call(
        _wprep_kernel,
        grid=(PROJ_W // W_PREP_ROWS,),
        in_specs=[pl.BlockSpec(memory_space=pl.ANY)],
        out_specs=[pl.BlockSpec((k, W_PREP_ROWS), lambda i: (0, i)),
                   pl.BlockSpec((k, LANES), lambda i: (0, 0))],
        out_shape=[jax.ShapeDtypeStruct((k, PROJ_W), BF16), jax.ShapeDtypeStruct((k, LANES), BF16)],
        scratch_shapes=[
            pltpu.VMEM((2, W_PREP_ROWS, k), F32),
            pltpu.VMEM((LANES, k), F32),
            pltpu.SemaphoreType.DMA((2,)),
            pltpu.SemaphoreType.DMA((1,)),
        ],
        compiler_params=pltpu.CompilerParams(
            dimension_semantics=("arbitrary",), vmem_limit_bytes=VMEM_LIMIT),
        name="w_prep",
    )(w_t)


def _inproj_kernel(x_ref, g_ref, w_ref, wlr_ref, proj_ref, glr_ref, xn_ref):
    @pl.when(pl.program_id(1) == 0)
    def _():
        x = x_ref[...]
        ms = jnp.mean(x * x, axis=-1, keepdims=True)
        xn = (x * lax.rsqrt(ms + RMS_EPS) * g_ref[...]).astype(BF16)
        xn_ref[...] = xn
        glr_ref[...] = jnp.dot(xn, wlr_ref[...], preferred_element_type=F32)

    proj_ref[...] = jnp.dot(xn_ref[...], w_ref[...], preferred_element_type=F32).astype(BF16)


def _in_proj(x2, g, w_main, w_lr):
    m = x2.shape[0]
    tm = min(IN_TM, m)
    return pl.pallas_call(
        _inproj_kernel,
        grid=(m // tm, PROJ_W // IN_TN),
        in_specs=[
            pl.BlockSpec((tm, D_MODEL), lambda i, j: (i, 0)),
            pl.BlockSpec((1, D_MODEL), lambda i, j: (0, 0)),
            pl.BlockSpec((D_MODEL, IN_TN), lambda i, j: (0, j)),
            pl.BlockSpec((D_MODEL, LANES), lambda i, j: (0, 0)),
        ],
        out_specs=[
            pl.BlockSpec((tm, IN_TN), lambda i, j: (i, j)),
            pl.BlockSpec((tm, LANES), lambda i, j: (i, 0)),
        ],
        out_shape=[
            jax.ShapeDtypeStruct((m, PROJ_W), BF16),
            jax.ShapeDtypeStruct((m, LANES), F32),
        ],
        scratch_shapes=[pltpu.VMEM((tm, D_MODEL), BF16)],
        compiler_params=pltpu.CompilerParams(
            dimension_semantics=("arbitrary", "arbitrary"), vmem_limit_bytes=VMEM_LIMIT),
        name="in_proj",
    )(x2, g, w_main, w_lr)


def _gla_kernel(q_ref, k_ref, v_ref, r_ref, glr_ref, wup_ref, bgk_ref, ng_ref, o_ref,
                la_ref, oi_ref, qi_ref, ki_ref, qd_ref, kd_ref, a_ref, kv_ref, dec_ref, sp_ref):
    t = q_ref.shape[0]
    c = GLA_CHUNK
    nchunk = t // c
    z = jnp.dot(glr_ref[...].astype(BF16), wup_ref[...], preferred_element_type=F32) + bgk_ref[...]
    la_ref[...] = (jnp.minimum(z, 0.0) - jnp.log(1.0 + jnp.exp(-jnp.abs(z)))) * (1.0 / GLA_GATE_NORM)

    ii = lax.broadcasted_iota(jnp.int32, (c, c), 0)
    jj = lax.broadcasted_iota(jnp.int32, (c, c), 1)
    causal = jj <= ii
    tri = causal.astype(BF16)
    nt = (((1,), (1,)), ((), ()))
    tn = (((0,), (0,)), ((), ()))

    def chunk_rows(n):
        return pl.ds(pl.multiple_of(n * c, c), c)

    def decays(n, carry):
        rows = chunk_rows(n)
        la = la_ref[rows, :]
        hi = la.astype(BF16)
        r1 = la - hi.astype(F32)
        mid = r1.astype(BF16)
        lo = (r1 - mid.astype(F32)).astype(BF16)
        parts = jnp.dot(tri, jnp.concatenate([hi, mid, lo], axis=1), preferred_element_type=F32)
        bcum = parts[:, 0:GLA_DK] + parts[:, GLA_DK:2 * GLA_DK] + parts[:, 2 * GLA_DK:]
        b_mid = bcum[c // 2 - 1:c // 2, :]
        b_last = bcum[c - 1:c, :]
        q = q_ref[rows, :].astype(F32) * (GLA_DK ** -0.5)
        k = k_ref[rows, :].astype(F32)
        qi_ref[rows, :] = (q * jnp.exp(bcum - b_mid)).astype(BF16)
        ki_ref[rows, :] = (k * jnp.exp(b_mid - bcum)).astype(BF16)
        qd_ref[rows, :] = (q * jnp.exp(bcum)).astype(BF16)
        kd_ref[rows, :] = (k * jnp.exp(b_last - bcum)).astype(BF16)
        dec_ref[n] = jnp.broadcast_to(jnp.exp(b_last), dec_ref.shape[1:])
        return carry

    lax.fori_loop(0, nchunk, decays, 0, unroll=4)

    def scores(n, carry):
        rows = chunk_rows(n)
        a = lax.dot_general(qi_ref[rows, :], ki_ref[rows, :], nt, preferred_element_type=F32)
        a_ref[rows, :] = jnp.where(causal, a, 0.0).astype(BF16)
        return carry

    lax.fori_loop(0, nchunk, scores, 0, unroll=8)

    def intra(n, carry):
        rows = chunk_rows(n)
        v = v_ref[rows, :]
        oi_ref[rows, :] = jnp.dot(a_ref[rows, :], v, preferred_element_type=F32)
        kv_ref[n] = lax.dot_general(v, kd_ref[rows, :], tn, preferred_element_type=F32)
        return carry

    lax.fori_loop(0, nchunk, intra, 0, unroll=8)

    def scan(n, s_t):
        sp_ref[n] = s_t.astype(BF16)
        return s_t * dec_ref[n][0:1, :] + kv_ref[n]

    lax.fori_loop(0, nchunk, scan, jnp.zeros((GLA_DV, GLA_DK), F32))

    def inter(n, carry):
        rows = chunk_rows(n)
        o = oi_ref[rows, :] + lax.dot_general(qd_ref[rows, :], sp_ref[n], nt,
                                              preferred_element_type=F32)
        ms = jnp.mean(o * o, axis=-1, keepdims=True)
        o = o * lax.rsqrt(ms + RMS_EPS) * ng_ref[...]
        o = o * _silu(r_ref[rows, :].astype(F32))
        o_ref[rows, :] = o.astype(BF16)
        return carry

    lax.fori_loop(0, nchunk, inter, 0, unroll=8)


def _gla(proj, glr, wup, bgk, ng, batch, seq):
    m = proj.shape[0]
    return pl.pallas_call(
        _gla_kernel,
        grid=(batch, GLA_HEADS),
        in_specs=[
            pl.BlockSpec((seq, GLA_DK), lambda b, h: (b, OFF_GQ // GLA_DK + h)),
            pl.BlockSpec((seq, GLA_DK), lambda b, h: (b, OFF_GK // GLA_DK + h)),
            pl.BlockSpec((seq, GLA_DV), lambda b, h: (b, OFF_GV // GLA_DV + h)),
            pl.BlockSpec((seq, GLA_DV), lambda b, h: (b, OFF_GR // GLA_DV + h)),
            pl.BlockSpec((seq, LANES), lambda b, h: (b, 0)),
            pl.BlockSpec((LANES, GLA_DK), lambda b, h: (0, h)),
            pl.BlockSpec((1, GLA_DK), lambda b, h: (0, h)),
            pl.BlockSpec((1, GLA_DV), lambda b, h: (0, h)),
        ],
        out_specs=pl.BlockSpec((seq, GLA_DV), lambda b, h: (b, h)),
        out_shape=jax.ShapeDtypeStruct((m, GLA_V_W), BF16),
        scratch_shapes=[
            pltpu.VMEM((seq, GLA_DK), F32),
            pltpu.VMEM((seq, GLA_DV), F32),
            pltpu.VMEM((seq, GLA_DK), BF16),
            pltpu.VMEM((seq, GLA_DK), BF16),
            pltpu.VMEM((seq, GLA_DK), BF16),
            pltpu.VMEM((seq, GLA_DK), BF16),
            pltpu.VMEM((seq, GLA_CHUNK), BF16),
            pltpu.VMEM((seq // GLA_CHUNK, GLA_DV, GLA_DK), F32),
            pltpu.VMEM((seq // GLA_CHUNK, 8, GLA_DK), F32),
            pltpu.VMEM((seq // GLA_CHUNK, GLA_DV, GLA_DK), BF16),
        ],
        compiler_params=pltpu.CompilerParams(
            dimension_semantics=("arbitrary", "arbitrary"), vmem_limit_bytes=VMEM_LIMIT),
        name="gla",
    )(proj, proj, proj, proj, glr, wup, bgk, ng)


def _swa_kernel(sink_ref, q_ref, kp_ref, kc_ref, vp_ref, vc_ref, o_ref):
    blk = SWA_BLOCK
    half = SWA_HEAD_DIM
    n = pl.program_id(1)
    k_all = jnp.concatenate([kp_ref[...], kc_ref[...]], axis=0)
    v_all = jnp.concatenate([vp_ref[...], vc_ref[...]], axis=0)
    qi = lax.broadcasted_iota(jnp.int32, (blk, 2 * blk), 0)
    kj = lax.broadcasted_iota(jnp.int32, (blk, 2 * blk), 1)
    rel = qi + blk - kj
    valid = (rel >= 0) & (rel < blk) & ((kj >= blk) | (n > 0))
    sink_col = kj == 0
    lane_kv = lax.broadcasted_iota(jnp.int32, (2 * blk, LANES), 1)
    lo_kv = lane_kv < half
    lane_q = lax.broadcasted_iota(jnp.int32, (blk, LANES), 1)
    lo_q = lane_q < half
    nt = (((1,), (1,)), ((), ()))
    key_idx = lax.broadcasted_iota(jnp.int32, (2 * blk, LANES), 0).astype(F32)
    key_cols = jnp.where((lane_kv & (half - 1)) < ALIBI_PARTS, key_idx, 0.0)
    q_pos = lax.broadcasted_iota(jnp.int32, (blk, 1), 0).astype(F32) + float(blk)
    q_scale = jnp.asarray(SWA_HEAD_DIM ** -0.5, BF16)

    for p in range(SWA_KV_HEADS // 2):
        k_slab = k_all[:, p * LANES:(p + 1) * LANES].astype(F32)
        v_slab = v_all[:, p * LANES:(p + 1) * LANES].astype(F32)
        k_roll = pltpu.roll(k_slab, half, 1)
        v_roll = pltpu.roll(v_slab, half, 1)
        for hh in range(2):
            h = 2 * p + hh
            k_lo, k_hi = (k_slab, k_roll) if hh == 0 else (k_roll, k_slab)
            kd = (jnp.where(lo_kv, k_lo, key_cols).astype(BF16),
                  jnp.where(lo_kv, key_cols, k_hi).astype(BF16))
            v_lo, v_hi = (v_slab, v_roll) if hh == 0 else (v_roll, v_slab)
            vd = jnp.where(lo_kv, v_lo, v_hi)
            vd = jnp.where(key_idx == 0.0, 0.0, vd).astype(BF16)
            for gp in range(SWA_GROUP // 2):
                col = (h * (SWA_GROUP // 2) + gp) * LANES
                qs = q_ref[:, col:col + LANES] * q_scale
                outs = []
                for gg in range(2):
                    head = h * SWA_GROUP + 2 * gp + gg
                    parts = _bf16_parts(2.0 ** (-8.0 * (head + 1) / SWA_Q_HEADS), ALIBI_PARTS)
                    slope = sum(parts)
                    aug = jnp.zeros((blk, LANES), F32)
                    for idx, part in enumerate(parts):
                        aug = jnp.where(lane_q == (half if gg == 0 else 0) + idx, part, aug)
                    keep = lo_q if gg == 0 else jnp.logical_not(lo_q)
                    qm = jnp.where(keep, qs, aug.astype(BF16))
                    s = lax.dot_general(qm, kd[gg], nt, preferred_element_type=F32)
                    sink = sink_ref[head] + slope * q_pos
                    s = jnp.where(sink_col, sink, jnp.where(valid, s, -jnp.inf))
                    mx = jnp.max(s, axis=-1, keepdims=True)
                    pe = jnp.exp(s - mx)
                    den = jnp.sum(pe, axis=-1, keepdims=True)
                    o = jnp.dot(pe.astype(BF16), vd, preferred_element_type=F32)
                    outs.append(o / den)
                o_ref[:, col:col + LANES] = jnp.where(lo_q, outs[0], outs[1]).astype(BF16)


def _swa(proj, sinks, batch, seq):
    m = proj.shape[0]
    nb = seq // SWA_BLOCK
    qcol = OFF_SQ // SWA_Q_W
    kcol = OFF_SK // SWA_KV_W
    vcol = OFF_SV // SWA_KV_W
    cur = lambda c: (lambda b, n: (b * nb + n, c))
    prev = lambda c: (lambda b, n: (b * nb + jnp.maximum(n - 1, 0), c))
    return pl.pallas_call(
        _swa_kernel,
        grid=(batch, nb),
        in_specs=[
            pl.BlockSpec(memory_space=pltpu.SMEM),
            pl.BlockSpec((SWA_BLOCK, SWA_Q_W), cur(qcol)),
            pl.BlockSpec((SWA_BLOCK, SWA_KV_W), prev(kcol)),
            pl.BlockSpec((SWA_BLOCK, SWA_KV_W), cur(kcol)),
            pl.BlockSpec((SWA_BLOCK, SWA_KV_W), prev(vcol)),
            pl.BlockSpec((SWA_BLOCK, SWA_KV_W), cur(vcol)),
        ],
        out_specs=pl.BlockSpec((SWA_BLOCK, SWA_Q_W), lambda b, n: (b * nb + n, 0)),
        out_shape=jax.ShapeDtypeStruct((m, SWA_Q_W), BF16),
        compiler_params=pltpu.CompilerParams(
            dimension_semantics=("arbitrary", "arbitrary"), vmem_limit_bytes=VMEM_LIMIT),
        name="swa",
    )(sinks, proj, proj, proj, proj, proj)


def _out_route_kernel(og_ref, os_ref, x_ref, wo_ref, g_ref, wr_ref, br_ref,
                      h_ref, hn_ref, route_ref, cnt_ref, hnb_ref):
    t = pl.program_id(0)
    slot = lax.rem(t, 2)

    @pl.when(t == 0)
    def _():
        hnb_ref[...] = jnp.zeros_like(hnb_ref)

    logits = jnp.dot(hnb_ref[1 - slot], wr_ref[...], preferred_element_type=F32) + br_ref[...]

    h = x_ref[...]
    h = h + jnp.dot(og_ref[...], wo_ref[0:GLA_V_W, :], preferred_element_type=F32)
    h = h + jnp.dot(os_ref[...], wo_ref[GLA_V_W:, :], preferred_element_type=F32)
    h_ref[...] = h
    ms = jnp.mean(h * h, axis=-1, keepdims=True)
    hn = h * lax.rsqrt(ms + RMS_EPS) * g_ref[...]
    hn_ref[...] = hn
    hnb_ref[slot] = hn.astype(BF16)

    lane = lax.broadcasted_iota(jnp.int32, logits.shape, 1)
    lanef = lane.astype(F32)
    big = float(LANES)
    ninf = -jnp.inf
    gl = jnp.where(lane < N_GROUPS, logits, ninf)
    gmax = jnp.max(gl, axis=-1, keepdims=True)
    g_p = 1.0 / jnp.sum(jnp.exp(gl - gmax), axis=-1, keepdims=True)
    g_idx = jnp.min(jnp.where(gl == gmax, lanef, big), axis=-1, keepdims=True)
    lo = N_GROUPS + EXPERTS_PER_GROUP * g_idx
    el = jnp.where((lanef >= lo) & (lanef < lo + EXPERTS_PER_GROUP), logits, ninf)
    m1 = jnp.max(el, axis=-1, keepdims=True)
    i1 = jnp.min(jnp.where(el == m1, lanef, big), axis=-1, keepdims=True)
    el2 = jnp.where(lanef == i1, ninf, el)
    m2 = jnp.max(el2, axis=-1, keepdims=True)
    i2 = jnp.min(jnp.where(el2 == m2, lanef, big), axis=-1, keepdims=True)
    d = jnp.exp(m2 - m1)
    c1 = g_p / (1.0 + d)
    c2 = g_p * d / (1.0 + d)
    tm = logits.shape[0]
    chosen = ((lanef == i1) | (lanef == i2)).astype(BF16)
    ri = lax.broadcasted_iota(jnp.int32, (tm, tm), 0)
    ci = lax.broadcasted_iota(jnp.int32, (tm, tm), 1)
    earlier = jnp.dot((ci < ri).astype(BF16), chosen, preferred_element_type=F32)
    r1 = jnp.sum(jnp.where(lanef == i1, earlier, 0.0), axis=-1, keepdims=True)
    r2 = jnp.sum(jnp.where(lanef == i2, earlier, 0.0), axis=-1, keepdims=True)
    cnt = jnp.sum(chosen.astype(F32), axis=0, keepdims=True)
    cnt_ref[...] = jnp.broadcast_to(cnt, cnt_ref.shape)

    fields = [i1 - N_GROUPS, i2 - N_GROUPS, c1, c2, r1, r2]
    route = jnp.zeros_like(logits)
    for idx, val in enumerate(fields):
        route = jnp.where(lane == idx, val, route)
    route_ref[...] = route[:, 0:ROUTE_W]


def _out_route(o_gla, o_swa, x2, w_out, g, w_rt, b_rt):
    m = x2.shape[0]
    tm = min(OUT_TM, m)
    ntile = m // tm
    row = lambda i: (jnp.minimum(i, ntile - 1), 0)
    routed = lambda i: (jnp.maximum(i - 1, 0), 0)
    fixed = lambda i: (0, 0)
    return pl.pallas_call(
        _out_route_kernel,
        grid=(ntile + 1,),
        in_specs=[
            pl.BlockSpec((tm, GLA_V_W), row),
            pl.BlockSpec((tm, SWA_Q_W), row),
            pl.BlockSpec((tm, D_MODEL), row),
            pl.BlockSpec((GLA_V_W + SWA_Q_W, D_MODEL), fixed),
            pl.BlockSpec((1, D_MODEL), fixed),
            pl.BlockSpec((D_MODEL, LANES), fixed),
            pl.BlockSpec((1, LANES), fixed),
        ],
        out_specs=[
            pl.BlockSpec((tm, D_MODEL), row),
            pl.BlockSpec((tm, D_MODEL), row),
            pl.BlockSpec((tm, ROUTE_W), routed),
            pl.BlockSpec((1, 8, LANES), lambda i: (jnp.maximum(i - 1, 0), 0, 0)),
        ],
        out_shape=[
            jax.ShapeDtypeStruct((m, D_MODEL), F32),
            jax.ShapeDtypeStruct((m, D_MODEL), F32),
            jax.ShapeDtypeStruct((m, ROUTE_W), F32),
            jax.ShapeDtypeStruct((m // tm, 8, LANES), F32),
        ],
        scratch_shapes=[pltpu.VMEM((2, tm, D_MODEL), BF16)],
        compiler_params=pltpu.CompilerParams(
            dimension_semantics=("arbitrary",), vmem_limit_bytes=VMEM_LIMIT),
        name="out_route",
    )(o_gla, o_swa, x2, w_out, g, w_rt, b_rt)


LANES_LOG2 = 7
MOE_ROW_GROUP_LOG2 = 3
MOE_ROW_GROUP = 1 << MOE_ROW_GROUP_LOG2
IDS_WINDOW_ROWS = 3


def _moe_kernel(te_ref, cb_ref, nv_ref, ne_ref, nt_ref, tok_hbm, hn_hbm, wg_hbm, wu_hbm, wd_hbm, y_ref,
                xbuf, gsem, ids_smem, isem, wgs, wus, wds, wsem, wgb, wub, wdb):
    tr = MOE_TR
    grp = MOE_ROW_GROUP
    t = pl.program_id(0)
    nt = nt_ref[0]
    gs = lax.rem(t, 2)
    cs = 1 - gs

    def ids_copy(tile, s):
        row0 = lax.shift_right_logical(cb_ref[tile], LANES_LOG2)
        return pltpu.make_async_copy(tok_hbm.at[pl.ds(row0, IDS_WINDOW_ROWS)], ids_smem.at[s], isem.at[s])

    def weight_copies(e):
        return (pltpu.make_async_copy(wg_hbm.at[e], wgs, wsem.at[0]),
                pltpu.make_async_copy(wu_hbm.at[e], wus, wsem.at[1]),
                pltpu.make_async_copy(wd_hbm.at[e], wds, wsem.at[2]))

    def rows_used(tile):
        groups = lax.shift_right_logical(nv_ref[tile] + (grp - 1), MOE_ROW_GROUP_LOG2)
        return lax.shift_left(groups, MOE_ROW_GROUP_LOG2)

    @pl.when(t == 0)
    def _():
        xbuf[...] = jnp.zeros_like(xbuf)
        ids_copy(0, 0).start()
        for cp in weight_copies(te_ref[0]):
            cp.start()

    @pl.when(t < nt)
    def _():
        ids_copy(t, gs).wait()

        @pl.when(t + 1 < nt)
        def _():
            ids_copy(t + 1, cs).start()

        off = cb_ref[t] & (LANES - 1)
        n_real = nv_ref[t]
        for g0 in range(0, tr, grp):
            @pl.when(g0 < n_real)
            def _():
                for r in range(g0, g0 + grp):
                    q = off + r
                    tok = ids_smem[gs, lax.shift_right_logical(q, LANES_LOG2), q & (LANES - 1)]
                    pltpu.make_async_copy(hn_hbm.at[pl.ds(tok, 1)], xbuf.at[gs, pl.ds(r, 1)],
                                          gsem.at[gs]).start(priority=r % 2)

    @pl.when((t >= 1) & (t <= nt))
    def _():
        c = t - 1
        changed = (c == 0) | (te_ref[c] != te_ref[jnp.maximum(c - 1, 0)])

        @pl.when(changed)
        def _():
            for cp in weight_copies(te_ref[c]):
                cp.wait()
            wgb[...] = wgs[...].astype(BF16)
            wub[...] = wus[...].astype(BF16)
            wdb[...] = wds[...].astype(BF16)
            nxt = ne_ref[c]

            @pl.when(nxt >= 0)
            def _():
                for cp in weight_copies(nxt):
                    cp.start()

        n_rows = pl.multiple_of(rows_used(c), grp)
        pltpu.make_async_copy(hn_hbm.at[pl.ds(0, n_rows)], xbuf.at[cs, pl.ds(0, n_rows)],
                              gsem.at[cs]).wait()
        x = xbuf[cs].astype(BF16)
        g = jnp.dot(x, wgb[...], preferred_element_type=F32)
        u = jnp.dot(x, wub[...], preferred_element_type=F32)
        hmid = (_silu(g) * u).astype(BF16)
        y_ref[...] = jnp.dot(hmid, wdb[...], preferred_element_type=F32)

    @pl.when(t > nt)
    def _():
        y_ref[...] = jnp.zeros_like(y_ref)


def _moe(hn, w_gate, w_up, w_down, plan):
    n_tiles = plan["tile_expert"].shape[0]
    p_rows = n_tiles * MOE_TR
    prev = lambda t: jnp.maximum(t - 1, 0)
    any_spec = pl.BlockSpec(memory_space=pl.ANY)
    grid_spec = pltpu.PrefetchScalarGridSpec(
        num_scalar_prefetch=5,
        grid=(n_tiles + 1,),
        in_specs=[any_spec] * 5,
        out_specs=pl.BlockSpec((MOE_TR, D_MODEL), lambda t, *_: (prev(t), 0)),
        scratch_shapes=[
            pltpu.VMEM((2, MOE_TR, D_MODEL), F32),
            pltpu.SemaphoreType.DMA((2,)),
            pltpu.SMEM((2, IDS_WINDOW_ROWS, LANES), jnp.int32),
            pltpu.SemaphoreType.DMA((2,)),
            pltpu.VMEM((D_MODEL, D_EXPERT), F32),
            pltpu.VMEM((D_MODEL, D_EXPERT), F32),
            pltpu.VMEM((D_EXPERT, D_MODEL), F32),
            pltpu.SemaphoreType.DMA((3,)),
            pltpu.VMEM((D_MODEL, D_EXPERT), BF16),
            pltpu.VMEM((D_MODEL, D_EXPERT), BF16),
            pltpu.VMEM((D_EXPERT, D_MODEL), BF16),
        ],
    )
    return pl.pallas_call(
        _moe_kernel,
        grid_spec=grid_spec,
        out_shape=jax.ShapeDtypeStruct((p_rows, D_MODEL), F32),
        compiler_params=pltpu.CompilerParams(
            dimension_semantics=("arbitrary",), vmem_limit_bytes=VMEM_LIMIT),
        name="moe",
    )(plan["tile_expert"], plan["tile_base"], plan["tile_rows"], plan["next_expert"], plan["num_tiles"],
      plan["sorted_tokens"], hn, w_gate, w_up, w_down)


def _combine_kernel(pos_hbm, y_hbm, h_ref, route_ref, g_ref, o_ref, ybuf, gsem, ids_smem, isem):
    tm = h_ref.shape[0]
    t = pl.program_id(0)
    ntile = pl.num_programs(0) - 1

    def ids_copy(tile, s):
        return pltpu.make_async_copy(pos_hbm.at[tile], ids_smem.at[s], isem.at[s])

    @pl.when(t == 0)
    def _():
        ids_copy(0, 0).start()

    @pl.when(t < ntile)
    def _():
        s = lax.rem(t, 2)
        ids_copy(t, s).wait()

        @pl.when(t + 1 < ntile)
        def _():
            ids_copy(t + 1, 1 - s).start()

        for kk in range(2):
            for r in range(tm):
                flat = kk * tm + r
                row = ids_smem[s, flat // LANES, flat % LANES]
                pltpu.make_async_copy(y_hbm.at[pl.ds(row, 1)], ybuf.at[s, kk, pl.ds(r, 1)],
                                      gsem.at[s, kk]).start(priority=r % 2)

    @pl.when(t >= 1)
    def _():
        s = lax.rem(t - 1, 2)
        for kk in range(2):
            pltpu.make_async_copy(y_hbm.at[pl.ds(0, tm)], ybuf.at[s, kk], gsem.at[s, kk]).wait()
        route = route_ref[...]
        h = h_ref[...] + route[:, 2:3] * ybuf[s, 0] + route[:, 3:4] * ybuf[s, 1]
        ms = jnp.mean(h * h, axis=-1, keepdims=True)
        o_ref[...] = h * lax.rsqrt(ms + RMS_EPS) * g_ref[...]


def _combine(y, h, route, g, pos2d):
    m = h.shape[0]
    tm = min(CMB_TM, m)
    ntile = m // tm
    pos3 = pos2d.reshape(ntile, tm, 2).transpose(0, 2, 1).reshape(ntile, 2 * tm // LANES, LANES)
    row = lambda i: (jnp.maximum(i - 1, 0), 0)
    return pl.pallas_call(
        _combine_kernel,
        grid=(ntile + 1,),
        in_specs=[
            pl.BlockSpec(memory_space=pl.ANY),
            pl.BlockSpec(memory_space=pl.ANY),
            pl.BlockSpec((tm, D_MODEL), row),
            pl.BlockSpec((tm, ROUTE_W), row),
            pl.BlockSpec((1, D_MODEL), lambda i: (0, 0)),
        ],
        out_specs=pl.BlockSpec((tm, D_MODEL), row),
        out_shape=jax.ShapeDtypeStruct((m, D_MODEL), F32),
        scratch_shapes=[
            pltpu.VMEM((2, 2, tm, D_MODEL), F32),
            pltpu.SemaphoreType.DMA((2, 2)),
            pltpu.SMEM((2, 2 * tm // LANES, LANES), jnp.int32),
            pltpu.SemaphoreType.DMA((2,)),
        ],
        compiler_params=pltpu.CompilerParams(
            dimension_semantics=("arbitrary",), vmem_limit_bytes=VMEM_LIMIT),
        name="combine",
    )(pos3, y, h, route, g)


def _dispatch_plan(route, tile_counts, m):
    tr = MOE_TR
    p_rows = 2 * m + N_EXPERTS * tr
    n_tiles = p_rows // tr
    n_tok_tiles = tile_counts.shape[0]
    cnt = tile_counts[:, 0, N_GROUPS:N_GROUPS + N_EXPERTS].astype(jnp.int32)
    before_tile = jnp.cumsum(cnt, axis=0) - cnt
    counts = jnp.sum(cnt, axis=0)
    padded = ((counts + tr - 1) // tr) * tr
    ends = jnp.cumsum(padded)
    starts = ends - padded
    base = (starts[None, :] + before_tile).astype(jnp.int32)
    expert = route[:, 0:2].astype(jnp.int32).reshape(n_tok_tiles, m // n_tok_tiles, 2)
    rank = route[:, 4:6].astype(jnp.int32).reshape(n_tok_tiles, m // n_tok_tiles, 2)
    hit = expert[..., None] == jnp.arange(N_EXPERTS, dtype=jnp.int32)
    pos = rank + jnp.sum(jnp.where(hit, base[:, None, None, :], 0), axis=-1)
    pos = pos.reshape(m, 2)
    num_tiles = (ends[-1] // tr).astype(jnp.int32)
    tile_idx = jnp.minimum(jnp.arange(n_tiles, dtype=jnp.int32), num_tiles - 1)
    tile_expert = jnp.sum((tile_idx[:, None] * tr >= ends[None, :]).astype(jnp.int32), axis=1)
    n_slots = 2 * m
    slot = jnp.arange(n_slots, dtype=jnp.int32)
    sorted_key = jnp.sort(route[:, 0:2].astype(jnp.int32).reshape(-1) * n_slots + slot)
    sorted_tokens = (sorted_key & (n_slots - 1)) >> 1
    sorted_tokens = jnp.pad(sorted_tokens, (0, IDS_WINDOW_ROWS * LANES)).reshape(-1, LANES)
    dense_starts = jnp.cumsum(counts) - counts
    in_expert = tile_idx - (starts // tr)[tile_expert]
    tile_base = dense_starts[tile_expert] + in_expert * tr
    tile_rows = jnp.clip(counts[tile_expert] - in_expert * tr, 0, tr)
    tile_rows = jnp.where(jnp.arange(n_tiles) < num_tiles, tile_rows, 0)
    ids = jnp.arange(N_EXPERTS, dtype=jnp.int32)
    present = jnp.where(counts > 0, ids, N_EXPERTS)
    next_ge = lax.cummin(present, axis=0, reverse=True)
    next_gt = jnp.concatenate([next_ge[1:], jnp.full((1,), N_EXPERTS, jnp.int32)])
    next_gt = jnp.where(next_gt >= N_EXPERTS, -1, next_gt)
    plan = dict(tile_expert=tile_expert.astype(jnp.int32), tile_base=tile_base.astype(jnp.int32),
                tile_rows=tile_rows.astype(jnp.int32), next_expert=next_gt[tile_expert].astype(jnp.int32),
                num_tiles=num_tiles.reshape(1), sorted_tokens=sorted_tokens)
    return plan, pos


def kernel(x, norm_mix_g, w_in, w_gk_up, b_gk, gla_norm_g, swa_sinks, w_out, norm_ffn_g,
           w_group, b_group, w_router, b_router, w_gate, w_up, w_down, norm_final_g):
    batch, seq, d = x.shape
    m = batch * seq
    assert w_in.shape[0] == 1, "single-layer block"
    x2 = x.reshape(m, d)
    w_main, w_lr = _w_prep(jnp.transpose(w_in[0]))
    wup = jnp.pad(w_gk_up[0], ((0, LANES - GLA_RANK), (0, 0))).astype(BF16)
    w_rt = jnp.pad(jnp.concatenate([w_group[0], w_router[0]], axis=1),
                   ((0, 0), (0, LANES - N_GROUPS - N_EXPERTS))).astype(BF16)
    b_rt = jnp.pad(jnp.concatenate([b_group[0], b_router[0]]),
                   (0, LANES - N_GROUPS - N_EXPERTS)).reshape(1, LANES)

    proj, glr = _in_proj(x2, norm_mix_g[0].reshape(1, d), w_main, w_lr)
    o_gla = _gla(proj, glr, wup, b_gk[0].reshape(1, GLA_QK_W),
                 gla_norm_g[0].reshape(1, GLA_V_W), batch, seq)
    o_swa = _swa(proj, swa_sinks[0], batch, seq)
    h_mid, hn, route, tile_counts = _out_route(o_gla, o_swa, x2, w_out[0].astype(BF16),
                                               norm_ffn_g[0].reshape(1, d), w_rt, b_rt)
    plan, pos2d = _dispatch_plan(route, tile_counts, m)
    y = _moe(hn, w_gate[0], w_up[0], w_down[0], plan)
    out = _combine(y, h_mid, route, norm_final_g.reshape(1, d), pos2d)
    return out.reshape(batch, seq, d)
```

```python
import functools

import jax
import jax.numpy as jnp
import numpy as np
from jax import lax
from jax.experimental import pallas as pl
from jax.experimental.pallas import tpu as pltpu

F32 = jnp.float32
BF16 = jnp.bfloat16

D_MODEL = 2048
GLA_HEADS = 4
GLA_DK = 128
GLA_DV = 256
GLA_RANK = 16
GLA_GATE_NORM = 16.0
GLA_CHUNK = 64
SWA_Q_HEADS = 16
SWA_KV_HEADS = 4
SWA_GROUP = SWA_Q_HEADS // SWA_KV_HEADS
SWA_HEAD_DIM = 64
SWA_BLOCK = 128
N_GROUPS = 4
EXPERTS_PER_GROUP = 16
N_EXPERTS = N_GROUPS * EXPERTS_PER_GROUP
D_EXPERT = 256
RMS_EPS = 1e-6

GLA_QK_W = GLA_HEADS * GLA_DK
GLA_V_W = GLA_HEADS * GLA_DV
SWA_Q_W = SWA_Q_HEADS * SWA_HEAD_DIM
SWA_KV_W = SWA_KV_HEADS * SWA_HEAD_DIM
OFF_GQ = 0
OFF_GK = OFF_GQ + GLA_QK_W
OFF_GV = OFF_GK + GLA_QK_W
OFF_GR = OFF_GV + GLA_V_W
OFF_SQ = OFF_GR + GLA_V_W
OFF_SK = OFF_SQ + SWA_Q_W
OFF_SV = OFF_SK + SWA_KV_W
PROJ_W = OFF_SV + SWA_KV_W
LANES = 128

IN_TM = 1024
IN_TN = 2304
OUT_TM = 256
ROUTE_W = 8
MOE_TR = 256
CMB_TM = 256
VMEM_LIMIT = 56 * 1024 * 1024


def _silu(x):
    return x / (1.0 + jnp.exp(-x))


ALIBI_PARTS = 3


def _bf16_parts(value, n):
    parts, rest = [], np.float32(value)
    for _ in range(n):
        piece = np.float32(np.asarray(rest).astype(jnp.bfloat16))
        parts.append(float(piece))
        rest = np.float32(rest - piece)
    return parts


W_PREP_ROWS = 384
OFF_LR = OFF_GR + GLA_V_W


def _wprep_kernel(wt_hbm, wm_ref, wl_ref, buf, lrbuf, sem, lrsem):
    rows = W_PREP_ROWS
    i = pl.program_id(0)
    n = pl.num_programs(0)
    n_lo = OFF_LR // rows
    slot = lax.rem(i, 2)

    def block_copy(step, s):
        start = jnp.where(step < n_lo, step * rows, OFF_LR + GLA_RANK + (step - n_lo) * rows)
        return pltpu.make_async_copy(wt_hbm.at[pl.ds(pl.multiple_of(start, 8), rows)], buf.at[s], sem.at[s])

    @pl.when(i == 0)
    def _():
        block_copy(0, 0).start()
        lrbuf[...] = jnp.zeros_like(lrbuf)
        lr_copy = pltpu.make_async_copy(wt_hbm.at[pl.ds(OFF_LR, GLA_RANK)], lrbuf.at[pl.ds(0, GLA_RANK)],
                                        lrsem.at[0])
        lr_copy.start()
        lr_copy.wait()
        wl_ref[...] = lrbuf[...].T.astype(BF16)

    @pl.when(i + 1 < n)
    def _():
        block_copy(i + 1, 1 - slot).start()

    block_copy(i, slot).wait()
    wm_ref[...] = buf[slot].T.astype(BF16)


def _w_prep(w_t):
    n, k = w_t.shape
    assert OFF_LR % W_PREP_ROWS == 0 and (PROJ_W - OFF_LR) % W_PREP_ROWS == 0 and n == PROJ_W + GLA_RANK
    return pl.pallas_call(
        _wprep_kernel,
        grid=(PROJ_W // W_PREP_ROWS,),
        in_specs=[pl.BlockSpec(memory_space=pl.ANY)],
        out_specs=[pl.BlockSpec((k, W_PREP_ROWS), lambda i: (0, i)),
                   pl.BlockSpec((k, LANES), lambda i: (0, 0))],
        out_shape=[jax.ShapeDtypeStruct((k, PROJ_W), BF16), jax.ShapeDtypeStruct((k, LANES), BF16)],
        scratch_shapes=[
            pltpu.VMEM((2, W_PREP_ROWS, k), F32),
            pltpu.VMEM((LANES, k), F32),
            pltpu.SemaphoreType.DMA((2,)),
            pltpu.SemaphoreType.DMA((1,)),
        ],
        compiler_params=pltpu.CompilerParams(
            dimension_semantics=("arbitrary",), vmem_limit_bytes=VMEM_LIMIT),
        name="w_prep",
    )(w_t)


def _inproj_kernel(x_ref, g_ref, w_ref, wlr_ref, proj_ref, glr_ref, xn_ref):
    @pl.when(pl.program_id(1) == 0)
    def _():
        x = x_ref[...]
        ms = jnp.mean(x * x, axis=-1, keepdims=True)
        xn = (x * lax.rsqrt(ms + RMS_EPS) * g_ref[...]).astype(BF16)
        xn_ref[...] = xn
        glr_ref[...] = jnp.dot(xn, wlr_ref[...], preferred_element_type=F32)

    proj_ref[...] = jnp.dot(xn_ref[...], w_ref[...], preferred_element_type=F32).astype(BF16)


def _in_proj(x2, g, w_main, w_lr):
    m = x2.shape[0]
    tm = min(IN_TM, m)
    return pl.pallas_call(
        _inproj_kernel,
        grid=(m // tm, PROJ_W // IN_TN),
        in_specs=[
            pl.BlockSpec((tm, D_MODEL), lambda i, j: (i, 0)),
            pl.BlockSpec((1, D_MODEL), lambda i, j: (0, 0)),
            pl.BlockSpec((D_MODEL, IN_TN), lambda i, j: (0, j)),
            pl.BlockSpec((D_MODEL, LANES), lambda i, j: (0, 0)),
        ],
        out_specs=[
            pl.BlockSpec((tm, IN_TN), lambda i, j: (i, j)),
            pl.BlockSpec((tm, LANES), lambda i, j: (i, 0)),
        ],
        out_shape=[
            jax.ShapeDtypeStruct((m, PROJ_W), BF16),
            jax.ShapeDtypeStruct((m, LANES), F32),
        ],
        scratch_shapes=[pltpu.VMEM((tm, D_MODEL), BF16)],
        compiler_params=pltpu.CompilerParams(
            dimension_semantics=("arbitrary", "arbitrary"), vmem_limit_bytes=VMEM_LIMIT),
        name="in_proj",
    )(x2, g, w_main, w_lr)


def _gla_kernel(q_ref, k_ref, v_ref, r_ref, glr_ref, wup_ref, bgk_ref, ng_ref, o_ref,
                la_ref, oi_ref, qi_ref, ki_ref, qd_ref, kd_ref, a_ref, kv_ref, dec_ref, sp_ref):
    t = q_ref.shape[0]
    c = GLA_CHUNK
    nchunk = t // c
    z = jnp.dot(glr_ref[...].astype(BF16), wup_ref[...], preferred_element_type=F32) + bgk_ref[...]
    la_ref[...] = (jnp.minimum(z, 0.0) - jnp.log(1.0 + jnp.exp(-jnp.abs(z)))) * (1.0 / GLA_GATE_NORM)

    ii = lax.broadcasted_iota(jnp.int32, (c, c), 0)
    jj = lax.broadcasted_iota(jnp.int32, (c, c), 1)
    causal = jj <= ii
    tri = causal.astype(BF16)
    nt = (((1,), (1,)), ((), ()))
    tn = (((0,), (0,)), ((), ()))

    def chunk_rows(n):
        return pl.ds(pl.multiple_of(n * c, c), c)

    def decays(n, carry):
        rows = chunk_rows(n)
        la = la_ref[rows, :]
        hi = la.astype(BF16)
        r1 = la - hi.astype(F32)
        mid = r1.astype(BF16)
        lo = (r1 - mid.astype(F32)).astype(BF16)
        parts = jnp.dot(tri, jnp.concatenate([hi, mid, lo], axis=1), preferred_element_type=F32)
        bcum = parts[:, 0:GLA_DK] + parts[:, GLA_DK:2 * GLA_DK] + parts[:, 2 * GLA_DK:]
        b_mid = bcum[c // 2 - 1:c // 2, :]
        b_last = bcum[c - 1:c, :]
        q = q_ref[rows, :].astype(F32) * (GLA_DK ** -0.5)
        k = k_ref[rows, :].astype(F32)
        qi_ref[rows, :] = (q * jnp.exp(bcum - b_mid)).astype(BF16)
        ki_ref[rows, :] = (k * jnp.exp(b_mid - bcum)).astype(BF16)
        qd_ref[rows, :] = (q * jnp.exp(bcum)).astype(BF16)
        kd_ref[rows, :] = (k * jnp.exp(b_last - bcum)).astype(BF16)
        dec_ref[n] = jnp.broadcast_to(jnp.exp(b_last), dec_ref.shape[1:])
        return carry

    lax.fori_loop(0, nchunk, decays, 0, unroll=4)

    def scores(n, carry):
        rows = chunk_rows(n)
        a = lax.dot_general(qi_ref[rows, :], ki_ref[rows, :], nt, preferred_element_type=F32)
        a_ref[rows, :] = jnp.where(causal, a, 0.0).astype(BF16)
        return carry

    lax.fori_loop(0, nchunk, scores, 0, unroll=8)

    def intra(n, carry):
        rows = chunk_rows(n)
        v = v_ref[rows, :]
        oi_ref[rows, :] = jnp.dot(a_ref[rows, :], v, preferred_element_type=F32)
        kv_ref[n] = lax.dot_general(v, kd_ref[rows, :], tn, preferred_element_type=F32)
        return carry

    lax.fori_loop(0, nchunk, intra, 0, unroll=8)

    def scan(n, s_t):
        sp_ref[n] = s_t.astype(BF16)
        return s_t * dec_ref[n][0:1, :] + kv_ref[n]

    lax.fori_loop(0, nchunk, scan, jnp.zeros((GLA_DV, GLA_DK), F32))

    def inter(n, carry):
        rows = chunk_rows(n)
        o = oi_ref[rows, :] + lax.dot_general(qd_ref[rows, :], sp_ref[n], nt,
                                              preferred_element_type=F32)
        ms = jnp.mean(o * o, axis=-1, keepdims=True)
        o = o * lax.rsqrt(ms + RMS_EPS) * ng_ref[...]
        o = o * _silu(r_ref[rows, :].astype(F32))
        o_ref[rows, :] = o.astype(BF16)
        return carry

    lax.fori_loop(0, nchunk, inter, 0, unroll=8)


def _gla(proj, glr, wup, bgk, ng, batch, seq):
    m = proj.shape[0]
    return pl.pallas_call(
        _gla_kernel,
        grid=(batch, GLA_HEADS),
        in_specs=[
            pl.BlockSpec((seq, GLA_DK), lambda b, h: (b, OFF_GQ // GLA_DK + h)),
            pl.BlockSpec((seq, GLA_DK), lambda b, h: (b, OFF_GK // GLA_DK + h)),
            pl.BlockSpec((seq, GLA_DV), lambda b, h: (b, OFF_GV // GLA_DV + h)),
            pl.BlockSpec((seq, GLA_DV), lambda b, h: (b, OFF_GR // GLA_DV + h)),
            pl.BlockSpec((seq, LANES), lambda b, h: (b, 0)),
            pl.BlockSpec((LANES, GLA_DK), lambda b, h: (0, h)),
            pl.BlockSpec((1, GLA_DK), lambda b, h: (0, h)),
            pl.BlockSpec((1, GLA_DV), lambda b, h: (0, h)),
        ],
        out_specs=pl.BlockSpec((seq, GLA_DV), lambda b, h: (b, h)),
        out_shape=jax.ShapeDtypeStruct((m, GLA_V_W), BF16),
        scratch_shapes=[
            pltpu.VMEM((seq, GLA_DK), F32),
            pltpu.VMEM((seq, GLA_DV), F32),
            pltpu.VMEM((seq, GLA_DK), BF16),
            pltpu.VMEM((seq, GLA_DK), BF16),
            pltpu.VMEM((seq, GLA_DK), BF16),
            pltpu.VMEM((seq, GLA_DK), BF16),
            pltpu.VMEM((seq, GLA_CHUNK), BF16),
            pltpu.VMEM((seq // GLA_CHUNK, GLA_DV, GLA_DK), F32),
            pltpu.VMEM((seq // GLA_CHUNK, 8, GLA_DK), F32),
            pltpu.VMEM((seq // GLA_CHUNK, GLA_DV, GLA_DK), BF16),
        ],
        compiler_params=pltpu.CompilerParams(
            dimension_semantics=("arbitrary", "arbitrary"), vmem_limit_bytes=VMEM_LIMIT),
        name="gla",
    )(proj, proj, proj, proj, glr, wup, bgk, ng)


def _swa_kernel(sink_ref, q_ref, kp_ref, kc_ref, vp_ref, vc_ref, o_ref):
    blk = SWA_BLOCK
    half = SWA_HEAD_DIM
    n = pl.program_id(1)
    k_all = jnp.concatenate([kp_ref[...], kc_ref[...]], axis=0)
    v_all = jnp.concatenate([vp_ref[...], vc_ref[...]], axis=0)
    qi = lax.broadcasted_iota(jnp.int32, (blk, 2 * blk), 0)
    kj = lax.broadcasted_iota(jnp.int32, (blk, 2 * blk), 1)
    rel = qi + blk - kj
    valid = (rel >= 0) & (rel < blk) & ((kj >= blk) | (n > 0))
    sink_col = kj == 0
    lane_kv = lax.broadcasted_iota(jnp.int32, (2 * blk, LANES), 1)
    lo_kv = lane_kv < half
    lane_q = lax.broadcasted_iota(jnp.int32, (blk, LANES), 1)
    lo_q = lane_q < half
    nt = (((1,), (1,)), ((), ()))
    key_idx = lax.broadcasted_iota(jnp.int32, (2 * blk, LANES), 0).astype(F32)
    key_cols = jnp.where((lane_kv & (half - 1)) < ALIBI_PARTS, key_idx, 0.0)
    q_pos = lax.broadcasted_iota(jnp.int32, (blk, 1), 0).astype(F32) + float(blk)
    q_scale = jnp.asarray(SWA_HEAD_DIM ** -0.5, BF16)

    for p in range(SWA_KV_HEADS // 2):
        k_slab = k_all[:, p * LANES:(p + 1) * LANES].astype(F32)
        v_slab = v_all[:, p * LANES:(p + 1) * LANES].astype(F32)
        k_roll = pltpu.roll(k_slab, half, 1)
        v_roll = pltpu.roll(v_slab, half, 1)
        for hh in range(2):
            h = 2 * p + hh
            k_lo, k_hi = (k_slab, k_roll) if hh == 0 else (k_roll, k_slab)
            kd = (jnp.where(lo_kv, k_lo, key_cols).astype(BF16),
                  jnp.where(lo_kv, key_cols, k_hi).astype(BF16))
            v_lo, v_hi = (v_slab, v_roll) if hh == 0 else (v_roll, v_slab)
            vd = jnp.where(lo_kv, v_lo, v_hi)
            vd = jnp.where(key_idx == 0.0, 0.0, vd).astype(BF16)
            for gp in range(SWA_GROUP // 2):
                col = (h * (SWA_GROUP // 2) + gp) * LANES
                qs = q_ref[:, col:col + LANES] * q_scale
                outs = []
                for gg in range(2):
                    head = h * SWA_GROUP + 2 * gp + gg
                    parts = _bf16_parts(2.0 ** (-8.0 * (head + 1) / SWA_Q_HEADS), ALIBI_PARTS)
                    slope = sum(parts)
                    aug = jnp.zeros((blk, LANES), F32)
                    for idx, part in enumerate(parts):
                        aug = jnp.where(lane_q == (half if gg == 0 else 0) + idx, part, aug)
                    keep = lo_q if gg == 0 else jnp.logical_not(lo_q)
                    qm = jnp.where(keep, qs, aug.astype(BF16))
                    s = lax.dot_general(qm, kd[gg], nt, preferred_element_type=F32)
                    sink = sink_ref[head] + slope * q_pos
                    s = jnp.where(sink_col, sink, jnp.where(valid, s, -jnp.inf))
                    mx = jnp.max(s, axis=-1, keepdims=True)
                    pe = jnp.exp(s - mx)
                    den = jnp.sum(pe, axis=-1, keepdims=True)
                    o = jnp.dot(pe.astype(BF16), vd, preferred_element_type=F32)
                    outs.append(o / den)
                o_ref[:, col:col + LANES] = jnp.where(lo_q, outs[0], outs[1]).astype(BF16)


def _swa(proj, sinks, batch, seq):
    m = proj.shape[0]
    nb = seq // SWA_BLOCK
    qcol = OFF_SQ // SWA_Q_W
    kcol = OFF_SK // SWA_KV_W
    vcol = OFF_SV // SWA_KV_W
    cur = lambda c: (lambda b, n: (b * nb + n, c))
    prev = lambda c: (lambda b, n: (b * nb + jnp.maximum(n - 1, 0), c))
    return pl.pallas_call(
        _swa_kernel,
        grid=(batch, nb),
        in_specs=[
            pl.BlockSpec(memory_space=pltpu.SMEM),
            pl.BlockSpec((SWA_BLOCK, SWA_Q_W), cur(qcol)),
            pl.BlockSpec((SWA_BLOCK, SWA_KV_W), prev(kcol)),
            pl.BlockSpec((SWA_BLOCK, SWA_KV_W), cur(kcol)),
            pl.BlockSpec((SWA_BLOCK, SWA_KV_W), prev(vcol)),
            pl.BlockSpec((SWA_BLOCK, SWA_KV_W), cur(vcol)),
        ],
        out_specs=pl.BlockSpec((SWA_BLOCK, SWA_Q_W), lambda b, n: (b * nb + n, 0)),
        out_shape=jax.ShapeDtypeStruct((m, SWA_Q_W), BF16),
        compiler_params=pltpu.CompilerParams(
            dimension_semantics=("arbitrary", "arbitrary"), vmem_limit_bytes=VMEM_LIMIT),
        name="swa",
    )(sinks, proj, proj, proj, proj, proj)


def _out_route_kernel(og_ref, os_ref, x_ref, wo_ref, g_ref, wr_ref, br_ref,
                      h_ref, hn_ref, route_ref, route_t_ref, cnt_ref, hnb_ref):
    t = pl.program_id(0)
    slot = lax.rem(t, 2)

    @pl.when(t == 0)
    def _():
        hnb_ref[...] = jnp.zeros_like(hnb_ref)

    logits = jnp.dot(hnb_ref[1 - slot], wr_ref[...], preferred_element_type=F32) + br_ref[...]

    h = x_ref[...]
    h = h + jnp.dot(og_ref[...], wo_ref[0:GLA_V_W, :], preferred_element_type=F32)
    h = h + jnp.dot(os_ref[...], wo_ref[GLA_V_W:, :], preferred_element_type=F32)
    h_ref[...] = h
    ms = jnp.mean(h * h, axis=-1, keepdims=True)
    hn = h * lax.rsqrt(ms + RMS_EPS) * g_ref[...]
    hn_ref[...] = hn
    hnb_ref[slot] = hn.astype(BF16)

    lane = lax.broadcasted_iota(jnp.int32, logits.shape, 1)
    lanef = lane.astype(F32)
    big = float(LANES)
    ninf = -jnp.inf
    gl = jnp.where(lane < N_GROUPS, logits, ninf)
    gmax = jnp.max(gl, axis=-1, keepdims=True)
    g_p = 1.0 / jnp.sum(jnp.exp(gl - gmax), axis=-1, keepdims=True)
    g_idx = jnp.min(jnp.where(gl == gmax, lanef, big), axis=-1, keepdims=True)
    lo = N_GROUPS + EXPERTS_PER_GROUP * g_idx
    el = jnp.where((lanef >= lo) & (lanef < lo + EXPERTS_PER_GROUP), logits, ninf)
    m1 = jnp.max(el, axis=-1, keepdims=True)
    i1 = jnp.min(jnp.where(el == m1, lanef, big), axis=-1, keepdims=True)
    el2 = jnp.where(lanef == i1, ninf, el)
    m2 = jnp.max(el2, axis=-1, keepdims=True)
    i2 = jnp.min(jnp.where(el2 == m2, lanef, big), axis=-1, keepdims=True)
    d = jnp.exp(m2 - m1)
    c1 = g_p / (1.0 + d)
    c2 = g_p * d / (1.0 + d)
    tm = logits.shape[0]
    chosen = ((lanef == i1) | (lanef == i2)).astype(BF16)
    ri = lax.broadcasted_iota(jnp.int32, (tm, tm), 0)
    ci = lax.broadcasted_iota(jnp.int32, (tm, tm), 1)
    earlier = jnp.dot((ci < ri).astype(BF16), chosen, preferred_element_type=F32)
    r1 = jnp.sum(jnp.where(lanef == i1, earlier, 0.0), axis=-1, keepdims=True)
    r2 = jnp.sum(jnp.where(lanef == i2, earlier, 0.0), axis=-1, keepdims=True)
    cnt = jnp.sum(chosen.astype(F32), axis=0, keepdims=True)
    cnt_ref[...] = jnp.broadcast_to(cnt, cnt_ref.shape)

    fields = [i1 - N_GROUPS, i2 - N_GROUPS, c1, c2, r1, r2]
    route = jnp.zeros_like(logits)
    for idx, val in enumerate(fields):
        route = jnp.where(lane == idx, val, route)
    route_ref[...] = route[:, 0:ROUTE_W]
    route_t_ref[...] = route.T[0:ROUTE_W, :][None]


def _out_route(o_gla, o_swa, x2, w_out, g, w_rt, b_rt):
    m = x2.shape[0]
    tm = min(OUT_TM, m)
    ntile = m // tm
    row = lambda i: (jnp.minimum(i, ntile - 1), 0)
    routed = lambda i: (jnp.maximum(i - 1, 0), 0)
    fixed = lambda i: (0, 0)
    return pl.pallas_call(
        _out_route_kernel,
        grid=(ntile + 1,),
        in_specs=[
            pl.BlockSpec((tm, GLA_V_W), row),
            pl.BlockSpec((tm, SWA_Q_W), row),
            pl.BlockSpec((tm, D_MODEL), row),
            pl.BlockSpec((GLA_V_W + SWA_Q_W, D_MODEL), fixed),
            pl.BlockSpec((1, D_MODEL), fixed),
            pl.BlockSpec((D_MODEL, LANES), fixed),
            pl.BlockSpec((1, LANES), fixed),
        ],
        out_specs=[
            pl.BlockSpec((tm, D_MODEL), row),
            pl.BlockSpec((tm, D_MODEL), row),
            pl.BlockSpec((tm, ROUTE_W), routed),
            pl.BlockSpec((1, ROUTE_W, tm), lambda i: (jnp.maximum(i - 1, 0), 0, 0)),
            pl.BlockSpec((1, 8, LANES), lambda i: (jnp.maximum(i - 1, 0), 0, 0)),
        ],
        out_shape=[
            jax.ShapeDtypeStruct((m, D_MODEL), F32),
            jax.ShapeDtypeStruct((m, D_MODEL), F32),
            jax.ShapeDtypeStruct((m, ROUTE_W), F32),
            jax.ShapeDtypeStruct((m // tm, ROUTE_W, tm), F32),
            jax.ShapeDtypeStruct((m // tm, 8, LANES), F32),
        ],
        scratch_shapes=[pltpu.VMEM((2, tm, D_MODEL), BF16)],
        compiler_params=pltpu.CompilerParams(
            dimension_semantics=("arbitrary",), vmem_limit_bytes=VMEM_LIMIT),
        name="out_route",
    )(o_gla, o_swa, x2, w_out, g, w_rt, b_rt)


LANES_LOG2 = 7
MOE_ROW_GROUP_LOG2 = 3
MOE_ROW_GROUP = 1 << MOE_ROW_GROUP_LOG2
IDS_WINDOW_ROWS = 3


def _moe_kernel(te_ref, cb_ref, nv_ref, ne_ref, nt_ref, tok_hbm, hn_hbm, wg_hbm, wu_hbm, wd_hbm, y_ref,
                xbuf, gsem, ids_smem, isem, wgs, wus, wds, wsem, wgb, wub, wdb):
    tr = MOE_TR
    grp = MOE_ROW_GROUP
    t = pl.program_id(0)
    nt = nt_ref[0]
    gs = lax.rem(t, 2)
    cs = 1 - gs

    def ids_copy(tile, s):
        row0 = lax.shift_right_logical(cb_ref[tile], LANES_LOG2)
        return pltpu.make_async_copy(tok_hbm.at[pl.ds(row0, IDS_WINDOW_ROWS)], ids_smem.at[s], isem.at[s])

    def weight_copies(e):
        return (pltpu.make_async_copy(wg_hbm.at[e], wgs, wsem.at[0]),
                pltpu.make_async_copy(wu_hbm.at[e], wus, wsem.at[1]),
                pltpu.make_async_copy(wd_hbm.at[e], wds, wsem.at[2]))

    def rows_used(tile):
        groups = lax.shift_right_logical(nv_ref[tile] + (grp - 1), MOE_ROW_GROUP_LOG2)
        return lax.shift_left(groups, MOE_ROW_GROUP_LOG2)

    @pl.when(t == 0)
    def _():
        xbuf[...] = jnp.zeros_like(xbuf)
        ids_copy(0, 0).start()
        for cp in weight_copies(te_ref[0]):
            cp.start()

    @pl.when(t < nt)
    def _():
        ids_copy(t, gs).wait()

        @pl.when(t + 1 < nt)
        def _():
            ids_copy(t + 1, cs).start()

        off = cb_ref[t] & (LANES - 1)
        n_real = nv_ref[t]
        for g0 in range(0, tr, grp):
            @pl.when(g0 < n_real)
            def _():
                for r in range(g0, g0 + grp):
                    q = off + r
                    tok = ids_smem[gs, lax.shift_right_logical(q, LANES_LOG2), q & (LANES - 1)]
                    pltpu.make_async_copy(hn_hbm.at[pl.ds(tok, 1)], xbuf.at[gs, pl.ds(r, 1)],
                                          gsem.at[gs]).start(priority=r % 2)

    @pl.when((t >= 1) & (t <= nt))
    def _():
        c = t - 1
        changed = (c == 0) | (te_ref[c] != te_ref[jnp.maximum(c - 1, 0)])

        @pl.when(changed)
        def _():
            for cp in weight_copies(te_ref[c]):
                cp.wait()
            wgb[...] = wgs[...].astype(BF16)
            wub[...] = wus[...].astype(BF16)
            wdb[...] = wds[...].astype(BF16)
            nxt = ne_ref[c]

            @pl.when(nxt >= 0)
            def _():
                for cp in weight_copies(nxt):
                    cp.start()

        n_rows = pl.multiple_of(rows_used(c), grp)
        pltpu.make_async_copy(hn_hbm.at[pl.ds(0, n_rows)], xbuf.at[cs, pl.ds(0, n_rows)],
                              gsem.at[cs]).wait()
        x = xbuf[cs].astype(BF16)
        g = jnp.dot(x, wgb[...], preferred_element_type=F32)
        u = jnp.dot(x, wub[...], preferred_element_type=F32)
        hmid = (_silu(g) * u).astype(BF16)
        y_ref[...] = jnp.dot(hmid, wdb[...], preferred_element_type=F32)

    @pl.when(t > nt)
    def _():
        y_ref[...] = jnp.zeros_like(y_ref)


def _moe(hn, w_gate, w_up, w_down, plan):
    n_tiles = plan["tile_expert"].shape[0]
    p_rows = n_tiles * MOE_TR
    prev = lambda t: jnp.maximum(t - 1, 0)
    any_spec = pl.BlockSpec(memory_space=pl.ANY)
    grid_spec = pltpu.PrefetchScalarGridSpec(
        num_scalar_prefetch=5,
        grid=(n_tiles + 1,),
        in_specs=[any_spec] * 5,
        out_specs=pl.BlockSpec((MOE_TR, D_MODEL), lambda t, *_: (prev(t), 0)),
        scratch_shapes=[
            pltpu.VMEM((2, MOE_TR, D_MODEL), F32),
            pltpu.SemaphoreType.DMA((2,)),
            pltpu.SMEM((2, IDS_WINDOW_ROWS, LANES), jnp.int32),
            pltpu.SemaphoreType.DMA((2,)),
            pltpu.VMEM((D_MODEL, D_EXPERT), F32),
            pltpu.VMEM((D_MODEL, D_EXPERT), F32),
            pltpu.VMEM((D_EXPERT, D_MODEL), F32),
            pltpu.SemaphoreType.DMA((3,)),
            pltpu.VMEM((D_MODEL, D_EXPERT), BF16),
            pltpu.VMEM((D_MODEL, D_EXPERT), BF16),
            pltpu.VMEM((D_EXPERT, D_MODEL), BF16),
        ],
    )
    return pl.pallas_call(
        _moe_kernel,
        grid_spec=grid_spec,
        out_shape=jax.ShapeDtypeStruct((p_rows, D_MODEL), F32),
        compiler_params=pltpu.CompilerParams(
            dimension_semantics=("arbitrary",), vmem_limit_bytes=VMEM_LIMIT),
        name="moe",
    )(plan["tile_expert"], plan["tile_base"], plan["tile_rows"], plan["next_expert"], plan["num_tiles"],
      plan["sorted_tokens"], hn, w_gate, w_up, w_down)


def _combine_kernel(pos_hbm, y_hbm, h_ref, route_ref, g_ref, o_ref, ybuf, gsem, ids_smem, isem):
    tm = h_ref.shape[0]
    t = pl.program_id(0)
    ntile = pl.num_programs(0) - 1

    def ids_copy(tile, s):
        return pltpu.make_async_copy(pos_hbm.at[tile], ids_smem.at[s], isem.at[s])

    @pl.when(t == 0)
    def _():
        ids_copy(0, 0).start()

    @pl.when(t < ntile)
    def _():
        s = lax.rem(t, 2)
        ids_copy(t, s).wait()

        @pl.when(t + 1 < ntile)
        def _():
            ids_copy(t + 1, 1 - s).start()

        for kk in range(2):
            for r in range(tm):
                flat = kk * tm + r
                row = ids_smem[s, flat // LANES, flat % LANES]
                pltpu.make_async_copy(y_hbm.at[pl.ds(row, 1)], ybuf.at[s, kk, pl.ds(r, 1)],
                                      gsem.at[s, kk]).start(priority=r % 2)

    @pl.when(t >= 1)
    def _():
        s = lax.rem(t - 1, 2)
        for kk in range(2):
            pltpu.make_async_copy(y_hbm.at[pl.ds(0, tm)], ybuf.at[s, kk], gsem.at[s, kk]).wait()
        route = route_ref[...]
        h = h_ref[...] + route[:, 2:3] * ybuf[s, 0] + route[:, 3:4] * ybuf[s, 1]
        ms = jnp.mean(h * h, axis=-1, keepdims=True)
        o_ref[...] = h * lax.rsqrt(ms + RMS_EPS) * g_ref[...]


def _combine(y, h, route, g, pos3):
    m = h.shape[0]
    tm = min(CMB_TM, m)
    ntile = m // tm
    assert pos3.shape == (ntile, 2 * tm // LANES, LANES)
    row = lambda i: (jnp.maximum(i - 1, 0), 0)
    return pl.pallas_call(
        _combine_kernel,
        grid=(ntile + 1,),
        in_specs=[
            pl.BlockSpec(memory_space=pl.ANY),
            pl.BlockSpec(memory_space=pl.ANY),
            pl.BlockSpec((tm, D_MODEL), row),
            pl.BlockSpec((tm, ROUTE_W), row),
            pl.BlockSpec((1, D_MODEL), lambda i: (0, 0)),
        ],
        out_specs=pl.BlockSpec((tm, D_MODEL), row),
        out_shape=jax.ShapeDtypeStruct((m, D_MODEL), F32),
        scratch_shapes=[
            pltpu.VMEM((2, 2, tm, D_MODEL), F32),
            pltpu.SemaphoreType.DMA((2, 2)),
            pltpu.SMEM((2, 2 * tm // LANES, LANES), jnp.int32),
            pltpu.SemaphoreType.DMA((2,)),
        ],
        compiler_params=pltpu.CompilerParams(
            dimension_semantics=("arbitrary",), vmem_limit_bytes=VMEM_LIMIT),
        name="combine",
    )(pos3, y, h, route, g)


def _dispatch_plan(route_t, tile_counts, m):
    tr = MOE_TR
    p_rows = 2 * m + N_EXPERTS * tr
    n_tiles = p_rows // tr
    n_tok_tiles, _, tm = route_t.shape
    cnt = tile_counts[:, 0, N_GROUPS:N_GROUPS + N_EXPERTS].astype(jnp.int32)
    before_tile = jnp.cumsum(cnt, axis=0) - cnt
    counts = jnp.sum(cnt, axis=0)
    padded = ((counts + tr - 1) // tr) * tr
    ends = jnp.cumsum(padded)
    starts = ends - padded
    base = (starts[None, :] + before_tile).astype(jnp.int32)
    fields = route_t.astype(jnp.int32)
    experts = (fields[:, 0, :], fields[:, 1, :])
    ranks = (fields[:, 4, :], fields[:, 5, :])
    expert_ids = jnp.arange(N_EXPERTS, dtype=jnp.int32)
    rows = [rk + jnp.sum(jnp.where(ex[..., None] == expert_ids, base[:, None, :], 0), axis=-1)
            for ex, rk in zip(experts, ranks)]
    pos = jnp.concatenate(rows, axis=1).reshape(n_tok_tiles, 2 * tm // LANES, LANES)
    num_tiles = (ends[-1] // tr).astype(jnp.int32)
    tile_idx = jnp.minimum(jnp.arange(n_tiles, dtype=jnp.int32), num_tiles - 1)
    tile_expert = jnp.sum((tile_idx[:, None] * tr >= ends[None, :]).astype(jnp.int32), axis=1)
    n_slots = 2 * m
    assert n_slots & (n_slots - 1) == 0
    tok2 = 2 * jnp.arange(m, dtype=jnp.int32).reshape(n_tok_tiles, tm)
    keys = [ex * n_slots + tok2 + k for k, ex in enumerate(experts)]
    sorted_key = jnp.sort(jnp.concatenate(keys, axis=0).reshape(-1))
    sorted_tokens = (sorted_key & (n_slots - 1)) >> 1
    sorted_tokens = jnp.pad(sorted_tokens, (0, IDS_WINDOW_ROWS * LANES)).reshape(-1, LANES)
    dense_starts = jnp.cumsum(counts) - counts
    in_expert = tile_idx - (starts // tr)[tile_expert]
    tile_base = dense_starts[tile_expert] + in_expert * tr
    tile_rows = jnp.clip(counts[tile_expert] - in_expert * tr, 0, tr)
    tile_rows = jnp.where(jnp.arange(n_tiles) < num_tiles, tile_rows, 0)
    ids = jnp.arange(N_EXPERTS, dtype=jnp.int32)
    present = jnp.where(counts > 0, ids, N_EXPERTS)
    next_ge = lax.cummin(present, axis=0, reverse=True)
    next_gt = jnp.concatenate([next_ge[1:], jnp.full((1,), N_EXPERTS, jnp.int32)])
    next_gt = jnp.where(next_gt >= N_EXPERTS, -1, next_gt)
    plan = dict(tile_expert=tile_expert.astype(jnp.int32), tile_base=tile_base.astype(jnp.int32),
                tile_rows=tile_rows.astype(jnp.int32), next_expert=next_gt[tile_expert].astype(jnp.int32),
                num_tiles=num_tiles.reshape(1), sorted_tokens=sorted_tokens)
    return plan, pos


def kernel(x, norm_mix_g, w_in, w_gk_up, b_gk, gla_norm_g, swa_sinks, w_out, norm_ffn_g,
           w_group, b_group, w_router, b_router, w_gate, w_up, w_down, norm_final_g):
    batch, seq, d = x.shape
    m = batch * seq
    assert w_in.shape[0] == 1, "single-layer block"
    x2 = x.reshape(m, d)
    w_main, w_lr = _w_prep(jnp.transpose(w_in[0]))
    wup = jnp.pad(w_gk_up[0], ((0, LANES - GLA_RANK), (0, 0))).astype(BF16)
    w_rt = jnp.pad(jnp.concatenate([w_group[0], w_router[0]], axis=1),
                   ((0, 0), (0, LANES - N_GROUPS - N_EXPERTS))).astype(BF16)
    b_rt = jnp.pad(jnp.concatenate([b_group[0], b_router[0]]),
                   (0, LANES - N_GROUPS - N_EXPERTS)).reshape(1, LANES)

    proj, glr = _in_proj(x2, norm_mix_g[0].reshape(1, d), w_main, w_lr)
    o_gla = _gla(proj, glr, wup, b_gk[0].reshape(1, GLA_QK_W),
                 gla_norm_g[0].reshape(1, GLA_V_W), batch, seq)
    o_swa = _swa(proj, swa_sinks[0], batch, seq)
    h_mid, hn, route, route_t, tile_counts = _out_route(o_gla, o_swa, x2, w_out[0].astype(BF16),
                                                        norm_ffn_g[0].reshape(1, d), w_rt, b_rt)
    plan, pos3 = _dispatch_plan(route_t, tile_counts, m)
    y = _moe(hn, w_gate[0], w_up[0], w_down[0], plan)
    out = _combine(y, h_mid, route, norm_final_g.reshape(1, d), pos3)
    return out.reshape(batch, seq, d)
```

```python
import functools

import jax
import jax.numpy as jnp
import numpy as np
from jax import lax
from jax.experimental import pallas as pl
from jax.experimental.pallas import tpu as pltpu

F32 = jnp.float32
BF16 = jnp.bfloat16

D_MODEL = 2048
GLA_HEADS = 4
GLA_DK = 128
GLA_DV = 256
GLA_RANK = 16
GLA_GATE_NORM = 16.0
GLA_CHUNK = 64
SWA_Q_HEADS = 16
SWA_KV_HEADS = 4
SWA_GROUP = SWA_Q_HEADS // SWA_KV_HEADS
SWA_HEAD_DIM = 64
SWA_BLOCK = 128
N_GROUPS = 4
EXPERTS_PER_GROUP = 16
N_EXPERTS = N_GROUPS * EXPERTS_PER_GROUP
D_EXPERT = 256
RMS_EPS = 1e-6

GLA_QK_W = GLA_HEADS * GLA_DK
GLA_V_W = GLA_HEADS * GLA_DV
SWA_Q_W = SWA_Q_HEADS * SWA_HEAD_DIM
SWA_KV_W = SWA_KV_HEADS * SWA_HEAD_DIM
OFF_GQ = 0
OFF_GK = OFF_GQ + GLA_QK_W
OFF_GV = OFF_GK + GLA_QK_W
OFF_GR = OFF_GV + GLA_V_W
OFF_SQ = OFF_GR + GLA_V_W
OFF_SK = OFF_SQ + SWA_Q_W
OFF_SV = OFF_SK + SWA_KV_W
PROJ_W = OFF_SV + SWA_KV_W
LANES = 128

IN_TM = 1024
IN_TN = 2304
OUT_TM = 512
ROUTE_W = 8
MOE_TR = 256
CMB_TM = 256
VMEM_LIMIT = 56 * 1024 * 1024


def _silu(x):
    return x / (1.0 + jnp.exp(-x))


ALIBI_PARTS = 3


def _bf16_parts(value, n):
    parts, rest = [], np.float32(value)
    for _ in range(n):
        piece = np.float32(np.asarray(rest).astype(jnp.bfloat16))
        parts.append(float(piece))
        rest = np.float32(rest - piece)
    return parts


W_PREP_ROWS = 384
OFF_LR = OFF_GR + GLA_V_W


def _wprep_kernel(wt_hbm, wm_ref, wl_ref, buf, lrbuf, sem, lrsem):
    rows = W_PREP_ROWS
    i = pl.program_id(0)
    n = pl.num_programs(0)
    n_lo = OFF_LR // rows
    slot = lax.rem(i, 2)

    def block_copy(step, s):
        start = jnp.where(step < n_lo, step * rows, OFF_LR + GLA_RANK + (step - n_lo) * rows)
        return pltpu.make_async_copy(wt_hbm.at[pl.ds(pl.multiple_of(start, 8), rows)], buf.at[s], sem.at[s])

    @pl.when(i == 0)
    def _():
        block_copy(0, 0).start()
        lrbuf[...] = jnp.zeros_like(lrbuf)
        lr_copy = pltpu.make_async_copy(wt_hbm.at[pl.ds(OFF_LR, GLA_RANK)], lrbuf.at[pl.ds(0, GLA_RANK)],
                                        lrsem.at[0])
        lr_copy.start()
        lr_copy.wait()
        wl_ref[...] = lrbuf[...].T.astype(BF16)

    @pl.when(i + 1 < n)
    def _():
        block_copy(i + 1, 1 - slot).start()

    block_copy(i, slot).wait()
    wm_ref[...] = buf[slot].T.astype(BF16)


def _w_prep(w_t):
    n, k = w_t.shape
    assert OFF_LR % W_PREP_ROWS == 0 and (PROJ_W - OFF_LR) % W_PREP_ROWS == 0 and n == PROJ_W + GLA_RANK
    return pl.pallas_call(
        _wprep_kernel,
        grid=(PROJ_W // W_PREP_ROWS,),
        in_specs=[pl.BlockSpec(memory_space=pl.ANY)],
        out_specs=[pl.BlockSpec((k, W_PREP_ROWS), lambda i: (0, i)),
                   pl.BlockSpec((k, LANES), lambda i: (0, 0))],
        out_shape=[jax.ShapeDtypeStruct((k, PROJ_W), BF16), jax.ShapeDtypeStruct((k, LANES), BF16)],
        scratch_shapes=[
            pltpu.VMEM((2, W_PREP_ROWS, k), F32),
            pltpu.VMEM((LANES, k), F32),
            pltpu.SemaphoreType.DMA((2,)),
            pltpu.SemaphoreType.DMA((1,)),
        ],
        compiler_params=pltpu.CompilerParams(
            dimension_semantics=("arbitrary",), vmem_limit_bytes=VMEM_LIMIT),
        name="w_prep",
    )(w_t)


def _inproj_kernel(x_ref, g_ref, w_ref, wlr_ref, proj_ref, glr_ref, xn_ref):
    @pl.when(pl.program_id(1) == 0)
    def _():
        x = x_ref[...]
        ms = jnp.mean(x * x, axis=-1, keepdims=True)
        xn = (x * lax.rsqrt(ms + RMS_EPS) * g_ref[...]).astype(BF16)
        xn_ref[...] = xn
        glr_ref[...] = jnp.dot(xn, wlr_ref[...], preferred_element_type=F32)

    proj_ref[...] = jnp.dot(xn_ref[...], w_ref[...], preferred_element_type=F32).astype(BF16)


def _in_proj(x2, g, w_main, w_lr):
    m = x2.shape[0]
    tm = min(IN_TM, m)
    return pl.pallas_call(
        _inproj_kernel,
        grid=(m // tm, PROJ_W // IN_TN),
        in_specs=[
            pl.BlockSpec((tm, D_MODEL), lambda i, j: (i, 0)),
            pl.BlockSpec((1, D_MODEL), lambda i, j: (0, 0)),
            pl.BlockSpec((D_MODEL, IN_TN), lambda i, j: (0, j)),
            pl.BlockSpec((D_MODEL, LANES), lambda i, j: (0, 0)),
        ],
        out_specs=[
            pl.BlockSpec((tm, IN_TN), lambda i, j: (i, j)),
            pl.BlockSpec((tm, LANES), lambda i, j: (i, 0)),
        ],
        out_shape=[
            jax.ShapeDtypeStruct((m, PROJ_W), BF16),
            jax.ShapeDtypeStruct((m, LANES), F32),
        ],
        scratch_shapes=[pltpu.VMEM((tm, D_MODEL), BF16)],
        compiler_params=pltpu.CompilerParams(
            dimension_semantics=("arbitrary", "arbitrary"), vmem_limit_bytes=VMEM_LIMIT),
        name="in_proj",
    )(x2, g, w_main, w_lr)


def _gla_kernel(q_ref, k_ref, v_ref, r_ref, glr_ref, wup_ref, bgk_ref, ng_ref, o_ref,
                la_ref, oi_ref, qi_ref, ki_ref, qd_ref, kd_ref, a_ref, kv_ref, dec_ref, sp_ref):
    t = q_ref.shape[0]
    c = GLA_CHUNK
    nchunk = t // c
    z = jnp.dot(glr_ref[...].astype(BF16), wup_ref[...], preferred_element_type=F32) + bgk_ref[...]
    la_ref[...] = (jnp.minimum(z, 0.0) - jnp.log(1.0 + jnp.exp(-jnp.abs(z)))) * (1.0 / GLA_GATE_NORM)

    ii = lax.broadcasted_iota(jnp.int32, (c, c), 0)
    jj = lax.broadcasted_iota(jnp.int32, (c, c), 1)
    causal = jj <= ii
    tri = causal.astype(BF16)
    nt = (((1,), (1,)), ((), ()))
    tn = (((0,), (0,)), ((), ()))

    def chunk_rows(n):
        return pl.ds(pl.multiple_of(n * c, c), c)

    def decays(n, carry):
        rows = chunk_rows(n)
        la = la_ref[rows, :]
        hi = la.astype(BF16)
        r1 = la - hi.astype(F32)
        mid = r1.astype(BF16)
        lo = (r1 - mid.astype(F32)).astype(BF16)
        parts = jnp.dot(tri, jnp.concatenate([hi, mid, lo], axis=1), preferred_element_type=F32)
        bcum = parts[:, 0:GLA_DK] + parts[:, GLA_DK:2 * GLA_DK] + parts[:, 2 * GLA_DK:]
        b_mid = bcum[c // 2 - 1:c // 2, :]
        b_last = bcum[c - 1:c, :]
        q = q_ref[rows, :].astype(F32) * (GLA_DK ** -0.5)
        k = k_ref[rows, :].astype(F32)
        qi_ref[rows, :] = (q * jnp.exp(bcum - b_mid)).astype(BF16)
        ki_ref[rows, :] = (k * jnp.exp(b_mid - bcum)).astype(BF16)
        qd_ref[rows, :] = (q * jnp.exp(bcum)).astype(BF16)
        kd_ref[rows, :] = (k * jnp.exp(b_last - bcum)).astype(BF16)
        dec_ref[n] = jnp.broadcast_to(jnp.exp(b_last), dec_ref.shape[1:])
        return carry

    lax.fori_loop(0, nchunk, decays, 0, unroll=4)

    def scores(n, carry):
        rows = chunk_rows(n)
        a = lax.dot_general(qi_ref[rows, :], ki_ref[rows, :], nt, preferred_element_type=F32)
        a_ref[rows, :] = jnp.where(causal, a, 0.0).astype(BF16)
        return carry

    lax.fori_loop(0, nchunk, scores, 0, unroll=8)

    def intra(n, carry):
        rows = chunk_rows(n)
        v = v_ref[rows, :]
        oi_ref[rows, :] = jnp.dot(a_ref[rows, :], v, preferred_element_type=F32)
        kv_ref[n] = lax.dot_general(v, kd_ref[rows, :], tn, preferred_element_type=F32)
        return carry

    lax.fori_loop(0, nchunk, intra, 0, unroll=8)

    def scan(n, s_t):
        sp_ref[n] = s_t.astype(BF16)
        return s_t * dec_ref[n][0:1, :] + kv_ref[n]

    lax.fori_loop(0, nchunk, scan, jnp.zeros((GLA_DV, GLA_DK), F32))

    def inter(n, carry):
        rows = chunk_rows(n)
        o = oi_ref[rows, :] + lax.dot_general(qd_ref[rows, :], sp_ref[n], nt,
                                              preferred_element_type=F32)
        ms = jnp.mean(o * o, axis=-1, keepdims=True)
        o = o * lax.rsqrt(ms + RMS_EPS) * ng_ref[...]
        o = o * _silu(r_ref[rows, :].astype(F32))
        o_ref[rows, :] = o.astype(BF16)
        return carry

    lax.fori_loop(0, nchunk, inter, 0, unroll=8)


def _gla(proj, glr, wup, bgk, ng, batch, seq):
    m = proj.shape[0]
    return pl.pallas_call(
        _gla_kernel,
        grid=(batch, GLA_HEADS),
        in_specs=[
            pl.BlockSpec((seq, GLA_DK), lambda b, h: (b, OFF_GQ // GLA_DK + h)),
            pl.BlockSpec((seq, GLA_DK), lambda b, h: (b, OFF_GK // GLA_DK + h)),
            pl.BlockSpec((seq, GLA_DV), lambda b, h: (b, OFF_GV // GLA_DV + h)),
            pl.BlockSpec((seq, GLA_DV), lambda b, h: (b, OFF_GR // GLA_DV + h)),
            pl.BlockSpec((seq, LANES), lambda b, h: (b, 0)),
            pl.BlockSpec((LANES, GLA_DK), lambda b, h: (0, h)),
            pl.BlockSpec((1, GLA_DK), lambda b, h: (0, h)),
            pl.BlockSpec((1, GLA_DV), lambda b, h: (0, h)),
        ],
        out_specs=pl.BlockSpec((seq, GLA_DV), lambda b, h: (b, h)),
        out_shape=jax.ShapeDtypeStruct((m, GLA_V_W), BF16),
        scratch_shapes=[
            pltpu.VMEM((seq, GLA_DK), F32),
            pltpu.VMEM((seq, GLA_DV), F32),
            pltpu.VMEM((seq, GLA_DK), BF16),
            pltpu.VMEM((seq, GLA_DK), BF16),
            pltpu.VMEM((seq, GLA_DK), BF16),
            pltpu.VMEM((seq, GLA_DK), BF16),
            pltpu.VMEM((seq, GLA_CHUNK), BF16),
            pltpu.VMEM((seq // GLA_CHUNK, GLA_DV, GLA_DK), F32),
            pltpu.VMEM((seq // GLA_CHUNK, 8, GLA_DK), F32),
            pltpu.VMEM((seq // GLA_CHUNK, GLA_DV, GLA_DK), BF16),
        ],
        compiler_params=pltpu.CompilerParams(
            dimension_semantics=("arbitrary", "arbitrary"), vmem_limit_bytes=VMEM_LIMIT),
        name="gla",
    )(proj, proj, proj, proj, glr, wup, bgk, ng)


def _swa_kernel(sink_ref, q_ref, kp_ref, kc_ref, vp_ref, vc_ref, o_ref):
    blk = SWA_BLOCK
    half = SWA_HEAD_DIM
    n = pl.program_id(1)
    k_all = jnp.concatenate([kp_ref[...], kc_ref[...]], axis=0)
    v_all = jnp.concatenate([vp_ref[...], vc_ref[...]], axis=0)
    qi = lax.broadcasted_iota(jnp.int32, (blk, 2 * blk), 0)
    kj = lax.broadcasted_iota(jnp.int32, (blk, 2 * blk), 1)
    rel = qi + blk - kj
    valid = (rel >= 0) & (rel < blk) & ((kj >= blk) | (n > 0))
    sink_col = kj == 0
    lane_kv = lax.broadcasted_iota(jnp.int32, (2 * blk, LANES), 1)
    lo_kv = lane_kv < half
    lane_q = lax.broadcasted_iota(jnp.int32, (blk, LANES), 1)
    lo_q = lane_q < half
    nt = (((1,), (1,)), ((), ()))
    key_idx = lax.broadcasted_iota(jnp.int32, (2 * blk, LANES), 0).astype(F32)
    key_cols = jnp.where((lane_kv & (half - 1)) < ALIBI_PARTS, key_idx, 0.0)
    q_pos = lax.broadcasted_iota(jnp.int32, (blk, 1), 0).astype(F32) + float(blk)
    q_scale = jnp.asarray(SWA_HEAD_DIM ** -0.5, BF16)

    for p in range(SWA_KV_HEADS // 2):
        k_slab = k_all[:, p * LANES:(p + 1) * LANES].astype(F32)
        v_slab = v_all[:, p * LANES:(p + 1) * LANES].astype(F32)
        k_roll = pltpu.roll(k_slab, half, 1)
        v_roll = pltpu.roll(v_slab, half, 1)
        for hh in range(2):
            h = 2 * p + hh
            k_lo, k_hi = (k_slab, k_roll) if hh == 0 else (k_roll, k_slab)
            kd = (jnp.where(lo_kv, k_lo, key_cols).astype(BF16),
                  jnp.where(lo_kv, key_cols, k_hi).astype(BF16))
            v_lo, v_hi = (v_slab, v_roll) if hh == 0 else (v_roll, v_slab)
            vd = jnp.where(lo_kv, v_lo, v_hi)
            vd = jnp.where(key_idx == 0.0, 0.0, vd).astype(BF16)
            for gp in range(SWA_GROUP // 2):
                col = (h * (SWA_GROUP // 2) + gp) * LANES
                qs = q_ref[:, col:col + LANES] * q_scale
                outs = []
                for gg in range(2):
                    head = h * SWA_GROUP + 2 * gp + gg
                    parts = _bf16_parts(2.0 ** (-8.0 * (head + 1) / SWA_Q_HEADS), ALIBI_PARTS)
                    slope = sum(parts)
                    aug = jnp.zeros((blk, LANES), F32)
                    for idx, part in enumerate(parts):
                        aug = jnp.where(lane_q == (half if gg == 0 else 0) + idx, part, aug)
                    keep = lo_q if gg == 0 else jnp.logical_not(lo_q)
                    qm = jnp.where(keep, qs, aug.astype(BF16))
                    s = lax.dot_general(qm, kd[gg], nt, preferred_element_type=F32)
                    sink = sink_ref[head] + slope * q_pos
                    s = jnp.where(sink_col, sink, jnp.where(valid, s, -jnp.inf))
                    mx = jnp.max(s, axis=-1, keepdims=True)
                    pe = jnp.exp(s - mx)
                    den = jnp.sum(pe, axis=-1, keepdims=True)
                    o = jnp.dot(pe.astype(BF16), vd, preferred_element_type=F32)
                    outs.append(o / den)
                o_ref[:, col:col + LANES] = jnp.where(lo_q, outs[0], outs[1]).astype(BF16)


def _swa(proj, sinks, batch, seq):
    m = proj.shape[0]
    nb = seq // SWA_BLOCK
    qcol = OFF_SQ // SWA_Q_W
    kcol = OFF_SK // SWA_KV_W
    vcol = OFF_SV // SWA_KV_W
    cur = lambda c: (lambda b, n: (b * nb + n, c))
    prev = lambda c: (lambda b, n: (b * nb + jnp.maximum(n - 1, 0), c))
    return pl.pallas_call(
        _swa_kernel,
        grid=(batch, nb),
        in_specs=[
            pl.BlockSpec(memory_space=pltpu.SMEM),
            pl.BlockSpec((SWA_BLOCK, SWA_Q_W), cur(qcol)),
            pl.BlockSpec((SWA_BLOCK, SWA_KV_W), prev(kcol)),
            pl.BlockSpec((SWA_BLOCK, SWA_KV_W), cur(kcol)),
            pl.BlockSpec((SWA_BLOCK, SWA_KV_W), prev(vcol)),
            pl.BlockSpec((SWA_BLOCK, SWA_KV_W), cur(vcol)),
        ],
        out_specs=pl.BlockSpec((SWA_BLOCK, SWA_Q_W), lambda b, n: (b * nb + n, 0)),
        out_shape=jax.ShapeDtypeStruct((m, SWA_Q_W), BF16),
        compiler_params=pltpu.CompilerParams(
            dimension_semantics=("arbitrary", "arbitrary"), vmem_limit_bytes=VMEM_LIMIT),
        name="swa",
    )(sinks, proj, proj, proj, proj, proj)


def _out_route_kernel(og_ref, os_ref, x_ref, wo_ref, g_ref, wr_ref, br_ref,
                      h_ref, hn_ref, route_ref, route_t_ref, cnt_ref, hnb_ref):
    t = pl.program_id(0)
    slot = lax.rem(t, 2)

    @pl.when(t == 0)
    def _():
        hnb_ref[...] = jnp.zeros_like(hnb_ref)

    logits = jnp.dot(hnb_ref[1 - slot], wr_ref[...], preferred_element_type=F32) + br_ref[...]

    h = x_ref[...]
    h = h + jnp.dot(og_ref[...], wo_ref[0:GLA_V_W, :], preferred_element_type=F32)
    h = h + jnp.dot(os_ref[...], wo_ref[GLA_V_W:, :], preferred_element_type=F32)
    h_ref[...] = h
    ms = jnp.mean(h * h, axis=-1, keepdims=True)
    hn = h * lax.rsqrt(ms + RMS_EPS) * g_ref[...]
    hn_ref[...] = hn
    hnb_ref[slot] = hn.astype(BF16)

    lane = lax.broadcasted_iota(jnp.int32, logits.shape, 1)
    lanef = lane.astype(F32)
    big = float(LANES)
    ninf = -jnp.inf
    gl = jnp.where(lane < N_GROUPS, logits, ninf)
    gmax = jnp.max(gl, axis=-1, keepdims=True)
    g_p = 1.0 / jnp.sum(jnp.exp(gl - gmax), axis=-1, keepdims=True)
    g_idx = jnp.min(jnp.where(gl == gmax, lanef, big), axis=-1, keepdims=True)
    lo = N_GROUPS + EXPERTS_PER_GROUP * g_idx
    el = jnp.where((lanef >= lo) & (lanef < lo + EXPERTS_PER_GROUP), logits, ninf)
    m1 = jnp.max(el, axis=-1, keepdims=True)
    i1 = jnp.min(jnp.where(el == m1, lanef, big), axis=-1, keepdims=True)
    el2 = jnp.where(lanef == i1, ninf, el)
    m2 = jnp.max(el2, axis=-1, keepdims=True)
    i2 = jnp.min(jnp.where(el2 == m2, lanef, big), axis=-1, keepdims=True)
    d = jnp.exp(m2 - m1)
    c1 = g_p / (1.0 + d)
    c2 = g_p * d / (1.0 + d)
    tm = logits.shape[0]
    chosen = ((lanef == i1) | (lanef == i2)).astype(BF16)
    ri = lax.broadcasted_iota(jnp.int32, (tm, tm), 0)
    ci = lax.broadcasted_iota(jnp.int32, (tm, tm), 1)
    earlier = jnp.dot((ci < ri).astype(BF16), chosen, preferred_element_type=F32)
    r1 = jnp.sum(jnp.where(lanef == i1, earlier, 0.0), axis=-1, keepdims=True)
    r2 = jnp.sum(jnp.where(lanef == i2, earlier, 0.0), axis=-1, keepdims=True)
    cnt = jnp.sum(chosen.astype(F32), axis=0, keepdims=True)
    cnt_ref[...] = jnp.broadcast_to(cnt, cnt_ref.shape)

    fields = [i1 - N_GROUPS, i2 - N_GROUPS, c1, c2, r1, r2]
    route = jnp.zeros_like(logits)
    for idx, val in enumerate(fields):
        route = jnp.where(lane == idx, val, route)
    route_ref[...] = route[:, 0:ROUTE_W]
    route_t_ref[...] = route.T[0:ROUTE_W, :][None]


def _out_route(o_gla, o_swa, x2, w_out, g, w_rt, b_rt):
    m = x2.shape[0]
    tm = min(OUT_TM, m)
    ntile = m // tm
    row = lambda i: (jnp.minimum(i, ntile - 1), 0)
    routed = lambda i: (jnp.maximum(i - 1, 0), 0)
    fixed = lambda i: (0, 0)
    return pl.pallas_call(
        _out_route_kernel,
        grid=(ntile + 1,),
        in_specs=[
            pl.BlockSpec((tm, GLA_V_W), row),
            pl.BlockSpec((tm, SWA_Q_W), row),
            pl.BlockSpec((tm, D_MODEL), row),
            pl.BlockSpec((GLA_V_W + SWA_Q_W, D_MODEL), fixed),
            pl.BlockSpec((1, D_MODEL), fixed),
            pl.BlockSpec((D_MODEL, LANES), fixed),
            pl.BlockSpec((1, LANES), fixed),
        ],
        out_specs=[
            pl.BlockSpec((tm, D_MODEL), row),
            pl.BlockSpec((tm, D_MODEL), row),
            pl.BlockSpec((tm, ROUTE_W), routed),
            pl.BlockSpec((1, ROUTE_W, tm), lambda i: (jnp.maximum(i - 1, 0), 0, 0)),
            pl.BlockSpec((1, 8, LANES), lambda i: (jnp.maximum(i - 1, 0), 0, 0)),
        ],
        out_shape=[
            jax.ShapeDtypeStruct((m, D_MODEL), F32),
            jax.ShapeDtypeStruct((m, D_MODEL), F32),
            jax.ShapeDtypeStruct((m, ROUTE_W), F32),
            jax.ShapeDtypeStruct((m // tm, ROUTE_W, tm), F32),
            jax.ShapeDtypeStruct((m // tm, 8, LANES), F32),
        ],
        scratch_shapes=[pltpu.VMEM((2, tm, D_MODEL), BF16)],
        compiler_params=pltpu.CompilerParams(
            dimension_semantics=("arbitrary",), vmem_limit_bytes=VMEM_LIMIT),
        name="out_route",
    )(o_gla, o_swa, x2, w_out, g, w_rt, b_rt)


LANES_LOG2 = 7
MOE_ROW_GROUP_LOG2 = 3
MOE_ROW_GROUP = 1 << MOE_ROW_GROUP_LOG2
IDS_WINDOW_ROWS = 3


def _moe_kernel(te_ref, cb_ref, nv_ref, ne_ref, nt_ref, tok_hbm, hn_hbm, wg_hbm, wu_hbm, wd_hbm, y_ref,
                xbuf, gsem, ids_smem, isem, wgs, wus, wds, wsem, wgb, wub, wdb):
    tr = MOE_TR
    grp = MOE_ROW_GROUP
    t = pl.program_id(0)
    nt = nt_ref[0]
    gs = lax.rem(t, 2)
    cs = 1 - gs

    def ids_copy(tile, s):
        row0 = lax.shift_right_logical(cb_ref[tile], LANES_LOG2)
        return pltpu.make_async_copy(tok_hbm.at[pl.ds(row0, IDS_WINDOW_ROWS)], ids_smem.at[s], isem.at[s])

    def weight_copies(e):
        return (pltpu.make_async_copy(wg_hbm.at[e], wgs, wsem.at[0]),
                pltpu.make_async_copy(wu_hbm.at[e], wus, wsem.at[1]),
                pltpu.make_async_copy(wd_hbm.at[e], wds, wsem.at[2]))

    def rows_used(tile):
        groups = lax.shift_right_logical(nv_ref[tile] + (grp - 1), MOE_ROW_GROUP_LOG2)
        return lax.shift_left(groups, MOE_ROW_GROUP_LOG2)

    @pl.when(t == 0)
    def _():
        xbuf[...] = jnp.zeros_like(xbuf)
        ids_copy(0, 0).start()
        for cp in weight_copies(te_ref[0]):
            cp.start()

    @pl.when(t < nt)
    def _():
        ids_copy(t, gs).wait()

        @pl.when(t + 1 < nt)
        def _():
            ids_copy(t + 1, cs).start()

        off = cb_ref[t] & (LANES - 1)
        n_real = nv_ref[t]
        for g0 in range(0, tr, grp):
            @pl.when(g0 < n_real)
            def _():
                for r in range(g0, g0 + grp):
                    q = off + r
                    tok = ids_smem[gs, lax.shift_right_logical(q, LANES_LOG2), q & (LANES - 1)]
                    pltpu.make_async_copy(hn_hbm.at[pl.ds(tok, 1)], xbuf.at[gs, pl.ds(r, 1)],
                                          gsem.at[gs]).start(priority=r % 2)

    @pl.when((t >= 1) & (t <= nt))
    def _():
        c = t - 1
        changed = (c == 0) | (te_ref[c] != te_ref[jnp.maximum(c - 1, 0)])

        @pl.when(changed)
        def _():
            for cp in weight_copies(te_ref[c]):
                cp.wait()
            wgb[...] = wgs[...].astype(BF16)
            wub[...] = wus[...].astype(BF16)
            wdb[...] = wds[...].astype(BF16)
            nxt = ne_ref[c]

            @pl.when(nxt >= 0)
            def _():
                for cp in weight_copies(nxt):
                    cp.start()

        n_rows = pl.multiple_of(rows_used(c), grp)
        pltpu.make_async_copy(hn_hbm.at[pl.ds(0, n_rows)], xbuf.at[cs, pl.ds(0, n_rows)],
                              gsem.at[cs]).wait()
        x = xbuf[cs].astype(BF16)
        g = jnp.dot(x, wgb[...], preferred_element_type=F32)
        u = jnp.dot(x, wub[...], preferred_element_type=F32)
        hmid = (_silu(g) * u).astype(BF16)
        y_ref[...] = jnp.dot(hmid, wdb[...], preferred_element_type=F32)

    @pl.when(t > nt)
    def _():
        y_ref[...] = jnp.zeros_like(y_ref)


def _moe(hn, w_gate, w_up, w_down, plan):
    n_tiles = plan["tile_expert"].shape[0]
    p_rows = n_tiles * MOE_TR
    prev = lambda t: jnp.maximum(t - 1, 0)
    any_spec = pl.BlockSpec(memory_space=pl.ANY)
    grid_spec = pltpu.PrefetchScalarGridSpec(
        num_scalar_prefetch=5,
        grid=(n_tiles + 1,),
        in_specs=[any_spec] * 5,
        out_specs=pl.BlockSpec((MOE_TR, D_MODEL), lambda t, *_: (prev(t), 0)),
        scratch_shapes=[
            pltpu.VMEM((2, MOE_TR, D_MODEL), F32),
            pltpu.SemaphoreType.DMA((2,)),
            pltpu.SMEM((2, IDS_WINDOW_ROWS, LANES), jnp.int32),
            pltpu.SemaphoreType.DMA((2,)),
            pltpu.VMEM((D_MODEL, D_EXPERT), F32),
            pltpu.VMEM((D_MODEL, D_EXPERT), F32),
            pltpu.VMEM((D_EXPERT, D_MODEL), F32),
            pltpu.SemaphoreType.DMA((3,)),
            pltpu.VMEM((D_MODEL, D_EXPERT), BF16),
            pltpu.VMEM((D_MODEL, D_EXPERT), BF16),
            pltpu.VMEM((D_EXPERT, D_MODEL), BF16),
        ],
    )
    return pl.pallas_call(
        _moe_kernel,
        grid_spec=grid_spec,
        out_shape=jax.ShapeDtypeStruct((p_rows, D_MODEL), F32),
        compiler_params=pltpu.CompilerParams(
            dimension_semantics=("arbitrary",), vmem_limit_bytes=VMEM_LIMIT),
        name="moe",
    )(plan["tile_expert"], plan["tile_base"], plan["tile_rows"], plan["next_expert"], plan["num_tiles"],
      plan["sorted_tokens"], hn, w_gate, w_up, w_down)


def _combine_kernel(pos_hbm, y_hbm, h_ref, route_ref, g_ref, o_ref, ybuf, gsem, ids_smem, isem):
    tm = h_ref.shape[0]
    t = pl.program_id(0)
    ntile = pl.num_programs(0) - 1

    def ids_copy(tile, s):
        return pltpu.make_async_copy(pos_hbm.at[tile], ids_smem.at[s], isem.at[s])

    @pl.when(t == 0)
    def _():
        ids_copy(0, 0).start()

    @pl.when(t < ntile)
    def _():
        s = lax.rem(t, 2)
        ids_copy(t, s).wait()

        @pl.when(t + 1 < ntile)
        def _():
            ids_copy(t + 1, 1 - s).start()

        for kk in range(2):
            for r in range(tm):
                flat = kk * tm + r
                row = ids_smem[s, flat // LANES, flat % LANES]
                pltpu.make_async_copy(y_hbm.at[pl.ds(row, 1)], ybuf.at[s, kk, pl.ds(r, 1)],
                                      gsem.at[s, kk]).start(priority=r % 2)

    @pl.when(t >= 1)
    def _():
        s = lax.rem(t - 1, 2)
        for kk in range(2):
            pltpu.make_async_copy(y_hbm.at[pl.ds(0, tm)], ybuf.at[s, kk], gsem.at[s, kk]).wait()
        route = route_ref[...]
        h = h_ref[...] + route[:, 2:3] * ybuf[s, 0] + route[:, 3:4] * ybuf[s, 1]
        ms = jnp.mean(h * h, axis=-1, keepdims=True)
        o_ref[...] = h * lax.rsqrt(ms + RMS_EPS) * g_ref[...]


def _combine(y, h, route, g, pos3):
    m = h.shape[0]
    tm = min(CMB_TM, m)
    ntile = m // tm
    assert pos3.shape == (ntile, 2 * tm // LANES, LANES)
    row = lambda i: (jnp.maximum(i - 1, 0), 0)
    return pl.pallas_call(
        _combine_kernel,
        grid=(ntile + 1,),
        in_specs=[
            pl.BlockSpec(memory_space=pl.ANY),
            pl.BlockSpec(memory_space=pl.ANY),
            pl.BlockSpec((tm, D_MODEL), row),
            pl.BlockSpec((tm, ROUTE_W), row),
            pl.BlockSpec((1, D_MODEL), lambda i: (0, 0)),
        ],
        out_specs=pl.BlockSpec((tm, D_MODEL), row),
        out_shape=jax.ShapeDtypeStruct((m, D_MODEL), F32),
        scratch_shapes=[
            pltpu.VMEM((2, 2, tm, D_MODEL), F32),
            pltpu.SemaphoreType.DMA((2, 2)),
            pltpu.SMEM((2, 2 * tm // LANES, LANES), jnp.int32),
            pltpu.SemaphoreType.DMA((2,)),
        ],
        compiler_params=pltpu.CompilerParams(
            dimension_semantics=("arbitrary",), vmem_limit_bytes=VMEM_LIMIT),
        name="combine",
    )(pos3, y, h, route, g)


def _dispatch_plan(route_t, tile_counts, m):
    tr = MOE_TR
    p_rows = 2 * m + N_EXPERTS * tr
    n_tiles = p_rows // tr
    n_tok_tiles, _, tm = route_t.shape
    cnt = tile_counts[:, 0, N_GROUPS:N_GROUPS + N_EXPERTS].astype(jnp.int32)
    before_tile = jnp.cumsum(cnt, axis=0) - cnt
    counts = jnp.sum(cnt, axis=0)
    padded = ((counts + tr - 1) // tr) * tr
    ends = jnp.cumsum(padded)
    starts = ends - padded
    base = (starts[None, :] + before_tile).astype(jnp.int32)
    fields = route_t.astype(jnp.int32)
    experts = (fields[:, 0, :], fields[:, 1, :])
    ranks = (fields[:, 4, :], fields[:, 5, :])
    expert_ids = jnp.arange(N_EXPERTS, dtype=jnp.int32)
    rows = [rk + jnp.sum(jnp.where(ex[..., None] == expert_ids, base[:, None, :], 0), axis=-1)
            for ex, rk in zip(experts, ranks)]
    ctm = min(CMB_TM, m)
    pos = jnp.concatenate([r.reshape(m // ctm, ctm) for r in rows], axis=1)
    pos = pos.reshape(m // ctm, 2 * ctm // LANES, LANES)
    num_tiles = (ends[-1] // tr).astype(jnp.int32)
    tile_idx = jnp.minimum(jnp.arange(n_tiles, dtype=jnp.int32), num_tiles - 1)
    tile_expert = jnp.sum((tile_idx[:, None] * tr >= ends[None, :]).astype(jnp.int32), axis=1)
    n_slots = 2 * m
    assert n_slots & (n_slots - 1) == 0
    tok2 = 2 * jnp.arange(m, dtype=jnp.int32).reshape(n_tok_tiles, tm)
    keys = [ex * n_slots + tok2 + k for k, ex in enumerate(experts)]
    sorted_key = jnp.sort(jnp.concatenate(keys, axis=0).reshape(-1))
    sorted_tokens = (sorted_key & (n_slots - 1)) >> 1
    sorted_tokens = jnp.pad(sorted_tokens, (0, IDS_WINDOW_ROWS * LANES)).reshape(-1, LANES)
    dense_starts = jnp.cumsum(counts) - counts
    in_expert = tile_idx - (starts // tr)[tile_expert]
    tile_base = dense_starts[tile_expert] + in_expert * tr
    tile_rows = jnp.clip(counts[tile_expert] - in_expert * tr, 0, tr)
    tile_rows = jnp.where(jnp.arange(n_tiles) < num_tiles, tile_rows, 0)
    ids = jnp.arange(N_EXPERTS, dtype=jnp.int32)
    present = jnp.where(counts > 0, ids, N_EXPERTS)
    next_ge = lax.cummin(present, axis=0, reverse=True)
    next_gt = jnp.concatenate([next_ge[1:], jnp.full((1,), N_EXPERTS, jnp.int32)])
    next_gt = jnp.where(next_gt >= N_EXPERTS, -1, next_gt)
    plan = dict(tile_expert=tile_expert.astype(jnp.int32), tile_base=tile_base.astype(jnp.int32),
                tile_rows=tile_rows.astype(jnp.int32), next_expert=next_gt[tile_expert].astype(jnp.int32),
                num_tiles=num_tiles.reshape(1), sorted_tokens=sorted_tokens)
    return plan, pos


def kernel(x, norm_mix_g, w_in, w_gk_up, b_gk, gla_norm_g, swa_sinks, w_out, norm_ffn_g,
           w_group, b_group, w_router, b_router, w_gate, w_up, w_down, norm_final_g):
    batch, seq, d = x.shape
    m = batch * seq
    assert w_in.shape[0] == 1, "single-layer block"
    x2 = x.reshape(m, d)
    w_main, w_lr = _w_prep(jnp.transpose(w_in[0]))
    wup = jnp.pad(w_gk_up[0], ((0, LANES - GLA_RANK), (0, 0))).astype(BF16)
    w_rt = jnp.pad(jnp.concatenate([w_group[0], w_router[0]], axis=1),
                   ((0, 0), (0, LANES - N_GROUPS - N_EXPERTS))).astype(BF16)
    b_rt = jnp.pad(jnp.concatenate([b_group[0], b_router[0]]),
                   (0, LANES - N_GROUPS - N_EXPERTS)).reshape(1, LANES)

    proj, glr = _in_proj(x2, norm_mix_g[0].reshape(1, d), w_main, w_lr)
    o_gla = _gla(proj, glr, wup, b_gk[0].reshape(1, GLA_QK_W),
                 gla_norm_g[0].reshape(1, GLA_V_W), batch, seq)
    o_swa = _swa(proj, swa_sinks[0], batch, seq)
    h_mid, hn, route, route_t, tile_counts = _out_route(o_gla, o_swa, x2, w_out[0].astype(BF16),
                                                        norm_ffn_g[0].reshape(1, d), w_rt, b_rt)
    plan, pos3 = _dispatch_plan(route_t, tile_counts, m)
    y = _moe(hn, w_gate[0], w_up[0], w_down[0], plan)
    out = _combine(y, h_mid, route, norm_final_g.reshape(1, d), pos3)
    return out.reshape(batch, seq, d)
```

```python
import functools

import jax
import jax.numpy as jnp
import numpy as np
from jax import lax
from jax.experimental import pallas as pl
from jax.experimental.pallas import tpu as pltpu

F32 = jnp.float32
BF16 = jnp.bfloat16

D_MODEL = 2048
GLA_HEADS = 4
GLA_DK = 128
GLA_DV = 256
GLA_RANK = 16
GLA_GATE_NORM = 16.0
GLA_CHUNK = 64
SWA_Q_HEADS = 16
SWA_KV_HEADS = 4
SWA_GROUP = SWA_Q_HEADS // SWA_KV_HEADS
SWA_HEAD_DIM = 64
SWA_BLOCK = 128
N_GROUPS = 4
EXPERTS_PER_GROUP = 16
N_EXPERTS = N_GROUPS * EXPERTS_PER_GROUP
D_EXPERT = 256
RMS_EPS = 1e-6

GLA_QK_W = GLA_HEADS * GLA_DK
GLA_V_W = GLA_HEADS * GLA_DV
SWA_Q_W = SWA_Q_HEADS * SWA_HEAD_DIM
SWA_KV_W = SWA_KV_HEADS * SWA_HEAD_DIM
OFF_GQ = 0
OFF_GK = OFF_GQ + GLA_QK_W
OFF_GV = OFF_GK + GLA_QK_W
OFF_GR = OFF_GV + GLA_V_W
OFF_SQ = OFF_GR + GLA_V_W
OFF_SK = OFF_SQ + SWA_Q_W
OFF_SV = OFF_SK + SWA_KV_W
PROJ_W = OFF_SV + SWA_KV_W
LANES = 128

IN_TM = 1024
IN_TN = 2304
OUT_TM = 512
ROUTE_W = 8
MOE_TR = 256
CMB_TM = 256
VMEM_LIMIT = 56 * 1024 * 1024


def _silu(x):
    return x / (1.0 + jnp.exp(-x))


ALIBI_PARTS = 3


def _bf16_parts(value, n):
    parts, rest = [], np.float32(value)
    for _ in range(n):
        piece = np.float32(np.asarray(rest).astype(jnp.bfloat16))
        parts.append(float(piece))
        rest = np.float32(rest - piece)
    return parts


W_PREP_ROWS = 384
OFF_LR = OFF_GR + GLA_V_W


def _wprep_kernel(wt_hbm, wm_ref, wl_ref, buf, lrbuf, sem, lrsem):
    rows = W_PREP_ROWS
    i = pl.program_id(0)
    n = pl.num_programs(0)
    n_lo = OFF_LR // rows
    slot = lax.rem(i, 2)

    def block_copy(step, s):
        start = jnp.where(step < n_lo, step * rows, OFF_LR + GLA_RANK + (step - n_lo) * rows)
        return pltpu.make_async_copy(wt_hbm.at[pl.ds(pl.multiple_of(start, 8), rows)], buf.at[s], sem.at[s])

    @pl.when(i == 0)
    def _():
        block_copy(0, 0).start()
        lrbuf[...] = jnp.zeros_like(lrbuf)
        lr_copy = pltpu.make_async_copy(wt_hbm.at[pl.ds(OFF_LR, GLA_RANK)], lrbuf.at[pl.ds(0, GLA_RANK)],
                                        lrsem.at[0])
        lr_copy.start()
        lr_copy.wait()
        wl_ref[...] = lrbuf[...].T.astype(BF16)

    @pl.when(i + 1 < n)
    def _():
        block_copy(i + 1, 1 - slot).start()

    block_copy(i, slot).wait()
    wm_ref[...] = buf[slot].T.astype(BF16)


def _w_prep(w_t):
    n, k = w_t.shape
    assert OFF_LR % W_PREP_ROWS == 0 and (PROJ_W - OFF_LR) % W_PREP_ROWS == 0 and n == PROJ_W + GLA_RANK
    return pl.pallas_call(
        _wprep_kernel,
        grid=(PROJ_W // W_PREP_ROWS,),
        in_specs=[pl.BlockSpec(memory_space=pl.ANY)],
        out_specs=[pl.BlockSpec((k, W_PREP_ROWS), lambda i: (0, i)),
                   pl.BlockSpec((k, LANES), lambda i: (0, 0))],
        out_shape=[jax.ShapeDtypeStruct((k, PROJ_W), BF16), jax.ShapeDtypeStruct((k, LANES), BF16)],
        scratch_shapes=[
            pltpu.VMEM((2, W_PREP_ROWS, k), F32),
            pltpu.VMEM((LANES, k), F32),
            pltpu.SemaphoreType.DMA((2,)),
            pltpu.SemaphoreType.DMA((1,)),
        ],
        compiler_params=pltpu.CompilerParams(
            dimension_semantics=("arbitrary",), vmem_limit_bytes=VMEM_LIMIT),
        name="w_prep",
    )(w_t)


def _inproj_kernel(x_ref, g_ref, w_ref, wlr_ref, proj_ref, glr_ref, xn_ref):
    @pl.when(pl.program_id(1) == 0)
    def _():
        x = x_ref[...]
        ms = jnp.mean(x * x, axis=-1, keepdims=True)
        xn = (x * lax.rsqrt(ms + RMS_EPS) * g_ref[...]).astype(BF16)
        xn_ref[...] = xn
        glr_ref[...] = jnp.dot(xn, wlr_ref[...], preferred_element_type=F32)

    proj_ref[...] = jnp.dot(xn_ref[...], w_ref[...], preferred_element_type=F32).astype(BF16)


def _in_proj(x2, g, w_main, w_lr):
    m = x2.shape[0]
    tm = min(IN_TM, m)
    return pl.pallas_call(
        _inproj_kernel,
        grid=(m // tm, PROJ_W // IN_TN),
        in_specs=[
            pl.BlockSpec((tm, D_MODEL), lambda i, j: (i, 0)),
            pl.BlockSpec((1, D_MODEL), lambda i, j: (0, 0)),
            pl.BlockSpec((D_MODEL, IN_TN), lambda i, j: (0, j)),
            pl.BlockSpec((D_MODEL, LANES), lambda i, j: (0, 0)),
        ],
        out_specs=[
            pl.BlockSpec((tm, IN_TN), lambda i, j: (i, j)),
            pl.BlockSpec((tm, LANES), lambda i, j: (i, 0)),
        ],
        out_shape=[
            jax.ShapeDtypeStruct((m, PROJ_W), BF16),
            jax.ShapeDtypeStruct((m, LANES), F32),
        ],
        scratch_shapes=[pltpu.VMEM((tm, D_MODEL), BF16)],
        compiler_params=pltpu.CompilerParams(
            dimension_semantics=("arbitrary", "arbitrary"), vmem_limit_bytes=VMEM_LIMIT),
        name="in_proj",
    )(x2, g, w_main, w_lr)


def _gla_kernel(q_ref, k_ref, v_ref, r_ref, glr_ref, wup_ref, bgk_ref, ng_ref, o_ref,
                la_ref, oi_ref, qi_ref, ki_ref, qd_ref, kd_ref, a_ref, kv_ref, dec_ref, sp_ref):
    t = q_ref.shape[0]
    c = GLA_CHUNK
    nchunk = t // c
    z = jnp.dot(glr_ref[...].astype(BF16), wup_ref[...], preferred_element_type=F32) + bgk_ref[...]
    la_ref[...] = (jnp.minimum(z, 0.0) - jnp.log(1.0 + jnp.exp(-jnp.abs(z)))) * (1.0 / GLA_GATE_NORM)

    ii = lax.broadcasted_iota(jnp.int32, (c, c), 0)
    jj = lax.broadcasted_iota(jnp.int32, (c, c), 1)
    causal = jj <= ii
    tri = causal.astype(BF16)
    nt = (((1,), (1,)), ((), ()))
    tn = (((0,), (0,)), ((), ()))

    def chunk_rows(n):
        return pl.ds(pl.multiple_of(n * c, c), c)

    def decays(n, carry):
        rows = chunk_rows(n)
        la = la_ref[rows, :]
        hi = la.astype(BF16)
        r1 = la - hi.astype(F32)
        mid = r1.astype(BF16)
        lo = (r1 - mid.astype(F32)).astype(BF16)
        parts = jnp.dot(tri, jnp.concatenate([hi, mid, lo], axis=1), preferred_element_type=F32)
        bcum = parts[:, 0:GLA_DK] + parts[:, GLA_DK:2 * GLA_DK] + parts[:, 2 * GLA_DK:]
        b_mid = bcum[c // 2 - 1:c // 2, :]
        b_last = bcum[c - 1:c, :]
        q = q_ref[rows, :].astype(F32) * (GLA_DK ** -0.5)
        k = k_ref[rows, :].astype(F32)
        qi_ref[rows, :] = (q * jnp.exp(bcum - b_mid)).astype(BF16)
        ki_ref[rows, :] = (k * jnp.exp(b_mid - bcum)).astype(BF16)
        qd_ref[rows, :] = (q * jnp.exp(bcum)).astype(BF16)
        kd_ref[rows, :] = (k * jnp.exp(b_last - bcum)).astype(BF16)
        dec_ref[n] = jnp.broadcast_to(jnp.exp(b_last), dec_ref.shape[1:])
        return carry

    lax.fori_loop(0, nchunk, decays, 0, unroll=4)

    def scores(n, carry):
        rows = chunk_rows(n)
        a = lax.dot_general(qi_ref[rows, :], ki_ref[rows, :], nt, preferred_element_type=F32)
        a_ref[rows, :] = jnp.where(causal, a, 0.0).astype(BF16)
        return carry

    lax.fori_loop(0, nchunk, scores, 0, unroll=8)

    def intra(n, carry):
        rows = chunk_rows(n)
        v = v_ref[rows, :]
        oi_ref[rows, :] = jnp.dot(a_ref[rows, :], v, preferred_element_type=F32)
        kv_ref[n] = lax.dot_general(v, kd_ref[rows, :], tn, preferred_element_type=F32)
        return carry

    lax.fori_loop(0, nchunk, intra, 0, unroll=8)

    def scan(n, s_t):
        sp_ref[n] = s_t.astype(BF16)
        return s_t * dec_ref[n][0:1, :] + kv_ref[n]

    lax.fori_loop(0, nchunk, scan, jnp.zeros((GLA_DV, GLA_DK), F32))

    def inter(n, carry):
        rows = chunk_rows(n)
        o = oi_ref[rows, :] + lax.dot_general(qd_ref[rows, :], sp_ref[n], nt,
                                              preferred_element_type=F32)
        ms = jnp.mean(o * o, axis=-1, keepdims=True)
        o = o * lax.rsqrt(ms + RMS_EPS) * ng_ref[...]
        o = o * _silu(r_ref[rows, :].astype(F32))
        o_ref[rows, :] = o.astype(BF16)
        return carry

    lax.fori_loop(0, nchunk, inter, 0, unroll=8)


def _gla(proj, glr, wup, bgk, ng, batch, seq):
    m = proj.shape[0]
    return pl.pallas_call(
        _gla_kernel,
        grid=(batch, GLA_HEADS),
        in_specs=[
            pl.BlockSpec((seq, GLA_DK), lambda b, h: (b, OFF_GQ // GLA_DK + h)),
            pl.BlockSpec((seq, GLA_DK), lambda b, h: (b, OFF_GK // GLA_DK + h)),
            pl.BlockSpec((seq, GLA_DV), lambda b, h: (b, OFF_GV // GLA_DV + h)),
            pl.BlockSpec((seq, GLA_DV), lambda b, h: (b, OFF_GR // GLA_DV + h)),
            pl.BlockSpec((seq, LANES), lambda b, h: (b, 0)),
            pl.BlockSpec((LANES, GLA_DK), lambda b, h: (0, h)),
            pl.BlockSpec((1, GLA_DK), lambda b, h: (0, h)),
            pl.BlockSpec((1, GLA_DV), lambda b, h: (0, h)),
        ],
        out_specs=pl.BlockSpec((seq, GLA_DV), lambda b, h: (b, h)),
        out_shape=jax.ShapeDtypeStruct((m, GLA_V_W), BF16),
        scratch_shapes=[
            pltpu.VMEM((seq, GLA_DK), F32),
            pltpu.VMEM((seq, GLA_DV), F32),
            pltpu.VMEM((seq, GLA_DK), BF16),
            pltpu.VMEM((seq, GLA_DK), BF16),
            pltpu.VMEM((seq, GLA_DK), BF16),
            pltpu.VMEM((seq, GLA_DK), BF16),
            pltpu.VMEM((seq, GLA_CHUNK), BF16),
            pltpu.VMEM((seq // GLA_CHUNK, GLA_DV, GLA_DK), F32),
            pltpu.VMEM((seq // GLA_CHUNK, 8, GLA_DK), F32),
            pltpu.VMEM((seq // GLA_CHUNK, GLA_DV, GLA_DK), BF16),
        ],
        compiler_params=pltpu.CompilerParams(
            dimension_semantics=("arbitrary", "arbitrary"), vmem_limit_bytes=VMEM_LIMIT),
        name="gla",
    )(proj, proj, proj, proj, glr, wup, bgk, ng)


def _swa_kernel(sink_ref, q_ref, kp_ref, kc_ref, vp_ref, vc_ref, o_ref):
    blk = SWA_BLOCK
    half = SWA_HEAD_DIM
    n = pl.program_id(1)
    k_all = jnp.concatenate([kp_ref[...], kc_ref[...]], axis=0)
    v_all = jnp.concatenate([vp_ref[...], vc_ref[...]], axis=0)
    qi = lax.broadcasted_iota(jnp.int32, (blk, 2 * blk), 0)
    kj = lax.broadcasted_iota(jnp.int32, (blk, 2 * blk), 1)
    rel = qi + blk - kj
    valid = (rel >= 0) & (rel < blk) & ((kj >= blk) | (n > 0))
    sink_col = kj == 0
    lane_kv = lax.broadcasted_iota(jnp.int32, (2 * blk, LANES), 1)
    lo_kv = lane_kv < half
    lane_q = lax.broadcasted_iota(jnp.int32, (blk, LANES), 1)
    lo_q = lane_q < half
    nt = (((1,), (1,)), ((), ()))
    key_idx = lax.broadcasted_iota(jnp.int32, (2 * blk, LANES), 0).astype(F32)
    key_cols = jnp.where((lane_kv & (half - 1)) < ALIBI_PARTS, key_idx, 0.0)
    q_pos = lax.broadcasted_iota(jnp.int32, (blk, 1), 0).astype(F32) + float(blk)
    q_scale = jnp.asarray(SWA_HEAD_DIM ** -0.5, BF16)

    for p in range(SWA_KV_HEADS // 2):
        k_slab = k_all[:, p * LANES:(p + 1) * LANES].astype(F32)
        v_slab = v_all[:, p * LANES:(p + 1) * LANES].astype(F32)
        k_roll = pltpu.roll(k_slab, half, 1)
        v_roll = pltpu.roll(v_slab, half, 1)
        for hh in range(2):
            h = 2 * p + hh
            k_lo, k_hi = (k_slab, k_roll) if hh == 0 else (k_roll, k_slab)
            kd = (jnp.where(lo_kv, k_lo, key_cols).astype(BF16),
                  jnp.where(lo_kv, key_cols, k_hi).astype(BF16))
            v_lo, v_hi = (v_slab, v_roll) if hh == 0 else (v_roll, v_slab)
            vd = jnp.where(lo_kv, v_lo, v_hi)
            vd = jnp.where(key_idx == 0.0, 0.0, vd).astype(BF16)
            for gp in range(SWA_GROUP // 2):
                col = (h * (SWA_GROUP // 2) + gp) * LANES
                qs = q_ref[:, col:col + LANES] * q_scale
                outs = []
                for gg in range(2):
                    head = h * SWA_GROUP + 2 * gp + gg
                    parts = _bf16_parts(2.0 ** (-8.0 * (head + 1) / SWA_Q_HEADS), ALIBI_PARTS)
                    slope = sum(parts)
                    aug = jnp.zeros((blk, LANES), F32)
                    for idx, part in enumerate(parts):
                        aug = jnp.where(lane_q == (half if gg == 0 else 0) + idx, part, aug)
                    keep = lo_q if gg == 0 else jnp.logical_not(lo_q)
                    qm = jnp.where(keep, qs, aug.astype(BF16))
                    s = lax.dot_general(qm, kd[gg], nt, preferred_element_type=F32)
                    sink = sink_ref[head] + slope * q_pos
                    s = jnp.where(sink_col, sink, jnp.where(valid, s, -jnp.inf))
                    mx = jnp.max(s, axis=-1, keepdims=True)
                    pe = jnp.exp(s - mx)
                    den = jnp.sum(pe, axis=-1, keepdims=True)
                    o = jnp.dot(pe.astype(BF16), vd, preferred_element_type=F32)
                    outs.append(o / den)
                o_ref[:, col:col + LANES] = jnp.where(lo_q, outs[0], outs[1]).astype(BF16)


def _swa(proj, sinks, batch, seq):
    m = proj.shape[0]
    nb = seq // SWA_BLOCK
    qcol = OFF_SQ // SWA_Q_W
    kcol = OFF_SK // SWA_KV_W
    vcol = OFF_SV // SWA_KV_W
    cur = lambda c: (lambda b, n: (b * nb + n, c))
    prev = lambda c: (lambda b, n: (b * nb + jnp.maximum(n - 1, 0), c))
    return pl.pallas_call(
        _swa_kernel,
        grid=(batch, nb),
        in_specs=[
            pl.BlockSpec(memory_space=pltpu.SMEM),
            pl.BlockSpec((SWA_BLOCK, SWA_Q_W), cur(qcol)),
            pl.BlockSpec((SWA_BLOCK, SWA_KV_W), prev(kcol)),
            pl.BlockSpec((SWA_BLOCK, SWA_KV_W), cur(kcol)),
            pl.BlockSpec((SWA_BLOCK, SWA_KV_W), prev(vcol)),
            pl.BlockSpec((SWA_BLOCK, SWA_KV_W), cur(vcol)),
        ],
        out_specs=pl.BlockSpec((SWA_BLOCK, SWA_Q_W), lambda b, n: (b * nb + n, 0)),
        out_shape=jax.ShapeDtypeStruct((m, SWA_Q_W), BF16),
        compiler_params=pltpu.CompilerParams(
            dimension_semantics=("arbitrary", "arbitrary"), vmem_limit_bytes=VMEM_LIMIT),
        name="swa",
    )(sinks, proj, proj, proj, proj, proj)


def _out_route_kernel(og_ref, os_ref, x_ref, wo_ref, g_ref, wr_ref, br_ref,
                      h_ref, hn_ref, route_ref, route_t_ref, cnt_ref, hnb_ref):
    t = pl.program_id(0)
    slot = lax.rem(t, 2)

    @pl.when(t == 0)
    def _():
        hnb_ref[...] = jnp.zeros_like(hnb_ref)

    logits = jnp.dot(hnb_ref[1 - slot], wr_ref[...], preferred_element_type=F32) + br_ref[...]

    h = x_ref[...]
    h = h + jnp.dot(og_ref[...], wo_ref[0:GLA_V_W, :], preferred_element_type=F32)
    h = h + jnp.dot(os_ref[...], wo_ref[GLA_V_W:, :], preferred_element_type=F32)
    h_ref[...] = h
    ms = jnp.mean(h * h, axis=-1, keepdims=True)
    hn = h * lax.rsqrt(ms + RMS_EPS) * g_ref[...]
    hn_ref[...] = hn
    hnb_ref[slot] = hn.astype(BF16)

    lane = lax.broadcasted_iota(jnp.int32, logits.shape, 1)
    lanef = lane.astype(F32)
    big = float(LANES)
    ninf = -jnp.inf
    gl = jnp.where(lane < N_GROUPS, logits, ninf)
    gmax = jnp.max(gl, axis=-1, keepdims=True)
    g_p = 1.0 / jnp.sum(jnp.exp(gl - gmax), axis=-1, keepdims=True)
    g_idx = jnp.min(jnp.where(gl == gmax, lanef, big), axis=-1, keepdims=True)
    lo = N_GROUPS + EXPERTS_PER_GROUP * g_idx
    el = jnp.where((lanef >= lo) & (lanef < lo + EXPERTS_PER_GROUP), logits, ninf)
    m1 = jnp.max(el, axis=-1, keepdims=True)
    i1 = jnp.min(jnp.where(el == m1, lanef, big), axis=-1, keepdims=True)
    el2 = jnp.where(lanef == i1, ninf, el)
    m2 = jnp.max(el2, axis=-1, keepdims=True)
    i2 = jnp.min(jnp.where(el2 == m2, lanef, big), axis=-1, keepdims=True)
    d = jnp.exp(m2 - m1)
    c1 = g_p / (1.0 + d)
    c2 = g_p * d / (1.0 + d)
    tm = logits.shape[0]
    chosen = ((lanef == i1) | (lanef == i2)).astype(BF16)
    ri = lax.broadcasted_iota(jnp.int32, (tm, tm), 0)
    ci = lax.broadcasted_iota(jnp.int32, (tm, tm), 1)
    earlier = jnp.dot((ci < ri).astype(BF16), chosen, preferred_element_type=F32)
    r1 = jnp.sum(jnp.where(lanef == i1, earlier, 0.0), axis=-1, keepdims=True)
    r2 = jnp.sum(jnp.where(lanef == i2, earlier, 0.0), axis=-1, keepdims=True)
    cnt = jnp.sum(chosen.astype(F32), axis=0, keepdims=True)
    cnt_ref[...] = jnp.broadcast_to(cnt, cnt_ref.shape)

    fields = [i1 - N_GROUPS, i2 - N_GROUPS, c1, c2, r1, r2]
    route = jnp.zeros_like(logits)
    for idx, val in enumerate(fields):
        route = jnp.where(lane == idx, val, route)
    route_ref[...] = route[:, 0:ROUTE_W]
    route_t_ref[...] = route.T[0:ROUTE_W, :][None]


def _out_route(o_gla, o_swa, x2, w_out, g, w_rt, b_rt):
    m = x2.shape[0]
    tm = min(OUT_TM, m)
    ntile = m // tm
    row = lambda i: (jnp.minimum(i, ntile - 1), 0)
    routed = lambda i: (jnp.maximum(i - 1, 0), 0)
    fixed = lambda i: (0, 0)
    return pl.pallas_call(
        _out_route_kernel,
        grid=(ntile + 1,),
        in_specs=[
            pl.BlockSpec((tm, GLA_V_W), row),
            pl.BlockSpec((tm, SWA_Q_W), row),
            pl.BlockSpec((tm, D_MODEL), row),
            pl.BlockSpec((GLA_V_W + SWA_Q_W, D_MODEL), fixed),
            pl.BlockSpec((1, D_MODEL), fixed),
            pl.BlockSpec((D_MODEL, LANES), fixed),
            pl.BlockSpec((1, LANES), fixed),
        ],
        out_specs=[
            pl.BlockSpec((tm, D_MODEL), row),
            pl.BlockSpec((tm, D_MODEL), row),
            pl.BlockSpec((tm, ROUTE_W), routed),
            pl.BlockSpec((1, ROUTE_W, tm), lambda i: (jnp.maximum(i - 1, 0), 0, 0)),
            pl.BlockSpec((1, 8, LANES), lambda i: (jnp.maximum(i - 1, 0), 0, 0)),
        ],
        out_shape=[
            jax.ShapeDtypeStruct((m, D_MODEL), F32),
            jax.ShapeDtypeStruct((m, D_MODEL), F32),
            jax.ShapeDtypeStruct((m, ROUTE_W), F32),
            jax.ShapeDtypeStruct((m // tm, ROUTE_W, tm), F32),
            jax.ShapeDtypeStruct((m // tm, 8, LANES), F32),
        ],
        scratch_shapes=[pltpu.VMEM((2, tm, D_MODEL), BF16)],
        compiler_params=pltpu.CompilerParams(
            dimension_semantics=("arbitrary",), vmem_limit_bytes=VMEM_LIMIT),
        name="out_route",
    )(o_gla, o_swa, x2, w_out, g, w_rt, b_rt)


LANES_LOG2 = 7
MOE_ROW_GROUP_LOG2 = 3
MOE_ROW_GROUP = 1 << MOE_ROW_GROUP_LOG2
IDS_WINDOW_ROWS = 3
MOE_TAIL_ROWS = N_EXPERTS * MOE_ROW_GROUP


def _moe_kernel(te_ref, cb_ref, nv_ref, ne_ref, yb_ref, meta_ref, tok_hbm, hn_hbm, wg_hbm, wu_hbm, wd_hbm,
                y_hbm, xbuf, gsem, ids_smem, isem, wgs, wus, wds, wsem, wgb, wub, wdb, ybuf, ysem):
    tr = MOE_TR
    grp = MOE_ROW_GROUP
    t = pl.program_id(0)
    nt = meta_ref[0]
    gs = lax.rem(t, 2)
    cs = 1 - gs

    def ids_copy(tile, s):
        row0 = lax.shift_right_logical(cb_ref[tile], LANES_LOG2)
        return pltpu.make_async_copy(tok_hbm.at[pl.ds(row0, IDS_WINDOW_ROWS)], ids_smem.at[s], isem.at[s])

    def weight_copies(e):
        return (pltpu.make_async_copy(wg_hbm.at[e], wgs, wsem.at[0]),
                pltpu.make_async_copy(wu_hbm.at[e], wus, wsem.at[1]),
                pltpu.make_async_copy(wd_hbm.at[e], wds, wsem.at[2]))

    def rows_used(tile):
        groups = lax.shift_right_logical(nv_ref[tile] + (grp - 1), MOE_ROW_GROUP_LOG2)
        return lax.shift_left(groups, MOE_ROW_GROUP_LOG2)

    def y_copy(tile, s):
        n_rows = pl.multiple_of(rows_used(tile), grp)
        dst = y_hbm.at[pl.ds(pl.multiple_of(yb_ref[tile], grp), n_rows)]
        return pltpu.make_async_copy(ybuf.at[s, pl.ds(0, n_rows)], dst, ysem.at[s])

    @pl.when(t == 0)
    def _():
        xbuf[...] = jnp.zeros_like(xbuf)
        ids_copy(0, 0).start()
        for cp in weight_copies(te_ref[0]):
            cp.start()

    @pl.when(t < nt)
    def _():
        ids_copy(t, gs).wait()

        @pl.when(t + 1 < nt)
        def _():
            ids_copy(t + 1, cs).start()

        off = cb_ref[t] & (LANES - 1)
        n_real = nv_ref[t]
        for g0 in range(0, tr, grp):
            @pl.when(g0 < n_real)
            def _():
                for r in range(g0, g0 + grp):
                    q = off + r
                    tok = ids_smem[gs, lax.shift_right_logical(q, LANES_LOG2), q & (LANES - 1)]
                    pltpu.make_async_copy(hn_hbm.at[pl.ds(tok, 1)], xbuf.at[gs, pl.ds(r, 1)],
                                          gsem.at[gs]).start(priority=r % 2)

    @pl.when((t >= 1) & (t <= nt))
    def _():
        c = t - 1
        changed = (c == 0) | (te_ref[c] != te_ref[jnp.maximum(c - 1, 0)])

        @pl.when(changed)
        def _():
            for cp in weight_copies(te_ref[c]):
                cp.wait()
            wgb[...] = wgs[...].astype(BF16)
            wub[...] = wus[...].astype(BF16)
            wdb[...] = wds[...].astype(BF16)
            nxt = ne_ref[c]

            @pl.when(nxt >= 0)
            def _():
                for cp in weight_copies(nxt):
                    cp.start()

        n_rows = pl.multiple_of(rows_used(c), grp)
        pltpu.make_async_copy(hn_hbm.at[pl.ds(0, n_rows)], xbuf.at[cs, pl.ds(0, n_rows)],
                              gsem.at[cs]).wait()
        x = xbuf[cs].astype(BF16)
        g = jnp.dot(x, wgb[...], preferred_element_type=F32)
        u = jnp.dot(x, wub[...], preferred_element_type=F32)
        hmid = (_silu(g) * u).astype(BF16)

        @pl.when(c >= 2)
        def _():
            y_copy(c - 2, cs).wait()

        ybuf[cs] = jnp.dot(hmid, wdb[...], preferred_element_type=F32)
        y_copy(c, cs).start()

        @pl.when(t == nt)
        def _():
            @pl.when(c >= 1)
            def _():
                y_copy(c - 1, gs).wait()

            y_copy(c, cs).wait()
            ybuf[gs] = jnp.zeros((tr, D_MODEL), F32)
            tail0, n_tail = meta_ref[1], meta_ref[2]
            for k in range(MOE_TAIL_ROWS // tr):
                n_k = pl.multiple_of(jnp.clip(n_tail - k * tr, 0, tr), grp)

                @pl.when(n_k > 0)
                def _():
                    dst = y_hbm.at[pl.ds(pl.multiple_of(tail0 + k * tr, grp), n_k)]
                    cp = pltpu.make_async_copy(ybuf.at[gs, pl.ds(0, n_k)], dst, ysem.at[gs])
                    cp.start()
                    cp.wait()


def _moe(hn, w_gate, w_up, w_down, plan):
    n_tiles = plan["tile_expert"].shape[0]
    y_rows = 2 * hn.shape[0] + MOE_TAIL_ROWS
    any_spec = pl.BlockSpec(memory_space=pl.ANY)
    grid_spec = pltpu.PrefetchScalarGridSpec(
        num_scalar_prefetch=6,
        grid=(n_tiles + 1,),
        in_specs=[any_spec] * 5,
        out_specs=any_spec,
        scratch_shapes=[
            pltpu.VMEM((2, MOE_TR, D_MODEL), F32),
            pltpu.SemaphoreType.DMA((2,)),
            pltpu.SMEM((2, IDS_WINDOW_ROWS, LANES), jnp.int32),
            pltpu.SemaphoreType.DMA((2,)),
            pltpu.VMEM((D_MODEL, D_EXPERT), F32),
            pltpu.VMEM((D_MODEL, D_EXPERT), F32),
            pltpu.VMEM((D_EXPERT, D_MODEL), F32),
            pltpu.SemaphoreType.DMA((3,)),
            pltpu.VMEM((D_MODEL, D_EXPERT), BF16),
            pltpu.VMEM((D_MODEL, D_EXPERT), BF16),
            pltpu.VMEM((D_EXPERT, D_MODEL), BF16),
            pltpu.VMEM((2, MOE_TR, D_MODEL), F32),
            pltpu.SemaphoreType.DMA((2,)),
        ],
    )
    return pl.pallas_call(
        _moe_kernel,
        grid_spec=grid_spec,
        out_shape=jax.ShapeDtypeStruct((y_rows, D_MODEL), F32),
        compiler_params=pltpu.CompilerParams(
            dimension_semantics=("arbitrary",), vmem_limit_bytes=VMEM_LIMIT),
        name="moe",
    )(plan["tile_expert"], plan["tile_base"], plan["tile_rows"], plan["next_expert"], plan["tile_out"],
      plan["meta"], plan["sorted_tokens"], hn, w_gate, w_up, w_down)


def _combine_kernel(pos_hbm, y_hbm, h_ref, route_ref, g_ref, o_ref, ybuf, gsem, ids_smem, isem):
    tm = h_ref.shape[0]
    t = pl.program_id(0)
    ntile = pl.num_programs(0) - 1

    def ids_copy(tile, s):
        return pltpu.make_async_copy(pos_hbm.at[tile], ids_smem.at[s], isem.at[s])

    @pl.when(t == 0)
    def _():
        ids_copy(0, 0).start()

    @pl.when(t < ntile)
    def _():
        s = lax.rem(t, 2)
        ids_copy(t, s).wait()

        @pl.when(t + 1 < ntile)
        def _():
            ids_copy(t + 1, 1 - s).start()

        for kk in range(2):
            for r in range(tm):
                flat = kk * tm + r
                row = ids_smem[s, flat // LANES, flat % LANES]
                pltpu.make_async_copy(y_hbm.at[pl.ds(row, 1)], ybuf.at[s, kk, pl.ds(r, 1)],
                                      gsem.at[s, kk]).start(priority=r % 2)

    @pl.when(t >= 1)
    def _():
        s = lax.rem(t - 1, 2)
        for kk in range(2):
            pltpu.make_async_copy(y_hbm.at[pl.ds(0, tm)], ybuf.at[s, kk], gsem.at[s, kk]).wait()
        route = route_ref[...]
        h = h_ref[...] + route[:, 2:3] * ybuf[s, 0] + route[:, 3:4] * ybuf[s, 1]
        ms = jnp.mean(h * h, axis=-1, keepdims=True)
        o_ref[...] = h * lax.rsqrt(ms + RMS_EPS) * g_ref[...]


def _combine(y, h, route, g, pos3):
    m = h.shape[0]
    tm = min(CMB_TM, m)
    ntile = m // tm
    assert pos3.shape == (ntile, 2 * tm // LANES, LANES)
    row = lambda i: (jnp.maximum(i - 1, 0), 0)
    return pl.pallas_call(
        _combine_kernel,
        grid=(ntile + 1,),
        in_specs=[
            pl.BlockSpec(memory_space=pl.ANY),
            pl.BlockSpec(memory_space=pl.ANY),
            pl.BlockSpec((tm, D_MODEL), row),
            pl.BlockSpec((tm, ROUTE_W), row),
            pl.BlockSpec((1, D_MODEL), lambda i: (0, 0)),
        ],
        out_specs=pl.BlockSpec((tm, D_MODEL), row),
        out_shape=jax.ShapeDtypeStruct((m, D_MODEL), F32),
        scratch_shapes=[
            pltpu.VMEM((2, 2, tm, D_MODEL), F32),
            pltpu.SemaphoreType.DMA((2, 2)),
            pltpu.SMEM((2, 2 * tm // LANES, LANES), jnp.int32),
            pltpu.SemaphoreType.DMA((2,)),
        ],
        compiler_params=pltpu.CompilerParams(
            dimension_semantics=("arbitrary",), vmem_limit_bytes=VMEM_LIMIT),
        name="combine",
    )(pos3, y, h, route, g)


def _dispatch_plan(route_t, tile_counts, m):
    tr = MOE_TR
    p_rows = 2 * m + N_EXPERTS * tr
    n_tiles = p_rows // tr
    n_tok_tiles, _, tm = route_t.shape
    cnt = tile_counts[:, 0, N_GROUPS:N_GROUPS + N_EXPERTS].astype(jnp.int32)
    before_tile = jnp.cumsum(cnt, axis=0) - cnt
    counts = jnp.sum(cnt, axis=0)
    padded = ((counts + tr - 1) // tr) * tr
    ends = jnp.cumsum(padded)
    starts = ends - padded
    grp = MOE_ROW_GROUP
    y_len = ((counts + grp - 1) // grp) * grp
    y_starts = jnp.cumsum(y_len) - y_len
    base = (y_starts[None, :] + before_tile).astype(jnp.int32)
    fields = route_t.astype(jnp.int32)
    experts = (fields[:, 0, :], fields[:, 1, :])
    ranks = (fields[:, 4, :], fields[:, 5, :])
    expert_ids = jnp.arange(N_EXPERTS, dtype=jnp.int32)
    rows = [rk + jnp.sum(jnp.where(ex[..., None] == expert_ids, base[:, None, :], 0), axis=-1)
            for ex, rk in zip(experts, ranks)]
    ctm = min(CMB_TM, m)
    pos = jnp.concatenate([r.reshape(m // ctm, ctm) for r in rows], axis=1)
    pos = pos.reshape(m // ctm, 2 * ctm // LANES, LANES)
    num_tiles = (ends[-1] // tr).astype(jnp.int32)
    tile_idx = jnp.minimum(jnp.arange(n_tiles, dtype=jnp.int32), num_tiles - 1)
    tile_expert = jnp.sum((tile_idx[:, None] * tr >= ends[None, :]).astype(jnp.int32), axis=1)
    n_slots = 2 * m
    assert n_slots & (n_slots - 1) == 0
    tok2 = 2 * jnp.arange(m, dtype=jnp.int32).reshape(n_tok_tiles, tm)
    keys = [ex * n_slots + tok2 + k for k, ex in enumerate(experts)]
    sorted_key = jnp.sort(jnp.concatenate(keys, axis=0).reshape(-1))
    sorted_tokens = (sorted_key & (n_slots - 1)) >> 1
    sorted_tokens = jnp.pad(sorted_tokens, (0, IDS_WINDOW_ROWS * LANES)).reshape(-1, LANES)
    dense_starts = jnp.cumsum(counts) - counts
    in_expert = tile_idx - (starts // tr)[tile_expert]
    tile_base = dense_starts[tile_expert] + in_expert * tr
    tile_rows = jnp.clip(counts[tile_expert] - in_expert * tr, 0, tr)
    tile_rows = jnp.where(jnp.arange(n_tiles) < num_tiles, tile_rows, 0)
    ids = jnp.arange(N_EXPERTS, dtype=jnp.int32)
    present = jnp.where(counts > 0, ids, N_EXPERTS)
    next_ge = lax.cummin(present, axis=0, reverse=True)
    next_gt = jnp.concatenate([next_ge[1:], jnp.full((1,), N_EXPERTS, jnp.int32)])
    next_gt = jnp.where(next_gt >= N_EXPERTS, -1, next_gt)
    tile_out = y_starts[tile_expert] + in_expert * tr
    y_used = jnp.sum(y_len)
    meta = jnp.stack([num_tiles, y_used, 2 * m + MOE_TAIL_ROWS - y_used]).astype(jnp.int32)
    plan = dict(tile_expert=tile_expert.astype(jnp.int32), tile_base=tile_base.astype(jnp.int32),
                tile_rows=tile_rows.astype(jnp.int32), next_expert=next_gt[tile_expert].astype(jnp.int32),
                tile_out=tile_out.astype(jnp.int32), meta=meta, sorted_tokens=sorted_tokens)
    return plan, pos


def kernel(x, norm_mix_g, w_in, w_gk_up, b_gk, gla_norm_g, swa_sinks, w_out, norm_ffn_g,
           w_group, b_group, w_router, b_router, w_gate, w_up, w_down, norm_final_g):
    batch, seq, d = x.shape
    m = batch * seq
    assert w_in.shape[0] == 1, "single-layer block"
    x2 = x.reshape(m, d)
    w_main, w_lr = _w_prep(jnp.transpose(w_in[0]))
    wup = jnp.pad(w_gk_up[0], ((0, LANES - GLA_RANK), (0, 0))).astype(BF16)
    w_rt = jnp.pad(jnp.concatenate([w_group[0], w_router[0]], axis=1),
                   ((0, 0), (0, LANES - N_GROUPS - N_EXPERTS))).astype(BF16)
    b_rt = jnp.pad(jnp.concatenate([b_group[0], b_router[0]]),
                   (0, LANES - N_GROUPS - N_EXPERTS)).reshape(1, LANES)

    proj, glr = _in_proj(x2, norm_mix_g[0].reshape(1, d), w_main, w_lr)
    o_gla = _gla(proj, glr, wup, b_gk[0].reshape(1, GLA_QK_W),
                 gla_norm_g[0].reshape(1, GLA_V_W), batch, seq)
    o_swa = _swa(proj, swa_sinks[0], batch, seq)
    h_mid, hn, route, route_t, tile_counts = _out_route(o_gla, o_swa, x2, w_out[0].astype(BF16),
                                                        norm_ffn_g[0].reshape(1, d), w_rt, b_rt)
    plan, pos3 = _dispatch_plan(route_t, tile_counts, m)
    y = _moe(hn, w_gate[0], w_up[0], w_down[0], plan)
    out = _combine(y, h_mid, route, norm_final_g.reshape(1, d), pos3)
    return out.reshape(batch, seq, d)
```

```python
import functools

import jax
import jax.numpy as jnp
import numpy as np
from jax import lax
from jax.experimental import pallas as pl
from jax.experimental.pallas import tpu as pltpu

F32 = jnp.float32
BF16 = jnp.bfloat16

D_MODEL = 2048
GLA_HEADS = 4
GLA_DK = 128
GLA_DV = 256
GLA_RANK = 16
GLA_GATE_NORM = 16.0
GLA_CHUNK = 64
SWA_Q_HEADS = 16
SWA_KV_HEADS = 4
SWA_GROUP = SWA_Q_HEADS // SWA_KV_HEADS
SWA_HEAD_DIM = 64
SWA_BLOCK = 128
N_GROUPS = 4
EXPERTS_PER_GROUP = 16
N_EXPERTS = N_GROUPS * EXPERTS_PER_GROUP
D_EXPERT = 256
RMS_EPS = 1e-6

GLA_QK_W = GLA_HEADS * GLA_DK
GLA_V_W = GLA_HEADS * GLA_DV
SWA_Q_W = SWA_Q_HEADS * SWA_HEAD_DIM
SWA_KV_W = SWA_KV_HEADS * SWA_HEAD_DIM
OFF_GQ = 0
OFF_GK = OFF_GQ + GLA_QK_W
OFF_GV = OFF_GK + GLA_QK_W
OFF_GR = OFF_GV + GLA_V_W
OFF_SQ = OFF_GR + GLA_V_W
OFF_SK = OFF_SQ + SWA_Q_W
OFF_SV = OFF_SK + SWA_KV_W
PROJ_W = OFF_SV + SWA_KV_W
LANES = 128

IN_TM = 1024
IN_TN = 2304
OUT_TM = 512
ROUTE_W = 8
MOE_TR = 256
CMB_TM = 256
VMEM_LIMIT = 56 * 1024 * 1024


def _silu(x):
    return x / (1.0 + jnp.exp(-x))


ALIBI_PARTS = 3


def _bf16_parts(value, n):
    parts, rest = [], np.float32(value)
    for _ in range(n):
        piece = np.float32(np.asarray(rest).astype(jnp.bfloat16))
        parts.append(float(piece))
        rest = np.float32(rest - piece)
    return parts


W_PREP_ROWS = 384
OFF_LR = OFF_GR + GLA_V_W


def _wprep_kernel(wt_hbm, wm_ref, wl_ref, buf, lrbuf, sem, lrsem):
    rows = W_PREP_ROWS
    i = pl.program_id(0)
    n = pl.num_programs(0)
    n_lo = OFF_LR // rows
    slot = lax.rem(i, 2)

    def block_copy(step, s):
        start = jnp.where(step < n_lo, step * rows, OFF_LR + GLA_RANK + (step - n_lo) * rows)
        return pltpu.make_async_copy(wt_hbm.at[pl.ds(pl.multiple_of(start, 8), rows)], buf.at[s], sem.at[s])

    @pl.when(i == 0)
    def _():
        block_copy(0, 0).start()
        lrbuf[...] = jnp.zeros_like(lrbuf)
        lr_copy = pltpu.make_async_copy(wt_hbm.at[pl.ds(OFF_LR, GLA_RANK)], lrbuf.at[pl.ds(0, GLA_RANK)],
                                        lrsem.at[0])
        lr_copy.start()
        lr_copy.wait()
        wl_ref[...] = lrbuf[...].T.astype(BF16)

    @pl.when(i + 1 < n)
    def _():
        block_copy(i + 1, 1 - slot).start()

    block_copy(i, slot).wait()
    wm_ref[...] = buf[slot].T.astype(BF16)


def _w_prep(w_t):
    n, k = w_t.shape
    assert OFF_LR % W_PREP_ROWS == 0 and (PROJ_W - OFF_LR) % W_PREP_ROWS == 0 and n == PROJ_W + GLA_RANK
    return pl.pallas_call(
        _wprep_kernel,
        grid=(PROJ_W // W_PREP_ROWS,),
        in_specs=[pl.BlockSpec(memory_space=pl.ANY)],
        out_specs=[pl.BlockSpec((k, W_PREP_ROWS), lambda i: (0, i)),
                   pl.BlockSpec((k, LANES), lambda i: (0, 0))],
        out_shape=[jax.ShapeDtypeStruct((k, PROJ_W), BF16), jax.ShapeDtypeStruct((k, LANES), BF16)],
        scratch_shapes=[
            pltpu.VMEM((2, W_PREP_ROWS, k), F32),
            pltpu.VMEM((LANES, k), F32),
            pltpu.SemaphoreType.DMA((2,)),
            pltpu.SemaphoreType.DMA((1,)),
        ],
        compiler_params=pltpu.CompilerParams(
            dimension_semantics=("arbitrary",), vmem_limit_bytes=VMEM_LIMIT),
        name="w_prep",
    )(w_t)


def _inproj_kernel(x_ref, g_ref, w_ref, wlr_ref, proj_ref, glr_ref, xn_ref):
    @pl.when(pl.program_id(1) == 0)
    def _():
        x = x_ref[...]
        ms = jnp.mean(x * x, axis=-1, keepdims=True)
        xn = (x * lax.rsqrt(ms + RMS_EPS) * g_ref[...]).astype(BF16)
        xn_ref[...] = xn
        glr_ref[...] = jnp.dot(xn, wlr_ref[...], preferred_element_type=F32)

    proj_ref[...] = jnp.dot(xn_ref[...], w_ref[...], preferred_element_type=F32).astype(BF16)


def _in_proj(x2, g, w_main, w_lr):
    m = x2.shape[0]
    tm = min(IN_TM, m)
    return pl.pallas_call(
        _inproj_kernel,
        grid=(m // tm, PROJ_W // IN_TN),
        in_specs=[
            pl.BlockSpec((tm, D_MODEL), lambda i, j: (i, 0)),
            pl.BlockSpec((1, D_MODEL), lambda i, j: (0, 0)),
            pl.BlockSpec((D_MODEL, IN_TN), lambda i, j: (0, j)),
            pl.BlockSpec((D_MODEL, LANES), lambda i, j: (0, 0)),
        ],
        out_specs=[
            pl.BlockSpec((tm, IN_TN), lambda i, j: (i, j)),
            pl.BlockSpec((tm, LANES), lambda i, j: (i, 0)),
        ],
        out_shape=[
            jax.ShapeDtypeStruct((m, PROJ_W), BF16),
            jax.ShapeDtypeStruct((m, LANES), F32),
        ],
        scratch_shapes=[pltpu.VMEM((tm, D_MODEL), BF16)],
        compiler_params=pltpu.CompilerParams(
            dimension_semantics=("arbitrary", "arbitrary"), vmem_limit_bytes=VMEM_LIMIT),
        name="in_proj",
    )(x2, g, w_main, w_lr)


def _gla_kernel(q_ref, k_ref, v_ref, r_ref, glr_ref, wup_ref, bgk_ref, ng_ref, o_ref,
                la_ref, oi_ref, qi_ref, ki_ref, qd_ref, kd_ref, a_ref, kv_ref, dec_ref, sp_ref):
    t = q_ref.shape[0]
    c = GLA_CHUNK
    nchunk = t // c
    z = jnp.dot(glr_ref[...].astype(BF16), wup_ref[...], preferred_element_type=F32) + bgk_ref[...]
    la_ref[...] = (jnp.minimum(z, 0.0) - jnp.log(1.0 + jnp.exp(-jnp.abs(z)))) * (1.0 / GLA_GATE_NORM)

    ii = lax.broadcasted_iota(jnp.int32, (c, c), 0)
    jj = lax.broadcasted_iota(jnp.int32, (c, c), 1)
    causal = jj <= ii
    tri = causal.astype(BF16)
    nt = (((1,), (1,)), ((), ()))
    tn = (((0,), (0,)), ((), ()))

    def chunk_rows(n):
        return pl.ds(pl.multiple_of(n * c, c), c)

    def decays(n, carry):
        rows = chunk_rows(n)
        la = la_ref[rows, :]
        hi = la.astype(BF16)
        r1 = la - hi.astype(F32)
        mid = r1.astype(BF16)
        lo = (r1 - mid.astype(F32)).astype(BF16)
        parts = jnp.dot(tri, jnp.concatenate([hi, mid, lo], axis=1), preferred_element_type=F32)
        bcum = parts[:, 0:GLA_DK] + parts[:, GLA_DK:2 * GLA_DK] + parts[:, 2 * GLA_DK:]
        b_mid = bcum[c // 2 - 1:c // 2, :]
        b_last = bcum[c - 1:c, :]
        q = q_ref[rows, :].astype(F32) * (GLA_DK ** -0.5)
        k = k_ref[rows, :].astype(F32)
        qi_ref[rows, :] = (q * jnp.exp(bcum - b_mid)).astype(BF16)
        ki_ref[rows, :] = (k * jnp.exp(b_mid - bcum)).astype(BF16)
        qd_ref[rows, :] = (q * jnp.exp(bcum)).astype(BF16)
        kd_ref[rows, :] = (k * jnp.exp(b_last - bcum)).astype(BF16)
        dec_ref[n] = jnp.broadcast_to(jnp.exp(b_last), dec_ref.shape[1:])
        return carry

    lax.fori_loop(0, nchunk, decays, 0, unroll=4)

    def scores(n, carry):
        rows = chunk_rows(n)
        a = lax.dot_general(qi_ref[rows, :], ki_ref[rows, :], nt, preferred_element_type=F32)
        a_ref[rows, :] = jnp.where(causal, a, 0.0).astype(BF16)
        return carry

    lax.fori_loop(0, nchunk, scores, 0, unroll=8)

    def intra(n, carry):
        rows = chunk_rows(n)
        v = v_ref[rows, :]
        oi_ref[rows, :] = jnp.dot(a_ref[rows, :], v, preferred_element_type=F32)
        kv_ref[n] = lax.dot_general(v, kd_ref[rows, :], tn, preferred_element_type=F32)
        return carry

    lax.fori_loop(0, nchunk, intra, 0, unroll=8)

    def scan(n, s_t):
        sp_ref[n] = s_t.astype(BF16)
        return s_t * dec_ref[n][0:1, :] + kv_ref[n]

    lax.fori_loop(0, nchunk, scan, jnp.zeros((GLA_DV, GLA_DK), F32))

    def inter(n, carry):
        rows = chunk_rows(n)
        o = oi_ref[rows, :] + lax.dot_general(qd_ref[rows, :], sp_ref[n], nt,
                                              preferred_element_type=F32)
        ms = jnp.mean(o * o, axis=-1, keepdims=True)
        o = o * lax.rsqrt(ms + RMS_EPS) * ng_ref[...]
        o = o * _silu(r_ref[rows, :].astype(F32))
        o_ref[rows, :] = o.astype(BF16)
        return carry

    lax.fori_loop(0, nchunk, inter, 0, unroll=8)


def _gla(proj, glr, wup, bgk, ng, batch, seq):
    m = proj.shape[0]
    return pl.pallas_call(
        _gla_kernel,
        grid=(batch, GLA_HEADS),
        in_specs=[
            pl.BlockSpec((seq, GLA_DK), lambda b, h: (b, OFF_GQ // GLA_DK + h)),
            pl.BlockSpec((seq, GLA_DK), lambda b, h: (b, OFF_GK // GLA_DK + h)),
            pl.BlockSpec((seq, GLA_DV), lambda b, h: (b, OFF_GV // GLA_DV + h)),
            pl.BlockSpec((seq, GLA_DV), lambda b, h: (b, OFF_GR // GLA_DV + h)),
            pl.BlockSpec((seq, LANES), lambda b, h: (b, 0)),
            pl.BlockSpec((LANES, GLA_DK), lambda b, h: (0, h)),
            pl.BlockSpec((1, GLA_DK), lambda b, h: (0, h)),
            pl.BlockSpec((1, GLA_DV), lambda b, h: (0, h)),
        ],
        out_specs=pl.BlockSpec((seq, GLA_DV), lambda b, h: (b, h)),
        out_shape=jax.ShapeDtypeStruct((m, GLA_V_W), BF16),
        scratch_shapes=[
            pltpu.VMEM((seq, GLA_DK), F32),
            pltpu.VMEM((seq, GLA_DV), F32),
            pltpu.VMEM((seq, GLA_DK), BF16),
            pltpu.VMEM((seq, GLA_DK), BF16),
            pltpu.VMEM((seq, GLA_DK), BF16),
            pltpu.VMEM((seq, GLA_DK), BF16),
            pltpu.VMEM((seq, GLA_CHUNK), BF16),
            pltpu.VMEM((seq // GLA_CHUNK, GLA_DV, GLA_DK), F32),
            pltpu.VMEM((seq // GLA_CHUNK, 8, GLA_DK), F32),
            pltpu.VMEM((seq // GLA_CHUNK, GLA_DV, GLA_DK), BF16),
        ],
        compiler_params=pltpu.CompilerParams(
            dimension_semantics=("arbitrary", "arbitrary"), vmem_limit_bytes=VMEM_LIMIT),
        name="gla",
    )(proj, proj, proj, proj, glr, wup, bgk, ng)


def _swa_kernel(sink_ref, q_ref, kp_ref, kc_ref, vp_ref, vc_ref, o_ref):
    blk = SWA_BLOCK
    half = SWA_HEAD_DIM
    n = pl.program_id(1)
    k_all = jnp.concatenate([kp_ref[...], kc_ref[...]], axis=0)
    v_all = jnp.concatenate([vp_ref[...], vc_ref[...]], axis=0)
    qi = lax.broadcasted_iota(jnp.int32, (blk, 2 * blk), 0)
    kj = lax.broadcasted_iota(jnp.int32, (blk, 2 * blk), 1)
    rel = qi + blk - kj
    valid = (rel >= 0) & (rel < blk) & ((kj >= blk) | (n > 0))
    sink_col = kj == 0
    lane_kv = lax.broadcasted_iota(jnp.int32, (2 * blk, LANES), 1)
    lo_kv = lane_kv < half
    lane_q = lax.broadcasted_iota(jnp.int32, (blk, LANES), 1)
    lo_q = lane_q < half
    nt = (((1,), (1,)), ((), ()))
    key_idx = lax.broadcasted_iota(jnp.int32, (2 * blk, LANES), 0).astype(F32)
    key_cols = jnp.where((lane_kv & (half - 1)) < ALIBI_PARTS, key_idx, 0.0)
    q_pos = lax.broadcasted_iota(jnp.int32, (blk, 1), 0).astype(F32) + float(blk)
    q_scale = jnp.asarray(SWA_HEAD_DIM ** -0.5, BF16)

    for p in range(SWA_KV_HEADS // 2):
        k_slab = k_all[:, p * LANES:(p + 1) * LANES].astype(F32)
        v_slab = v_all[:, p * LANES:(p + 1) * LANES].astype(F32)
        k_roll = pltpu.roll(k_slab, half, 1)
        v_roll = pltpu.roll(v_slab, half, 1)
        for hh in range(2):
            h = 2 * p + hh
            k_lo, k_hi = (k_slab, k_roll) if hh == 0 else (k_roll, k_slab)
            kd = (jnp.where(lo_kv, k_lo, key_cols).astype(BF16),
                  jnp.where(lo_kv, key_cols, k_hi).astype(BF16))
            v_lo, v_hi = (v_slab, v_roll) if hh == 0 else (v_roll, v_slab)
            vd = jnp.where(lo_kv, v_lo, v_hi)
            vd = jnp.where(key_idx == 0.0, 0.0, vd).astype(BF16)
            for gp in range(SWA_GROUP // 2):
                col = (h * (SWA_GROUP // 2) + gp) * LANES
                qs = q_ref[:, col:col + LANES] * q_scale
                outs = []
                for gg in range(2):
                    head = h * SWA_GROUP + 2 * gp + gg
                    parts = _bf16_parts(2.0 ** (-8.0 * (head + 1) / SWA_Q_HEADS), ALIBI_PARTS)
                    slope = sum(parts)
                    aug = jnp.zeros((blk, LANES), F32)
                    for idx, part in enumerate(parts):
                        aug = jnp.where(lane_q == (half if gg == 0 else 0) + idx, part, aug)
                    keep = lo_q if gg == 0 else jnp.logical_not(lo_q)
                    qm = jnp.where(keep, qs, aug.astype(BF16))
                    s = lax.dot_general(qm, kd[gg], nt, preferred_element_type=F32)
                    sink = sink_ref[head] + slope * q_pos
                    s = jnp.where(sink_col, sink, jnp.where(valid, s, -jnp.inf))
                    mx = jnp.max(s, axis=-1, keepdims=True)
                    pe = jnp.exp(s - mx)
                    den = jnp.sum(pe, axis=-1, keepdims=True)
                    o = jnp.dot(pe.astype(BF16), vd, preferred_element_type=F32)
                    outs.append(o / den)
                o_ref[:, col:col + LANES] = jnp.where(lo_q, outs[0], outs[1]).astype(BF16)


def _swa(proj, sinks, batch, seq):
    m = proj.shape[0]
    nb = seq // SWA_BLOCK
    qcol = OFF_SQ // SWA_Q_W
    kcol = OFF_SK // SWA_KV_W
    vcol = OFF_SV // SWA_KV_W
    cur = lambda c: (lambda b, n: (b * nb + n, c))
    prev = lambda c: (lambda b, n: (b * nb + jnp.maximum(n - 1, 0), c))
    return pl.pallas_call(
        _swa_kernel,
        grid=(batch, nb),
        in_specs=[
            pl.BlockSpec(memory_space=pltpu.SMEM),
            pl.BlockSpec((SWA_BLOCK, SWA_Q_W), cur(qcol)),
            pl.BlockSpec((SWA_BLOCK, SWA_KV_W), prev(kcol)),
            pl.BlockSpec((SWA_BLOCK, SWA_KV_W), cur(kcol)),
            pl.BlockSpec((SWA_BLOCK, SWA_KV_W), prev(vcol)),
            pl.BlockSpec((SWA_BLOCK, SWA_KV_W), cur(vcol)),
        ],
        out_specs=pl.BlockSpec((SWA_BLOCK, SWA_Q_W), lambda b, n: (b * nb + n, 0)),
        out_shape=jax.ShapeDtypeStruct((m, SWA_Q_W), BF16),
        compiler_params=pltpu.CompilerParams(
            dimension_semantics=("arbitrary", "arbitrary"), vmem_limit_bytes=VMEM_LIMIT),
        name="swa",
    )(sinks, proj, proj, proj, proj, proj)


def _out_route_kernel(og_ref, os_ref, x_ref, wo_ref, g_ref, wr_ref, br_ref,
                      h_ref, hn_ref, route_ref, route_t_ref, cnt_ref, hnb_ref):
    t = pl.program_id(0)
    slot = lax.rem(t, 2)

    @pl.when(t == 0)
    def _():
        hnb_ref[...] = jnp.zeros_like(hnb_ref)

    logits = jnp.dot(hnb_ref[1 - slot], wr_ref[...], preferred_element_type=F32) + br_ref[...]

    h = x_ref[...]
    h = h + jnp.dot(og_ref[...], wo_ref[0:GLA_V_W, :], preferred_element_type=F32)
    h = h + jnp.dot(os_ref[...], wo_ref[GLA_V_W:, :], preferred_element_type=F32)
    h_ref[...] = h
    ms = jnp.mean(h * h, axis=-1, keepdims=True)
    hn = h * lax.rsqrt(ms + RMS_EPS) * g_ref[...]
    hn_ref[...] = hn
    hnb_ref[slot] = hn.astype(BF16)

    lane = lax.broadcasted_iota(jnp.int32, logits.shape, 1)
    lanef = lane.astype(F32)
    big = float(LANES)
    ninf = -jnp.inf
    gl = jnp.where(lane < N_GROUPS, logits, ninf)
    gmax = jnp.max(gl, axis=-1, keepdims=True)
    g_p = 1.0 / jnp.sum(jnp.exp(gl - gmax), axis=-1, keepdims=True)
    g_idx = jnp.min(jnp.where(gl == gmax, lanef, big), axis=-1, keepdims=True)
    lo = N_GROUPS + EXPERTS_PER_GROUP * g_idx
    el = jnp.where((lanef >= lo) & (lanef < lo + EXPERTS_PER_GROUP), logits, ninf)
    m1 = jnp.max(el, axis=-1, keepdims=True)
    i1 = jnp.min(jnp.where(el == m1, lanef, big), axis=-1, keepdims=True)
    el2 = jnp.where(lanef == i1, ninf, el)
    m2 = jnp.max(el2, axis=-1, keepdims=True)
    i2 = jnp.min(jnp.where(el2 == m2, lanef, big), axis=-1, keepdims=True)
    d = jnp.exp(m2 - m1)
    c1 = g_p / (1.0 + d)
    c2 = g_p * d / (1.0 + d)
    tm = logits.shape[0]
    chosen = ((lanef == i1) | (lanef == i2)).astype(BF16)
    ri = lax.broadcasted_iota(jnp.int32, (tm, tm), 0)
    ci = lax.broadcasted_iota(jnp.int32, (tm, tm), 1)
    earlier = jnp.dot((ci < ri).astype(BF16), chosen, preferred_element_type=F32)
    r1 = jnp.sum(jnp.where(lanef == i1, earlier, 0.0), axis=-1, keepdims=True)
    r2 = jnp.sum(jnp.where(lanef == i2, earlier, 0.0), axis=-1, keepdims=True)
    cnt = jnp.sum(chosen.astype(F32), axis=0, keepdims=True)
    cnt_ref[...] = jnp.broadcast_to(cnt, cnt_ref.shape)

    fields = [i1 - N_GROUPS, i2 - N_GROUPS, c1, c2, r1, r2]
    route = jnp.zeros_like(logits)
    for idx, val in enumerate(fields):
        route = jnp.where(lane == idx, val, route)
    route_ref[...] = route[:, 0:ROUTE_W]
    route_t_ref[...] = route.T[0:ROUTE_W, :][None]


def _out_route(o_gla, o_swa, x2, w_out, g, w_rt, b_rt):
    m = x2.shape[0]
    tm = min(OUT_TM, m)
    ntile = m // tm
    row = lambda i: (jnp.minimum(i, ntile - 1), 0)
    routed = lambda i: (jnp.maximum(i - 1, 0), 0)
    fixed = lambda i: (0, 0)
    return pl.pallas_call(
        _out_route_kernel,
        grid=(ntile + 1,),
        in_specs=[
            pl.BlockSpec((tm, GLA_V_W), row),
            pl.BlockSpec((tm, SWA_Q_W), row),
            pl.BlockSpec((tm, D_MODEL), row),
            pl.BlockSpec((GLA_V_W + SWA_Q_W, D_MODEL), fixed),
            pl.BlockSpec((1, D_MODEL), fixed),
            pl.BlockSpec((D_MODEL, LANES), fixed),
            pl.BlockSpec((1, LANES), fixed),
        ],
        out_specs=[
            pl.BlockSpec((tm, D_MODEL), row),
            pl.BlockSpec((tm, D_MODEL), row),
            pl.BlockSpec((tm, ROUTE_W), routed),
            pl.BlockSpec((1, ROUTE_W, tm), lambda i: (jnp.maximum(i - 1, 0), 0, 0)),
            pl.BlockSpec((1, 8, LANES), lambda i: (jnp.maximum(i - 1, 0), 0, 0)),
        ],
        out_shape=[
            jax.ShapeDtypeStruct((m, D_MODEL), F32),
            jax.ShapeDtypeStruct((m, D_MODEL), F32),
            jax.ShapeDtypeStruct((m, ROUTE_W), F32),
            jax.ShapeDtypeStruct((m // tm, ROUTE_W, tm), F32),
            jax.ShapeDtypeStruct((m // tm, 8, LANES), F32),
        ],
        scratch_shapes=[pltpu.VMEM((2, tm, D_MODEL), BF16)],
        compiler_params=pltpu.CompilerParams(
            dimension_semantics=("arbitrary",), vmem_limit_bytes=VMEM_LIMIT),
        name="out_route",
    )(o_gla, o_swa, x2, w_out, g, w_rt, b_rt)


LANES_LOG2 = 7
MOE_ROW_GROUP_LOG2 = 3
MOE_ROW_GROUP = 1 << MOE_ROW_GROUP_LOG2
MOE_GATHER_GROUP_LOG2 = 4
MOE_GATHER_GROUP = 1 << MOE_GATHER_GROUP_LOG2
MOE_TAIL_ROWS = N_EXPERTS * MOE_ROW_GROUP


def _moe_kernel(te_ref, nv_ref, ne_ref, yb_ref, meta_ref, tok_hbm, hn_hbm, wg_hbm, wu_hbm, wd_hbm,
                y_hbm, xbuf, gsem, ids_smem, isem, wgs, wus, wds, wsem, wgb, wub, wdb, ybuf, ysem):
    tr = MOE_TR
    grp = MOE_ROW_GROUP
    t = pl.program_id(0)
    nt = meta_ref[0]
    gs = lax.rem(t, 2)
    cs = 1 - gs

    def ids_copy(tile, s):
        return pltpu.make_async_copy(tok_hbm.at[tile], ids_smem.at[s], isem.at[s])

    def gather_rows(tile):
        groups = lax.shift_right_logical(nv_ref[tile] + (MOE_GATHER_GROUP - 1), MOE_GATHER_GROUP_LOG2)
        return lax.shift_left(groups, MOE_GATHER_GROUP_LOG2)

    def weight_copies(e):
        return (pltpu.make_async_copy(wg_hbm.at[e], wgs, wsem.at[0]),
                pltpu.make_async_copy(wu_hbm.at[e], wus, wsem.at[1]),
                pltpu.make_async_copy(wd_hbm.at[e], wds, wsem.at[2]))

    def rows_used(tile):
        groups = lax.shift_right_logical(nv_ref[tile] + (grp - 1), MOE_ROW_GROUP_LOG2)
        return lax.shift_left(groups, MOE_ROW_GROUP_LOG2)

    def y_copy(tile, s):
        n_rows = pl.multiple_of(rows_used(tile), grp)
        dst = y_hbm.at[pl.ds(pl.multiple_of(yb_ref[tile], grp), n_rows)]
        return pltpu.make_async_copy(ybuf.at[s, pl.ds(0, n_rows)], dst, ysem.at[s])

    @pl.when(t == 0)
    def _():
        xbuf[...] = jnp.zeros_like(xbuf)
        ids_copy(0, 0).start()
        for cp in weight_copies(te_ref[0]):
            cp.start()

    @pl.when(t < nt)
    def _():
        ids_copy(t, gs).wait()

        @pl.when(t + 1 < nt)
        def _():
            ids_copy(t + 1, cs).start()

        n_real = nv_ref[t]
        for slot in range(2):
            @pl.when(gs == slot)
            def _():
                for g0 in range(0, tr, MOE_GATHER_GROUP):
                    @pl.when(g0 < n_real)
                    def _():
                        for r in range(g0, g0 + MOE_GATHER_GROUP):
                            tok = ids_smem[slot, r // LANES, r % LANES]
                            pltpu.make_async_copy(hn_hbm.at[pl.ds(tok, 1)], xbuf.at[slot, pl.ds(r, 1)],
                                                  gsem.at[slot]).start(priority=r % 2)

    @pl.when((t >= 1) & (t <= nt))
    def _():
        c = t - 1
        changed = (c == 0) | (te_ref[c] != te_ref[jnp.maximum(c - 1, 0)])

        @pl.when(changed)
        def _():
            for cp in weight_copies(te_ref[c]):
                cp.wait()
            wgb[...] = wgs[...].astype(BF16)
            wub[...] = wus[...].astype(BF16)
            wdb[...] = wds[...].astype(BF16)
            nxt = ne_ref[c]

            @pl.when(nxt >= 0)
            def _():
                for cp in weight_copies(nxt):
                    cp.start()

        n_rows = pl.multiple_of(gather_rows(c), MOE_GATHER_GROUP)
        pltpu.make_async_copy(hn_hbm.at[pl.ds(0, n_rows)], xbuf.at[cs, pl.ds(0, n_rows)],
                              gsem.at[cs]).wait()
        x = xbuf[cs].astype(BF16)
        g = jnp.dot(x, wgb[...], preferred_element_type=F32)
        u = jnp.dot(x, wub[...], preferred_element_type=F32)
        hmid = (_silu(g) * u).astype(BF16)

        @pl.when(c >= 2)
        def _():
            y_copy(c - 2, cs).wait()

        ybuf[cs] = jnp.dot(hmid, wdb[...], preferred_element_type=F32)
        y_copy(c, cs).start()

        @pl.when(t == nt)
        def _():
            @pl.when(c >= 1)
            def _():
                y_copy(c - 1, gs).wait()

            y_copy(c, cs).wait()
            ybuf[gs] = jnp.zeros((tr, D_MODEL), F32)
            tail0, n_tail = meta_ref[1], meta_ref[2]
            for k in range(MOE_TAIL_ROWS // tr):
                n_k = pl.multiple_of(jnp.clip(n_tail - k * tr, 0, tr), grp)

                @pl.when(n_k > 0)
                def _():
                    dst = y_hbm.at[pl.ds(pl.multiple_of(tail0 + k * tr, grp), n_k)]
                    cp = pltpu.make_async_copy(ybuf.at[gs, pl.ds(0, n_k)], dst, ysem.at[gs])
                    cp.start()
                    cp.wait()


def _moe(hn, w_gate, w_up, w_down, plan):
    n_tiles = plan["tile_expert"].shape[0]
    y_rows = 2 * hn.shape[0] + MOE_TAIL_ROWS
    any_spec = pl.BlockSpec(memory_space=pl.ANY)
    grid_spec = pltpu.PrefetchScalarGridSpec(
        num_scalar_prefetch=5,
        grid=(n_tiles + 1,),
        in_specs=[any_spec] * 5,
        out_specs=any_spec,
        scratch_shapes=[
            pltpu.VMEM((2, MOE_TR, D_MODEL), F32),
            pltpu.SemaphoreType.DMA((2,)),
            pltpu.SMEM((2, MOE_TR // LANES, LANES), jnp.int32),
            pltpu.SemaphoreType.DMA((2,)),
            pltpu.VMEM((D_MODEL, D_EXPERT), F32),
            pltpu.VMEM((D_MODEL, D_EXPERT), F32),
            pltpu.VMEM((D_EXPERT, D_MODEL), F32),
            pltpu.SemaphoreType.DMA((3,)),
            pltpu.VMEM((D_MODEL, D_EXPERT), BF16),
            pltpu.VMEM((D_MODEL, D_EXPERT), BF16),
            pltpu.VMEM((D_EXPERT, D_MODEL), BF16),
            pltpu.VMEM((2, MOE_TR, D_MODEL), F32),
            pltpu.SemaphoreType.DMA((2,)),
        ],
    )
    return pl.pallas_call(
        _moe_kernel,
        grid_spec=grid_spec,
        out_shape=jax.ShapeDtypeStruct((y_rows, D_MODEL), F32),
        compiler_params=pltpu.CompilerParams(
            dimension_semantics=("arbitrary",), vmem_limit_bytes=VMEM_LIMIT),
        name="moe",
    )(plan["tile_expert"], plan["tile_rows"], plan["next_expert"], plan["tile_out"],
      plan["meta"], plan["tile_tokens"], hn, w_gate, w_up, w_down)


def _combine_kernel(pos_hbm, y_hbm, h_ref, route_ref, g_ref, o_ref, ybuf, gsem, ids_smem, isem):
    tm = h_ref.shape[0]
    t = pl.program_id(0)
    ntile = pl.num_programs(0) - 1

    def ids_copy(tile, s):
        return pltpu.make_async_copy(pos_hbm.at[tile], ids_smem.at[s], isem.at[s])

    @pl.when(t == 0)
    def _():
        ids_copy(0, 0).start()

    @pl.when(t < ntile)
    def _():
        s = lax.rem(t, 2)
        ids_copy(t, s).wait()

        @pl.when(t + 1 < ntile)
        def _():
            ids_copy(t + 1, 1 - s).start()

        for slot in range(2):
            @pl.when(s == slot)
            def _():
                for kk in range(2):
                    for r in range(tm):
                        flat = kk * tm + r
                        row = ids_smem[slot, flat // LANES, flat % LANES]
                        pltpu.make_async_copy(y_hbm.at[pl.ds(row, 1)], ybuf.at[slot, kk, pl.ds(r, 1)],
                                              gsem.at[slot, kk]).start(priority=r % 2)

    @pl.when(t >= 1)
    def _():
        s = lax.rem(t - 1, 2)
        for kk in range(2):
            pltpu.make_async_copy(y_hbm.at[pl.ds(0, tm)], ybuf.at[s, kk], gsem.at[s, kk]).wait()
        route = route_ref[...]
        h = h_ref[...] + route[:, 2:3] * ybuf[s, 0] + route[:, 3:4] * ybuf[s, 1]
        ms = jnp.mean(h * h, axis=-1, keepdims=True)
        o_ref[...] = h * lax.rsqrt(ms + RMS_EPS) * g_ref[...]


def _combine(y, h, route, g, pos3):
    m = h.shape[0]
    tm = min(CMB_TM, m)
    ntile = m // tm
    assert pos3.shape == (ntile, 2 * tm // LANES, LANES)
    row = lambda i: (jnp.maximum(i - 1, 0), 0)
    return pl.pallas_call(
        _combine_kernel,
        grid=(ntile + 1,),
        in_specs=[
            pl.BlockSpec(memory_space=pl.ANY),
            pl.BlockSpec(memory_space=pl.ANY),
            pl.BlockSpec((tm, D_MODEL), row),
            pl.BlockSpec((tm, ROUTE_W), row),
            pl.BlockSpec((1, D_MODEL), lambda i: (0, 0)),
        ],
        out_specs=pl.BlockSpec((tm, D_MODEL), row),
        out_shape=jax.ShapeDtypeStruct((m, D_MODEL), F32),
        scratch_shapes=[
            pltpu.VMEM((2, 2, tm, D_MODEL), F32),
            pltpu.SemaphoreType.DMA((2, 2)),
            pltpu.SMEM((2, 2 * tm // LANES, LANES), jnp.int32),
            pltpu.SemaphoreType.DMA((2,)),
        ],
        compiler_params=pltpu.CompilerParams(
            dimension_semantics=("arbitrary",), vmem_limit_bytes=VMEM_LIMIT),
        name="combine",
    )(pos3, y, h, route, g)


def _dispatch_plan(route_t, tile_counts, m):
    tr = MOE_TR
    p_rows = 2 * m + N_EXPERTS * tr
    n_tiles = p_rows // tr
    n_tok_tiles, _, tm = route_t.shape
    cnt = tile_counts[:, 0, N_GROUPS:N_GROUPS + N_EXPERTS].astype(jnp.int32)
    before_tile = jnp.cumsum(cnt, axis=0) - cnt
    counts = jnp.sum(cnt, axis=0)
    padded = ((counts + tr - 1) // tr) * tr
    ends = jnp.cumsum(padded)
    starts = ends - padded
    grp = MOE_ROW_GROUP
    y_len = ((counts + grp - 1) // grp) * grp
    y_starts = jnp.cumsum(y_len) - y_len
    base = (y_starts[None, :] + before_tile).astype(jnp.int32)
    fields = route_t.astype(jnp.int32)
    experts = (fields[:, 0, :], fields[:, 1, :])
    ranks = (fields[:, 4, :], fields[:, 5, :])
    expert_ids = jnp.arange(N_EXPERTS, dtype=jnp.int32)
    rows = [rk + jnp.sum(jnp.where(ex[..., None] == expert_ids, base[:, None, :], 0), axis=-1)
            for ex, rk in zip(experts, ranks)]
    ctm = min(CMB_TM, m)
    pos = jnp.concatenate([r.reshape(m // ctm, ctm) for r in rows], axis=1)
    pos = pos.reshape(m // ctm, 2 * ctm // LANES, LANES)
    num_tiles = (ends[-1] // tr).astype(jnp.int32)
    tile_idx = jnp.minimum(jnp.arange(n_tiles, dtype=jnp.int32), num_tiles - 1)
    tile_expert = jnp.sum((tile_idx[:, None] * tr >= ends[None, :]).astype(jnp.int32), axis=1)
    n_slots = 2 * m
    assert n_slots & (n_slots - 1) == 0
    tok2 = 2 * jnp.arange(m, dtype=jnp.int32).reshape(n_tok_tiles, tm)
    keys = [ex * n_slots + tok2 + k for k, ex in enumerate(experts)]
    sorted_key = jnp.sort(jnp.concatenate(keys, axis=0).reshape(-1))
    sorted_tokens = (sorted_key & (n_slots - 1)) >> 1
    dense_starts = jnp.cumsum(counts) - counts
    in_expert = tile_idx - (starts // tr)[tile_expert]
    tile_base = dense_starts[tile_expert] + in_expert * tr
    tile_rows = jnp.clip(counts[tile_expert] - in_expert * tr, 0, tr)
    tile_rows = jnp.where(jnp.arange(n_tiles) < num_tiles, tile_rows, 0)
    window = tile_base[:, None] + jnp.arange(tr, dtype=jnp.int32)[None, :]
    tile_tokens = jnp.pad(sorted_tokens, (0, tr))[window].reshape(n_tiles, tr // LANES, LANES)
    ids = jnp.arange(N_EXPERTS, dtype=jnp.int32)
    present = jnp.where(counts > 0, ids, N_EXPERTS)
    next_ge = lax.cummin(present, axis=0, reverse=True)
    next_gt = jnp.concatenate([next_ge[1:], jnp.full((1,), N_EXPERTS, jnp.int32)])
    next_gt = jnp.where(next_gt >= N_EXPERTS, -1, next_gt)
    tile_out = y_starts[tile_expert] + in_expert * tr
    y_used = jnp.sum(y_len)
    meta = jnp.stack([num_tiles, y_used, 2 * m + MOE_TAIL_ROWS - y_used]).astype(jnp.int32)
    plan = dict(tile_expert=tile_expert.astype(jnp.int32), tile_base=tile_base.astype(jnp.int32),
                tile_rows=tile_rows.astype(jnp.int32), next_expert=next_gt[tile_expert].astype(jnp.int32),
                tile_out=tile_out.astype(jnp.int32), meta=meta, tile_tokens=tile_tokens)
    return plan, pos


def kernel(x, norm_mix_g, w_in, w_gk_up, b_gk, gla_norm_g, swa_sinks, w_out, norm_ffn_g,
           w_group, b_group, w_router, b_router, w_gate, w_up, w_down, norm_final_g):
    batch, seq, d = x.shape
    m = batch * seq
    assert w_in.shape[0] == 1, "single-layer block"
    x2 = x.reshape(m, d)
    w_main, w_lr = _w_prep(jnp.transpose(w_in[0]))
    wup = jnp.pad(w_gk_up[0], ((0, LANES - GLA_RANK), (0, 0))).astype(BF16)
    w_rt = jnp.pad(jnp.concatenate([w_group[0], w_router[0]], axis=1),
                   ((0, 0), (0, LANES - N_GROUPS - N_EXPERTS))).astype(BF16)
    b_rt = jnp.pad(jnp.concatenate([b_group[0], b_router[0]]),
                   (0, LANES - N_GROUPS - N_EXPERTS)).reshape(1, LANES)

    proj, glr = _in_proj(x2, norm_mix_g[0].reshape(1, d), w_main, w_lr)
    o_gla = _gla(proj, glr, wup, b_gk[0].reshape(1, GLA_QK_W),
                 gla_norm_g[0].reshape(1, GLA_V_W), batch, seq)
    o_swa = _swa(proj, swa_sinks[0], batch, seq)
    h_mid, hn, route, route_t, tile_counts = _out_route(o_gla, o_swa, x2, w_out[0].astype(BF16),
                                                        norm_ffn_g[0].reshape(1, d), w_rt, b_rt)
    plan, pos3 = _dispatch_plan(route_t, tile_counts, m)
    y = _moe(hn, w_gate[0], w_up[0], w_down[0], plan)
    out = _combine(y, h_mid, route, norm_final_g.reshape(1, d), pos3)
    return out.reshape(batch, seq, d)
```

```python
import functools

import jax
import jax.numpy as jnp
import numpy as np
from jax import lax
from jax.experimental import pallas as pl
from jax.experimental.pallas import tpu as pltpu

F32 = jnp.float32
BF16 = jnp.bfloat16

D_MODEL = 2048
GLA_HEADS = 4
GLA_DK = 128
GLA_DV = 256
GLA_RANK = 16
GLA_GATE_NORM = 16.0
GLA_CHUNK = 64
SWA_Q_HEADS = 16
SWA_KV_HEADS = 4
SWA_GROUP = SWA_Q_HEADS // SWA_KV_HEADS
SWA_HEAD_DIM = 64
SWA_BLOCK = 128
N_GROUPS = 4
EXPERTS_PER_GROUP = 16
N_EXPERTS = N_GROUPS * EXPERTS_PER_GROUP
D_EXPERT = 256
RMS_EPS = 1e-6

GLA_QK_W = GLA_HEADS * GLA_DK
GLA_V_W = GLA_HEADS * GLA_DV
SWA_Q_W = SWA_Q_HEADS * SWA_HEAD_DIM
SWA_KV_W = SWA_KV_HEADS * SWA_HEAD_DIM
OFF_GQ = 0
OFF_GK = OFF_GQ + GLA_QK_W
OFF_GV = OFF_GK + GLA_QK_W
OFF_GR = OFF_GV + GLA_V_W
OFF_SQ = OFF_GR + GLA_V_W
OFF_SK = OFF_SQ + SWA_Q_W
OFF_SV = OFF_SK + SWA_KV_W
PROJ_W = OFF_SV + SWA_KV_W
LANES = 128

IN_TM = 512
IN_TN = 4608
OUT_TM = 512
ROUTE_W = 8
MOE_TR = 256
CMB_TM = 256
VMEM_LIMIT = 56 * 1024 * 1024


def _silu(x):
    return x / (1.0 + jnp.exp(-x))


ALIBI_PARTS = 3


def _bf16_parts(value, n):
    parts, rest = [], np.float32(value)
    for _ in range(n):
        piece = np.float32(np.asarray(rest).astype(jnp.bfloat16))
        parts.append(float(piece))
        rest = np.float32(rest - piece)
    return parts


W_PREP_ROWS = 384
OFF_LR = OFF_GR + GLA_V_W


def _wprep_kernel(wt_hbm, wm_ref, wl_ref, buf, lrbuf, sem, lrsem):
    rows = W_PREP_ROWS
    i = pl.program_id(0)
    n = pl.num_programs(0)
    n_lo = OFF_LR // rows
    slot = lax.rem(i, 2)

    def block_copy(step, s):
        start = jnp.where(step < n_lo, step * rows, OFF_LR + GLA_RANK + (step - n_lo) * rows)
        return pltpu.make_async_copy(wt_hbm.at[pl.ds(pl.multiple_of(start, 8), rows)], buf.at[s], sem.at[s])

    @pl.when(i == 0)
    def _():
        block_copy(0, 0).start()
        lrbuf[...] = jnp.zeros_like(lrbuf)
        lr_copy = pltpu.make_async_copy(wt_hbm.at[pl.ds(OFF_LR, GLA_RANK)], lrbuf.at[pl.ds(0, GLA_RANK)],
                                        lrsem.at[0])
        lr_copy.start()
        lr_copy.wait()
        wl_ref[...] = lrbuf[...].T.astype(BF16)

    @pl.when(i + 1 < n)
    def _():
        block_copy(i + 1, 1 - slot).start()

    block_copy(i, slot).wait()
    wm_ref[...] = buf[slot].T.astype(BF16)


def _w_prep(w_t):
    n, k = w_t.shape
    assert OFF_LR % W_PREP_ROWS == 0 and (PROJ_W - OFF_LR) % W_PREP_ROWS == 0 and n == PROJ_W + GLA_RANK
    return pl.pallas_call(
        _wprep_kernel,
        grid=(PROJ_W // W_PREP_ROWS,),
        in_specs=[pl.BlockSpec(memory_space=pl.ANY)],
        out_specs=[pl.BlockSpec((k, W_PREP_ROWS), lambda i: (0, i)),
                   pl.BlockSpec((k, LANES), lambda i: (0, 0))],
        out_shape=[jax.ShapeDtypeStruct((k, PROJ_W), BF16), jax.ShapeDtypeStruct((k, LANES), BF16)],
        scratch_shapes=[
            pltpu.VMEM((2, W_PREP_ROWS, k), F32),
            pltpu.VMEM((LANES, k), F32),
            pltpu.SemaphoreType.DMA((2,)),
            pltpu.SemaphoreType.DMA((1,)),
        ],
        compiler_params=pltpu.CompilerParams(
            dimension_semantics=("arbitrary",), vmem_limit_bytes=VMEM_LIMIT),
        name="w_prep",
    )(w_t)


def _inproj_kernel(x_ref, g_ref, w_ref, wlr_ref, proj_ref, glr_ref, xn_ref):
    @pl.when(pl.program_id(1) == 0)
    def _():
        x = x_ref[...]
        ms = jnp.mean(x * x, axis=-1, keepdims=True)
        xn = (x * lax.rsqrt(ms + RMS_EPS) * g_ref[...]).astype(BF16)
        xn_ref[...] = xn
        glr_ref[...] = jnp.dot(xn, wlr_ref[...], preferred_element_type=F32)

    proj_ref[...] = jnp.dot(xn_ref[...], w_ref[...], preferred_element_type=F32).astype(BF16)


def _in_proj(x2, g, w_main, w_lr):
    m = x2.shape[0]
    tm = min(IN_TM, m)
    return pl.pallas_call(
        _inproj_kernel,
        grid=(m // tm, PROJ_W // IN_TN),
        in_specs=[
            pl.BlockSpec((tm, D_MODEL), lambda i, j: (i, 0)),
            pl.BlockSpec((1, D_MODEL), lambda i, j: (0, 0)),
            pl.BlockSpec((D_MODEL, IN_TN), lambda i, j: (0, j),
                         pipeline_mode=pl.Buffered(1) if IN_TN == PROJ_W else None),
            pl.BlockSpec((D_MODEL, LANES), lambda i, j: (0, 0)),
        ],
        out_specs=[
            pl.BlockSpec((tm, IN_TN), lambda i, j: (i, j)),
            pl.BlockSpec((tm, LANES), lambda i, j: (i, 0)),
        ],
        out_shape=[
            jax.ShapeDtypeStruct((m, PROJ_W), BF16),
            jax.ShapeDtypeStruct((m, LANES), F32),
        ],
        scratch_shapes=[pltpu.VMEM((tm, D_MODEL), BF16)],
        compiler_params=pltpu.CompilerParams(
            dimension_semantics=("arbitrary", "arbitrary"), vmem_limit_bytes=VMEM_LIMIT),
        name="in_proj",
    )(x2, g, w_main, w_lr)


def _gla_kernel(q_ref, k_ref, v_ref, r_ref, glr_ref, wup_ref, bgk_ref, ng_ref, o_ref,
                la_ref, oi_ref, qi_ref, ki_ref, qd_ref, kd_ref, a_ref, kv_ref, dec_ref, sp_ref):
    t = q_ref.shape[0]
    c = GLA_CHUNK
    nchunk = t // c
    z = jnp.dot(glr_ref[...].astype(BF16), wup_ref[...], preferred_element_type=F32) + bgk_ref[...]
    la_ref[...] = (jnp.minimum(z, 0.0) - jnp.log(1.0 + jnp.exp(-jnp.abs(z)))) * (1.0 / GLA_GATE_NORM)

    ii = lax.broadcasted_iota(jnp.int32, (c, c), 0)
    jj = lax.broadcasted_iota(jnp.int32, (c, c), 1)
    causal = jj <= ii
    tri = causal.astype(BF16)
    nt = (((1,), (1,)), ((), ()))
    tn = (((0,), (0,)), ((), ()))

    def chunk_rows(n):
        return pl.ds(pl.multiple_of(n * c, c), c)

    def decays(n, carry):
        rows = chunk_rows(n)
        la = la_ref[rows, :]
        hi = la.astype(BF16)
        r1 = la - hi.astype(F32)
        mid = r1.astype(BF16)
        lo = (r1 - mid.astype(F32)).astype(BF16)
        parts = jnp.dot(tri, jnp.concatenate([hi, mid, lo], axis=1), preferred_element_type=F32)
        bcum = parts[:, 0:GLA_DK] + parts[:, GLA_DK:2 * GLA_DK] + parts[:, 2 * GLA_DK:]
        b_mid = bcum[c // 2 - 1:c // 2, :]
        b_last = bcum[c - 1:c, :]
        q = q_ref[rows, :].astype(F32) * (GLA_DK ** -0.5)
        k = k_ref[rows, :].astype(F32)
        qi_ref[rows, :] = (q * jnp.exp(bcum - b_mid)).astype(BF16)
        ki_ref[rows, :] = (k * jnp.exp(b_mid - bcum)).astype(BF16)
        qd_ref[rows, :] = (q * jnp.exp(bcum)).astype(BF16)
        kd_ref[rows, :] = (k * jnp.exp(b_last - bcum)).astype(BF16)
        dec_ref[n] = jnp.broadcast_to(jnp.exp(b_last), dec_ref.shape[1:])
        return carry

    lax.fori_loop(0, nchunk, decays, 0, unroll=4)

    def scores(n, carry):
        rows = chunk_rows(n)
        a = lax.dot_general(qi_ref[rows, :], ki_ref[rows, :], nt, preferred_element_type=F32)
        a_ref[rows, :] = jnp.where(causal, a, 0.0).astype(BF16)
        return carry

    lax.fori_loop(0, nchunk, scores, 0, unroll=8)

    def intra(n, carry):
        rows = chunk_rows(n)
        v = v_ref[rows, :]
        oi_ref[rows, :] = jnp.dot(a_ref[rows, :], v, preferred_element_type=F32)
        kv_ref[n] = lax.dot_general(v, kd_ref[rows, :], tn, preferred_element_type=F32)
        return carry

    lax.fori_loop(0, nchunk, intra, 0, unroll=8)

    def scan(n, s_t):
        sp_ref[n] = s_t.astype(BF16)
        return s_t * dec_ref[n][0:1, :] + kv_ref[n]

    lax.fori_loop(0, nchunk, scan, jnp.zeros((GLA_DV, GLA_DK), F32))

    def inter(n, carry):
        rows = chunk_rows(n)
        o = oi_ref[rows, :] + lax.dot_general(qd_ref[rows, :], sp_ref[n], nt,
                                              preferred_element_type=F32)
        ms = jnp.mean(o * o, axis=-1, keepdims=True)
        o = o * lax.rsqrt(ms + RMS_EPS) * ng_ref[...]
        o = o * _silu(r_ref[rows, :].astype(F32))
        o_ref[rows, :] = o.astype(BF16)
        return carry

    lax.fori_loop(0, nchunk, inter, 0, unroll=8)


def _gla(proj, glr, wup, bgk, ng, batch, seq):
    m = proj.shape[0]
    return pl.pallas_call(
        _gla_kernel,
        grid=(batch, GLA_HEADS),
        in_specs=[
            pl.BlockSpec((seq, GLA_DK), lambda b, h: (b, OFF_GQ // GLA_DK + h)),
            pl.BlockSpec((seq, GLA_DK), lambda b, h: (b, OFF_GK // GLA_DK + h)),
            pl.BlockSpec((seq, GLA_DV), lambda b, h: (b, OFF_GV // GLA_DV + h)),
            pl.BlockSpec((seq, GLA_DV), lambda b, h: (b, OFF_GR // GLA_DV + h)),
            pl.BlockSpec((seq, LANES), lambda b, h: (b, 0)),
            pl.BlockSpec((LANES, GLA_DK), lambda b, h: (0, h)),
            pl.BlockSpec((1, GLA_DK), lambda b, h: (0, h)),
            pl.BlockSpec((1, GLA_DV), lambda b, h: (0, h)),
        ],
        out_specs=pl.BlockSpec((seq, GLA_DV), lambda b, h: (b, h)),
        out_shape=jax.ShapeDtypeStruct((m, GLA_V_W), BF16),
        scratch_shapes=[
            pltpu.VMEM((seq, GLA_DK), F32),
            pltpu.VMEM((seq, GLA_DV), F32),
            pltpu.VMEM((seq, GLA_DK), BF16),
            pltpu.VMEM((seq, GLA_DK), BF16),
            pltpu.VMEM((seq, GLA_DK), BF16),
            pltpu.VMEM((seq, GLA_DK), BF16),
            pltpu.VMEM((seq, GLA_CHUNK), BF16),
            pltpu.VMEM((seq // GLA_CHUNK, GLA_DV, GLA_DK), F32),
            pltpu.VMEM((seq // GLA_CHUNK, 8, GLA_DK), F32),
            pltpu.VMEM((seq // GLA_CHUNK, GLA_DV, GLA_DK), BF16),
        ],
        compiler_params=pltpu.CompilerParams(
            dimension_semantics=("arbitrary", "arbitrary"), vmem_limit_bytes=VMEM_LIMIT),
        name="gla",
    )(proj, proj, proj, proj, glr, wup, bgk, ng)


def _swa_kernel(sink_ref, q_ref, kp_ref, kc_ref, vp_ref, vc_ref, o_ref):
    blk = SWA_BLOCK
    half = SWA_HEAD_DIM
    n = pl.program_id(1)
    k_all = jnp.concatenate([kp_ref[...], kc_ref[...]], axis=0)
    v_all = jnp.concatenate([vp_ref[...], vc_ref[...]], axis=0)
    qi = lax.broadcasted_iota(jnp.int32, (blk, 2 * blk), 0)
    kj = lax.broadcasted_iota(jnp.int32, (blk, 2 * blk), 1)
    rel = qi + blk - kj
    valid = (rel >= 0) & (rel < blk) & ((kj >= blk) | (n > 0))
    sink_col = kj == 0
    lane_kv = lax.broadcasted_iota(jnp.int32, (2 * blk, LANES), 1)
    lo_kv = lane_kv < half
    lane_q = lax.broadcasted_iota(jnp.int32, (blk, LANES), 1)
    lo_q = lane_q < half
    nt = (((1,), (1,)), ((), ()))
    key_idx = lax.broadcasted_iota(jnp.int32, (2 * blk, LANES), 0).astype(F32)
    key_cols = jnp.where((lane_kv & (half - 1)) < ALIBI_PARTS, key_idx, 0.0)
    q_pos = lax.broadcasted_iota(jnp.int32, (blk, 1), 0).astype(F32) + float(blk)
    q_scale = jnp.asarray(SWA_HEAD_DIM ** -0.5, BF16)

    for p in range(SWA_KV_HEADS // 2):
        k_slab = k_all[:, p * LANES:(p + 1) * LANES].astype(F32)
        v_slab = v_all[:, p * LANES:(p + 1) * LANES].astype(F32)
        k_roll = pltpu.roll(k_slab, half, 1)
        v_roll = pltpu.roll(v_slab, half, 1)
        for hh in range(2):
            h = 2 * p + hh
            k_lo, k_hi = (k_slab, k_roll) if hh == 0 else (k_roll, k_slab)
            kd = (jnp.where(lo_kv, k_lo, key_cols).astype(BF16),
                  jnp.where(lo_kv, key_cols, k_hi).astype(BF16))
            v_lo, v_hi = (v_slab, v_roll) if hh == 0 else (v_roll, v_slab)
            vd = jnp.where(lo_kv, v_lo, v_hi)
            vd = jnp.where(key_idx == 0.0, 0.0, vd).astype(BF16)
            for gp in range(SWA_GROUP // 2):
                col = (h * (SWA_GROUP // 2) + gp) * LANES
                qs = q_ref[:, col:col + LANES] * q_scale
                outs = []
                for gg in range(2):
                    head = h * SWA_GROUP + 2 * gp + gg
                    parts = _bf16_parts(2.0 ** (-8.0 * (head + 1) / SWA_Q_HEADS), ALIBI_PARTS)
                    slope = sum(parts)
                    aug = jnp.zeros((blk, LANES), F32)
                    for idx, part in enumerate(parts):
                        aug = jnp.where(lane_q == (half if gg == 0 else 0) + idx, part, aug)
                    keep = lo_q if gg == 0 else jnp.logical_not(lo_q)
                    qm = jnp.where(keep, qs, aug.astype(BF16))
                    s = lax.dot_general(qm, kd[gg], nt, preferred_element_type=F32)
                    sink = sink_ref[head] + slope * q_pos
                    s = jnp.where(sink_col, sink, jnp.where(valid, s, -jnp.inf))
                    mx = jnp.max(s, axis=-1, keepdims=True)
                    pe = jnp.exp(s - mx)
                    den = jnp.sum(pe, axis=-1, keepdims=True)
                    o = jnp.dot(pe.astype(BF16), vd, preferred_element_type=F32)
                    outs.append(o / den)
                o_ref[:, col:col + LANES] = jnp.where(lo_q, outs[0], outs[1]).astype(BF16)


def _swa(proj, sinks, batch, seq):
    m = proj.shape[0]
    nb = seq // SWA_BLOCK
    qcol = OFF_SQ // SWA_Q_W
    kcol = OFF_SK // SWA_KV_W
    vcol = OFF_SV // SWA_KV_W
    cur = lambda c: (lambda b, n: (b * nb + n, c))
    prev = lambda c: (lambda b, n: (b * nb + jnp.maximum(n - 1, 0), c))
    return pl.pallas_call(
        _swa_kernel,
        grid=(batch, nb),
        in_specs=[
            pl.BlockSpec(memory_space=pltpu.SMEM),
            pl.BlockSpec((SWA_BLOCK, SWA_Q_W), cur(qcol)),
            pl.BlockSpec((SWA_BLOCK, SWA_KV_W), prev(kcol)),
            pl.BlockSpec((SWA_BLOCK, SWA_KV_W), cur(kcol)),
            pl.BlockSpec((SWA_BLOCK, SWA_KV_W), prev(vcol)),
            pl.BlockSpec((SWA_BLOCK, SWA_KV_W), cur(vcol)),
        ],
        out_specs=pl.BlockSpec((SWA_BLOCK, SWA_Q_W), lambda b, n: (b * nb + n, 0)),
        out_shape=jax.ShapeDtypeStruct((m, SWA_Q_W), BF16),
        compiler_params=pltpu.CompilerParams(
            dimension_semantics=("arbitrary", "arbitrary"), vmem_limit_bytes=VMEM_LIMIT),
        name="swa",
    )(sinks, proj, proj, proj, proj, proj)


def _out_route_kernel(og_ref, os_ref, x_ref, wo_ref, g_ref, wr_ref, br_ref,
                      h_ref, hn_ref, route_ref, route_t_ref, cnt_ref, hnb_ref):
    t = pl.program_id(0)
    slot = lax.rem(t, 2)

    @pl.when(t == 0)
    def _():
        hnb_ref[...] = jnp.zeros_like(hnb_ref)

    logits = jnp.dot(hnb_ref[1 - slot], wr_ref[...], preferred_element_type=F32) + br_ref[...]

    h = x_ref[...]
    h = h + jnp.dot(og_ref[...], wo_ref[0:GLA_V_W, :], preferred_element_type=F32)
    h = h + jnp.dot(os_ref[...], wo_ref[GLA_V_W:, :], preferred_element_type=F32)
    h_ref[...] = h
    ms = jnp.mean(h * h, axis=-1, keepdims=True)
    hn = h * lax.rsqrt(ms + RMS_EPS) * g_ref[...]
    hn_ref[...] = hn
    hnb_ref[slot] = hn.astype(BF16)

    lane = lax.broadcasted_iota(jnp.int32, logits.shape, 1)
    lanef = lane.astype(F32)
    big = float(LANES)
    ninf = -jnp.inf
    gl = jnp.where(lane < N_GROUPS, logits, ninf)
    gmax = jnp.max(gl, axis=-1, keepdims=True)
    g_p = 1.0 / jnp.sum(jnp.exp(gl - gmax), axis=-1, keepdims=True)
    g_idx = jnp.min(jnp.where(gl == gmax, lanef, big), axis=-1, keepdims=True)
    lo = N_GROUPS + EXPERTS_PER_GROUP * g_idx
    el = jnp.where((lanef >= lo) & (lanef < lo + EXPERTS_PER_GROUP), logits, ninf)
    m1 = jnp.max(el, axis=-1, keepdims=True)
    i1 = jnp.min(jnp.where(el == m1, lanef, big), axis=-1, keepdims=True)
    el2 = jnp.where(lanef == i1, ninf, el)
    m2 = jnp.max(el2, axis=-1, keepdims=True)
    i2 = jnp.min(jnp.where(el2 == m2, lanef, big), axis=-1, keepdims=True)
    d = jnp.exp(m2 - m1)
    c1 = g_p / (1.0 + d)
    c2 = g_p * d / (1.0 + d)
    tm = logits.shape[0]
    chosen = ((lanef == i1) | (lanef == i2)).astype(BF16)
    ri = lax.broadcasted_iota(jnp.int32, (tm, tm), 0)
    ci = lax.broadcasted_iota(jnp.int32, (tm, tm), 1)
    earlier = jnp.dot((ci < ri).astype(BF16), chosen, preferred_element_type=F32)
    r1 = jnp.sum(jnp.where(lanef == i1, earlier, 0.0), axis=-1, keepdims=True)
    r2 = jnp.sum(jnp.where(lanef == i2, earlier, 0.0), axis=-1, keepdims=True)
    cnt = jnp.sum(chosen.astype(F32), axis=0, keepdims=True)
    cnt_ref[...] = jnp.broadcast_to(cnt, cnt_ref.shape)

    fields = [i1 - N_GROUPS, i2 - N_GROUPS, c1, c2, r1, r2]
    route = jnp.zeros_like(logits)
    for idx, val in enumerate(fields):
        route = jnp.where(lane == idx, val, route)
    route_ref[...] = route[:, 0:ROUTE_W]
    route_t_ref[...] = route.T[0:ROUTE_W, :][None]


def _out_route(o_gla, o_swa, x2, w_out, g, w_rt, b_rt):
    m = x2.shape[0]
    tm = min(OUT_TM, m)
    ntile = m // tm
    row = lambda i: (jnp.minimum(i, ntile - 1), 0)
    routed = lambda i: (jnp.maximum(i - 1, 0), 0)
    fixed = lambda i: (0, 0)
    return pl.pallas_call(
        _out_route_kernel,
        grid=(ntile + 1,),
        in_specs=[
            pl.BlockSpec((tm, GLA_V_W), row),
            pl.BlockSpec((tm, SWA_Q_W), row),
            pl.BlockSpec((tm, D_MODEL), row),
            pl.BlockSpec((GLA_V_W + SWA_Q_W, D_MODEL), fixed),
            pl.BlockSpec((1, D_MODEL), fixed),
            pl.BlockSpec((D_MODEL, LANES), fixed),
            pl.BlockSpec((1, LANES), fixed),
        ],
        out_specs=[
            pl.BlockSpec((tm, D_MODEL), row),
            pl.BlockSpec((tm, D_MODEL), row),
            pl.BlockSpec((tm, ROUTE_W), routed),
            pl.BlockSpec((1, ROUTE_W, tm), lambda i: (jnp.maximum(i - 1, 0), 0, 0)),
            pl.BlockSpec((1, 8, LANES), lambda i: (jnp.maximum(i - 1, 0), 0, 0)),
        ],
        out_shape=[
            jax.ShapeDtypeStruct((m, D_MODEL), F32),
            jax.ShapeDtypeStruct((m, D_MODEL), F32),
            jax.ShapeDtypeStruct((m, ROUTE_W), F32),
            jax.ShapeDtypeStruct((m // tm, ROUTE_W, tm), F32),
            jax.ShapeDtypeStruct((m // tm, 8, LANES), F32),
        ],
        scratch_shapes=[pltpu.VMEM((2, tm, D_MODEL), BF16)],
        compiler_params=pltpu.CompilerParams(
            dimension_semantics=("arbitrary",), vmem_limit_bytes=VMEM_LIMIT),
        name="out_route",
    )(o_gla, o_swa, x2, w_out, g, w_rt, b_rt)


LANES_LOG2 = 7
MOE_ROW_GROUP_LOG2 = 3
MOE_ROW_GROUP = 1 << MOE_ROW_GROUP_LOG2
MOE_GATHER_GROUP_LOG2 = 4
MOE_GATHER_GROUP = 1 << MOE_GATHER_GROUP_LOG2
MOE_TAIL_ROWS = N_EXPERTS * MOE_ROW_GROUP


def _moe_kernel(te_ref, nv_ref, ne_ref, yb_ref, meta_ref, tok_hbm, hn_hbm, wg_hbm, wu_hbm, wd_hbm,
                y_hbm, xbuf, gsem, ids_smem, isem, wgs, wus, wds, wsem, wgb, wub, wdb, ybuf, ysem):
    tr = MOE_TR
    grp = MOE_ROW_GROUP
    t = pl.program_id(0)
    nt = meta_ref[0]
    gs = lax.rem(t, 2)
    cs = 1 - gs

    def ids_copy(tile, s):
        return pltpu.make_async_copy(tok_hbm.at[tile], ids_smem.at[s], isem.at[s])

    def gather_rows(tile):
        groups = lax.shift_right_logical(nv_ref[tile] + (MOE_GATHER_GROUP - 1), MOE_GATHER_GROUP_LOG2)
        return lax.shift_left(groups, MOE_GATHER_GROUP_LOG2)

    def weight_copies(e):
        return (pltpu.make_async_copy(wg_hbm.at[e], wgs, wsem.at[0]),
                pltpu.make_async_copy(wu_hbm.at[e], wus, wsem.at[1]),
                pltpu.make_async_copy(wd_hbm.at[e], wds, wsem.at[2]))

    def rows_used(tile):
        groups = lax.shift_right_logical(nv_ref[tile] + (grp - 1), MOE_ROW_GROUP_LOG2)
        return lax.shift_left(groups, MOE_ROW_GROUP_LOG2)

    def y_copy(tile, s):
        n_rows = pl.multiple_of(rows_used(tile), grp)
        dst = y_hbm.at[pl.ds(pl.multiple_of(yb_ref[tile], grp), n_rows)]
        return pltpu.make_async_copy(ybuf.at[s, pl.ds(0, n_rows)], dst, ysem.at[s])

    @pl.when(t == 0)
    def _():
        xbuf[...] = jnp.zeros_like(xbuf)
        ids_copy(0, 0).start()
        for cp in weight_copies(te_ref[0]):
            cp.start()

    @pl.when(t < nt)
    def _():
        ids_copy(t, gs).wait()

        @pl.when(t + 1 < nt)
        def _():
            ids_copy(t + 1, cs).start()

        n_real = nv_ref[t]
        for slot in range(2):
            @pl.when(gs == slot)
            def _():
                for g0 in range(0, tr, MOE_GATHER_GROUP):
                    @pl.when(g0 < n_real)
                    def _():
                        for r in range(g0, g0 + MOE_GATHER_GROUP):
                            tok = ids_smem[slot, r // LANES, r % LANES]
                            pltpu.make_async_copy(hn_hbm.at[pl.ds(tok, 1)], xbuf.at[slot, pl.ds(r, 1)],
                                                  gsem.at[slot]).start(priority=r % 2)

    @pl.when((t >= 1) & (t <= nt))
    def _():
        c = t - 1
        changed = (c == 0) | (te_ref[c] != te_ref[jnp.maximum(c - 1, 0)])

        @pl.when(changed)
        def _():
            for cp in weight_copies(te_ref[c]):
                cp.wait()
            wgb[...] = wgs[...].astype(BF16)
            wub[...] = wus[...].astype(BF16)
            wdb[...] = wds[...].astype(BF16)
            nxt = ne_ref[c]

            @pl.when(nxt >= 0)
            def _():
                for cp in weight_copies(nxt):
                    cp.start()

        n_rows = pl.multiple_of(gather_rows(c), MOE_GATHER_GROUP)
        pltpu.make_async_copy(hn_hbm.at[pl.ds(0, n_rows)], xbuf.at[cs, pl.ds(0, n_rows)],
                              gsem.at[cs]).wait()
        x = xbuf[cs].astype(BF16)
        g = jnp.dot(x, wgb[...], preferred_element_type=F32)
        u = jnp.dot(x, wub[...], preferred_element_type=F32)
        hmid = (_silu(g) * u).astype(BF16)

        @pl.when(c >= 2)
        def _():
            y_copy(c - 2, cs).wait()

        ybuf[cs] = jnp.dot(hmid, wdb[...], preferred_element_type=F32)
        y_copy(c, cs).start()

        @pl.when(t == nt)
        def _():
            @pl.when(c >= 1)
            def _():
                y_copy(c - 1, gs).wait()

            y_copy(c, cs).wait()
            ybuf[gs] = jnp.zeros((tr, D_MODEL), F32)
            tail0, n_tail = meta_ref[1], meta_ref[2]
            for k in range(MOE_TAIL_ROWS // tr):
                n_k = pl.multiple_of(jnp.clip(n_tail - k * tr, 0, tr), grp)

                @pl.when(n_k > 0)
                def _():
                    dst = y_hbm.at[pl.ds(pl.multiple_of(tail0 + k * tr, grp), n_k)]
                    cp = pltpu.make_async_copy(ybuf.at[gs, pl.ds(0, n_k)], dst, ysem.at[gs])
                    cp.start()
                    cp.wait()


def _moe(hn, w_gate, w_up, w_down, plan):
    n_tiles = plan["tile_expert"].shape[0]
    y_rows = 2 * hn.shape[0] + MOE_TAIL_ROWS
    any_spec = pl.BlockSpec(memory_space=pl.ANY)
    grid_spec = pltpu.PrefetchScalarGridSpec(
        num_scalar_prefetch=5,
        grid=(n_tiles + 1,),
        in_specs=[any_spec] * 5,
        out_specs=any_spec,
        scratch_shapes=[
            pltpu.VMEM((2, MOE_TR, D_MODEL), F32),
            pltpu.SemaphoreType.DMA((2,)),
            pltpu.SMEM((2, MOE_TR // LANES, LANES), jnp.int32),
            pltpu.SemaphoreType.DMA((2,)),
            pltpu.VMEM((D_MODEL, D_EXPERT), F32),
            pltpu.VMEM((D_MODEL, D_EXPERT), F32),
            pltpu.VMEM((D_EXPERT, D_MODEL), F32),
            pltpu.SemaphoreType.DMA((3,)),
            pltpu.VMEM((D_MODEL, D_EXPERT), BF16),
            pltpu.VMEM((D_MODEL, D_EXPERT), BF16),
            pltpu.VMEM((D_EXPERT, D_MODEL), BF16),
            pltpu.VMEM((2, MOE_TR, D_MODEL), F32),
            pltpu.SemaphoreType.DMA((2,)),
        ],
    )
    return pl.pallas_call(
        _moe_kernel,
        grid_spec=grid_spec,
        out_shape=jax.ShapeDtypeStruct((y_rows, D_MODEL), F32),
        compiler_params=pltpu.CompilerParams(
            dimension_semantics=("arbitrary",), vmem_limit_bytes=VMEM_LIMIT),
        name="moe",
    )(plan["tile_expert"], plan["tile_rows"], plan["next_expert"], plan["tile_out"],
      plan["meta"], plan["tile_tokens"], hn, w_gate, w_up, w_down)


def _combine_kernel(pos_hbm, y_hbm, h_ref, route_ref, g_ref, o_ref, ybuf, gsem, ids_smem, isem):
    tm = h_ref.shape[0]
    t = pl.program_id(0)
    ntile = pl.num_programs(0) - 1

    def ids_copy(tile, s):
        return pltpu.make_async_copy(pos_hbm.at[tile], ids_smem.at[s], isem.at[s])

    @pl.when(t == 0)
    def _():
        ids_copy(0, 0).start()

    @pl.when(t < ntile)
    def _():
        s = lax.rem(t, 2)
        ids_copy(t, s).wait()

        @pl.when(t + 1 < ntile)
        def _():
            ids_copy(t + 1, 1 - s).start()

        for slot in range(2):
            @pl.when(s == slot)
            def _():
                for kk in range(2):
                    for r in range(tm):
                        flat = kk * tm + r
                        row = ids_smem[slot, flat // LANES, flat % LANES]
                        pltpu.make_async_copy(y_hbm.at[pl.ds(row, 1)], ybuf.at[slot, kk, pl.ds(r, 1)],
                                              gsem.at[slot, kk]).start(priority=r % 2)

    @pl.when(t >= 1)
    def _():
        s = lax.rem(t - 1, 2)
        for kk in range(2):
            pltpu.make_async_copy(y_hbm.at[pl.ds(0, tm)], ybuf.at[s, kk], gsem.at[s, kk]).wait()
        route = route_ref[...]
        h = h_ref[...] + route[:, 2:3] * ybuf[s, 0] + route[:, 3:4] * ybuf[s, 1]
        ms = jnp.mean(h * h, axis=-1, keepdims=True)
        o_ref[...] = h * lax.rsqrt(ms + RMS_EPS) * g_ref[...]


def _combine(y, h, route, g, pos3):
    m = h.shape[0]
    tm = min(CMB_TM, m)
    ntile = m // tm
    assert pos3.shape == (ntile, 2 * tm // LANES, LANES)
    row = lambda i: (jnp.maximum(i - 1, 0), 0)
    return pl.pallas_call(
        _combine_kernel,
        grid=(ntile + 1,),
        in_specs=[
            pl.BlockSpec(memory_space=pl.ANY),
            pl.BlockSpec(memory_space=pl.ANY),
            pl.BlockSpec((tm, D_MODEL), row),
            pl.BlockSpec((tm, ROUTE_W), row),
            pl.BlockSpec((1, D_MODEL), lambda i: (0, 0)),
        ],
        out_specs=pl.BlockSpec((tm, D_MODEL), row),
        out_shape=jax.ShapeDtypeStruct((m, D_MODEL), F32),
        scratch_shapes=[
            pltpu.VMEM((2, 2, tm, D_MODEL), F32),
            pltpu.SemaphoreType.DMA((2, 2)),
            pltpu.SMEM((2, 2 * tm // LANES, LANES), jnp.int32),
            pltpu.SemaphoreType.DMA((2,)),
        ],
        compiler_params=pltpu.CompilerParams(
            dimension_semantics=("arbitrary",), vmem_limit_bytes=VMEM_LIMIT),
        name="combine",
    )(pos3, y, h, route, g)


def _dispatch_plan(route_t, tile_counts, m):
    tr = MOE_TR
    p_rows = 2 * m + N_EXPERTS * tr
    n_tiles = p_rows // tr
    n_tok_tiles, _, tm = route_t.shape
    cnt = tile_counts[:, 0, N_GROUPS:N_GROUPS + N_EXPERTS].astype(jnp.int32)
    before_tile = jnp.cumsum(cnt, axis=0) - cnt
    counts = jnp.sum(cnt, axis=0)
    padded = ((counts + tr - 1) // tr) * tr
    ends = jnp.cumsum(padded)
    starts = ends - padded
    grp = MOE_ROW_GROUP
    y_len = ((counts + grp - 1) // grp) * grp
    y_starts = jnp.cumsum(y_len) - y_len
    base = (y_starts[None, :] + before_tile).astype(jnp.int32)
    fields = route_t.astype(jnp.int32)
    experts = (fields[:, 0, :], fields[:, 1, :])
    ranks = (fields[:, 4, :], fields[:, 5, :])
    expert_ids = jnp.arange(N_EXPERTS, dtype=jnp.int32)
    rows = [rk + jnp.sum(jnp.where(ex[..., None] == expert_ids, base[:, None, :], 0), axis=-1)
            for ex, rk in zip(experts, ranks)]
    ctm = min(CMB_TM, m)
    pos = jnp.concatenate([r.reshape(m // ctm, ctm) for r in rows], axis=1)
    pos = pos.reshape(m // ctm, 2 * ctm // LANES, LANES)
    num_tiles = (ends[-1] // tr).astype(jnp.int32)
    tile_idx = jnp.minimum(jnp.arange(n_tiles, dtype=jnp.int32), num_tiles - 1)
    tile_expert = jnp.sum((tile_idx[:, None] * tr >= ends[None, :]).astype(jnp.int32), axis=1)
    n_slots = 2 * m
    n_fill = n_tiles * tr - n_slots
    stride = 1 << (n_slots + n_fill - 1).bit_length()
    assert (N_EXPERTS + 1) * stride < 2 ** 31
    tok2 = 2 * jnp.arange(m, dtype=jnp.int32).reshape(n_tok_tiles, tm)
    keys = [(ex * stride + tok2 + k).reshape(-1) for k, ex in enumerate(experts)]
    fill = jnp.arange(n_fill, dtype=jnp.int32)
    fill_expert = jnp.sum((fill[:, None] >= jnp.cumsum(padded - counts)[None, :]).astype(jnp.int32), axis=1)
    keys.append(fill_expert * stride + n_slots + fill)
    sorted_low = jnp.sort(jnp.concatenate(keys)) & (stride - 1)
    tile_tokens = jnp.where(sorted_low < n_slots, sorted_low >> 1, 0).reshape(n_tiles, tr // LANES, LANES)
    in_expert = tile_idx - (starts // tr)[tile_expert]
    tile_rows = jnp.clip(counts[tile_expert] - in_expert * tr, 0, tr)
    tile_rows = jnp.where(jnp.arange(n_tiles) < num_tiles, tile_rows, 0)
    ids = jnp.arange(N_EXPERTS, dtype=jnp.int32)
    present = jnp.where(counts > 0, ids, N_EXPERTS)
    next_ge = lax.cummin(present, axis=0, reverse=True)
    next_gt = jnp.concatenate([next_ge[1:], jnp.full((1,), N_EXPERTS, jnp.int32)])
    next_gt = jnp.where(next_gt >= N_EXPERTS, -1, next_gt)
    tile_out = y_starts[tile_expert] + in_expert * tr
    y_used = jnp.sum(y_len)
    meta = jnp.stack([num_tiles, y_used, 2 * m + MOE_TAIL_ROWS - y_used]).astype(jnp.int32)
    plan = dict(tile_expert=tile_expert.astype(jnp.int32),
                tile_rows=tile_rows.astype(jnp.int32), next_expert=next_gt[tile_expert].astype(jnp.int32),
                tile_out=tile_out.astype(jnp.int32), meta=meta, tile_tokens=tile_tokens)
    return plan, pos


def kernel(x, norm_mix_g, w_in, w_gk_up, b_gk, gla_norm_g, swa_sinks, w_out, norm_ffn_g,
           w_group, b_group, w_router, b_router, w_gate, w_up, w_down, norm_final_g):
    batch, seq, d = x.shape
    m = batch * seq
    assert w_in.shape[0] == 1, "single-layer block"
    x2 = x.reshape(m, d)
    w_main, w_lr = _w_prep(jnp.transpose(w_in[0]))
    wup = jnp.pad(w_gk_up[0], ((0, LANES - GLA_RANK), (0, 0))).astype(BF16)
    w_rt = jnp.pad(jnp.concatenate([w_group[0], w_router[0]], axis=1),
                   ((0, 0), (0, LANES - N_GROUPS - N_EXPERTS))).astype(BF16)
    b_rt = jnp.pad(jnp.concatenate([b_group[0], b_router[0]]),
                   (0, LANES - N_GROUPS - N_EXPERTS)).reshape(1, LANES)

    proj, glr = _in_proj(x2, norm_mix_g[0].reshape(1, d), w_main, w_lr)
    o_gla = _gla(proj, glr, wup, b_gk[0].reshape(1, GLA_QK_W),
                 gla_norm_g[0].reshape(1, GLA_V_W), batch, seq)
    o_swa = _swa(proj, swa_sinks[0], batch, seq)
    h_mid, hn, route, route_t, tile_counts = _out_route(o_gla, o_swa, x2, w_out[0].astype(BF16),
                                                        norm_ffn_g[0].reshape(1, d), w_rt, b_rt)
    plan, pos3 = _dispatch_plan(route_t, tile_counts, m)
    y = _moe(hn, w_gate[0], w_up[0], w_down[0], plan)
    out = _combine(y, h_mid, route, norm_final_g.reshape(1, d), pos3)
    return out.reshape(batch, seq, d)
```

```python
import functools

import jax
import jax.numpy as jnp
import numpy as np
from jax import lax
from jax.experimental import pallas as pl
from jax.experimental.pallas import tpu as pltpu

F32 = jnp.float32
BF16 = jnp.bfloat16

D_MODEL = 2048
GLA_HEADS = 4
GLA_DK = 128
GLA_DV = 256
GLA_RANK = 16
GLA_GATE_NORM = 16.0
GLA_CHUNK = 64
SWA_Q_HEADS = 16
SWA_KV_HEADS = 4
SWA_GROUP = SWA_Q_HEADS // SWA_KV_HEADS
SWA_HEAD_DIM = 64
SWA_BLOCK = 128
N_GROUPS = 4
EXPERTS_PER_GROUP = 16
N_EXPERTS = N_GROUPS * EXPERTS_PER_GROUP
D_EXPERT = 256
RMS_EPS = 1e-6

GLA_QK_W = GLA_HEADS * GLA_DK
GLA_V_W = GLA_HEADS * GLA_DV
SWA_Q_W = SWA_Q_HEADS * SWA_HEAD_DIM
SWA_KV_W = SWA_KV_HEADS * SWA_HEAD_DIM
OFF_GQ = 0
OFF_GK = OFF_GQ + GLA_QK_W
OFF_GV = OFF_GK + GLA_QK_W
OFF_GR = OFF_GV + GLA_V_W
OFF_SQ = OFF_GR + GLA_V_W
OFF_SK = OFF_SQ + SWA_Q_W
OFF_SV = OFF_SK + SWA_KV_W
PROJ_W = OFF_SV + SWA_KV_W
LANES = 128

IN_TM = 512
IN_TN = 4608
OUT_TM = 512
ROUTE_W = 8
MOE_TR = 256
CMB_TM = 256
VMEM_LIMIT = 56 * 1024 * 1024


def _silu(x):
    return x / (1.0 + jnp.exp(-x))


ALIBI_PARTS = 3


def _bf16_parts(value, n):
    parts, rest = [], np.float32(value)
    for _ in range(n):
        piece = np.float32(np.asarray(rest).astype(jnp.bfloat16))
        parts.append(float(piece))
        rest = np.float32(rest - piece)
    return parts


W_PREP_ROWS = 384
OFF_LR = OFF_GR + GLA_V_W


def _wprep_kernel(wt_hbm, wm_ref, wl_ref, buf, lrbuf, sem, lrsem):
    rows = W_PREP_ROWS
    i = pl.program_id(0)
    n = pl.num_programs(0)
    n_lo = OFF_LR // rows
    slot = lax.rem(i, 2)

    def block_copy(step, s):
        start = jnp.where(step < n_lo, step * rows, OFF_LR + GLA_RANK + (step - n_lo) * rows)
        return pltpu.make_async_copy(wt_hbm.at[pl.ds(pl.multiple_of(start, 8), rows)], buf.at[s], sem.at[s])

    @pl.when(i == 0)
    def _():
        block_copy(0, 0).start()
        lrbuf[...] = jnp.zeros_like(lrbuf)
        lr_copy = pltpu.make_async_copy(wt_hbm.at[pl.ds(OFF_LR, GLA_RANK)], lrbuf.at[pl.ds(0, GLA_RANK)],
                                        lrsem.at[0])
        lr_copy.start()
        lr_copy.wait()
        wl_ref[...] = lrbuf[...].T.astype(BF16)

    @pl.when(i + 1 < n)
    def _():
        block_copy(i + 1, 1 - slot).start()

    block_copy(i, slot).wait()
    wm_ref[...] = buf[slot].T.astype(BF16)


def _w_prep(w_t):
    n, k = w_t.shape
    assert OFF_LR % W_PREP_ROWS == 0 and (PROJ_W - OFF_LR) % W_PREP_ROWS == 0 and n == PROJ_W + GLA_RANK
    return pl.pallas_call(
        _wprep_kernel,
        grid=(PROJ_W // W_PREP_ROWS,),
        in_specs=[pl.BlockSpec(memory_space=pl.ANY)],
        out_specs=[pl.BlockSpec((k, W_PREP_ROWS), lambda i: (0, i)),
                   pl.BlockSpec((k, LANES), lambda i: (0, 0))],
        out_shape=[jax.ShapeDtypeStruct((k, PROJ_W), BF16), jax.ShapeDtypeStruct((k, LANES), BF16)],
        scratch_shapes=[
            pltpu.VMEM((2, W_PREP_ROWS, k), F32),
            pltpu.VMEM((LANES, k), F32),
            pltpu.SemaphoreType.DMA((2,)),
            pltpu.SemaphoreType.DMA((1,)),
        ],
        compiler_params=pltpu.CompilerParams(
            dimension_semantics=("arbitrary",), vmem_limit_bytes=VMEM_LIMIT),
        name="w_prep",
    )(w_t)


def _inproj_kernel(x_ref, g_ref, w_ref, wlr_ref, proj_ref, glr_ref, xn_ref):
    @pl.when(pl.program_id(1) == 0)
    def _():
        x = x_ref[...]
        ms = jnp.mean(x * x, axis=-1, keepdims=True)
        xn = (x * lax.rsqrt(ms + RMS_EPS) * g_ref[...]).astype(BF16)
        xn_ref[...] = xn
        glr_ref[...] = jnp.dot(xn, wlr_ref[...], preferred_element_type=F32)

    proj_ref[...] = jnp.dot(xn_ref[...], w_ref[...], preferred_element_type=F32).astype(BF16)


def _in_proj(x2, g, w_main, w_lr):
    m = x2.shape[0]
    tm = min(IN_TM, m)
    return pl.pallas_call(
        _inproj_kernel,
        grid=(m // tm, PROJ_W // IN_TN),
        in_specs=[
            pl.BlockSpec((tm, D_MODEL), lambda i, j: (i, 0)),
            pl.BlockSpec((1, D_MODEL), lambda i, j: (0, 0)),
            pl.BlockSpec((D_MODEL, IN_TN), lambda i, j: (0, j),
                         pipeline_mode=pl.Buffered(1) if IN_TN == PROJ_W else None),
            pl.BlockSpec((D_MODEL, LANES), lambda i, j: (0, 0)),
        ],
        out_specs=[
            pl.BlockSpec((tm, IN_TN), lambda i, j: (i, j)),
            pl.BlockSpec((tm, LANES), lambda i, j: (i, 0)),
        ],
        out_shape=[
            jax.ShapeDtypeStruct((m, PROJ_W), BF16),
            jax.ShapeDtypeStruct((m, LANES), F32),
        ],
        scratch_shapes=[pltpu.VMEM((tm, D_MODEL), BF16)],
        compiler_params=pltpu.CompilerParams(
            dimension_semantics=("arbitrary", "arbitrary"), vmem_limit_bytes=VMEM_LIMIT),
        name="in_proj",
    )(x2, g, w_main, w_lr)


def _gla_kernel(q_ref, k_ref, v_ref, r_ref, glr_ref, wup_ref, bgk_ref, ng_ref, o_ref,
                la_ref, oi_ref, qi_ref, ki_ref, qd_ref, kd_ref, a_ref, kv_ref, dec_ref, sp_ref):
    t = q_ref.shape[0]
    c = GLA_CHUNK
    nchunk = t // c
    z = jnp.dot(glr_ref[...].astype(BF16), wup_ref[...], preferred_element_type=F32) + bgk_ref[...]
    la_ref[...] = (jnp.minimum(z, 0.0) - jnp.log(1.0 + jnp.exp(-jnp.abs(z)))) * (1.0 / GLA_GATE_NORM)

    ii = lax.broadcasted_iota(jnp.int32, (c, c), 0)
    jj = lax.broadcasted_iota(jnp.int32, (c, c), 1)
    causal = jj <= ii
    tri = causal.astype(BF16)
    nt = (((1,), (1,)), ((), ()))
    tn = (((0,), (0,)), ((), ()))

    def chunk_rows(n):
        return pl.ds(pl.multiple_of(n * c, c), c)

    def decays(n, carry):
        rows = chunk_rows(n)
        la = la_ref[rows, :]
        hi = la.astype(BF16)
        r1 = la - hi.astype(F32)
        mid = r1.astype(BF16)
        lo = (r1 - mid.astype(F32)).astype(BF16)
        parts = jnp.dot(tri, jnp.concatenate([hi, mid, lo], axis=1), preferred_element_type=F32)
        bcum = parts[:, 0:GLA_DK] + parts[:, GLA_DK:2 * GLA_DK] + parts[:, 2 * GLA_DK:]
        b_mid = bcum[c // 2 - 1:c // 2, :]
        b_last = bcum[c - 1:c, :]
        q = q_ref[rows, :].astype(F32) * (GLA_DK ** -0.5)
        k = k_ref[rows, :].astype(F32)
        qi_ref[rows, :] = (q * jnp.exp(bcum - b_mid)).astype(BF16)
        ki_ref[rows, :] = (k * jnp.exp(b_mid - bcum)).astype(BF16)
        qd_ref[rows, :] = (q * jnp.exp(bcum)).astype(BF16)
        kd_ref[rows, :] = (k * jnp.exp(b_last - bcum)).astype(BF16)
        dec_ref[n] = jnp.broadcast_to(jnp.exp(b_last), dec_ref.shape[1:])
        return carry

    lax.fori_loop(0, nchunk, decays, 0, unroll=4)

    def scores(n, carry):
        rows = chunk_rows(n)
        a = lax.dot_general(qi_ref[rows, :], ki_ref[rows, :], nt, preferred_element_type=F32)
        a_ref[rows, :] = jnp.where(causal, a, 0.0).astype(BF16)
        return carry

    lax.fori_loop(0, nchunk, scores, 0, unroll=8)

    def intra(n, carry):
        rows = chunk_rows(n)
        v = v_ref[rows, :]
        oi_ref[rows, :] = jnp.dot(a_ref[rows, :], v, preferred_element_type=F32)
        kv_ref[n] = lax.dot_general(v, kd_ref[rows, :], tn, preferred_element_type=F32)
        return carry

    lax.fori_loop(0, nchunk, intra, 0, unroll=8)

    def scan(n, s_t):
        sp_ref[n] = s_t.astype(BF16)
        return s_t * dec_ref[n][0:1, :] + kv_ref[n]

    lax.fori_loop(0, nchunk, scan, jnp.zeros((GLA_DV, GLA_DK), F32))

    def inter(n, carry):
        rows = chunk_rows(n)
        o = oi_ref[rows, :] + lax.dot_general(qd_ref[rows, :], sp_ref[n], nt,
                                              preferred_element_type=F32)
        ms = jnp.mean(o * o, axis=-1, keepdims=True)
        o = o * lax.rsqrt(ms + RMS_EPS) * ng_ref[...]
        o = o * _silu(r_ref[rows, :].astype(F32))
        o_ref[rows, :] = o.astype(BF16)
        return carry

    lax.fori_loop(0, nchunk, inter, 0, unroll=8)


def _gla(proj, glr, wup, bgk, ng, batch, seq):
    m = proj.shape[0]
    return pl.pallas_call(
        _gla_kernel,
        grid=(batch, GLA_HEADS),
        in_specs=[
            pl.BlockSpec((seq, GLA_DK), lambda b, h: (b, OFF_GQ // GLA_DK + h)),
            pl.BlockSpec((seq, GLA_DK), lambda b, h: (b, OFF_GK // GLA_DK + h)),
            pl.BlockSpec((seq, GLA_DV), lambda b, h: (b, OFF_GV // GLA_DV + h)),
            pl.BlockSpec((seq, GLA_DV), lambda b, h: (b, OFF_GR // GLA_DV + h)),
            pl.BlockSpec((seq, LANES), lambda b, h: (b, 0)),
            pl.BlockSpec((LANES, GLA_DK), lambda b, h: (0, h)),
            pl.BlockSpec((1, GLA_DK), lambda b, h: (0, h)),
            pl.BlockSpec((1, GLA_DV), lambda b, h: (0, h)),
        ],
        out_specs=pl.BlockSpec((seq, GLA_DV), lambda b, h: (b, h)),
        out_shape=jax.ShapeDtypeStruct((m, GLA_V_W), BF16),
        scratch_shapes=[
            pltpu.VMEM((seq, GLA_DK), F32),
            pltpu.VMEM((seq, GLA_DV), F32),
            pltpu.VMEM((seq, GLA_DK), BF16),
            pltpu.VMEM((seq, GLA_DK), BF16),
            pltpu.VMEM((seq, GLA_DK), BF16),
            pltpu.VMEM((seq, GLA_DK), BF16),
            pltpu.VMEM((seq, GLA_CHUNK), BF16),
            pltpu.VMEM((seq // GLA_CHUNK, GLA_DV, GLA_DK), F32),
            pltpu.VMEM((seq // GLA_CHUNK, 8, GLA_DK), F32),
            pltpu.VMEM((seq // GLA_CHUNK, GLA_DV, GLA_DK), BF16),
        ],
        compiler_params=pltpu.CompilerParams(
            dimension_semantics=("arbitrary", "arbitrary"), vmem_limit_bytes=VMEM_LIMIT),
        name="gla",
    )(proj, proj, proj, proj, glr, wup, bgk, ng)


def _swa_kernel(sink_ref, q_ref, kp_ref, kc_ref, vp_ref, vc_ref, o_ref):
    blk = SWA_BLOCK
    half = SWA_HEAD_DIM
    n = pl.program_id(1)
    k_all = jnp.concatenate([kp_ref[...], kc_ref[...]], axis=0)
    v_all = jnp.concatenate([vp_ref[...], vc_ref[...]], axis=0)
    qi = lax.broadcasted_iota(jnp.int32, (blk, 2 * blk), 0)
    kj = lax.broadcasted_iota(jnp.int32, (blk, 2 * blk), 1)
    rel = qi + blk - kj
    valid = (rel >= 0) & (rel < blk) & ((kj >= blk) | (n > 0))
    sink_col = kj == 0
    lane_kv = lax.broadcasted_iota(jnp.int32, (2 * blk, LANES), 1)
    lo_kv = lane_kv < half
    lane_q = lax.broadcasted_iota(jnp.int32, (blk, LANES), 1)
    lo_q = lane_q < half
    nt = (((1,), (1,)), ((), ()))
    key_idx = lax.broadcasted_iota(jnp.int32, (2 * blk, LANES), 0).astype(F32)
    key_cols = jnp.where((lane_kv & (half - 1)) < ALIBI_PARTS, key_idx, 0.0)
    q_pos = lax.broadcasted_iota(jnp.int32, (blk, 1), 0).astype(F32) + float(blk)
    q_scale = jnp.asarray(SWA_HEAD_DIM ** -0.5, BF16)

    for p in range(SWA_KV_HEADS // 2):
        k_slab = k_all[:, p * LANES:(p + 1) * LANES].astype(F32)
        v_slab = v_all[:, p * LANES:(p + 1) * LANES].astype(F32)
        k_roll = pltpu.roll(k_slab, half, 1)
        v_roll = pltpu.roll(v_slab, half, 1)
        for hh in range(2):
            h = 2 * p + hh
            k_lo, k_hi = (k_slab, k_roll) if hh == 0 else (k_roll, k_slab)
            kd = (jnp.where(lo_kv, k_lo, key_cols).astype(BF16),
                  jnp.where(lo_kv, key_cols, k_hi).astype(BF16))
            v_lo, v_hi = (v_slab, v_roll) if hh == 0 else (v_roll, v_slab)
            vd = jnp.where(lo_kv, v_lo, v_hi)
            vd = jnp.where(key_idx == 0.0, 0.0, vd).astype(BF16)
            for gp in range(SWA_GROUP // 2):
                col = (h * (SWA_GROUP // 2) + gp) * LANES
                qs = q_ref[:, col:col + LANES] * q_scale
                outs = []
                for gg in range(2):
                    head = h * SWA_GROUP + 2 * gp + gg
                    parts = _bf16_parts(2.0 ** (-8.0 * (head + 1) / SWA_Q_HEADS), ALIBI_PARTS)
                    slope = sum(parts)
                    aug = jnp.zeros((blk, LANES), F32)
                    for idx, part in enumerate(parts):
                        aug = jnp.where(lane_q == (half if gg == 0 else 0) + idx, part, aug)
                    keep = lo_q if gg == 0 else jnp.logical_not(lo_q)
                    qm = jnp.where(keep, qs, aug.astype(BF16))
                    s = lax.dot_general(qm, kd[gg], nt, preferred_element_type=F32)
                    sink = sink_ref[head] + slope * q_pos
                    s = jnp.where(sink_col, sink, jnp.where(valid, s, -jnp.inf))
                    mx = jnp.max(s, axis=-1, keepdims=True)
                    pe = jnp.exp(s - mx)
                    den = jnp.sum(pe, axis=-1, keepdims=True)
                    o = jnp.dot(pe.astype(BF16), vd, preferred_element_type=F32)
                    outs.append(o / den)
                o_ref[:, col:col + LANES] = jnp.where(lo_q, outs[0], outs[1]).astype(BF16)


def _swa(proj, sinks, batch, seq):
    m = proj.shape[0]
    nb = seq // SWA_BLOCK
    qcol = OFF_SQ // SWA_Q_W
    kcol = OFF_SK // SWA_KV_W
    vcol = OFF_SV // SWA_KV_W
    cur = lambda c: (lambda b, n: (b * nb + n, c))
    prev = lambda c: (lambda b, n: (b * nb + jnp.maximum(n - 1, 0), c))
    return pl.pallas_call(
        _swa_kernel,
        grid=(batch, nb),
        in_specs=[
            pl.BlockSpec(memory_space=pltpu.SMEM),
            pl.BlockSpec((SWA_BLOCK, SWA_Q_W), cur(qcol)),
            pl.BlockSpec((SWA_BLOCK, SWA_KV_W), prev(kcol)),
            pl.BlockSpec((SWA_BLOCK, SWA_KV_W), cur(kcol)),
            pl.BlockSpec((SWA_BLOCK, SWA_KV_W), prev(vcol)),
            pl.BlockSpec((SWA_BLOCK, SWA_KV_W), cur(vcol)),
        ],
        out_specs=pl.BlockSpec((SWA_BLOCK, SWA_Q_W), lambda b, n: (b * nb + n, 0)),
        out_shape=jax.ShapeDtypeStruct((m, SWA_Q_W), BF16),
        compiler_params=pltpu.CompilerParams(
            dimension_semantics=("arbitrary", "arbitrary"), vmem_limit_bytes=VMEM_LIMIT),
        name="swa",
    )(sinks, proj, proj, proj, proj, proj)


def _out_route_kernel(og_ref, os_ref, x_ref, wo_ref, g_ref, wr_ref, br_ref,
                      h_ref, hn_ref, route_ref, route_t_ref, cnt_ref, hnb_ref):
    t = pl.program_id(0)
    slot = lax.rem(t, 2)

    @pl.when(t == 0)
    def _():
        hnb_ref[...] = jnp.zeros_like(hnb_ref)

    logits = jnp.dot(hnb_ref[1 - slot], wr_ref[...], preferred_element_type=F32) + br_ref[...]

    h = x_ref[...]
    h = h + jnp.dot(og_ref[...], wo_ref[0:GLA_V_W, :], preferred_element_type=F32)
    h = h + jnp.dot(os_ref[...], wo_ref[GLA_V_W:, :], preferred_element_type=F32)
    h_ref[...] = h
    ms = jnp.mean(h * h, axis=-1, keepdims=True)
    hn = h * lax.rsqrt(ms + RMS_EPS) * g_ref[...]
    hn_ref[...] = hn
    hnb_ref[slot] = hn.astype(BF16)

    lane = lax.broadcasted_iota(jnp.int32, logits.shape, 1)
    lanef = lane.astype(F32)
    big = float(LANES)
    ninf = -jnp.inf
    gl = jnp.where(lane < N_GROUPS, logits, ninf)
    gmax = jnp.max(gl, axis=-1, keepdims=True)
    g_p = 1.0 / jnp.sum(jnp.exp(gl - gmax), axis=-1, keepdims=True)
    g_idx = jnp.min(jnp.where(gl == gmax, lanef, big), axis=-1, keepdims=True)
    lo = N_GROUPS + EXPERTS_PER_GROUP * g_idx
    el = jnp.where((lanef >= lo) & (lanef < lo + EXPERTS_PER_GROUP), logits, ninf)
    m1 = jnp.max(el, axis=-1, keepdims=True)
    i1 = jnp.min(jnp.where(el == m1, lanef, big), axis=-1, keepdims=True)
    el2 = jnp.where(lanef == i1, ninf, el)
    m2 = jnp.max(el2, axis=-1, keepdims=True)
    i2 = jnp.min(jnp.where(el2 == m2, lanef, big), axis=-1, keepdims=True)
    d = jnp.exp(m2 - m1)
    c1 = g_p / (1.0 + d)
    c2 = g_p * d / (1.0 + d)
    tm = logits.shape[0]
    chosen = ((lanef == i1) | (lanef == i2)).astype(BF16)
    ri = lax.broadcasted_iota(jnp.int32, (tm, tm), 0)
    ci = lax.broadcasted_iota(jnp.int32, (tm, tm), 1)
    earlier = jnp.dot((ci < ri).astype(BF16), chosen, preferred_element_type=F32)
    r1 = jnp.sum(jnp.where(lanef == i1, earlier, 0.0), axis=-1, keepdims=True)
    r2 = jnp.sum(jnp.where(lanef == i2, earlier, 0.0), axis=-1, keepdims=True)
    cnt = jnp.sum(chosen.astype(F32), axis=0, keepdims=True)
    cnt_ref[...] = jnp.broadcast_to(cnt, cnt_ref.shape)

    fields = [i1 - N_GROUPS, i2 - N_GROUPS, c1, c2, r1, r2]
    route = jnp.zeros_like(logits)
    for idx, val in enumerate(fields):
        route = jnp.where(lane == idx, val, route)
    route_ref[...] = route[:, 0:ROUTE_W]
    route_t_ref[...] = route.T[0:ROUTE_W, :][None]


def _out_route(o_gla, o_swa, x2, w_out, g, w_rt, b_rt):
    m = x2.shape[0]
    tm = min(OUT_TM, m)
    ntile = m // tm
    row = lambda i: (jnp.minimum(i, ntile - 1), 0)
    routed = lambda i: (jnp.maximum(i - 1, 0), 0)
    fixed = lambda i: (0, 0)
    return pl.pallas_call(
        _out_route_kernel,
        grid=(ntile + 1,),
        in_specs=[
            pl.BlockSpec((tm, GLA_V_W), row),
            pl.BlockSpec((tm, SWA_Q_W), row),
            pl.BlockSpec((tm, D_MODEL), row),
            pl.BlockSpec((GLA_V_W + SWA_Q_W, D_MODEL), fixed),
            pl.BlockSpec((1, D_MODEL), fixed),
            pl.BlockSpec((D_MODEL, LANES), fixed),
            pl.BlockSpec((1, LANES), fixed),
        ],
        out_specs=[
            pl.BlockSpec((tm, D_MODEL), row),
            pl.BlockSpec((tm, D_MODEL), row),
            pl.BlockSpec((tm, ROUTE_W), routed),
            pl.BlockSpec((1, ROUTE_W, tm), lambda i: (jnp.maximum(i - 1, 0), 0, 0)),
            pl.BlockSpec((1, 8, LANES), lambda i: (jnp.maximum(i - 1, 0), 0, 0)),
        ],
        out_shape=[
            jax.ShapeDtypeStruct((m, D_MODEL), F32),
            jax.ShapeDtypeStruct((m, D_MODEL), F32),
            jax.ShapeDtypeStruct((m, ROUTE_W), F32),
            jax.ShapeDtypeStruct((m // tm, ROUTE_W, tm), F32),
            jax.ShapeDtypeStruct((m // tm, 8, LANES), F32),
        ],
        scratch_shapes=[pltpu.VMEM((2, tm, D_MODEL), BF16)],
        compiler_params=pltpu.CompilerParams(
            dimension_semantics=("arbitrary",), vmem_limit_bytes=VMEM_LIMIT),
        name="out_route",
    )(o_gla, o_swa, x2, w_out, g, w_rt, b_rt)


LANES_LOG2 = 7
MOE_ROW_GROUP_LOG2 = 3
MOE_ROW_GROUP = 1 << MOE_ROW_GROUP_LOG2
MOE_GATHER_GROUP_LOG2 = 4
MOE_GATHER_GROUP = 1 << MOE_GATHER_GROUP_LOG2
MOE_TAIL_ROWS = N_EXPERTS * MOE_ROW_GROUP


def _moe_kernel(te_ref, nv_ref, ne_ref, yb_ref, meta_ref, tok_hbm, hn_hbm, wg_hbm, wu_hbm, wd_hbm,
                y_hbm, xbuf, gsem, ids_smem, isem, wgs, wus, wds, wsem, wgb, wub, wdb, ybuf, ysem):
    tr = MOE_TR
    grp = MOE_ROW_GROUP
    t = pl.program_id(0)
    nt = meta_ref[0]
    gs = lax.rem(t, 2)
    cs = 1 - gs

    def ids_copy(tile, s):
        return pltpu.make_async_copy(tok_hbm.at[tile], ids_smem.at[s], isem.at[s])

    def gather_rows(tile):
        groups = lax.shift_right_logical(nv_ref[tile] + (MOE_GATHER_GROUP - 1), MOE_GATHER_GROUP_LOG2)
        return lax.shift_left(groups, MOE_GATHER_GROUP_LOG2)

    def weight_copies(e):
        return (pltpu.make_async_copy(wg_hbm.at[e], wgs, wsem.at[0]),
                pltpu.make_async_copy(wu_hbm.at[e], wus, wsem.at[1]),
                pltpu.make_async_copy(wd_hbm.at[e], wds, wsem.at[2]))

    def rows_used(tile):
        groups = lax.shift_right_logical(nv_ref[tile] + (grp - 1), MOE_ROW_GROUP_LOG2)
        return lax.shift_left(groups, MOE_ROW_GROUP_LOG2)

    def y_copy(tile, s):
        n_rows = pl.multiple_of(rows_used(tile), grp)
        dst = y_hbm.at[pl.ds(pl.multiple_of(yb_ref[tile], grp), n_rows)]
        return pltpu.make_async_copy(ybuf.at[s, pl.ds(0, n_rows)], dst, ysem.at[s])

    @pl.when(t == 0)
    def _():
        xbuf[...] = jnp.zeros_like(xbuf)
        ids_copy(0, 0).start()
        for cp in weight_copies(te_ref[0]):
            cp.start()

    @pl.when(t < nt)
    def _():
        ids_copy(t, gs).wait()

        @pl.when(t + 1 < nt)
        def _():
            ids_copy(t + 1, cs).start()

        n_real = nv_ref[t]
        for slot in range(2):
            @pl.when(gs == slot)
            def _():
                for g0 in range(0, tr, MOE_GATHER_GROUP):
                    @pl.when(g0 < n_real)
                    def _():
                        for r in range(g0, g0 + MOE_GATHER_GROUP):
                            tok = ids_smem[slot, r // LANES, r % LANES]
                            pltpu.make_async_copy(hn_hbm.at[pl.ds(tok, 1)], xbuf.at[slot, pl.ds(r, 1)],
                                                  gsem.at[slot]).start(priority=r % 2)

    @pl.when((t >= 1) & (t <= nt))
    def _():
        c = t - 1
        changed = (c == 0) | (te_ref[c] != te_ref[jnp.maximum(c - 1, 0)])

        @pl.when(changed)
        def _():
            for cp in weight_copies(te_ref[c]):
                cp.wait()
            wgb[...] = wgs[...].astype(BF16)
            wub[...] = wus[...].astype(BF16)
            wdb[...] = wds[...].astype(BF16)
            nxt = ne_ref[c]

            @pl.when(nxt >= 0)
            def _():
                for cp in weight_copies(nxt):
                    cp.start()

        n_rows = pl.multiple_of(gather_rows(c), MOE_GATHER_GROUP)
        pltpu.make_async_copy(hn_hbm.at[pl.ds(0, n_rows)], xbuf.at[cs, pl.ds(0, n_rows)],
                              gsem.at[cs]).wait()
        x = xbuf[cs].astype(BF16)
        g = jnp.dot(x, wgb[...], preferred_element_type=F32)
        u = jnp.dot(x, wub[...], preferred_element_type=F32)
        hmid = (_silu(g) * u).astype(BF16)

        @pl.when(c >= 2)
        def _():
            y_copy(c - 2, cs).wait()

        ybuf[cs] = jnp.dot(hmid, wdb[...], preferred_element_type=F32)
        y_copy(c, cs).start()

        @pl.when(t == nt)
        def _():
            @pl.when(c >= 1)
            def _():
                y_copy(c - 1, gs).wait()

            y_copy(c, cs).wait()
            ybuf[gs] = jnp.zeros((tr, D_MODEL), F32)
            tail0, n_tail = meta_ref[1], meta_ref[2]
            for k in range(MOE_TAIL_ROWS // tr):
                n_k = pl.multiple_of(jnp.clip(n_tail - k * tr, 0, tr), grp)

                @pl.when(n_k > 0)
                def _():
                    dst = y_hbm.at[pl.ds(pl.multiple_of(tail0 + k * tr, grp), n_k)]
                    cp = pltpu.make_async_copy(ybuf.at[gs, pl.ds(0, n_k)], dst, ysem.at[gs])
                    cp.start()
                    cp.wait()


def _moe(hn, w_gate, w_up, w_down, plan):
    n_tiles = plan["tile_expert"].shape[0]
    y_rows = 2 * hn.shape[0] + MOE_TAIL_ROWS
    any_spec = pl.BlockSpec(memory_space=pl.ANY)
    grid_spec = pltpu.PrefetchScalarGridSpec(
        num_scalar_prefetch=5,
        grid=(n_tiles + 1,),
        in_specs=[any_spec] * 5,
        out_specs=any_spec,
        scratch_shapes=[
            pltpu.VMEM((2, MOE_TR, D_MODEL), F32),
            pltpu.SemaphoreType.DMA((2,)),
            pltpu.SMEM((2, MOE_TR // LANES, LANES), jnp.int32),
            pltpu.SemaphoreType.DMA((2,)),
            pltpu.VMEM((D_MODEL, D_EXPERT), F32),
            pltpu.VMEM((D_MODEL, D_EXPERT), F32),
            pltpu.VMEM((D_EXPERT, D_MODEL), F32),
            pltpu.SemaphoreType.DMA((3,)),
            pltpu.VMEM((D_MODEL, D_EXPERT), BF16),
            pltpu.VMEM((D_MODEL, D_EXPERT), BF16),
            pltpu.VMEM((D_EXPERT, D_MODEL), BF16),
            pltpu.VMEM((2, MOE_TR, D_MODEL), F32),
            pltpu.SemaphoreType.DMA((2,)),
        ],
    )
    return pl.pallas_call(
        _moe_kernel,
        grid_spec=grid_spec,
        out_shape=jax.ShapeDtypeStruct((y_rows, D_MODEL), F32),
        compiler_params=pltpu.CompilerParams(
            dimension_semantics=("arbitrary",), vmem_limit_bytes=VMEM_LIMIT),
        name="moe",
    )(plan["tile_expert"], plan["tile_rows"], plan["next_expert"], plan["tile_out"],
      plan["meta"], plan["tile_tokens"], hn, w_gate, w_up, w_down)


def _combine_kernel(pos_hbm, y_hbm, h_ref, route_ref, g_ref, o_ref, ybuf, gsem, ids_smem, isem):
    tm = h_ref.shape[0]
    t = pl.program_id(0)
    ntile = pl.num_programs(0) - 1

    def ids_copy(tile, s):
        return pltpu.make_async_copy(pos_hbm.at[tile], ids_smem.at[s], isem.at[s])

    @pl.when(t == 0)
    def _():
        ids_copy(0, 0).start()

    @pl.when(t < ntile)
    def _():
        s = lax.rem(t, 2)
        ids_copy(t, s).wait()

        @pl.when(t + 1 < ntile)
        def _():
            ids_copy(t + 1, 1 - s).start()

        for slot in range(2):
            @pl.when(s == slot)
            def _():
                for kk in range(2):
                    for r in range(tm):
                        flat = kk * tm + r
                        row = ids_smem[slot, flat // LANES, flat % LANES]
                        pltpu.make_async_copy(y_hbm.at[pl.ds(row, 1)], ybuf.at[slot, kk, pl.ds(r, 1)],
                                              gsem.at[slot, kk]).start(priority=r % 2)

    @pl.when(t >= 1)
    def _():
        s = lax.rem(t - 1, 2)
        for kk in range(2):
            pltpu.make_async_copy(y_hbm.at[pl.ds(0, tm)], ybuf.at[s, kk], gsem.at[s, kk]).wait()
        route = route_ref[...]
        h = h_ref[...] + route[:, 2:3] * ybuf[s, 0] + route[:, 3:4] * ybuf[s, 1]
        ms = jnp.mean(h * h, axis=-1, keepdims=True)
        o_ref[...] = h * lax.rsqrt(ms + RMS_EPS) * g_ref[...]


def _combine(y, h, route, g, pos3):
    m = h.shape[0]
    tm = min(CMB_TM, m)
    ntile = m // tm
    assert pos3.shape == (ntile, 2 * tm // LANES, LANES)
    row = lambda i: (jnp.maximum(i - 1, 0), 0)
    return pl.pallas_call(
        _combine_kernel,
        grid=(ntile + 1,),
        in_specs=[
            pl.BlockSpec(memory_space=pl.ANY),
            pl.BlockSpec(memory_space=pl.ANY),
            pl.BlockSpec((tm, D_MODEL), row),
            pl.BlockSpec((tm, ROUTE_W), row),
            pl.BlockSpec((1, D_MODEL), lambda i: (0, 0)),
        ],
        out_specs=pl.BlockSpec((tm, D_MODEL), row),
        out_shape=jax.ShapeDtypeStruct((m, D_MODEL), F32),
        scratch_shapes=[
            pltpu.VMEM((2, 2, tm, D_MODEL), F32),
            pltpu.SemaphoreType.DMA((2, 2)),
            pltpu.SMEM((2, 2 * tm // LANES, LANES), jnp.int32),
            pltpu.SemaphoreType.DMA((2,)),
        ],
        compiler_params=pltpu.CompilerParams(
            dimension_semantics=("arbitrary",), vmem_limit_bytes=VMEM_LIMIT),
        name="combine",
    )(pos3, y, h, route, g)


def _dispatch_plan(route_t, tile_counts, m):
    tr = MOE_TR
    p_rows = 2 * m + N_EXPERTS * tr
    n_tiles = p_rows // tr
    n_tok_tiles, _, tm = route_t.shape
    cnt = tile_counts[:, 0, N_GROUPS:N_GROUPS + N_EXPERTS].astype(jnp.int32)
    before_tile = jnp.cumsum(cnt, axis=0) - cnt
    counts = jnp.sum(cnt, axis=0)
    padded = ((counts + tr - 1) // tr) * tr
    ends = jnp.cumsum(padded)
    starts = ends - padded
    grp = MOE_ROW_GROUP
    y_len = ((counts + grp - 1) // grp) * grp
    y_starts = jnp.cumsum(y_len) - y_len
    base = (y_starts[None, :] + before_tile).astype(jnp.int32)
    fields = route_t.astype(jnp.int32)
    experts = (fields[:, 0, :], fields[:, 1, :])
    ranks = (fields[:, 4, :], fields[:, 5, :])
    expert_ids = jnp.arange(N_EXPERTS, dtype=jnp.int32)
    rows = [rk + jnp.sum(jnp.where(ex[..., None] == expert_ids, base[:, None, :], 0), axis=-1)
            for ex, rk in zip(experts, ranks)]
    ctm = min(CMB_TM, m)
    pos = jnp.concatenate([r.reshape(m // ctm, ctm) for r in rows], axis=1)
    pos = pos.reshape(m // ctm, 2 * ctm // LANES, LANES)
    num_tiles = (ends[-1] // tr).astype(jnp.int32)
    tile_idx = jnp.minimum(jnp.arange(n_tiles, dtype=jnp.int32), num_tiles - 1)
    tile_expert = jnp.sum((tile_idx[:, None] * tr >= ends[None, :]).astype(jnp.int32), axis=1)
    n_slots = 2 * m
    n_fill = n_tiles * tr - n_slots
    stride = 1 << (n_slots + n_fill - 1).bit_length()
    assert (N_EXPERTS + 1) * stride < 2 ** 31
    tok2 = 2 * jnp.arange(m, dtype=jnp.int32).reshape(n_tok_tiles, tm)
    keys = [(ex * stride + tok2 + k).reshape(-1) for k, ex in enumerate(experts)]
    fill = jnp.arange(n_fill, dtype=jnp.int32)
    fill_expert = jnp.sum((fill[:, None] >= jnp.cumsum(padded - counts)[None, :]).astype(jnp.int32), axis=1)
    keys.append(fill_expert * stride + n_slots + fill)
    sorted_low = jnp.sort(jnp.concatenate(keys)) & (stride - 1)
    tile_tokens = jnp.where(sorted_low < n_slots, sorted_low >> 1, sorted_low % m)
    tile_tokens = tile_tokens.reshape(n_tiles, tr // LANES, LANES)
    in_expert = tile_idx - (starts // tr)[tile_expert]
    tile_rows = jnp.clip(counts[tile_expert] - in_expert * tr, 0, tr)
    tile_rows = jnp.where(jnp.arange(n_tiles) < num_tiles, tile_rows, 0)
    ids = jnp.arange(N_EXPERTS, dtype=jnp.int32)
    present = jnp.where(counts > 0, ids, N_EXPERTS)
    next_ge = lax.cummin(present, axis=0, reverse=True)
    next_gt = jnp.concatenate([next_ge[1:], jnp.full((1,), N_EXPERTS, jnp.int32)])
    next_gt = jnp.where(next_gt >= N_EXPERTS, -1, next_gt)
    tile_out = y_starts[tile_expert] + in_expert * tr
    y_used = jnp.sum(y_len)
    meta = jnp.stack([num_tiles, y_used, 2 * m + MOE_TAIL_ROWS - y_used]).astype(jnp.int32)
    plan = dict(tile_expert=tile_expert.astype(jnp.int32),
                tile_rows=tile_rows.astype(jnp.int32), next_expert=next_gt[tile_expert].astype(jnp.int32),
                tile_out=tile_out.astype(jnp.int32), meta=meta, tile_tokens=tile_tokens)
    return plan, pos


def kernel(x, norm_mix_g, w_in, w_gk_up, b_gk, gla_norm_g, swa_sinks, w_out, norm_ffn_g,
           w_group, b_group, w_router, b_router, w_gate, w_up, w_down, norm_final_g):
    batch, seq, d = x.shape
    m = batch * seq
    assert w_in.shape[0] == 1, "single-layer block"
    x2 = x.reshape(m, d)
    w_main, w_lr = _w_prep(jnp.transpose(w_in[0]))
    wup = jnp.pad(w_gk_up[0], ((0, LANES - GLA_RANK), (0, 0))).astype(BF16)
    w_rt = jnp.pad(jnp.concatenate([w_group[0], w_router[0]], axis=1),
                   ((0, 0), (0, LANES - N_GROUPS - N_EXPERTS))).astype(BF16)
    b_rt = jnp.pad(jnp.concatenate([b_group[0], b_router[0]]),
                   (0, LANES - N_GROUPS - N_EXPERTS)).reshape(1, LANES)

    proj, glr = _in_proj(x2, norm_mix_g[0].reshape(1, d), w_main, w_lr)
    o_gla = _gla(proj, glr, wup, b_gk[0].reshape(1, GLA_QK_W),
                 gla_norm_g[0].reshape(1, GLA_V_W), batch, seq)
    o_swa = _swa(proj, swa_sinks[0], batch, seq)
    h_mid, hn, route, route_t, tile_counts = _out_route(o_gla, o_swa, x2, w_out[0].astype(BF16),
                                                        norm_ffn_g[0].reshape(1, d), w_rt, b_rt)
    plan, pos3 = _dispatch_plan(route_t, tile_counts, m)
    y = _moe(hn, w_gate[0], w_up[0], w_down[0], plan)
    out = _combine(y, h_mid, route, norm_final_g.reshape(1, d), pos3)
    return out.reshape(batch, seq, d)
```

```python
import functools

import jax
import jax.numpy as jnp
import numpy as np
from jax import lax
from jax.experimental import pallas as pl
from jax.experimental.pallas import tpu as pltpu

F32 = jnp.float32
BF16 = jnp.bfloat16

D_MODEL = 2048
GLA_HEADS = 4
GLA_DK = 128
GLA_DV = 256
GLA_RANK = 16
GLA_GATE_NORM = 16.0
GLA_CHUNK = 64
SWA_Q_HEADS = 16
SWA_KV_HEADS = 4
SWA_GROUP = SWA_Q_HEADS // SWA_KV_HEADS
SWA_HEAD_DIM = 64
SWA_BLOCK = 128
N_GROUPS = 4
EXPERTS_PER_GROUP = 16
N_EXPERTS = N_GROUPS * EXPERTS_PER_GROUP
D_EXPERT = 256
RMS_EPS = 1e-6

GLA_QK_W = GLA_HEADS * GLA_DK
GLA_V_W = GLA_HEADS * GLA_DV
SWA_Q_W = SWA_Q_HEADS * SWA_HEAD_DIM
SWA_KV_W = SWA_KV_HEADS * SWA_HEAD_DIM
OFF_GQ = 0
OFF_GK = OFF_GQ + GLA_QK_W
OFF_GV = OFF_GK + GLA_QK_W
OFF_GR = OFF_GV + GLA_V_W
OFF_SQ = OFF_GR + GLA_V_W
OFF_SK = OFF_SQ + SWA_Q_W
OFF_SV = OFF_SK + SWA_KV_W
PROJ_W = OFF_SV + SWA_KV_W
LANES = 128

IN_TM = 512
IN_TN = 4608
OUT_TM = 512
ROUTE_W = 8
MOE_TR = 256
CMB_TM = 256
VMEM_LIMIT = 56 * 1024 * 1024


def _silu(x):
    return x / (1.0 + jnp.exp(-x))


ALIBI_PARTS = 3


def _bf16_parts(value, n):
    parts, rest = [], np.float32(value)
    for _ in range(n):
        piece = np.float32(np.asarray(rest).astype(jnp.bfloat16))
        parts.append(float(piece))
        rest = np.float32(rest - piece)
    return parts


W_PREP_ROWS = 384
OFF_LR = OFF_GR + GLA_V_W


def _wprep_kernel(wt_hbm, wm_ref, wl_ref, buf, lrbuf, sem, lrsem):
    rows = W_PREP_ROWS
    i = pl.program_id(0)
    n = pl.num_programs(0)
    n_lo = OFF_LR // rows
    slot = lax.rem(i, 2)

    def block_copy(step, s):
        start = jnp.where(step < n_lo, step * rows, OFF_LR + GLA_RANK + (step - n_lo) * rows)
        return pltpu.make_async_copy(wt_hbm.at[pl.ds(pl.multiple_of(start, 8), rows)], buf.at[s], sem.at[s])

    @pl.when(i == 0)
    def _():
        block_copy(0, 0).start()
        lrbuf[...] = jnp.zeros_like(lrbuf)
        lr_copy = pltpu.make_async_copy(wt_hbm.at[pl.ds(OFF_LR, GLA_RANK)], lrbuf.at[pl.ds(0, GLA_RANK)],
                                        lrsem.at[0])
        lr_copy.start()
        lr_copy.wait()
        wl_ref[...] = lrbuf[...].T.astype(BF16)

    @pl.when(i + 1 < n)
    def _():
        block_copy(i + 1, 1 - slot).start()

    block_copy(i, slot).wait()
    wm_ref[...] = buf[slot].T.astype(BF16)


def _w_prep(w_t):
    n, k = w_t.shape
    assert OFF_LR % W_PREP_ROWS == 0 and (PROJ_W - OFF_LR) % W_PREP_ROWS == 0 and n == PROJ_W + GLA_RANK
    return pl.pallas_call(
        _wprep_kernel,
        grid=(PROJ_W // W_PREP_ROWS,),
        in_specs=[pl.BlockSpec(memory_space=pl.ANY)],
        out_specs=[pl.BlockSpec((k, W_PREP_ROWS), lambda i: (0, i)),
                   pl.BlockSpec((k, LANES), lambda i: (0, 0))],
        out_shape=[jax.ShapeDtypeStruct((k, PROJ_W), BF16), jax.ShapeDtypeStruct((k, LANES), BF16)],
        scratch_shapes=[
            pltpu.VMEM((2, W_PREP_ROWS, k), F32),
            pltpu.VMEM((LANES, k), F32),
            pltpu.SemaphoreType.DMA((2,)),
            pltpu.SemaphoreType.DMA((1,)),
        ],
        compiler_params=pltpu.CompilerParams(
            dimension_semantics=("arbitrary",), vmem_limit_bytes=VMEM_LIMIT),
        name="w_prep",
    )(w_t)


def _inproj_kernel(x_ref, g_ref, w_ref, wlr_ref, proj_ref, glr_ref, xn_ref):
    @pl.when(pl.program_id(1) == 0)
    def _():
        x = x_ref[...]
        ms = jnp.mean(x * x, axis=-1, keepdims=True)
        xn = (x * lax.rsqrt(ms + RMS_EPS) * g_ref[...]).astype(BF16)
        xn_ref[...] = xn
        glr_ref[...] = jnp.dot(xn, wlr_ref[...], preferred_element_type=F32)

    proj_ref[...] = jnp.dot(xn_ref[...], w_ref[...], preferred_element_type=F32).astype(BF16)


def _in_proj(x2, g, w_main, w_lr):
    m = x2.shape[0]
    tm = min(IN_TM, m)
    return pl.pallas_call(
        _inproj_kernel,
        grid=(m // tm, PROJ_W // IN_TN),
        in_specs=[
            pl.BlockSpec((tm, D_MODEL), lambda i, j: (i, 0)),
            pl.BlockSpec((1, D_MODEL), lambda i, j: (0, 0)),
            pl.BlockSpec((D_MODEL, IN_TN), lambda i, j: (0, j),
                         pipeline_mode=pl.Buffered(1) if IN_TN == PROJ_W else None),
            pl.BlockSpec((D_MODEL, LANES), lambda i, j: (0, 0)),
        ],
        out_specs=[
            pl.BlockSpec((tm, IN_TN), lambda i, j: (i, j)),
            pl.BlockSpec((tm, LANES), lambda i, j: (i, 0)),
        ],
        out_shape=[
            jax.ShapeDtypeStruct((m, PROJ_W), BF16),
            jax.ShapeDtypeStruct((m, LANES), F32),
        ],
        scratch_shapes=[pltpu.VMEM((tm, D_MODEL), BF16)],
        compiler_params=pltpu.CompilerParams(
            dimension_semantics=("arbitrary", "arbitrary"), vmem_limit_bytes=VMEM_LIMIT),
        name="in_proj",
    )(x2, g, w_main, w_lr)


def _gla_kernel(q_ref, k_ref, v_ref, r_ref, glr_ref, wup_ref, bgk_ref, ng_ref, o_ref,
                la_ref, oi_ref, qi_ref, ki_ref, qd_ref, kd_ref, a_ref, kv_ref, dec_ref, sp_ref):
    t = q_ref.shape[0]
    c = GLA_CHUNK
    nchunk = t // c
    z = jnp.dot(glr_ref[...].astype(BF16), wup_ref[...], preferred_element_type=F32) + bgk_ref[...]
    la_ref[...] = (jnp.minimum(z, 0.0) - jnp.log(1.0 + jnp.exp(-jnp.abs(z)))) * (1.0 / GLA_GATE_NORM)

    ii = lax.broadcasted_iota(jnp.int32, (c, c), 0)
    jj = lax.broadcasted_iota(jnp.int32, (c, c), 1)
    causal = jj <= ii
    tri = causal.astype(BF16)
    nt = (((1,), (1,)), ((), ()))
    tn = (((0,), (0,)), ((), ()))

    def chunk_rows(n):
        return pl.ds(pl.multiple_of(n * c, c), c)

    def decays(n, carry):
        rows = chunk_rows(n)
        la = la_ref[rows, :]
        hi = la.astype(BF16)
        r1 = la - hi.astype(F32)
        mid = r1.astype(BF16)
        lo = (r1 - mid.astype(F32)).astype(BF16)
        parts = jnp.dot(tri, jnp.concatenate([hi, mid, lo], axis=1), preferred_element_type=F32)
        bcum = parts[:, 0:GLA_DK] + parts[:, GLA_DK:2 * GLA_DK] + parts[:, 2 * GLA_DK:]
        b_mid = bcum[c // 2 - 1:c // 2, :]
        b_last = bcum[c - 1:c, :]
        q = q_ref[rows, :].astype(F32) * (GLA_DK ** -0.5)
        k = k_ref[rows, :].astype(F32)
        qi_ref[rows, :] = (q * jnp.exp(bcum - b_mid)).astype(BF16)
        ki_ref[rows, :] = (k * jnp.exp(b_mid - bcum)).astype(BF16)
        qd_ref[rows, :] = (q * jnp.exp(bcum)).astype(BF16)
        kd_ref[rows, :] = (k * jnp.exp(b_last - bcum)).astype(BF16)
        dec_ref[n] = jnp.broadcast_to(jnp.exp(b_last), dec_ref.shape[1:])
        return carry

    lax.fori_loop(0, nchunk, decays, 0, unroll=8)

    def scores(n, carry):
        rows = chunk_rows(n)
        a = lax.dot_general(qi_ref[rows, :], ki_ref[rows, :], nt, preferred_element_type=F32)
        a_ref[rows, :] = jnp.where(causal, a, 0.0).astype(BF16)
        return carry

    lax.fori_loop(0, nchunk, scores, 0, unroll=8)

    def intra(n, carry):
        rows = chunk_rows(n)
        v = v_ref[rows, :]
        oi_ref[rows, :] = jnp.dot(a_ref[rows, :], v, preferred_element_type=F32)
        kv_ref[n] = lax.dot_general(v, kd_ref[rows, :], tn, preferred_element_type=F32)
        return carry

    lax.fori_loop(0, nchunk, intra, 0, unroll=8)

    def scan(n, s_t):
        sp_ref[n] = s_t.astype(BF16)
        return s_t * dec_ref[n][0:1, :] + kv_ref[n]

    lax.fori_loop(0, nchunk, scan, jnp.zeros((GLA_DV, GLA_DK), F32), unroll=2)

    def inter(n, carry):
        rows = chunk_rows(n)
        o = oi_ref[rows, :] + lax.dot_general(qd_ref[rows, :], sp_ref[n], nt,
                                              preferred_element_type=F32)
        ms = jnp.mean(o * o, axis=-1, keepdims=True)
        o = o * lax.rsqrt(ms + RMS_EPS) * ng_ref[...]
        o = o * _silu(r_ref[rows, :].astype(F32))
        o_ref[rows, :] = o.astype(BF16)
        return carry

    lax.fori_loop(0, nchunk, inter, 0, unroll=8)


def _gla(proj, glr, wup, bgk, ng, batch, seq):
    m = proj.shape[0]
    return pl.pallas_call(
        _gla_kernel,
        grid=(batch, GLA_HEADS),
        in_specs=[
            pl.BlockSpec((seq, GLA_DK), lambda b, h: (b, OFF_GQ // GLA_DK + h)),
            pl.BlockSpec((seq, GLA_DK), lambda b, h: (b, OFF_GK // GLA_DK + h)),
            pl.BlockSpec((seq, GLA_DV), lambda b, h: (b, OFF_GV // GLA_DV + h)),
            pl.BlockSpec((seq, GLA_DV), lambda b, h: (b, OFF_GR // GLA_DV + h)),
            pl.BlockSpec((seq, LANES), lambda b, h: (b, 0)),
            pl.BlockSpec((LANES, GLA_DK), lambda b, h: (0, h)),
            pl.BlockSpec((1, GLA_DK), lambda b, h: (0, h)),
            pl.BlockSpec((1, GLA_DV), lambda b, h: (0, h)),
        ],
        out_specs=pl.BlockSpec((seq, GLA_DV), lambda b, h: (b, h)),
        out_shape=jax.ShapeDtypeStruct((m, GLA_V_W), BF16),
        scratch_shapes=[
            pltpu.VMEM((seq, GLA_DK), F32),
            pltpu.VMEM((seq, GLA_DV), F32),
            pltpu.VMEM((seq, GLA_DK), BF16),
            pltpu.VMEM((seq, GLA_DK), BF16),
            pltpu.VMEM((seq, GLA_DK), BF16),
            pltpu.VMEM((seq, GLA_DK), BF16),
            pltpu.VMEM((seq, GLA_CHUNK), BF16),
            pltpu.VMEM((seq // GLA_CHUNK, GLA_DV, GLA_DK), F32),
            pltpu.VMEM((seq // GLA_CHUNK, 8, GLA_DK), F32),
            pltpu.VMEM((seq // GLA_CHUNK, GLA_DV, GLA_DK), BF16),
        ],
        compiler_params=pltpu.CompilerParams(
            dimension_semantics=("arbitrary", "arbitrary"), vmem_limit_bytes=VMEM_LIMIT),
        name="gla",
    )(proj, proj, proj, proj, glr, wup, bgk, ng)


def _swa_kernel(sink_ref, q_ref, kp_ref, kc_ref, vp_ref, vc_ref, o_ref):
    blk = SWA_BLOCK
    half = SWA_HEAD_DIM
    n = pl.program_id(1)
    k_all = jnp.concatenate([kp_ref[...], kc_ref[...]], axis=0)
    v_all = jnp.concatenate([vp_ref[...], vc_ref[...]], axis=0)
    qi = lax.broadcasted_iota(jnp.int32, (blk, 2 * blk), 0)
    kj = lax.broadcasted_iota(jnp.int32, (blk, 2 * blk), 1)
    rel = qi + blk - kj
    valid = (rel >= 0) & (rel < blk) & ((kj >= blk) | (n > 0))
    sink_col = kj == 0
    lane_kv = lax.broadcasted_iota(jnp.int32, (2 * blk, LANES), 1)
    lo_kv = lane_kv < half
    lane_q = lax.broadcasted_iota(jnp.int32, (blk, LANES), 1)
    lo_q = lane_q < half
    nt = (((1,), (1,)), ((), ()))
    key_idx = lax.broadcasted_iota(jnp.int32, (2 * blk, LANES), 0).astype(F32)
    key_cols = jnp.where((lane_kv & (half - 1)) < ALIBI_PARTS, key_idx, 0.0)
    q_pos = lax.broadcasted_iota(jnp.int32, (blk, 1), 0).astype(F32) + float(blk)
    q_scale = jnp.asarray(SWA_HEAD_DIM ** -0.5, BF16)

    for p in range(SWA_KV_HEADS // 2):
        k_slab = k_all[:, p * LANES:(p + 1) * LANES].astype(F32)
        v_slab = v_all[:, p * LANES:(p + 1) * LANES].astype(F32)
        k_roll = pltpu.roll(k_slab, half, 1)
        v_roll = pltpu.roll(v_slab, half, 1)
        for hh in range(2):
            h = 2 * p + hh
            k_lo, k_hi = (k_slab, k_roll) if hh == 0 else (k_roll, k_slab)
            kd = (jnp.where(lo_kv, k_lo, key_cols).astype(BF16),
                  jnp.where(lo_kv, key_cols, k_hi).astype(BF16))
            v_lo, v_hi = (v_slab, v_roll) if hh == 0 else (v_roll, v_slab)
            vd = jnp.where(lo_kv, v_lo, v_hi)
            vd = jnp.where(key_idx == 0.0, 0.0, vd).astype(BF16)
            for gp in range(SWA_GROUP // 2):
                col = (h * (SWA_GROUP // 2) + gp) * LANES
                qs = q_ref[:, col:col + LANES] * q_scale
                outs = []
                for gg in range(2):
                    head = h * SWA_GROUP + 2 * gp + gg
                    parts = _bf16_parts(2.0 ** (-8.0 * (head + 1) / SWA_Q_HEADS), ALIBI_PARTS)
                    slope = sum(parts)
                    aug = jnp.zeros((blk, LANES), F32)
                    for idx, part in enumerate(parts):
                        aug = jnp.where(lane_q == (half if gg == 0 else 0) + idx, part, aug)
                    keep = lo_q if gg == 0 else jnp.logical_not(lo_q)
                    qm = jnp.where(keep, qs, aug.astype(BF16))
                    s = lax.dot_general(qm, kd[gg], nt, preferred_element_type=F32)
                    sink = sink_ref[head] + slope * q_pos
                    s = jnp.where(sink_col, sink, jnp.where(valid, s, -jnp.inf))
                    mx = jnp.max(s, axis=-1, keepdims=True)
                    pe = jnp.exp(s - mx)
                    den = jnp.sum(pe, axis=-1, keepdims=True)
                    o = jnp.dot(pe.astype(BF16), vd, preferred_element_type=F32)
                    outs.append(o / den)
                o_ref[:, col:col + LANES] = jnp.where(lo_q, outs[0], outs[1]).astype(BF16)


def _swa(proj, sinks, batch, seq):
    m = proj.shape[0]
    nb = seq // SWA_BLOCK
    qcol = OFF_SQ // SWA_Q_W
    kcol = OFF_SK // SWA_KV_W
    vcol = OFF_SV // SWA_KV_W
    cur = lambda c: (lambda b, n: (b * nb + n, c))
    prev = lambda c: (lambda b, n: (b * nb + jnp.maximum(n - 1, 0), c))
    return pl.pallas_call(
        _swa_kernel,
        grid=(batch, nb),
        in_specs=[
            pl.BlockSpec(memory_space=pltpu.SMEM),
            pl.BlockSpec((SWA_BLOCK, SWA_Q_W), cur(qcol)),
            pl.BlockSpec((SWA_BLOCK, SWA_KV_W), prev(kcol)),
            pl.BlockSpec((SWA_BLOCK, SWA_KV_W), cur(kcol)),
            pl.BlockSpec((SWA_BLOCK, SWA_KV_W), prev(vcol)),
            pl.BlockSpec((SWA_BLOCK, SWA_KV_W), cur(vcol)),
        ],
        out_specs=pl.BlockSpec((SWA_BLOCK, SWA_Q_W), lambda b, n: (b * nb + n, 0)),
        out_shape=jax.ShapeDtypeStruct((m, SWA_Q_W), BF16),
        compiler_params=pltpu.CompilerParams(
            dimension_semantics=("arbitrary", "arbitrary"), vmem_limit_bytes=VMEM_LIMIT),
        name="swa",
    )(sinks, proj, proj, proj, proj, proj)


def _out_route_kernel(og_ref, os_ref, x_ref, wo_ref, g_ref, wr_ref, br_ref,
                      h_ref, hn_ref, route_ref, route_t_ref, cnt_ref, hnb_ref):
    t = pl.program_id(0)
    slot = lax.rem(t, 2)

    @pl.when(t == 0)
    def _():
        hnb_ref[...] = jnp.zeros_like(hnb_ref)

    logits = jnp.dot(hnb_ref[1 - slot], wr_ref[...], preferred_element_type=F32) + br_ref[...]

    h = x_ref[...]
    h = h + jnp.dot(og_ref[...], wo_ref[0:GLA_V_W, :], preferred_element_type=F32)
    h = h + jnp.dot(os_ref[...], wo_ref[GLA_V_W:, :], preferred_element_type=F32)
    h_ref[...] = h
    ms = jnp.mean(h * h, axis=-1, keepdims=True)
    hn = h * lax.rsqrt(ms + RMS_EPS) * g_ref[...]
    hn_ref[...] = hn
    hnb_ref[slot] = hn.astype(BF16)

    lane = lax.broadcasted_iota(jnp.int32, logits.shape, 1)
    lanef = lane.astype(F32)
    big = float(LANES)
    ninf = -jnp.inf
    gl = jnp.where(lane < N_GROUPS, logits, ninf)
    gmax = jnp.max(gl, axis=-1, keepdims=True)
    g_p = 1.0 / jnp.sum(jnp.exp(gl - gmax), axis=-1, keepdims=True)
    g_idx = jnp.min(jnp.where(gl == gmax, lanef, big), axis=-1, keepdims=True)
    lo = N_GROUPS + EXPERTS_PER_GROUP * g_idx
    el = jnp.where((lanef >= lo) & (lanef < lo + EXPERTS_PER_GROUP), logits, ninf)
    m1 = jnp.max(el, axis=-1, keepdims=True)
    i1 = jnp.min(jnp.where(el == m1, lanef, big), axis=-1, keepdims=True)
    el2 = jnp.where(lanef == i1, ninf, el)
    m2 = jnp.max(el2, axis=-1, keepdims=True)
    i2 = jnp.min(jnp.where(el2 == m2, lanef, big), axis=-1, keepdims=True)
    d = jnp.exp(m2 - m1)
    c1 = g_p / (1.0 + d)
    c2 = g_p * d / (1.0 + d)
    tm = logits.shape[0]
    chosen = ((lanef == i1) | (lanef == i2)).astype(BF16)
    ri = lax.broadcasted_iota(jnp.int32, (tm, tm), 0)
    ci = lax.broadcasted_iota(jnp.int32, (tm, tm), 1)
    earlier = jnp.dot((ci < ri).astype(BF16), chosen, preferred_element_type=F32)
    r1 = jnp.sum(jnp.where(lanef == i1, earlier, 0.0), axis=-1, keepdims=True)
    r2 = jnp.sum(jnp.where(lanef == i2, earlier, 0.0), axis=-1, keepdims=True)
    cnt = jnp.sum(chosen.astype(F32), axis=0, keepdims=True)
    cnt_ref[...] = jnp.broadcast_to(cnt, cnt_ref.shape)

    fields = [i1 - N_GROUPS, i2 - N_GROUPS, c1, c2, r1, r2]
    route = jnp.zeros_like(logits)
    for idx, val in enumerate(fields):
        route = jnp.where(lane == idx, val, route)
    route_ref[...] = route[:, 0:ROUTE_W]
    route_t_ref[...] = route.T[0:ROUTE_W, :][None]


def _out_route(o_gla, o_swa, x2, w_out, g, w_rt, b_rt):
    m = x2.shape[0]
    tm = min(OUT_TM, m)
    ntile = m // tm
    row = lambda i: (jnp.minimum(i, ntile - 1), 0)
    routed = lambda i: (jnp.maximum(i - 1, 0), 0)
    fixed = lambda i: (0, 0)
    return pl.pallas_call(
        _out_route_kernel,
        grid=(ntile + 1,),
        in_specs=[
            pl.BlockSpec((tm, GLA_V_W), row),
            pl.BlockSpec((tm, SWA_Q_W), row),
            pl.BlockSpec((tm, D_MODEL), row),
            pl.BlockSpec((GLA_V_W + SWA_Q_W, D_MODEL), fixed),
            pl.BlockSpec((1, D_MODEL), fixed),
            pl.BlockSpec((D_MODEL, LANES), fixed),
            pl.BlockSpec((1, LANES), fixed),
        ],
        out_specs=[
            pl.BlockSpec((tm, D_MODEL), row),
            pl.BlockSpec((tm, D_MODEL), row),
            pl.BlockSpec((tm, ROUTE_W), routed),
            pl.BlockSpec((1, ROUTE_W, tm), lambda i: (jnp.maximum(i - 1, 0), 0, 0)),
            pl.BlockSpec((1, 8, LANES), lambda i: (jnp.maximum(i - 1, 0), 0, 0)),
        ],
        out_shape=[
            jax.ShapeDtypeStruct((m, D_MODEL), F32),
            jax.ShapeDtypeStruct((m, D_MODEL), F32),
            jax.ShapeDtypeStruct((m, ROUTE_W), F32),
            jax.ShapeDtypeStruct((m // tm, ROUTE_W, tm), F32),
            jax.ShapeDtypeStruct((m // tm, 8, LANES), F32),
        ],
        scratch_shapes=[pltpu.VMEM((2, tm, D_MODEL), BF16)],
        compiler_params=pltpu.CompilerParams(
            dimension_semantics=("arbitrary",), vmem_limit_bytes=VMEM_LIMIT),
        name="out_route",
    )(o_gla, o_swa, x2, w_out, g, w_rt, b_rt)


LANES_LOG2 = 7
MOE_ROW_GROUP_LOG2 = 3
MOE_ROW_GROUP = 1 << MOE_ROW_GROUP_LOG2
MOE_GATHER_GROUP_LOG2 = 4
MOE_GATHER_GROUP = 1 << MOE_GATHER_GROUP_LOG2
MOE_TAIL_ROWS = N_EXPERTS * MOE_ROW_GROUP


def _moe_kernel(te_ref, nv_ref, ne_ref, yb_ref, meta_ref, tok_hbm, hn_hbm, wg_hbm, wu_hbm, wd_hbm,
                y_hbm, xbuf, gsem, ids_smem, isem, wgs, wus, wds, wsem, wgb, wub, wdb, ybuf, ysem):
    tr = MOE_TR
    grp = MOE_ROW_GROUP
    t = pl.program_id(0)
    nt = meta_ref[0]
    gs = lax.rem(t, 2)
    cs = 1 - gs

    def ids_copy(tile, s):
        return pltpu.make_async_copy(tok_hbm.at[tile], ids_smem.at[s], isem.at[s])

    def gather_rows(tile):
        groups = lax.shift_right_logical(nv_ref[tile] + (MOE_GATHER_GROUP - 1), MOE_GATHER_GROUP_LOG2)
        return lax.shift_left(groups, MOE_GATHER_GROUP_LOG2)

    def weight_copies(e):
        return (pltpu.make_async_copy(wg_hbm.at[e], wgs, wsem.at[0]),
                pltpu.make_async_copy(wu_hbm.at[e], wus, wsem.at[1]),
                pltpu.make_async_copy(wd_hbm.at[e], wds, wsem.at[2]))

    def rows_used(tile):
        groups = lax.shift_right_logical(nv_ref[tile] + (grp - 1), MOE_ROW_GROUP_LOG2)
        return lax.shift_left(groups, MOE_ROW_GROUP_LOG2)

    def y_copy(tile, s):
        n_rows = pl.multiple_of(rows_used(tile), grp)
        dst = y_hbm.at[pl.ds(pl.multiple_of(yb_ref[tile], grp), n_rows)]
        return pltpu.make_async_copy(ybuf.at[s, pl.ds(0, n_rows)], dst, ysem.at[s])

    @pl.when(t == 0)
    def _():
        xbuf[...] = jnp.zeros_like(xbuf)
        ids_copy(0, 0).start()
        for cp in weight_copies(te_ref[0]):
            cp.start()

    @pl.when(t < nt)
    def _():
        ids_copy(t, gs).wait()

        @pl.when(t + 1 < nt)
        def _():
            ids_copy(t + 1, cs).start()

        n_real = nv_ref[t]
        for slot in range(2):
            @pl.when(gs == slot)
            def _():
                for g0 in range(0, tr, MOE_GATHER_GROUP):
                    @pl.when(g0 < n_real)
                    def _():
                        for r in range(g0, g0 + MOE_GATHER_GROUP):
                            tok = ids_smem[slot, r // LANES, r % LANES]
                            pltpu.make_async_copy(hn_hbm.at[pl.ds(tok, 1)], xbuf.at[slot, pl.ds(r, 1)],
                                                  gsem.at[slot]).start(priority=r % 2)

    @pl.when((t >= 1) & (t <= nt))
    def _():
        c = t - 1
        changed = (c == 0) | (te_ref[c] != te_ref[jnp.maximum(c - 1, 0)])

        @pl.when(changed)
        def _():
            for cp in weight_copies(te_ref[c]):
                cp.wait()
            wgb[...] = wgs[...].astype(BF16)
            wub[...] = wus[...].astype(BF16)
            wdb[...] = wds[...].astype(BF16)
            nxt = ne_ref[c]

            @pl.when(nxt >= 0)
            def _():
                for cp in weight_copies(nxt):
                    cp.start()

        n_rows = pl.multiple_of(gather_rows(c), MOE_GATHER_GROUP)
        pltpu.make_async_copy(hn_hbm.at[pl.ds(0, n_rows)], xbuf.at[cs, pl.ds(0, n_rows)],
                              gsem.at[cs]).wait()
        x = xbuf[cs].astype(BF16)
        g = jnp.dot(x, wgb[...], preferred_element_type=F32)
        u = jnp.dot(x, wub[...], preferred_element_type=F32)
        hmid = (_silu(g) * u).astype(BF16)

        @pl.when(c >= 2)
        def _():
            y_copy(c - 2, cs).wait()

        ybuf[cs] = jnp.dot(hmid, wdb[...], preferred_element_type=F32)
        y_copy(c, cs).start()

        @pl.when(t == nt)
        def _():
            @pl.when(c >= 1)
            def _():
                y_copy(c - 1, gs).wait()

            y_copy(c, cs).wait()
            ybuf[gs] = jnp.zeros((tr, D_MODEL), F32)
            tail0, n_tail = meta_ref[1], meta_ref[2]
            for k in range(MOE_TAIL_ROWS // tr):
                n_k = pl.multiple_of(jnp.clip(n_tail - k * tr, 0, tr), grp)

                @pl.when(n_k > 0)
                def _():
                    dst = y_hbm.at[pl.ds(pl.multiple_of(tail0 + k * tr, grp), n_k)]
                    cp = pltpu.make_async_copy(ybuf.at[gs, pl.ds(0, n_k)], dst, ysem.at[gs])
                    cp.start()
                    cp.wait()


def _moe(hn, w_gate, w_up, w_down, plan):
    n_tiles = plan["tile_expert"].shape[0]
    y_rows = 2 * hn.shape[0] + MOE_TAIL_ROWS
    any_spec = pl.BlockSpec(memory_space=pl.ANY)
    grid_spec = pltpu.PrefetchScalarGridSpec(
        num_scalar_prefetch=5,
        grid=(n_tiles + 1,),
        in_specs=[any_spec] * 5,
        out_specs=any_spec,
        scratch_shapes=[
            pltpu.VMEM((2, MOE_TR, D_MODEL), F32),
            pltpu.SemaphoreType.DMA((2,)),
            pltpu.SMEM((2, MOE_TR // LANES, LANES), jnp.int32),
            pltpu.SemaphoreType.DMA((2,)),
            pltpu.VMEM((D_MODEL, D_EXPERT), F32),
            pltpu.VMEM((D_MODEL, D_EXPERT), F32),
            pltpu.VMEM((D_EXPERT, D_MODEL), F32),
            pltpu.SemaphoreType.DMA((3,)),
            pltpu.VMEM((D_MODEL, D_EXPERT), BF16),
            pltpu.VMEM((D_MODEL, D_EXPERT), BF16),
            pltpu.VMEM((D_EXPERT, D_MODEL), BF16),
            pltpu.VMEM((2, MOE_TR, D_MODEL), F32),
            pltpu.SemaphoreType.DMA((2,)),
        ],
    )
    return pl.pallas_call(
        _moe_kernel,
        grid_spec=grid_spec,
        out_shape=jax.ShapeDtypeStruct((y_rows, D_MODEL), F32),
        compiler_params=pltpu.CompilerParams(
            dimension_semantics=("arbitrary",), vmem_limit_bytes=VMEM_LIMIT),
        name="moe",
    )(plan["tile_expert"], plan["tile_rows"], plan["next_expert"], plan["tile_out"],
      plan["meta"], plan["tile_tokens"], hn, w_gate, w_up, w_down)


def _combine_kernel(pos_hbm, y_hbm, h_ref, route_ref, g_ref, o_ref, ybuf, gsem, ids_smem, isem):
    tm = h_ref.shape[0]
    t = pl.program_id(0)
    ntile = pl.num_programs(0) - 1

    def ids_copy(tile, s):
        return pltpu.make_async_copy(pos_hbm.at[tile], ids_smem.at[s], isem.at[s])

    @pl.when(t == 0)
    def _():
        ids_copy(0, 0).start()

    @pl.when(t < ntile)
    def _():
        s = lax.rem(t, 2)
        ids_copy(t, s).wait()

        @pl.when(t + 1 < ntile)
        def _():
            ids_copy(t + 1, 1 - s).start()

        for slot in range(2):
            @pl.when(s == slot)
            def _():
                for kk in range(2):
                    for r in range(tm):
                        flat = kk * tm + r
                        row = ids_smem[slot, flat // LANES, flat % LANES]
                        pltpu.make_async_copy(y_hbm.at[pl.ds(row, 1)], ybuf.at[slot, kk, pl.ds(r, 1)],
                                              gsem.at[slot, kk]).start(priority=r % 2)

    @pl.when(t >= 1)
    def _():
        s = lax.rem(t - 1, 2)
        for kk in range(2):
            pltpu.make_async_copy(y_hbm.at[pl.ds(0, tm)], ybuf.at[s, kk], gsem.at[s, kk]).wait()
        route = route_ref[...]
        h = h_ref[...] + route[:, 2:3] * ybuf[s, 0] + route[:, 3:4] * ybuf[s, 1]
        ms = jnp.mean(h * h, axis=-1, keepdims=True)
        o_ref[...] = h * lax.rsqrt(ms + RMS_EPS) * g_ref[...]


def _combine(y, h, route, g, pos3):
    m = h.shape[0]
    tm = min(CMB_TM, m)
    ntile = m // tm
    assert pos3.shape == (ntile, 2 * tm // LANES, LANES)
    row = lambda i: (jnp.maximum(i - 1, 0), 0)
    return pl.pallas_call(
        _combine_kernel,
        grid=(ntile + 1,),
        in_specs=[
            pl.BlockSpec(memory_space=pl.ANY),
            pl.BlockSpec(memory_space=pl.ANY),
            pl.BlockSpec((tm, D_MODEL), row),
            pl.BlockSpec((tm, ROUTE_W), row),
            pl.BlockSpec((1, D_MODEL), lambda i: (0, 0)),
        ],
        out_specs=pl.BlockSpec((tm, D_MODEL), row),
        out_shape=jax.ShapeDtypeStruct((m, D_MODEL), F32),
        scratch_shapes=[
            pltpu.VMEM((2, 2, tm, D_MODEL), F32),
            pltpu.SemaphoreType.DMA((2, 2)),
            pltpu.SMEM((2, 2 * tm // LANES, LANES), jnp.int32),
            pltpu.SemaphoreType.DMA((2,)),
        ],
        compiler_params=pltpu.CompilerParams(
            dimension_semantics=("arbitrary",), vmem_limit_bytes=VMEM_LIMIT),
        name="combine",
    )(pos3, y, h, route, g)


def _dispatch_plan(route_t, tile_counts, m):
    tr = MOE_TR
    p_rows = 2 * m + N_EXPERTS * tr
    n_tiles = p_rows // tr
    n_tok_tiles, _, tm = route_t.shape
    cnt = tile_counts[:, 0, N_GROUPS:N_GROUPS + N_EXPERTS].astype(jnp.int32)
    before_tile = jnp.cumsum(cnt, axis=0) - cnt
    counts = jnp.sum(cnt, axis=0)
    padded = ((counts + tr - 1) // tr) * tr
    ends = jnp.cumsum(padded)
    starts = ends - padded
    grp = MOE_ROW_GROUP
    y_len = ((counts + grp - 1) // grp) * grp
    y_starts = jnp.cumsum(y_len) - y_len
    base = (y_starts[None, :] + before_tile).astype(jnp.int32)
    fields = route_t.astype(jnp.int32)
    experts = (fields[:, 0, :], fields[:, 1, :])
    ranks = (fields[:, 4, :], fields[:, 5, :])
    expert_ids = jnp.arange(N_EXPERTS, dtype=jnp.int32)
    rows = [rk + jnp.sum(jnp.where(ex[..., None] == expert_ids, base[:, None, :], 0), axis=-1)
            for ex, rk in zip(experts, ranks)]
    ctm = min(CMB_TM, m)
    pos = jnp.concatenate([r.reshape(m // ctm, ctm) for r in rows], axis=1)
    pos = pos.reshape(m // ctm, 2 * ctm // LANES, LANES)
    num_tiles = (ends[-1] // tr).astype(jnp.int32)
    tile_idx = jnp.minimum(jnp.arange(n_tiles, dtype=jnp.int32), num_tiles - 1)
    tile_expert = jnp.sum((tile_idx[:, None] * tr >= ends[None, :]).astype(jnp.int32), axis=1)
    n_slots = 2 * m
    n_fill = n_tiles * tr - n_slots
    stride = 1 << (n_slots + n_fill - 1).bit_length()
    assert (N_EXPERTS + 1) * stride < 2 ** 31
    tok2 = 2 * jnp.arange(m, dtype=jnp.int32).reshape(n_tok_tiles, tm)
    keys = [(ex * stride + tok2 + k).reshape(-1) for k, ex in enumerate(experts)]
    fill = jnp.arange(n_fill, dtype=jnp.int32)
    fill_expert = jnp.sum((fill[:, None] >= jnp.cumsum(padded - counts)[None, :]).astype(jnp.int32), axis=1)
    keys.append(fill_expert * stride + n_slots + fill)
    sorted_low = jnp.sort(jnp.concatenate(keys)) & (stride - 1)
    tile_tokens = jnp.where(sorted_low < n_slots, sorted_low >> 1, sorted_low % m)
    tile_tokens = tile_tokens.reshape(n_tiles, tr // LANES, LANES)
    in_expert = tile_idx - (starts // tr)[tile_expert]
    tile_rows = jnp.clip(counts[tile_expert] - in_expert * tr, 0, tr)
    tile_rows = jnp.where(jnp.arange(n_tiles) < num_tiles, tile_rows, 0)
    ids = jnp.arange(N_EXPERTS, dtype=jnp.int32)
    present = jnp.where(counts > 0, ids, N_EXPERTS)
    next_ge = lax.cummin(present, axis=0, reverse=True)
    next_gt = jnp.concatenate([next_ge[1:], jnp.full((1,), N_EXPERTS, jnp.int32)])
    next_gt = jnp.where(next_gt >= N_EXPERTS, -1, next_gt)
    tile_out = y_starts[tile_expert] + in_expert * tr
    y_used = jnp.sum(y_len)
    meta = jnp.stack([num_tiles, y_used, 2 * m + MOE_TAIL_ROWS - y_used]).astype(jnp.int32)
    plan = dict(tile_expert=tile_expert.astype(jnp.int32),
                tile_rows=tile_rows.astype(jnp.int32), next_expert=next_gt[tile_expert].astype(jnp.int32),
                tile_out=tile_out.astype(jnp.int32), meta=meta, tile_tokens=tile_tokens)
    return plan, pos


def kernel(x, norm_mix_g, w_in, w_gk_up, b_gk, gla_norm_g, swa_sinks, w_out, norm_ffn_g,
           w_group, b_group, w_router, b_router, w_gate, w_up, w_down, norm_final_g):
    batch, seq, d = x.shape
    m = batch * seq
    assert w_in.shape[0] == 1, "single-layer block"
    x2 = x.reshape(m, d)
    w_main, w_lr = _w_prep(jnp.transpose(w_in[0]))
    wup = jnp.pad(w_gk_up[0], ((0, LANES - GLA_RANK), (0, 0))).astype(BF16)
    w_rt = jnp.pad(jnp.concatenate([w_group[0], w_router[0]], axis=1),
                   ((0, 0), (0, LANES - N_GROUPS - N_EXPERTS))).astype(BF16)
    b_rt = jnp.pad(jnp.concatenate([b_group[0], b_router[0]]),
                   (0, LANES - N_GROUPS - N_EXPERTS)).reshape(1, LANES)

    proj, glr = _in_proj(x2, norm_mix_g[0].reshape(1, d), w_main, w_lr)
    o_gla = _gla(proj, glr, wup, b_gk[0].reshape(1, GLA_QK_W),
                 gla_norm_g[0].reshape(1, GLA_V_W), batch, seq)
    o_swa = _swa(proj, swa_sinks[0], batch, seq)
    h_mid, hn, route, route_t, tile_counts = _out_route(o_gla, o_swa, x2, w_out[0].astype(BF16),
                                                        norm_ffn_g[0].reshape(1, d), w_rt, b_rt)
    plan, pos3 = _dispatch_plan(route_t, tile_counts, m)
    y = _moe(hn, w_gate[0], w_up[0], w_down[0], plan)
    out = _combine(y, h_mid, route, norm_final_g.reshape(1, d), pos3)
    return out.reshape(batch, seq, d)
```
